```python
import jax
import jax.numpy as jnp
from jax import lax
import numpy as np

D_MODEL = 1024
BATCH = 2
SEQ = 8192
DEPTH = 2
DEC_BATCH = 16
DEC_SEQ = 64
PAST_LEN = 2048

CHUNK = 64
N_META = 16
GLA_HEADS = 4
GLA_DK = D_MODEL // (2 * GLA_HEADS)
GLA_DV = D_MODEL // GLA_HEADS
GLA_RANK = 16
GLA_TAU = 16.0
ML_HEADS = 4
ML_DH = D_MODEL // ML_HEADS
ML_WIDTH = ML_HEADS * ML_DH
ML_CONV = 4
SB_HEADS = 8
SB_DH = D_MODEL // SB_HEADS
SB_QBLOCK = 128
N_BRANCH = 3
N_GROUPS = 4
EXPERTS_PER_GROUP = 8
N_EXPERTS = N_GROUPS * EXPERTS_PER_GROUP
TOP_K = 2
D_EXPERT = D_MODEL // 2
MOE_BLOCK = 128
DN_ALPHA = (2.0 * DEPTH) ** 0.25
DN_BETA = (8.0 * DEPTH) ** -0.25
LN_EPS = 1e-5
IN_SIZES = (GLA_HEADS * GLA_DK, GLA_HEADS * GLA_DK, GLA_HEADS * GLA_DV, GLA_HEADS * GLA_DV, GLA_RANK,
            ML_WIDTH, ML_HEADS, ML_HEADS, ML_WIDTH,
            SB_HEADS * SB_DH, SB_HEADS * SB_DH, SB_HEADS * SB_DH,
            N_BRANCH * D_MODEL)
D_IN = sum(IN_SIZES)

kernel_name = 'hybrid_gla_mlstm_stickbreak_hmoe'


def layer_norm(x, g, b):
    xf = x.astype(jnp.float32)
    mu = jnp.mean(xf, axis=-1, keepdims=True)
    var = jnp.mean(jnp.square(xf - mu), axis=-1, keepdims=True)
    return ((xf - mu) * lax.rsqrt(var + LN_EPS) * g + b).astype(x.dtype)


def head_norm(h, g):
    mu = jnp.mean(h, axis=-1, keepdims=True)
    var = jnp.mean(jnp.square(h - mu), axis=-1, keepdims=True)
    return (h - mu) * lax.rsqrt(var + LN_EPS) * g


def gla_chunk(S, q, k, v, la):
    L = q.shape[1]
    b = jnp.cumsum(la, axis=1)
    causal = jnp.tril(jnp.ones((L, L), bool))[None, :, :, None, None]
    decay = jnp.exp(jnp.where(causal, b[:, :, None] - b[:, None, :], -jnp.inf))
    att = jnp.einsum('bthd,bshd,btshd->bhts', q, k, decay)
    o = jnp.einsum('bhts,bshv->bthv', att, v) + jnp.einsum('bthd,bhdv->bthv', q * jnp.exp(b), S)
    b_end = b[:, -1]
    S_new = jnp.exp(b_end)[..., None] * S + jnp.einsum('bshd,bshv->bhdv', k * jnp.exp(b_end[:, None] - b), v)
    return S_new, o


def mlstm_chunk(state, q, k, v, li, lf):
    C, n, m = state
    L = q.shape[1]
    b = jnp.cumsum(lf, axis=1)
    a = b + m[:, None]
    causal = jnp.tril(jnp.ones((L, L), bool))[None, :, :, None]
    dmat = jnp.where(causal, b[:, :, None] - b[:, None, :] + li[:, None, :], -jnp.inf)
    mt = jnp.maximum(a, jnp.max(dmat, axis=2))
    w_intra = jnp.exp(dmat - mt[:, :, None])
    w_inter = jnp.exp(a - mt)
    qk = jnp.einsum('bthd,bshd->btsh', q, k) * w_intra
    num = jnp.einsum('btsh,bshv->bthv', qk, v) + w_inter[..., None] * jnp.einsum('bthd,bhdv->bthv', q, C)
    den = jnp.sum(qk, axis=2) + w_inter * jnp.einsum('bthd,bhd->bth', q, n)
    h = num / jnp.maximum(jnp.abs(den), jnp.exp(-mt))[..., None]
    m_new = mt[:, -1]
    wk = jnp.exp(b[:, -1:] - b + li - m_new[:, None])
    ws = jnp.exp(b[:, -1] + m - m_new)
    C_new = ws[..., None, None] * C + jnp.einsum('bsh,bshd,bshv->bhdv', wk, k, v)
    n_new = ws[..., None] * n + jnp.einsum('bsh,bshd->bhd', wk, k)
    return (C_new, n_new, m_new), h


def run_recurrent(step, state, seqs, prompt):
    if not prompt:
        return step(state, *seqs)
    state, out_meta = step(state, *[s[:, :N_META] for s in seqs])

    def to_chunks(s):
        B, T = s.shape[0], s.shape[1] - N_META
        return s[:, N_META:].reshape(B, T // CHUNK, CHUNK, *s.shape[2:]).swapaxes(0, 1)

    state, out = lax.scan(lambda c, xs: step(c, *xs), state, tuple(to_chunks(s) for s in seqs))
    nch, B = out.shape[0], out.shape[1]
    out = out.swapaxes(0, 1).reshape(B, nch * CHUNK, *out.shape[3:])
    return state, jnp.concatenate([out_meta, out], axis=1)


def sb_attend(q, k, v, qpos, kpos):
    f32 = jnp.float32
    z = jnp.einsum('bqhd,bkhd->bhqk', q.astype(f32), k.astype(f32)) * SB_DH ** -0.5
    vis = kpos[None, :] < qpos[:, None]
    log_1mb = jnp.where(vis, jax.nn.log_sigmoid(-z), 0.0)
    suffix = lax.cumsum(log_1mb, axis=3, reverse=True) - log_1mb
    att = jnp.where(vis, jnp.exp(jax.nn.log_sigmoid(z) + suffix), 0.0)
    return jnp.einsum('bhqk,bkhd->bqhd', att, v.astype(f32))


def token_mixer(xn, lp, st):
    prompt = st is None
    B, T, _ = xn.shape
    f32 = jnp.float32
    dt = xn.dtype
    splits = [int(i) for i in np.cumsum(IN_SIZES)[:-1]]
    z = xn @ lp['w_in']
    (g_q, g_k, g_v, g_r, g_lr, m_u, m_i, m_f, m_o, s_q, s_k, s_v, gate) = jnp.split(z, splits, axis=-1)
    if prompt:
        gla_S0 = jnp.zeros((B, GLA_HEADS, GLA_DK, GLA_DV), f32)
        ml0 = (jnp.zeros((B, ML_HEADS, ML_DH, ML_DH), f32), jnp.zeros((B, ML_HEADS, ML_DH), f32),
               jnp.zeros((B, ML_HEADS), f32))
        conv_buf = jnp.zeros((B, ML_CONV - 1, ML_WIDTH), dt)
    else:
        gla_S0 = st[0].astype(f32)
        ml0 = (st[1].astype(f32), st[2].astype(f32), st[3].astype(f32))
        conv_buf = st[4].astype(dt)

    gq = g_q.astype(f32).reshape(B, T, GLA_HEADS, GLA_DK) * GLA_DK ** -0.5
    gk = g_k.astype(f32).reshape(B, T, GLA_HEADS, GLA_DK)
    gv = g_v.astype(f32).reshape(B, T, GLA_HEADS, GLA_DV)
    gla_la = (jax.nn.log_sigmoid((g_lr @ lp['gla_w_a2'] + lp['gla_b_a']).astype(f32)) / GLA_TAU
              ).reshape(B, T, GLA_HEADS, GLA_DK)
    gla_S, o_gla = run_recurrent(gla_chunk, gla_S0, (gq, gk, gv, gla_la), prompt)

    u_pad = jnp.concatenate([conv_buf, m_u], axis=1)
    uc = lp['ml_conv_b'] + sum(u_pad[:, i:i + T] * lp['ml_conv_w'][i] for i in range(ML_CONV))
    conv_tail = u_pad[:, T:]
    uc = jax.nn.silu(uc).reshape(B, T, ML_HEADS, ML_DH)
    mq = jnp.einsum('bthd,hde->bthe', uc, lp['ml_wq']).astype(f32)
    mk = jnp.einsum('bthd,hde->bthe', uc, lp['ml_wk']).astype(f32) * ML_DH ** -0.5
    mv = jnp.einsum('bthd,hde->bthe', m_u.reshape(B, T, ML_HEADS, ML_DH), lp['ml_wv']).astype(f32)
    li = (m_i + lp['ml_b_i']).astype(f32)
    lf = jax.nn.log_sigmoid((m_f + lp['ml_b_f']).astype(f32))
    ml_state, h_ml = run_recurrent(mlstm_chunk, ml0, (mq, mk, mv, li, lf), prompt)

    sq = s_q.reshape(B, T, SB_HEADS, SB_DH)
    sk = s_k.reshape(B, T, SB_HEADS, SB_DH)
    sv = s_v.reshape(B, T, SB_HEADS, SB_DH)
    if prompt:
        pos = jnp.arange(T)
        o_meta = sb_attend(sq[:, :N_META], sk[:, :N_META], sv[:, :N_META], pos[:N_META], pos[:N_META])
        nb = (T - N_META) // SB_QBLOCK
        qb = sq[:, N_META:].reshape(B, nb, SB_QBLOCK, SB_HEADS, SB_DH).swapaxes(0, 1)
        pb = pos[N_META:].reshape(nb, SB_QBLOCK)
        o_f = lax.map(lambda a: sb_attend(a[0], sk, sv, a[1], pos), (qb, pb))
        o_sb = jnp.concatenate([o_meta, o_f.swapaxes(0, 1).reshape(B, T - N_META, SB_HEADS, SB_DH)], axis=1)
    else:
        k_all = jnp.concatenate([st[5].astype(dt), sk], axis=1)
        v_all = jnp.concatenate([st[6].astype(dt), sv], axis=1)
        n_past = st[5].shape[1]
        o_sb = sb_attend(sq, k_all, v_all, n_past + jnp.arange(T), jnp.arange(n_past + T))

    br_gla = ((head_norm(o_gla, lp['gla_norm_g']).reshape(B, T, -1) * jax.nn.silu(g_r.astype(f32))
               ).astype(dt) @ lp['w_br_gla'])
    br_ml = ((head_norm(h_ml, lp['ml_norm_g']).reshape(B, T, -1) * jax.nn.sigmoid(m_o.astype(f32))
              ).astype(dt) @ lp['w_br_ml'])
    br_sb = o_sb.reshape(B, T, -1).astype(dt) @ lp['w_br_sb']
    g = jax.nn.sigmoid(gate.astype(f32)).reshape(B, T, N_BRANCH, D_MODEL).astype(dt)
    merged = g[:, :, 0] * br_gla + g[:, :, 1] * br_ml + g[:, :, 2] * br_sb
    out = merged @ lp['w_out']
    new_state = (gla_S, ml_state[0], ml_state[1], ml_state[2], conv_tail, sk, sv)
    return out, new_state


def grouped_experts(x2d, eid, gate, w_gate, w_up, w_down):
    T, D = x2d.shape
    A = T * TOP_K
    NB = -(-(A + N_EXPERTS * (MOE_BLOCK - 1)) // MOE_BLOCK)
    flat_e = eid.reshape(-1)
    order = jnp.argsort(flat_e)
    sorted_e = flat_e[order]
    counts = jnp.bincount(flat_e, length=N_EXPERTS)
    padded = (counts + MOE_BLOCK - 1) // MOE_BLOCK * MOE_BLOCK
    ends_pad = jnp.cumsum(padded)
    start_pad = ends_pad - padded
    start_sorted = jnp.cumsum(counts) - counts
    dest = start_pad[sorted_e] + jnp.arange(A) - start_sorted[sorted_e]
    tok_pad = jnp.full((NB * MOE_BLOCK,), T, jnp.int32).at[dest].set((order // TOP_K).astype(jnp.int32))
    w_pad = jnp.zeros((NB * MOE_BLOCK,), jnp.float32).at[dest].set(gate.reshape(-1)[order])
    block_e = jnp.minimum(jnp.searchsorted(ends_pad, jnp.arange(NB) * MOE_BLOCK, side='right'), N_EXPERTS - 1)
    x_ext = jnp.concatenate([x2d, jnp.zeros((1, D), x2d.dtype)], axis=0)
    xb = x_ext[tok_pad].reshape(NB, MOE_BLOCK, D)

    def expert_block(args):
        xblk, e = args
        h = jax.nn.silu(xblk @ w_gate[e]) * (xblk @ w_up[e])
        return h @ w_down[e]

    yb = lax.map(expert_block, (xb, block_e)).reshape(NB * MOE_BLOCK, D)
    y = jnp.zeros((T + 1, D), x2d.dtype).at[tok_pad].add(yb * w_pad[:, None].astype(x2d.dtype))
    return y[:T]


def hier_moe(x2d, lp):
    T = x2d.shape[0]
    f32 = jnp.float32
    xf = x2d.astype(f32)
    g_logit = xf @ lp['w_rg'].astype(f32) + lp['b_rg']
    g_prob = jax.nn.softmax(g_logit, axis=-1)
    g_sel = jnp.argmax(g_logit, axis=-1)
    e_logit = (xf @ lp['w_re'].astype(f32) + lp['b_re']).reshape(T, N_GROUPS, EXPERTS_PER_GROUP)
    e_in = jnp.take_along_axis(e_logit, g_sel[:, None, None], axis=1)[:, 0]
    top_v, top_i = lax.top_k(e_in, TOP_K)
    gate = jax.nn.softmax(top_v, axis=-1) * jnp.take_along_axis(g_prob, g_sel[:, None], axis=1)
    eid = g_sel[:, None] * EXPERTS_PER_GROUP + top_i
    return grouped_experts(x2d, eid, gate, lp['w_e_gate'], lp['w_e_up'], lp['w_e_down'])


def trunk_layer(x, lp, st):
    mix, new_state = token_mixer(x, lp, st)
    x = layer_norm(DN_ALPHA * x + mix, lp['ln1_g'], lp['ln1_b'])
    B, T, D = x.shape
    ff = hier_moe(x.reshape(B * T, D), lp).reshape(B, T, D)
    x = layer_norm(DN_ALPHA * x + ff, lp['ln2_g'], lp['ln2_b'])
    return x, new_state


def setup_inputs(seed: int = 0) -> dict:
    key = jax.random.key(seed)
    ks = iter(jax.random.split(key, 64))
    D = D_MODEL

    def nrm(shape, scale):
        return jax.random.normal(next(ks), shape, jnp.float32) * scale

    def gain(shape):
        return 1.0 + nrm(shape, 0.01)

    return {
        'x_prompt': nrm((BATCH, SEQ, D), 1.0),
        'x_sample': nrm((DEC_BATCH, DEC_SEQ, D), 1.0),
        'state_gla': nrm((DEPTH, DEC_BATCH, GLA_HEADS, GLA_DK, GLA_DV), 0.1),
        'state_ml_C': nrm((DEPTH, DEC_BATCH, ML_HEADS, ML_DH, ML_DH), 0.1),
        'state_ml_n': nrm((DEPTH, DEC_BATCH, ML_HEADS, ML_DH), 0.1),
        'state_ml_m': nrm((DEPTH, DEC_BATCH, ML_HEADS), 1.0),
        'state_ml_conv': nrm((DEPTH, DEC_BATCH, ML_CONV - 1, ML_WIDTH), 1.0),
        'cache_sb_k': nrm((DEPTH, DEC_BATCH, N_META + PAST_LEN, SB_HEADS, SB_DH), 1.0),
        'cache_sb_v': nrm((DEPTH, DEC_BATCH, N_META + PAST_LEN, SB_HEADS, SB_DH), 1.0),
        'meta_tokens': nrm((N_META, D), 1.0),
        'ln_in_g': gain((D,)),
        'ln_in_b': nrm((D,), 0.01),
        'w_in': nrm((DEPTH, D, D_IN), D ** -0.5),
        'gla_w_a2': nrm((DEPTH, GLA_RANK, GLA_HEADS * GLA_DK), GLA_RANK ** -0.5),
        'gla_b_a': nrm((DEPTH, GLA_HEADS * GLA_DK), 0.01),
        'gla_norm_g': gain((DEPTH, GLA_DV)),
        'ml_conv_w': nrm((DEPTH, ML_CONV, ML_WIDTH), ML_CONV ** -0.5),
        'ml_conv_b': nrm((DEPTH, ML_WIDTH), 0.01),
        'ml_wq': nrm((DEPTH, ML_HEADS, ML_DH, ML_DH), ML_DH ** -0.5),
        'ml_wk': nrm((DEPTH, ML_HEADS, ML_DH, ML_DH), ML_DH ** -0.5),
        'ml_wv': nrm((DEPTH, ML_HEADS, ML_DH, ML_DH), ML_DH ** -0.5),
        'ml_b_i': nrm((DEPTH, ML_HEADS), 0.1),
        'ml_b_f': jnp.linspace(3.0, 6.0, ML_HEADS, dtype=jnp.float32)[None] + nrm((DEPTH, ML_HEADS), 0.1),
        'ml_norm_g': gain((DEPTH, ML_DH)),
        'w_br_gla': nrm((DEPTH, GLA_HEADS * GLA_DV, D), (GLA_HEADS * GLA_DV) ** -0.5),
        'w_br_ml': nrm((DEPTH, ML_WIDTH, D), ML_WIDTH ** -0.5),
        'w_br_sb': nrm((DEPTH, SB_HEADS * SB_DH, D), (SB_HEADS * SB_DH) ** -0.5),
        'w_out': nrm((DEPTH, D, D), DN_BETA * D ** -0.5),
        'ln1_g': gain((DEPTH, D)),
        'ln1_b': nrm((DEPTH, D), 0.01),
        'w_rg': nrm((DEPTH, D, N_GROUPS), D ** -0.5),
        'b_rg': nrm((DEPTH, N_GROUPS), 0.01),
        'w_re': nrm((DEPTH, D, N_EXPERTS), D ** -0.5),
        'b_re': nrm((DEPTH, N_EXPERTS), 0.01),
        'w_e_gate': nrm((DEPTH, N_EXPERTS, D, D_EXPERT), D ** -0.5),
        'w_e_up': nrm((DEPTH, N_EXPERTS, D, D_EXPERT), D ** -0.5),
        'w_e_down': nrm((DEPTH, N_EXPERTS, D_EXPERT, D), DN_BETA * D_EXPERT ** -0.5),
        'ln2_g': gain((DEPTH, D)),
        'ln2_b': nrm((DEPTH, D), 0.01),
    }


def reference(x_prompt, x_sample, state_gla, state_ml_C, state_ml_n, state_ml_m, state_ml_conv,
              cache_sb_k, cache_sb_v, meta_tokens, ln_in_g, ln_in_b, w_in, gla_w_a2, gla_b_a,
              gla_norm_g, ml_conv_w, ml_conv_b, ml_wq, ml_wk, ml_wv, ml_b_i, ml_b_f, ml_norm_g,
              w_br_gla, w_br_ml, w_br_sb, w_out, ln1_g, ln1_b, w_rg, b_rg, w_re, b_re,
              w_e_gate, w_e_up, w_e_down, ln2_g, ln2_b):
    layer_params = dict(w_in=w_in, gla_w_a2=gla_w_a2, gla_b_a=gla_b_a, gla_norm_g=gla_norm_g,
                        ml_conv_w=ml_conv_w, ml_conv_b=ml_conv_b, ml_wq=ml_wq, ml_wk=ml_wk, ml_wv=ml_wv,
                        ml_b_i=ml_b_i, ml_b_f=ml_b_f, ml_norm_g=ml_norm_g, w_br_gla=w_br_gla,
                        w_br_ml=w_br_ml, w_br_sb=w_br_sb, w_out=w_out, ln1_g=ln1_g, ln1_b=ln1_b,
                        w_rg=w_rg, b_rg=b_rg, w_re=w_re, b_re=b_re, w_e_gate=w_e_gate,
                        w_e_up=w_e_up, w_e_down=w_e_down, ln2_g=ln2_g, ln2_b=ln2_b)
    B = x_prompt.shape[0]
    meta = jnp.broadcast_to(meta_tokens.astype(x_prompt.dtype)[None], (B, N_META, D_MODEL))
    xp = layer_norm(jnp.concatenate([meta, x_prompt], axis=1), ln_in_g, ln_in_b)
    xs = layer_norm(x_sample, ln_in_g, ln_in_b)
    new_p = [[] for _ in range(7)]
    new_s = [[] for _ in range(7)]
    for l in range(DEPTH):
        lp = {name: arr[l] for name, arr in layer_params.items()}
        xp, st_p = trunk_layer(xp, lp, None)
        st_in = (state_gla[l], state_ml_C[l], state_ml_n[l], state_ml_m[l], state_ml_conv[l],
                 cache_sb_k[l], cache_sb_v[l])
        xs, st_s = trunk_layer(xs, lp, st_in)
        for i in range(7):
            new_p[i].append(st_p[i])
            new_s[i].append(st_s[i])
    sp = [jnp.stack(a, axis=0) for a in new_p]
    ss = [jnp.stack(a, axis=0) for a in new_s]
    y_prompt = xp[:, N_META:]
    return (y_prompt, xs, sp[0], ss[0], sp[1], ss[1], sp[2], ss[2], sp[3], ss[3],
            sp[4], ss[4], sp[5], ss[5], sp[6], ss[6])
```

```python
import functools

import numpy as np
import jax
import jax.numpy as jnp
from jax import lax
from jax.experimental import pallas as pl
from jax.experimental.pallas import tpu as pltpu

F32 = jnp.float32
BF16 = jnp.bfloat16

D_MODEL = 1024
N_META = 16
CHUNK = 64
PAD_FRONT = 112
GLA_HEADS, GLA_DK, GLA_DV, GLA_RANK, GLA_TAU = 4, 128, 256, 16, 16.0
ML_HEADS, ML_DH, ML_CONV = 4, 256, 4
SB_HEADS, SB_DH = 8, 128
N_GROUPS, EXPERTS_PER_GROUP, TOP_K = 4, 8, 2
N_EXPERTS = N_GROUPS * EXPERTS_PER_GROUP
D_EXPERT = D_MODEL // 2
MOE_BLK = 256
LN_EPS = 1e-5
NEG_BIG = -1e30
SB_SKIP = 110.0
SB_BK = 128

C_GQ, C_GK, C_GV, C_GR = 0, 512, 1024, 2048
C_MU, C_MO = 3072, 4096
C_SQ, C_SK, C_SV = 5120, 6144, 7168
C_GATE = 8192
Z_MAIN = 11264
L_LR, L_MI, L_MF = 0, 16, 20

VMEM_LIMIT = 48 * 1024 * 1024


def _cparams(sem):
    return pltpu.CompilerParams(dimension_semantics=sem, vmem_limit_bytes=VMEM_LIMIT)


def _pick(n, cands):
    for c in cands:
        if n % c == 0:
            return c
    raise ValueError(f"no tile in {cands} divides {n}")


def _log_sigmoid(x):
    return jnp.minimum(x, 0.0) - jnp.log1p(jnp.exp(-jnp.abs(x)))


def _sigmoid(x):
    return 1.0 / (1.0 + jnp.exp(-x))


def _split2(x):
    hi = x.astype(BF16)
    lo = (x - hi.astype(F32)).astype(BF16)
    return hi, lo


def _split3(x):
    hi = x.astype(BF16)
    r = x - hi.astype(F32)
    mid = r.astype(BF16)
    lo = (r - mid.astype(F32)).astype(BF16)
    return hi, mid, lo


def _dot(a, b):
    return jnp.dot(a, b, preferred_element_type=F32)


def _dot_nt(a, b):
    return lax.dot_general(a, b, (((1,), (1,)), ((), ())), preferred_element_type=F32)


def _dot_tn(a, b):
    return lax.dot_general(a, b, (((0,), (0,)), ((), ())), preferred_element_type=F32)


def _ln_math(x, g, b):
    mu = jnp.mean(x, axis=-1, keepdims=True)
    xc = x - mu
    var = jnp.mean(xc * xc, axis=-1, keepdims=True)
    return xc * lax.rsqrt(var + LN_EPS) * g + b


def _ln_kernel(x_ref, g_ref, b_ref, o_ref):
    o_ref[...] = _ln_math(x_ref[...], g_ref[...], b_ref[...])


def layer_norm_rows(x, g, b):
    n, d = x.shape
    tm = _pick(n, (512, 256, 128, 64))
    return pl.pallas_call(
        _ln_kernel,
        grid=(n // tm,),
        in_specs=[pl.BlockSpec((tm, d), lambda i: (i, 0)),
                  pl.BlockSpec((1, d), lambda i: (0, 0)),
                  pl.BlockSpec((1, d), lambda i: (0, 0))],
        out_specs=pl.BlockSpec((tm, d), lambda i: (i, 0)),
        out_shape=jax.ShapeDtypeStruct((n, d), F32),
        compiler_params=_cparams(("arbitrary",)),
        name="ln_in",
    )(x, g.reshape(1, d), b.reshape(1, d))


def _mm_kernel(x_ref, w_ref, o_ref):
    o_ref[...] = _dot(x_ref[...].astype(BF16), w_ref[...])


def in_projection(x, w_bf16, tn, name):
    n, d = x.shape
    cols = w_bf16.shape[1]
    tm = _pick(n, (768, 512, 384, 256, 128, 64))
    return pl.pallas_call(
        _mm_kernel,
        grid=(cols // tn, n // tm),
        in_specs=[pl.BlockSpec((tm, d), lambda j, i: (i, 0)),
                  pl.BlockSpec((d, tn), lambda j, i: (0, j))],
        out_specs=pl.BlockSpec((tm, tn), lambda j, i: (i, j)),
        out_shape=jax.ShapeDtypeStruct((n, cols), F32),
        compiler_params=_cparams(("arbitrary", "arbitrary")),
        name=name,
    )(x, w_bf16)


GLA_LEVELS = (32, 16, 8, 4, 2, 1)


def _gla_level_matrix():
    L = CHUNK
    t = np.arange(L)[:, None]
    j = np.arange(L)[None, :]
    blocks = []
    for m in (L,) + GLA_LEVELS:
        blocks.append(((j <= t) & (j // m == t // m)).astype(np.float32))
    for m in (L,) + GLA_LEVELS:
        blocks.append(((j > t) & (j // m == t // m)).astype(np.float32))
    return np.concatenate(blocks, axis=0)


def _gla_kernel(q_ref, k_ref, v_ref, r_ref, lr_ref, wa_ref, ba_ref, ng_ref, lvl_ref, s0_ref,
                o_ref, s_ref, *, n_pad):
    c = pl.program_id(1)
    L = CHUNK

    @pl.when(c == 0)
    def _():
        s_ref[...] = s0_ref[...]

    row = c * L + lax.broadcasted_iota(jnp.int32, (L, 1), 0)
    valid = row >= n_pad
    x = _dot(lr_ref[...].astype(BF16), wa_ref[...]) + ba_ref[...]
    la = jnp.where(valid, _log_sigmoid(x) * (1.0 / GLA_TAU), 0.0)
    la_hi, la_lo = _split2(la)
    lvl = lvl_ref[...]
    ex = jnp.exp(_dot(lvl, la_hi) + _dot(lvl, la_lo))
    nlev = len(GLA_LEVELS) + 1

    ti = lax.broadcasted_iota(jnp.int32, (L, L), 0)
    si = lax.broadcasted_iota(jnp.int32, (L, L), 1)
    level_masks = []
    for m in GLA_LEVELS:
        sh = m.bit_length() - 1
        tb = lax.shift_right_logical(ti, sh)
        sb = lax.shift_right_logical(si, sh)
        level_masks.append((lax.shift_right_logical(tb, 1) == lax.shift_right_logical(sb, 1))
                           & ((tb & 1) == 1) & ((sb & 1) == 0))
    eye = ti == si

    for h in range(GLA_HEADS):
        dk = slice(h * GLA_DK, (h + 1) * GLA_DK)
        dv = slice(h * GLA_DV, (h + 1) * GLA_DV)
        q = q_ref[:, dk] * (GLA_DK ** -0.5)
        k = jnp.where(valid, k_ref[:, dk], 0.0)
        v = jnp.where(valid, v_ref[:, dv], 0.0).astype(BF16)
        att = jnp.where(eye, _dot_nt(q.astype(BF16), k.astype(BF16)), 0.0)
        for lev in range(len(GLA_LEVELS)):
            qe = ex[(1 + lev) * L:(2 + lev) * L, dk]
            ke = ex[(nlev + 1 + lev) * L:(nlev + 2 + lev) * L, dk]
            a = _dot_nt((q * qe).astype(BF16), (k * ke).astype(BF16))
            att = jnp.where(level_masks[lev], a, att)
        eb = ex[0:L, dk]
        st = s_ref[0, h]
        o = _dot(att.astype(BF16), v) + _dot_nt((q * eb).astype(BF16), st.astype(BF16))
        ke_end = ex[nlev * L:(nlev + 1) * L, dk]
        s_ref[0, h] = st * eb[L - 1:L, :] + _dot_tn(v, (k * ke_end).astype(BF16))
        mu = jnp.mean(o, axis=-1, keepdims=True)
        oc = o - mu
        var = jnp.mean(oc * oc, axis=-1, keepdims=True)
        on = oc * lax.rsqrt(var + LN_EPS) * ng_ref[...]
        r = r_ref[:, dv]
        o_ref[:, dv] = (on * (r * _sigmoid(r))).astype(BF16)


def gla_branch(z_main, z_small, wa, ba, ng, lvl, s0t, *, row0, nb, nchunk, n_pad):
    L = CHUNK
    rb0 = row0 // L
    hk = GLA_HEADS * GLA_DK
    hv = GLA_HEADS * GLA_DV

    def rows(colblk):
        return lambda b, c: (rb0 + b * nchunk + c, colblk)

    return pl.pallas_call(
        functools.partial(_gla_kernel, n_pad=n_pad),
        grid=(nb, nchunk),
        in_specs=[pl.BlockSpec((L, hk), rows(C_GQ // hk)),
                  pl.BlockSpec((L, hk), rows(C_GK // hk)),
                  pl.BlockSpec((L, hv), rows(C_GV // hv)),
                  pl.BlockSpec((L, hv), rows(C_GR // hv)),
                  pl.BlockSpec((L, 128), rows(0)),
                  pl.BlockSpec((128, hk), lambda b, c: (0, 0)),
                  pl.BlockSpec((1, hk), lambda b, c: (0, 0)),
                  pl.BlockSpec((1, GLA_DV), lambda b, c: (0, 0)),
                  pl.BlockSpec(lvl.shape, lambda b, c: (0, 0)),
                  pl.BlockSpec((1, GLA_HEADS, GLA_DV, GLA_DK), lambda b, c: (b, 0, 0, 0))],
        out_specs=[pl.BlockSpec((L, hv), lambda b, c: (b * nchunk + c, 0)),
                   pl.BlockSpec((1, GLA_HEADS, GLA_DV, GLA_DK), lambda b, c: (b, 0, 0, 0))],
        out_shape=[jax.ShapeDtypeStruct((nb * nchunk * L, hv), BF16),
                   jax.ShapeDtypeStruct((nb, GLA_HEADS, GLA_DV, GLA_DK), F32)],
        compiler_params=_cparams(("arbitrary", "arbitrary")),
        name="gla",
    )(z_main, z_main, z_main, z_main, z_small, wa, ba, ng, lvl, s0t)


def _mlstm_kernel(u_ref, mo_ref, g_ref, gb_ref, cw_ref, cb_ref, wq_ref, wk_ref, wv_ref, ng_ref, tri_ref,
                  conv0_ref, c0_ref, n0_ref, m0_ref,
                  o_ref, c_ref, n_ref, m_ref, ext_ref, *, n_pad):
    c = pl.program_id(1)
    L = CHUNK
    W = ML_HEADS * ML_DH

    @pl.when(c == 0)
    def _():
        ext_ref[0:8, :] = conv0_ref[0]
        c_ref[...] = c0_ref[...]
        n_ref[...] = n0_ref[...]
        m_ref[...] = m0_ref[...]

    row = c * L + lax.broadcasted_iota(jnp.int32, (L, 1), 0)
    valid = row >= n_pad

    u = jnp.where(valid, u_ref[...], 0.0)
    ext_ref[8:8 + L, :] = u
    uc = cb_ref[...]
    for i in range(ML_CONV):
        uc = uc + ext_ref[8 - (ML_CONV - 1) + i:8 - (ML_CONV - 1) + i + L, :] * cw_ref[i:i + 1, :]
    ext_ref[0:8, :] = u[L - 8:L, :]
    ucs = (uc * _sigmoid(uc)).astype(BF16)
    ub = u_ref[...].astype(BF16)

    g = g_ref[...] + gb_ref[...]
    li = jnp.where(valid, g, NEG_BIG)
    lf = jnp.where(valid, _log_sigmoid(g), 0.0)
    tri = tri_ref[...]
    lf_hi, lf_lo = _split2(lf)
    b_col = _dot(tri, lf_hi) + _dot(tri, lf_lo)
    a_col = b_col + m_ref[0]
    li_row = li.T
    b_row = b_col.T

    ti = lax.broadcasted_iota(jnp.int32, (L, L), 0)
    si = lax.broadcasted_iota(jnp.int32, (L, L), 1)
    causal = si <= ti
    lane = lax.broadcasted_iota(jnp.int32, (1, 128), 1)
    m_vec = m_ref[0]

    for h in range(ML_HEADS):
        d = slice(h * ML_DH, (h + 1) * ML_DH)
        q = _dot(ucs[:, d], wq_ref[h]).astype(BF16)
        k = _dot(ucs[:, d], wk_ref[h]) * (ML_DH ** -0.5)
        v = _dot(ub[:, d], wv_ref[h]).astype(BF16)
        lf_l = L_MF + h
        li_l = L_MI + h
        bc = b_col[:, lf_l:lf_l + 1]
        ac = a_col[:, lf_l:lf_l + 1]
        lic = li[:, li_l:li_l + 1]
        dmat = jnp.where(causal, bc - b_row[lf_l:lf_l + 1, :] + li_row[li_l:li_l + 1, :], NEG_BIG)
        mt = jnp.maximum(ac, jnp.max(dmat, axis=-1, keepdims=True))
        w_intra = jnp.exp(dmat - mt)
        w_inter = jnp.exp(ac - mt)
        qk = _dot_nt(q, k.astype(BF16)) * w_intra
        cmat = c_ref[0, h]
        nrow = n_ref[0, h:h + 1, :]
        num = _dot(qk.astype(BF16), v) + w_inter * _dot(q, cmat.astype(BF16))
        qn = jnp.sum(q.astype(F32) * nrow.astype(BF16).astype(F32), axis=-1, keepdims=True)
        den = jnp.sum(qk, axis=-1, keepdims=True) + w_inter * qn
        hh = num / jnp.maximum(jnp.abs(den), jnp.exp(-mt))
        m_new = mt[L - 1:L, :]
        b_end = bc[L - 1:L, :]
        wk = jnp.exp(b_end - bc + lic - m_new)
        ws = jnp.exp(b_end + m_vec[:, lf_l:lf_l + 1] - m_new)
        kw = k * wk
        c_ref[0, h] = ws * cmat + _dot_tn(kw.astype(BF16), v)
        n_ref[0, h:h + 1, :] = ws * nrow + jnp.sum(kw, axis=0, keepdims=True)
        m_vec = jnp.where(lane == lf_l, m_new, m_vec)
        mu = jnp.mean(hh, axis=-1, keepdims=True)
        hc = hh - mu
        var = jnp.mean(hc * hc, axis=-1, keepdims=True)
        hn = hc * lax.rsqrt(var + LN_EPS) * ng_ref[...]
        o_ref[:, d] = (hn * _sigmoid(mo_ref[:, d])).astype(BF16)
    m_ref[0] = m_vec


def mlstm_branch(z_main, z_small, gbias, cw, cb, wq, wk, wv, ng, tri, conv0, c0, n0, m0,
                 *, row0, nb, nchunk, n_pad):
    L = CHUNK
    rb0 = row0 // L
    W = ML_HEADS * ML_DH

    def rows(colblk):
        return lambda b, c: (rb0 + b * nchunk + c, colblk)

    const2 = lambda b, c: (0, 0)
    const3 = lambda b, c: (0, 0, 0)
    return pl.pallas_call(
        functools.partial(_mlstm_kernel, n_pad=n_pad),
        grid=(nb, nchunk),
        in_specs=[pl.BlockSpec((L, W), rows(C_MU // W)),
                  pl.BlockSpec((L, W), rows(C_MO // W)),
                  pl.BlockSpec((L, 128), rows(0)),
                  pl.BlockSpec((1, 128), const2),
                  pl.BlockSpec((ML_CONV, W), const2),
                  pl.BlockSpec((1, W), const2),
                  pl.BlockSpec((ML_HEADS, ML_DH, ML_DH), const3),
                  pl.BlockSpec((ML_HEADS, ML_DH, ML_DH), const3),
                  pl.BlockSpec((ML_HEADS, ML_DH, ML_DH), const3),
                  pl.BlockSpec((1, ML_DH), const2),
                  pl.BlockSpec((L, L), const2),
                  pl.BlockSpec((1, 8, W), lambda b, c: (b, 0, 0)),
                  pl.BlockSpec((1, ML_HEADS, ML_DH, ML_DH), lambda b, c: (b, 0, 0, 0)),
                  pl.BlockSpec((1, ML_HEADS, ML_DH), lambda b, c: (b, 0, 0)),
                  pl.BlockSpec((1, 1, 128), lambda b, c: (b, 0, 0))],
        out_specs=[pl.BlockSpec((L, W), lambda b, c: (b * nchunk + c, 0)),
                   pl.BlockSpec((1, ML_HEADS, ML_DH, ML_DH), lambda b, c: (b, 0, 0, 0)),
                   pl.BlockSpec((1, ML_HEADS, ML_DH), lambda b, c: (b, 0, 0)),
                   pl.BlockSpec((1, 1, 128), lambda b, c: (b, 0, 0))],
        out_shape=[jax.ShapeDtypeStruct((nb * nchunk * L, W), BF16),
                   jax.ShapeDtypeStruct((nb, ML_HEADS, ML_DH, ML_DH), F32),
                   jax.ShapeDtypeStruct((nb, ML_HEADS, ML_DH), F32),
                   jax.ShapeDtypeStruct((nb, 1, 128), F32)],
        scratch_shapes=[pltpu.VMEM((8 + L, W), F32)],
        compiler_params=_cparams(("arbitrary", "arbitrary")),
        name="mlstm",
    )(z_main, z_main, z_small, gbias, cw, cb, wq, wk, wv, ng, tri, conv0, c0, n0, m0)


def _sb_kernel(q_ref, k_ref, v_ref, u_ref, o_ref, *, bq, q_off, n_pad):
    qi = pl.program_id(2)
    bk = SB_BK
    q = (q_ref[...] * (SB_DH ** -0.5)).astype(BF16)
    q_lo = q_off + qi * bq
    qpos = q_lo + lax.broadcasted_iota(jnp.int32, (bq, 1), 0)
    kio = lax.broadcasted_iota(jnp.int32, (1, bk), 1)
    umat = u_ref[...]
    j0 = (q_lo + bq - 2) // bk

    def body(carry):
        j, cs, acc, _ = carry
        start = pl.multiple_of(j * bk, bk)
        k = k_ref[pl.ds(start, bk), :].astype(BF16)
        v = v_ref[pl.ds(start, bk), :].astype(BF16)
        z = _dot_nt(q, k)
        kpos = start + kio
        vis = (kpos < qpos) & (kpos >= n_pad)
        t = jnp.log1p(jnp.exp(-jnp.abs(z)))
        ls = jnp.minimum(z, 0.0) - t
        l1 = jnp.where(vis, jnp.minimum(-z, 0.0) - t, 0.0)
        l1_hi, l1_lo = _split2(l1)
        suffix = _dot(l1_hi, umat) + _dot(l1_lo, umat)
        att = jnp.where(vis, jnp.exp(ls + suffix + cs), 0.0)
        acc = acc + _dot(att.astype(BF16), v)
        cs = cs + jnp.sum(l1, axis=-1, keepdims=True)
        return j - 1, cs, acc, jnp.max(cs)

    def cond(carry):
        j, _, _, cmax = carry
        return (j >= 0) & (cmax > -SB_SKIP)

    init = (j0, jnp.zeros((bq, 1), F32), jnp.zeros((bq, SB_DH), F32), jnp.float32(0.0))
    _, _, acc, _ = lax.while_loop(cond, body, init)
    o_ref[...] = acc.astype(BF16)


def sb_attention(q2d, k2d, v2d, umat, *, nb, tq, tk, bq, q_rb0, q_cb0, k_cb0, v_cb0, q_off, n_pad):
    nq = tq // bq
    return pl.pallas_call(
        functools.partial(_sb_kernel, bq=bq, q_off=q_off, n_pad=n_pad),
        grid=(nb, SB_HEADS, nq),
        in_specs=[pl.BlockSpec((bq, SB_DH), lambda b, h, i: (q_rb0 + b * nq + i, q_cb0 + h)),
                  pl.BlockSpec((tk, SB_DH), lambda b, h, i: (b, k_cb0 + h)),
                  pl.BlockSpec((tk, SB_DH), lambda b, h, i: (b, v_cb0 + h)),
                  pl.BlockSpec((SB_BK, SB_BK), lambda b, h, i: (0, 0))],
        out_specs=pl.BlockSpec((bq, SB_DH), lambda b, h, i: (b * nq + i, h)),
        out_shape=jax.ShapeDtypeStruct((nb * tq, SB_HEADS * SB_DH), BF16),
        compiler_params=_cparams(("arbitrary", "arbitrary", "arbitrary")),
        name="sb_attention",
    )(q2d, k2d, v2d, umat)


def _merge_kernel(ag_ref, am_ref, as_ref, g0_ref, g1_ref, g2_ref, x_ref,
                  wg_ref, wm_ref, ws_ref, wo_ref, lg_ref, lb_ref, o_ref, *, alpha):
    merged = (_sigmoid(g0_ref[...]) * _dot(ag_ref[...], wg_ref[...])
              + _sigmoid(g1_ref[...]) * _dot(am_ref[...], wm_ref[...])
              + _sigmoid(g2_ref[...]) * _dot(as_ref[...], ws_ref[...]))
    out = _dot(merged.astype(BF16), wo_ref[...])
    o_ref[...] = _ln_math(alpha * x_ref[...] + out, lg_ref[...], lb_ref[...])


def merge_ln1(a_gla, a_ml, a_sb, z_main, x, wg, wm, ws, wo, lg, lb, alpha):
    n, d = x.shape
    tm = _pick(n, (256, 128, 64))
    row = lambda i: (i, 0)
    const = lambda i: (0, 0)
    gcb = C_GATE // d
    return pl.pallas_call(
        functools.partial(_merge_kernel, alpha=alpha),
        grid=(n // tm,),
        in_specs=[pl.BlockSpec((tm, d), row), pl.BlockSpec((tm, d), row), pl.BlockSpec((tm, d), row),
                  pl.BlockSpec((tm, d), lambda i: (i, gcb)),
                  pl.BlockSpec((tm, d), lambda i: (i, gcb + 1)),
                  pl.BlockSpec((tm, d), lambda i: (i, gcb + 2)),
                  pl.BlockSpec((tm, d), row),
                  pl.BlockSpec((d, d), const), pl.BlockSpec((d, d), const),
                  pl.BlockSpec((d, d), const), pl.BlockSpec((d, d), const),
                  pl.BlockSpec((1, d), const), pl.BlockSpec((1, d), const)],
        out_specs=pl.BlockSpec((tm, d), row),
        out_shape=jax.ShapeDtypeStruct((n, d), F32),
        compiler_params=_cparams(("arbitrary",)),
        name="merge_ln1",
    )(a_gla, a_ml, a_sb, z_main, z_main, z_main, x, wg, wm, ws, wo, lg.reshape(1, d), lb.reshape(1, d))


R_ROWS = 8 + N_EXPERTS


def _router_kernel(x_ref, w_ref, b_ref, u_ref, idx_ref, gate_ref, cnt_ref):
    i = pl.program_id(0)
    tm = x_ref.shape[0]

    @pl.when(i == 0)
    def _():
        cnt_ref[...] = jnp.zeros_like(cnt_ref)

    xh, xl = _split2(x_ref[...])
    wh, wl = _split2(w_ref[...])
    logits = _dot_nt(wh, xh) + _dot_nt(wh, xl) + _dot_nt(wl, xh) + b_ref[...]

    g = [logits[j:j + 1, :] for j in range(N_GROUPS)]
    gmax = jnp.maximum(jnp.maximum(g[0], g[1]), jnp.maximum(g[2], g[3]))
    gsel = jnp.where(g[0] == gmax, 0, jnp.where(g[1] == gmax, 1, jnp.where(g[2] == gmax, 2, 3)))
    gden = (jnp.exp(g[0] - gmax) + jnp.exp(g[1] - gmax)) + (jnp.exp(g[2] - gmax) + jnp.exp(g[3] - gmax))
    gprob = 1.0 / gden
    e_in = jnp.where(gsel == 0, logits[8:16, :],
                     jnp.where(gsel == 1, logits[16:24, :],
                               jnp.where(gsel == 2, logits[24:32, :], logits[32:40, :])))
    ridx = lax.broadcasted_iota(jnp.int32, (EXPERTS_PER_GROUP, tm), 0)
    v1 = jnp.max(e_in, axis=0, keepdims=True)
    i1 = jnp.min(jnp.where(e_in == v1, ridx, EXPERTS_PER_GROUP), axis=0, keepdims=True)
    e2 = jnp.where(ridx == i1, -jnp.inf, e_in)
    v2 = jnp.max(e2, axis=0, keepdims=True)
    i2 = jnp.min(jnp.where(e2 == v2, ridx, EXPERTS_PER_GROUP), axis=0, keepdims=True)
    t = jnp.exp(v2 - v1)
    p1 = 1.0 / (1.0 + t)
    gate_ref[0:1, :] = p1 * gprob
    gate_ref[1:2, :] = (t * p1) * gprob
    gate_ref[2:8, :] = jnp.zeros((6, tm), F32)
    eid1 = gsel * EXPERTS_PER_GROUP + i1
    eid2 = gsel * EXPERTS_PER_GROUP + i2

    eidx = lax.broadcasted_iota(jnp.int32, (N_EXPERTS, tm), 0)
    oh1 = eidx == eid1
    oh2 = eidx == eid2
    oh1f = jnp.where(oh1, 1.0, 0.0)
    oh2f = jnp.where(oh2, 1.0, 0.0)
    umat = u_ref[...]
    pre1 = _dot(oh1f.astype(BF16), umat)
    pre2 = _dot(oh2f.astype(BF16), umat)
    cnt1 = jnp.sum(oh1f, axis=1, keepdims=True)
    cnt2 = jnp.sum(oh2f, axis=1, keepdims=True)
    base = cnt_ref[:, 0:1]
    rank1 = jnp.sum(jnp.where(oh1, base + pre1, 0.0), axis=0, keepdims=True)
    rank2 = jnp.sum(jnp.where(oh2, base + cnt1 + pre2, 0.0), axis=0, keepdims=True)
    idx_ref[0:1, :] = eid1
    idx_ref[1:2, :] = eid2
    idx_ref[2:3, :] = rank1.astype(jnp.int32)
    idx_ref[3:4, :] = rank2.astype(jnp.int32)
    idx_ref[4:8, :] = jnp.zeros((4, tm), jnp.int32)
    cnt_ref[...] = cnt_ref[...] + (cnt1 + cnt2)


def moe_router(x, w_r, b_r, umat):
    n, d = x.shape
    tm = umat.shape[0]
    return pl.pallas_call(
        _router_kernel,
        grid=(n // tm,),
        in_specs=[pl.BlockSpec((tm, d), lambda i: (i, 0)),
                  pl.BlockSpec((R_ROWS, d), lambda i: (0, 0)),
                  pl.BlockSpec((R_ROWS, 1), lambda i: (0, 0)),
                  pl.BlockSpec((tm, tm), lambda i: (0, 0))],
        out_specs=[pl.BlockSpec((8, tm), lambda i: (0, i)),
                   pl.BlockSpec((8, tm), lambda i: (0, i)),
                   pl.BlockSpec((N_EXPERTS, 128), lambda i: (0, 0))],
        out_shape=[jax.ShapeDtypeStruct((8, n), jnp.int32),
                   jax.ShapeDtypeStruct((8, n), F32),
                   jax.ShapeDtypeStruct((N_EXPERTS, 128), F32)],
        compiler_params=_cparams(("arbitrary",)),
        name="moe_router",
    )(x, w_r, b_r, umat)


def _row_copy(src_hbm, dst_vmem, src_row, dst_row, sem):
    return pltpu.make_async_copy(src_hbm.at[pl.ds(src_row, 1), :], dst_vmem.at[pl.ds(dst_row, 1), :], sem)


def _gather_rows(idx_ref, base, src_hbm, dst_vmem, nrows, sem):
    def start(r, carry):
        _row_copy(src_hbm, dst_vmem, idx_ref[base + r], r, sem).start()
        return carry

    lax.fori_loop(0, nrows, start, 0)

    def wait(r, carry):
        _row_copy(src_hbm, dst_vmem, 0, r, sem).wait()
        return carry

    lax.fori_loop(0, nrows, wait, 0)


def _moe_kernel(be_ref, tok_ref, nblk_ref, x_hbm, wg_ref, wu_ref, wd_ref, o_ref,
                xbuf, wgb, wub, wdb, sem):
    i = pl.program_id(0)
    prev_e = be_ref[jnp.maximum(i - 1, 0)]

    @pl.when((i == 0) | (be_ref[i] != prev_e))
    def _():
        wgb[...] = wg_ref[0].astype(BF16)
        wub[...] = wu_ref[0].astype(BF16)
        wdb[...] = wd_ref[0].astype(BF16)

    @pl.when(i < nblk_ref[0])
    def _():
        _gather_rows(tok_ref, i * MOE_BLK, x_hbm, xbuf, MOE_BLK, sem)
        xb = xbuf[...].astype(BF16)
        a = _dot(xb, wgb[...])
        hmid = (a * _sigmoid(a)) * _dot(xb, wub[...])
        o_ref[...] = _dot(hmid.astype(BF16), wdb[...])

    @pl.when(i >= nblk_ref[0])
    def _():
        o_ref[...] = jnp.zeros_like(o_ref)


def moe_experts(x, block_e, tok_pad, nblk, w_gate, w_up, w_down):
    n, d = x.shape
    nb = block_e.shape[0]
    grid_spec = pltpu.PrefetchScalarGridSpec(
        num_scalar_prefetch=3,
        grid=(nb,),
        in_specs=[pl.BlockSpec(memory_space=pl.ANY),
                  pl.BlockSpec((1, d, D_EXPERT), lambda i, be, tok, nblk: (be[i], 0, 0)),
                  pl.BlockSpec((1, d, D_EXPERT), lambda i, be, tok, nblk: (be[i], 0, 0)),
                  pl.BlockSpec((1, D_EXPERT, d), lambda i, be, tok, nblk: (be[i], 0, 0))],
        out_specs=pl.BlockSpec((MOE_BLK, d), lambda i, be, tok, nblk: (i, 0)),
        scratch_shapes=[pltpu.VMEM((MOE_BLK, d), F32),
                        pltpu.VMEM((d, D_EXPERT), BF16),
                        pltpu.VMEM((d, D_EXPERT), BF16),
                        pltpu.VMEM((D_EXPERT, d), BF16),
                        pltpu.SemaphoreType.DMA(())],
    )
    return pl.pallas_call(
        _moe_kernel,
        grid_spec=grid_spec,
        out_shape=jax.ShapeDtypeStruct((nb * MOE_BLK, d), F32),
        compiler_params=_cparams(("arbitrary",)),
        name="moe_experts",
    )(block_e, tok_pad, nblk, x, w_gate, w_up, w_down)


def _combine_kernel(dest_ref, yb_hbm, x_ref, gate_ref, lg_ref, lb_ref, o_ref, ybuf0, ybuf1, sem, *, alpha, n):
    i = pl.program_id(0)
    tm = x_ref.shape[0]
    _gather_rows(dest_ref, i * tm, yb_hbm, ybuf0, tm, sem)
    _gather_rows(dest_ref, n + i * tm, yb_hbm, ybuf1, tm, sem)
    gate = gate_ref[...]
    y = ybuf0[...] * gate[:, 0:1] + ybuf1[...] * gate[:, 1:2]
    o_ref[...] = _ln_math(alpha * x_ref[...] + y, lg_ref[...], lb_ref[...])


def moe_combine_ln2(x, yb, dest, gate_cols, lg, lb, alpha):
    n, d = x.shape
    tm = _pick(n, (256, 128, 64))
    grid_spec = pltpu.PrefetchScalarGridSpec(
        num_scalar_prefetch=1,
        grid=(n // tm,),
        in_specs=[pl.BlockSpec(memory_space=pl.ANY),
                  pl.BlockSpec((tm, d), lambda i, dest: (i, 0)),
                  pl.BlockSpec((tm, TOP_K), lambda i, dest: (i, 0)),
                  pl.BlockSpec((1, d), lambda i, dest: (0, 0)),
                  pl.BlockSpec((1, d), lambda i, dest: (0, 0))],
        out_specs=pl.BlockSpec((tm, d), lambda i, dest: (i, 0)),
        scratch_shapes=[pltpu.VMEM((tm, d), F32), pltpu.VMEM((tm, d), F32), pltpu.SemaphoreType.DMA(())],
    )
    return pl.pallas_call(
        functools.partial(_combine_kernel, alpha=alpha, n=n),
        grid_spec=grid_spec,
        out_shape=jax.ShapeDtypeStruct((n, d), F32),
        compiler_params=_cparams(("arbitrary",)),
        name="moe_combine_ln2",
    )(dest, yb, x, gate_cols, lg.reshape(1, d), lb.reshape(1, d))


def _tri_lower(n):
    t = np.arange(n)
    return (t[None, :] <= t[:, None]).astype(np.float32)


def _strict_upper(n):
    t = np.arange(n)
    return (t[:, None] < t[None, :]).astype(np.float32)


def kernel(x_prompt, x_sample, state_gla, state_ml_C, state_ml_n, state_ml_m, state_ml_conv, cache_sb_k, cache_sb_v, meta_tokens, ln_in_g, ln_in_b, w_in, gla_w_a2, gla_b_a, gla_norm_g, ml_conv_w, ml_conv_b, ml_wq, ml_wk, ml_wv, ml_b_i, ml_b_f, ml_norm_g, w_br_gla, w_br_ml, w_br_sb, w_out, ln1_g, ln1_b, w_rg, b_rg, w_re, b_re, w_e_gate, w_e_up, w_e_down, ln2_g, ln2_b):
    D = D_MODEL
    B, SEQ, _ = x_prompt.shape
    DB, DS, _ = x_sample.shape
    depth = w_in.shape[0]
    n_cache = cache_sb_k.shape[2]
    assert DS == CHUNK and SEQ % 128 == 0 and x_prompt.shape[2] == D
    alpha = (2.0 * depth) ** 0.25
    TP = PAD_FRONT + N_META + SEQ
    NP = B * TP
    NS = DB * DS
    N = NP + NS
    nchunk_p = TP // CHUNK
    pad_s = (-(n_cache + DS)) % SB_BK
    TKS = pad_s + n_cache + DS

    meta = jnp.broadcast_to(meta_tokens.astype(F32)[None], (B, N_META, D))
    xp = jnp.concatenate([jnp.zeros((B, PAD_FRONT, D), F32), meta, x_prompt], axis=1).reshape(NP, D)
    x = jnp.concatenate([xp, x_sample.reshape(NS, D)], axis=0)
    x = layer_norm_rows(x, ln_in_g, ln_in_b)

    lvl = jnp.asarray(_gla_level_matrix(), BF16)
    tri = jnp.asarray(_tri_lower(CHUNK), BF16)
    u_sb = jnp.asarray(_strict_upper(SB_BK).T, BF16)
    tm_r = _pick(N, (256, 128, 64))
    u_r = jnp.asarray(_strict_upper(tm_r), BF16)

    A = N * TOP_K
    NB = -(-(A + N_EXPERTS * (MOE_BLK - 1)) // MOE_BLK)

    outs_p = [[] for _ in range(7)]
    outs_s = [[] for _ in range(7)]
    for l in range(depth):
        wi = w_in[l]
        w_main = jnp.concatenate([wi[:, 0:3072], wi[:, 3088:4112], wi[:, 4120:]], axis=1).astype(BF16)
        w_small = jnp.concatenate([wi[:, 3072:3088], wi[:, 4112:4120],
                                   jnp.zeros((D, 128 - 24), F32)], axis=1).astype(BF16)
        z_main = in_projection(x, w_main, 1024, "in_proj_main")
        z_small = in_projection(x, w_small, 128, "in_proj_small")

        wa = jnp.concatenate([gla_w_a2[l], jnp.zeros((128 - GLA_RANK, GLA_HEADS * GLA_DK), F32)], axis=0).astype(BF16)
        ba = gla_b_a[l].reshape(1, -1)
        ng = gla_norm_g[l].reshape(1, -1)
        s0_p = jnp.zeros((B, GLA_HEADS, GLA_DV, GLA_DK), F32)
        s0_s = jnp.swapaxes(state_gla[l].astype(F32), -1, -2)
        a_gla_p, sT_p = gla_branch(z_main, z_small, wa, ba, ng, lvl, s0_p,
                                   row0=0, nb=B, nchunk=nchunk_p, n_pad=PAD_FRONT)
        a_gla_s, sT_s = gla_branch(z_main, z_small, wa, ba, ng, lvl, s0_s,
                                   row0=NP, nb=DB, nchunk=1, n_pad=0)
        a_gla = jnp.concatenate([a_gla_p, a_gla_s], axis=0)

        gbias = jnp.zeros((1, 128), F32).at[0, L_MI:L_MI + ML_HEADS].set(ml_b_i[l]).at[0, L_MF:L_MF + ML_HEADS].set(ml_b_f[l])
        cw = ml_conv_w[l]
        cb = ml_conv_b[l].reshape(1, -1)
        wq = ml_wq[l].astype(BF16)
        wk = ml_wk[l].astype(BF16)
        wv = ml_wv[l].astype(BF16)
        mng = ml_norm_g[l].reshape(1, -1)
        W = ML_HEADS * ML_DH
        conv0_p = jnp.zeros((B, 8, W), F32)
        conv0_s = jnp.concatenate([jnp.zeros((DB, 8 - (ML_CONV - 1), W), F32), state_ml_conv[l].astype(F32)], axis=1)
        m0_s = jnp.zeros((DB, 1, 128), F32).at[:, 0, L_MF:L_MF + ML_HEADS].set(state_ml_m[l].astype(F32))
        a_ml_p, c_p, n_p, m_p = mlstm_branch(
            z_main, z_small, gbias, cw, cb, wq, wk, wv, mng, tri, conv0_p,
            jnp.zeros((B, ML_HEADS, ML_DH, ML_DH), F32), jnp.zeros((B, ML_HEADS, ML_DH), F32),
            jnp.zeros((B, 1, 128), F32), row0=0, nb=B, nchunk=nchunk_p, n_pad=PAD_FRONT)
        a_ml_s, c_s, n_s, m_s = mlstm_branch(
            z_main, z_small, gbias, cw, cb, wq, wk, wv, mng, tri, conv0_s,
            state_ml_C[l].astype(F32), state_ml_n[l].astype(F32), m0_s,
            row0=NP, nb=DB, nchunk=1, n_pad=0)
        a_ml = jnp.concatenate([a_ml_p, a_ml_s], axis=0)

        zs = z_main[NP:]
        sk_s = zs[:, C_SK:C_SK + D].reshape(DB, DS, D)
        sv_s = zs[:, C_SV:C_SV + D].reshape(DB, DS, D)
        k_all = jnp.concatenate([jnp.zeros((DB, pad_s, D), F32), cache_sb_k[l].reshape(DB, n_cache, D).astype(F32), sk_s],
                                axis=1).reshape(DB * TKS, D)
        v_all = jnp.concatenate([jnp.zeros((DB, pad_s, D), F32), cache_sb_v[l].reshape(DB, n_cache, D).astype(F32), sv_s],
                                axis=1).reshape(DB * TKS, D)
        bq_p = 128
        a_sb_p = sb_attention(z_main, z_main, z_main, u_sb, nb=B, tq=TP, tk=TP, bq=bq_p,
                              q_rb0=0, q_cb0=C_SQ // SB_DH, k_cb0=C_SK // SB_DH, v_cb0=C_SV // SB_DH,
                              q_off=0, n_pad=PAD_FRONT)
        a_sb_s = sb_attention(z_main, k_all, v_all, u_sb, nb=DB, tq=DS, tk=TKS, bq=DS,
                              q_rb0=NP // DS, q_cb0=C_SQ // SB_DH, k_cb0=0, v_cb0=0,
                              q_off=pad_s + n_cache, n_pad=pad_s)
        a_sb = jnp.concatenate([a_sb_p, a_sb_s], axis=0)

        x1 = merge_ln1(a_gla, a_ml, a_sb, z_main, x,
                       w_br_gla[l].astype(BF16), w_br_ml[l].astype(BF16), w_br_sb[l].astype(BF16),
                       w_out[l].astype(BF16), ln1_g[l], ln1_b[l], alpha)

        w_r = jnp.concatenate([w_rg[l].T, jnp.zeros((8 - N_GROUPS, D), F32), w_re[l].T], axis=0)
        b_r = jnp.concatenate([b_rg[l], jnp.zeros((8 - N_GROUPS,), F32), b_re[l]]).reshape(R_ROWS, 1)
        idx, gates, cnt = moe_router(x1, w_r, b_r, u_r)
        counts = cnt[:, 0].astype(jnp.int32)
        padded = (counts + MOE_BLK - 1) // MOE_BLK * MOE_BLK
        ends_pad = jnp.cumsum(padded)
        start_pad = ends_pad - padded
        dest = (start_pad[idx[0:2]] + idx[2:4]).reshape(-1)
        tok = jnp.tile(jnp.arange(N, dtype=jnp.int32), TOP_K)
        tok_pad = jnp.zeros((NB * MOE_BLK,), jnp.int32).at[dest].set(tok)
        block_e = jnp.minimum(jnp.searchsorted(ends_pad, jnp.arange(NB, dtype=jnp.int32) * MOE_BLK, side='right'),
                              N_EXPERTS - 1).astype(jnp.int32)
        nblk = (ends_pad[-1:] // MOE_BLK).astype(jnp.int32)
        yb = moe_experts(x1, block_e, tok_pad, nblk, w_e_gate[l], w_e_up[l], w_e_down[l])
        x = moe_combine_ln2(x1, yb, dest, gates[0:2].T, ln2_g[l], ln2_b[l], alpha)

        zp = z_main[:NP].reshape(B, TP, Z_MAIN)
        zs3 = zs.reshape(DB, DS, Z_MAIN)
        kv_shape = (SB_HEADS, SB_DH)
        outs_p[0].append(jnp.swapaxes(sT_p, -1, -2))
        outs_s[0].append(jnp.swapaxes(sT_s, -1, -2))
        outs_p[1].append(c_p); outs_s[1].append(c_s)
        outs_p[2].append(n_p); outs_s[2].append(n_s)
        outs_p[3].append(m_p[:, 0, L_MF:L_MF + ML_HEADS]); outs_s[3].append(m_s[:, 0, L_MF:L_MF + ML_HEADS])
        outs_p[4].append(zp[:, TP - (ML_CONV - 1):, C_MU:C_MU + W]); outs_s[4].append(zs3[:, DS - (ML_CONV - 1):, C_MU:C_MU + W])
        outs_p[5].append(zp[:, PAD_FRONT:, C_SK:C_SK + D].reshape(B, TP - PAD_FRONT, *kv_shape))
        outs_s[5].append(zs3[:, :, C_SK:C_SK + D].reshape(DB, DS, *kv_shape))
        outs_p[6].append(zp[:, PAD_FRONT:, C_SV:C_SV + D].reshape(B, TP - PAD_FRONT, *kv_shape))
        outs_s[6].append(zs3[:, :, C_SV:C_SV + D].reshape(DB, DS, *kv_shape))

    sp = [jnp.stack(a, axis=0) for a in outs_p]
    ss = [jnp.stack(a, axis=0) for a in outs_s]
    y_prompt = x[:NP].reshape(B, TP, D)[:, PAD_FRONT + N_META:]
    y_sample = x[NP:].reshape(DB, DS, D)
    return (y_prompt, y_sample, sp[0], ss[0], sp[1], ss[1], sp[2], ss[2], sp[3], ss[3],
            sp[4], ss[4], sp[5], ss[5], sp[6], ss[6])
```

```python
import functools

import numpy as np
import jax
import jax.numpy as jnp
from jax import lax
from jax.experimental import pallas as pl
from jax.experimental.pallas import tpu as pltpu

F32 = jnp.float32
BF16 = jnp.bfloat16

D_MODEL = 1024
N_META = 16
CHUNK = 64
PAD_FRONT = 112
GLA_HEADS, GLA_DK, GLA_DV, GLA_RANK, GLA_TAU = 4, 128, 256, 16, 16.0
ML_HEADS, ML_DH, ML_CONV = 4, 256, 4
SB_HEADS, SB_DH = 8, 128
N_GROUPS, EXPERTS_PER_GROUP, TOP_K = 4, 8, 2
N_EXPERTS = N_GROUPS * EXPERTS_PER_GROUP
D_EXPERT = D_MODEL // 2
MOE_BLK = 256
LN_EPS = 1e-5
NEG_BIG = -1e30
SB_SKIP = 110.0
SB_BK = 128

C_GQ, C_GK, C_GV, C_GR = 0, 512, 1024, 2048
C_MU, C_MO = 3072, 4096
C_SQ = 5120
C_GATE = 6144
Z_MAIN = 9216
C_SK, C_SV = 0, 1024
SB_HG = 4
L_LR, L_MI, L_MF = 0, 16, 20

VMEM_LIMIT = 48 * 1024 * 1024


def _cparams(sem):
    return pltpu.CompilerParams(dimension_semantics=sem, vmem_limit_bytes=VMEM_LIMIT)


def _pick(n, cands):
    for c in cands:
        if n % c == 0:
            return c
    raise ValueError(f"no tile in {cands} divides {n}")


def _log_sigmoid(x):
    return jnp.minimum(x, 0.0) - jnp.log1p(jnp.exp(-jnp.abs(x)))


def _sigmoid(x):
    return 1.0 / (1.0 + jnp.exp(-x))


def _split2(x):
    hi = x.astype(BF16)
    lo = (x - hi.astype(F32)).astype(BF16)
    return hi, lo


def _split3(x):
    hi = x.astype(BF16)
    r = x - hi.astype(F32)
    mid = r.astype(BF16)
    lo = (r - mid.astype(F32)).astype(BF16)
    return hi, mid, lo


def _dot(a, b):
    return jnp.dot(a, b, preferred_element_type=F32)


def _dot_nt(a, b):
    return lax.dot_general(a, b, (((1,), (1,)), ((), ())), preferred_element_type=F32)


def _dot_tn(a, b):
    return lax.dot_general(a, b, (((0,), (0,)), ((), ())), preferred_element_type=F32)


def _ln_math(x, g, b):
    mu = jnp.mean(x, axis=-1, keepdims=True)
    xc = x - mu
    var = jnp.mean(xc * xc, axis=-1, keepdims=True)
    return xc * lax.rsqrt(var + LN_EPS) * g + b


def _ln_kernel(x_ref, g_ref, b_ref, o_ref):
    o_ref[...] = _ln_math(x_ref[...], g_ref[...], b_ref[...])


def layer_norm_rows(x, g, b):
    n, d = x.shape
    tm = _pick(n, (512, 256, 128, 64))
    return pl.pallas_call(
        _ln_kernel,
        grid=(n // tm,),
        in_specs=[pl.BlockSpec((tm, d), lambda i: (i, 0)),
                  pl.BlockSpec((1, d), lambda i: (0, 0)),
                  pl.BlockSpec((1, d), lambda i: (0, 0))],
        out_specs=pl.BlockSpec((tm, d), lambda i: (i, 0)),
        out_shape=jax.ShapeDtypeStruct((n, d), F32),
        compiler_params=_cparams(("arbitrary",)),
        name="ln_in",
    )(x, g.reshape(1, d), b.reshape(1, d))


def _mm_kernel(x_ref, w_ref, o_ref):
    o_ref[...] = _dot(x_ref[...].astype(BF16), w_ref[...])


def _mm2_kernel(x_ref, w_ref, o_ref, ob_ref):
    z = _dot(x_ref[...].astype(BF16), w_ref[...])
    o_ref[...] = z
    ob_ref[...] = z.astype(BF16)


def in_projection(x, w_bf16, tn, name, with_bf16=False):
    n, d = x.shape
    cols = w_bf16.shape[1]
    tm = _pick(n, (768, 512, 384, 256, 128, 64))
    out_spec = pl.BlockSpec((tm, tn), lambda j, i: (i, j))
    return pl.pallas_call(
        _mm2_kernel if with_bf16 else _mm_kernel,
        grid=(cols // tn, n // tm),
        in_specs=[pl.BlockSpec((tm, d), lambda j, i: (i, 0)),
                  pl.BlockSpec((d, tn), lambda j, i: (0, j))],
        out_specs=[out_spec, out_spec] if with_bf16 else out_spec,
        out_shape=([jax.ShapeDtypeStruct((n, cols), F32), jax.ShapeDtypeStruct((n, cols), BF16)]
                   if with_bf16 else jax.ShapeDtypeStruct((n, cols), F32)),
        compiler_params=_cparams(("arbitrary", "arbitrary")),
        name=name,
    )(x, w_bf16)


GLA_LEVELS = (32, 16, 8, 4, 2, 1)


def _gla_level_matrix():
    L = CHUNK
    t = np.arange(L)[:, None]
    j = np.arange(L)[None, :]
    blocks = []
    for m in (L,) + GLA_LEVELS:
        blocks.append(((j <= t) & (j // m == t // m)).astype(np.float32))
    for m in (L,) + GLA_LEVELS:
        blocks.append(((j > t) & (j // m == t // m)).astype(np.float32))
    return np.concatenate(blocks, axis=0)


def _gla_kernel(q_ref, k_ref, v_ref, r_ref, lr_ref, wa_ref, ba_ref, ng_ref, lvl_ref, s0_ref,
                o_ref, s_ref, *, n_pad):
    c = pl.program_id(1)
    L = CHUNK

    @pl.when(c == 0)
    def _():
        s_ref[...] = s0_ref[...]

    row = c * L + lax.broadcasted_iota(jnp.int32, (L, 1), 0)
    valid = row >= n_pad
    x = _dot(lr_ref[...].astype(BF16), wa_ref[...]) + ba_ref[...]
    la = jnp.where(valid, _log_sigmoid(x) * (1.0 / GLA_TAU), 0.0)
    la_hi, la_lo = _split2(la)
    lvl = lvl_ref[...]
    ex = jnp.exp(_dot(lvl, la_hi) + _dot(lvl, la_lo))
    nlev = len(GLA_LEVELS) + 1

    ti = lax.broadcasted_iota(jnp.int32, (L, L), 0)
    si = lax.broadcasted_iota(jnp.int32, (L, L), 1)
    level_masks = []
    for m in GLA_LEVELS:
        sh = m.bit_length() - 1
        tb = lax.shift_right_logical(ti, sh)
        sb = lax.shift_right_logical(si, sh)
        level_masks.append((lax.shift_right_logical(tb, 1) == lax.shift_right_logical(sb, 1))
                           & ((tb & 1) == 1) & ((sb & 1) == 0))
    eye = ti == si

    for h in range(GLA_HEADS):
        dk = slice(h * GLA_DK, (h + 1) * GLA_DK)
        dv = slice(h * GLA_DV, (h + 1) * GLA_DV)
        q = q_ref[:, dk] * (GLA_DK ** -0.5)
        k = jnp.where(valid, k_ref[:, dk], 0.0)
        v = jnp.where(valid, v_ref[:, dv], 0.0).astype(BF16)
        att = jnp.where(eye, _dot_nt(q.astype(BF16), k.astype(BF16)), 0.0)
        for lev in range(len(GLA_LEVELS)):
            qe = ex[(1 + lev) * L:(2 + lev) * L, dk]
            ke = ex[(nlev + 1 + lev) * L:(nlev + 2 + lev) * L, dk]
            a = _dot_nt((q * qe).astype(BF16), (k * ke).astype(BF16))
            att = jnp.where(level_masks[lev], a, att)
        eb = ex[0:L, dk]
        st = s_ref[0, h]
        o = _dot(att.astype(BF16), v) + _dot_nt((q * eb).astype(BF16), st.astype(BF16))
        ke_end = ex[nlev * L:(nlev + 1) * L, dk]
        s_ref[0, h] = st * eb[L - 1:L, :] + _dot_tn(v, (k * ke_end).astype(BF16))
        mu = jnp.mean(o, axis=-1, keepdims=True)
        oc = o - mu
        var = jnp.mean(oc * oc, axis=-1, keepdims=True)
        on = oc * lax.rsqrt(var + LN_EPS) * ng_ref[...]
        r = r_ref[:, dv]
        o_ref[:, dv] = (on * (r * _sigmoid(r))).astype(BF16)


def gla_branch(z_main, z_small, wa, ba, ng, lvl, s0t, *, row0, nb, nchunk, n_pad):
    L = CHUNK
    rb0 = row0 // L
    hk = GLA_HEADS * GLA_DK
    hv = GLA_HEADS * GLA_DV

    def rows(colblk):
        return lambda b, c: (rb0 + b * nchunk + c, colblk)

    return pl.pallas_call(
        functools.partial(_gla_kernel, n_pad=n_pad),
        grid=(nb, nchunk),
        in_specs=[pl.BlockSpec((L, hk), rows(C_GQ // hk)),
                  pl.BlockSpec((L, hk), rows(C_GK // hk)),
                  pl.BlockSpec((L, hv), rows(C_GV // hv)),
                  pl.BlockSpec((L, hv), rows(C_GR // hv)),
                  pl.BlockSpec((L, 128), rows(0)),
                  pl.BlockSpec((128, hk), lambda b, c: (0, 0)),
                  pl.BlockSpec((1, hk), lambda b, c: (0, 0)),
                  pl.BlockSpec((1, GLA_DV), lambda b, c: (0, 0)),
                  pl.BlockSpec(lvl.shape, lambda b, c: (0, 0)),
                  pl.BlockSpec((1, GLA_HEADS, GLA_DV, GLA_DK), lambda b, c: (b, 0, 0, 0))],
        out_specs=[pl.BlockSpec((L, hv), lambda b, c: (b * nchunk + c, 0)),
                   pl.BlockSpec((1, GLA_HEADS, GLA_DV, GLA_DK), lambda b, c: (b, 0, 0, 0))],
        out_shape=[jax.ShapeDtypeStruct((nb * nchunk * L, hv), BF16),
                   jax.ShapeDtypeStruct((nb, GLA_HEADS, GLA_DV, GLA_DK), F32)],
        compiler_params=_cparams(("arbitrary", "arbitrary")),
        name="gla",
    )(z_main, z_main, z_main, z_main, z_small, wa, ba, ng, lvl, s0t)


def _mlstm_kernel(u_ref, mo_ref, g_ref, gb_ref, cw_ref, cb_ref, wq_ref, wk_ref, wv_ref, ng_ref, tri_ref,
                  conv0_ref, c0_ref, n0_ref, m0_ref,
                  o_ref, c_ref, n_ref, m_ref, ext_ref, *, n_pad):
    c = pl.program_id(1)
    L = CHUNK
    W = ML_HEADS * ML_DH

    @pl.when(c == 0)
    def _():
        ext_ref[0:8, :] = conv0_ref[0]
        c_ref[...] = c0_ref[...]
        n_ref[...] = n0_ref[...]
        m_ref[...] = m0_ref[...]

    row = c * L + lax.broadcasted_iota(jnp.int32, (L, 1), 0)
    valid = row >= n_pad

    u = jnp.where(valid, u_ref[...], 0.0)
    ext_ref[8:8 + L, :] = u
    uc = cb_ref[...]
    for i in range(ML_CONV):
        uc = uc + ext_ref[8 - (ML_CONV - 1) + i:8 - (ML_CONV - 1) + i + L, :] * cw_ref[i:i + 1, :]
    ext_ref[0:8, :] = u[L - 8:L, :]
    ucs = (uc * _sigmoid(uc)).astype(BF16)
    ub = u_ref[...].astype(BF16)

    g = g_ref[...] + gb_ref[...]
    li = jnp.where(valid, g, NEG_BIG)
    lf = jnp.where(valid, _log_sigmoid(g), 0.0)
    tri = tri_ref[...]
    lf_hi, lf_lo = _split2(lf)
    b_col = _dot(tri, lf_hi) + _dot(tri, lf_lo)
    a_col = b_col + m_ref[0]
    li_row = li.T
    b_row = b_col.T

    ti = lax.broadcasted_iota(jnp.int32, (L, L), 0)
    si = lax.broadcasted_iota(jnp.int32, (L, L), 1)
    causal = si <= ti
    lane = lax.broadcasted_iota(jnp.int32, (1, 128), 1)
    m_vec = m_ref[0]

    for h in range(ML_HEADS):
        d = slice(h * ML_DH, (h + 1) * ML_DH)
        q = _dot(ucs[:, d], wq_ref[h]).astype(BF16)
        k = _dot(ucs[:, d], wk_ref[h]) * (ML_DH ** -0.5)
        v = _dot(ub[:, d], wv_ref[h]).astype(BF16)
        lf_l = L_MF + h
        li_l = L_MI + h
        bc = b_col[:, lf_l:lf_l + 1]
        ac = a_col[:, lf_l:lf_l + 1]
        lic = li[:, li_l:li_l + 1]
        dmat = jnp.where(causal, bc - b_row[lf_l:lf_l + 1, :] + li_row[li_l:li_l + 1, :], NEG_BIG)
        mt = jnp.maximum(ac, jnp.max(dmat, axis=-1, keepdims=True))
        w_intra = jnp.exp(dmat - mt)
        w_inter = jnp.exp(ac - mt)
        qk = _dot_nt(q, k.astype(BF16)) * w_intra
        cmat = c_ref[0, h]
        nrow = n_ref[0, h:h + 1, :]
        num = _dot(qk.astype(BF16), v) + w_inter * _dot(q, cmat.astype(BF16))
        qn = jnp.sum(q.astype(F32) * nrow.astype(BF16).astype(F32), axis=-1, keepdims=True)
        den = jnp.sum(qk, axis=-1, keepdims=True) + w_inter * qn
        hh = num / jnp.maximum(jnp.abs(den), jnp.exp(-mt))
        m_new = mt[L - 1:L, :]
        b_end = bc[L - 1:L, :]
        wk = jnp.exp(b_end - bc + lic - m_new)
        ws = jnp.exp(b_end + m_vec[:, lf_l:lf_l + 1] - m_new)
        kw = k * wk
        c_ref[0, h] = ws * cmat + _dot_tn(kw.astype(BF16), v)
        n_ref[0, h:h + 1, :] = ws * nrow + jnp.sum(kw, axis=0, keepdims=True)
        m_vec = jnp.where(lane == lf_l, m_new, m_vec)
        mu = jnp.mean(hh, axis=-1, keepdims=True)
        hc = hh - mu
        var = jnp.mean(hc * hc, axis=-1, keepdims=True)
        hn = hc * lax.rsqrt(var + LN_EPS) * ng_ref[...]
        o_ref[:, d] = (hn * _sigmoid(mo_ref[:, d])).astype(BF16)
    m_ref[0] = m_vec


def mlstm_branch(z_main, z_small, gbias, cw, cb, wq, wk, wv, ng, tri, conv0, c0, n0, m0,
                 *, row0, nb, nchunk, n_pad):
    L = CHUNK
    rb0 = row0 // L
    W = ML_HEADS * ML_DH

    def rows(colblk):
        return lambda b, c: (rb0 + b * nchunk + c, colblk)

    const2 = lambda b, c: (0, 0)
    const3 = lambda b, c: (0, 0, 0)
    return pl.pallas_call(
        functools.partial(_mlstm_kernel, n_pad=n_pad),
        grid=(nb, nchunk),
        in_specs=[pl.BlockSpec((L, W), rows(C_MU // W)),
                  pl.BlockSpec((L, W), rows(C_MO // W)),
                  pl.BlockSpec((L, 128), rows(0)),
                  pl.BlockSpec((1, 128), const2),
                  pl.BlockSpec((ML_CONV, W), const2),
                  pl.BlockSpec((1, W), const2),
                  pl.BlockSpec((ML_HEADS, ML_DH, ML_DH), const3),
                  pl.BlockSpec((ML_HEADS, ML_DH, ML_DH), const3),
                  pl.BlockSpec((ML_HEADS, ML_DH, ML_DH), const3),
                  pl.BlockSpec((1, ML_DH), const2),
                  pl.BlockSpec((L, L), const2),
                  pl.BlockSpec((1, 8, W), lambda b, c: (b, 0, 0)),
                  pl.BlockSpec((1, ML_HEADS, ML_DH, ML_DH), lambda b, c: (b, 0, 0, 0)),
                  pl.BlockSpec((1, ML_HEADS, ML_DH), lambda b, c: (b, 0, 0)),
                  pl.BlockSpec((1, 1, 128), lambda b, c: (b, 0, 0))],
        out_specs=[pl.BlockSpec((L, W), lambda b, c: (b * nchunk + c, 0)),
                   pl.BlockSpec((1, ML_HEADS, ML_DH, ML_DH), lambda b, c: (b, 0, 0, 0)),
                   pl.BlockSpec((1, ML_HEADS, ML_DH), lambda b, c: (b, 0, 0)),
                   pl.BlockSpec((1, 1, 128), lambda b, c: (b, 0, 0))],
        out_shape=[jax.ShapeDtypeStruct((nb * nchunk * L, W), BF16),
                   jax.ShapeDtypeStruct((nb, ML_HEADS, ML_DH, ML_DH), F32),
                   jax.ShapeDtypeStruct((nb, ML_HEADS, ML_DH), F32),
                   jax.ShapeDtypeStruct((nb, 1, 128), F32)],
        scratch_shapes=[pltpu.VMEM((8 + L, W), F32)],
        compiler_params=_cparams(("arbitrary", "arbitrary")),
        name="mlstm",
    )(z_main, z_main, z_small, gbias, cw, cb, wq, wk, wv, ng, tri, conv0, c0, n0, m0)


def _sb_block(q, k, v, vis, cs, umat):
    z = _dot_nt(q, k)
    t = jnp.log1p(jnp.exp(-jnp.abs(z)))
    ls = jnp.minimum(z, 0.0) - t
    l1 = jnp.minimum(-z, 0.0) - t
    if vis is not None:
        l1 = jnp.where(vis, l1, 0.0)
    l1_hi, l1_lo = _split2(l1)
    suffix = _dot(l1_hi, umat) + _dot(l1_lo, umat)
    att = jnp.exp(ls + suffix + cs)
    if vis is not None:
        att = jnp.where(vis, att, 0.0)
    return _dot(att.astype(BF16), v), jnp.sum(l1, axis=-1, keepdims=True)


def _sb_prompt_kernel(q_ref, k_ref, v_ref, u_ref, o_ref, acc_ref, *, bq, n_pad):
    qi = pl.program_id(2)
    bk = SB_BK
    q_lo = qi * bq
    qpos = q_lo + lax.broadcasted_iota(jnp.int32, (bq, 1), 0)
    kio = lax.broadcasted_iota(jnp.int32, (1, bk), 1)
    umat = u_ref[...]
    heads = [slice(h * SB_DH, (h + 1) * SB_DH) for h in range(SB_HG)]
    qs = [(q_ref[:, hs] * (SB_DH ** -0.5)).astype(BF16) for hs in heads]
    acc_ref[...] = jnp.zeros_like(acc_ref)
    j0 = (q_lo + bq - 2) // bk

    def body(carry):
        j, cs, _ = carry
        start = pl.multiple_of(j * bk, bk)
        kpos = start + kio
        vis = (kpos < qpos) & (kpos >= n_pad)
        new_cs = []
        cmax = None
        for h, hs in enumerate(heads):
            pv, rs = _sb_block(qs[h], k_ref[pl.ds(start, bk), hs], v_ref[pl.ds(start, bk), hs], vis, cs[h], umat)
            acc_ref[:, hs] += pv
            c = cs[h] + rs
            new_cs.append(c)
            m = jnp.max(c)
            cmax = m if cmax is None else jnp.maximum(cmax, m)
        return j - 1, tuple(new_cs), cmax

    def cond(carry):
        j, _, cmax = carry
        return (j >= 0) & (cmax > -SB_SKIP)

    init = (j0, tuple(jnp.zeros((bq, 1), F32) for _ in heads), jnp.float32(0.0))
    lax.while_loop(cond, body, init)
    o_ref[...] = acc_ref[...].astype(BF16)


def sb_attention_prompt(z_main, kvb, umat, *, nb, tp, bq, n_pad):
    nq = tp // bq
    gw = SB_HG * SB_DH
    ng = SB_HEADS // SB_HG
    return pl.pallas_call(
        functools.partial(_sb_prompt_kernel, bq=bq, n_pad=n_pad),
        grid=(nb, ng, nq),
        in_specs=[pl.BlockSpec((bq, gw), lambda b, g, i: (b * nq + i, C_SQ // gw + g)),
                  pl.BlockSpec((tp, gw), lambda b, g, i: (b, C_SK // gw + g)),
                  pl.BlockSpec((tp, gw), lambda b, g, i: (b, C_SV // gw + g)),
                  pl.BlockSpec((SB_BK, SB_BK), lambda b, g, i: (0, 0))],
        out_specs=pl.BlockSpec((bq, gw), lambda b, g, i: (b * nq + i, g)),
        out_shape=jax.ShapeDtypeStruct((nb * tp, SB_HEADS * SB_DH), BF16),
        scratch_shapes=[pltpu.VMEM((bq, gw), F32)],
        compiler_params=_cparams(("arbitrary", "arbitrary", "arbitrary")),
        name="sb_prompt",
    )(z_main, kvb, kvb, umat)


def _sb_sample_kernel(q_ref, kn_ref, vn_ref, kc_ref, vc_ref, u_ref, o_ref, acc_ref, *, n_cache):
    T = CHUNK
    bk = SB_BK
    nfull = n_cache // bk
    rem = n_cache % bk
    umat = u_ref[...]
    heads = [slice(h * SB_DH, (h + 1) * SB_DH) for h in range(SB_HEADS)]
    qs = [(q_ref[:, hs] * (SB_DH ** -0.5)).astype(BF16) for hs in heads]
    vis_new = lax.broadcasted_iota(jnp.int32, (1, T), 1) < lax.broadcasted_iota(jnp.int32, (T, 1), 0)
    cs0 = []
    cmax0 = None
    zero = jnp.zeros((T, 1), F32)
    for h, hs in enumerate(heads):
        pv, rs = _sb_block(qs[h], kn_ref[:, hs].astype(BF16), vn_ref[:, hs].astype(BF16), vis_new, zero,
                           umat[0:T, 0:T])
        acc_ref[:, hs] = pv
        cs0.append(rs)
        m = jnp.max(rs)
        cmax0 = m if cmax0 is None else jnp.maximum(cmax0, m)

    def body(carry):
        j, cs, _ = carry
        start = pl.multiple_of(n_cache - bk * (j + 1), 8)
        new_cs = []
        cmax = None
        for h, hs in enumerate(heads):
            pv, rs = _sb_block(qs[h], kc_ref[0, 0, pl.ds(start, bk), h, :].astype(BF16),
                               vc_ref[0, 0, pl.ds(start, bk), h, :].astype(BF16), None, cs[h], umat)
            acc_ref[:, hs] += pv
            c = cs[h] + rs
            new_cs.append(c)
            m = jnp.max(c)
            cmax = m if cmax is None else jnp.maximum(cmax, m)
        return j + 1, tuple(new_cs), cmax

    def cond(carry):
        j, _, cmax = carry
        return (j < nfull) & (cmax > -SB_SKIP)

    _, cs, _ = lax.while_loop(cond, body, (jnp.int32(0), tuple(cs0), cmax0))
    if rem:
        vis_rem = jnp.broadcast_to(lax.broadcasted_iota(jnp.int32, (1, bk), 1) < rem, (T, bk))
        for h, hs in enumerate(heads):
            pv, _ = _sb_block(qs[h], kc_ref[0, 0, 0:bk, h, :].astype(BF16), vc_ref[0, 0, 0:bk, h, :].astype(BF16),
                              vis_rem, cs[h], umat)
            acc_ref[:, hs] += pv
    o_ref[...] = acc_ref[...].astype(BF16)


def sb_attention_sample(z_main, z_kv, cache_k, cache_v, umat, *, layer, row0, nb):
    T = CHUNK
    hd = SB_HEADS * SB_DH
    n_cache = cache_k.shape[2]
    assert n_cache % 8 == 0 and n_cache >= SB_BK
    rb0 = row0 // T
    cache_spec = pl.BlockSpec((1, 1, n_cache, SB_HEADS, SB_DH), lambda b: (layer, b, 0, 0, 0))
    return pl.pallas_call(
        functools.partial(_sb_sample_kernel, n_cache=n_cache),
        grid=(nb,),
        in_specs=[pl.BlockSpec((T, hd), lambda b: (rb0 + b, C_SQ // hd)),
                  pl.BlockSpec((T, hd), lambda b: (rb0 + b, C_SK // hd)),
                  pl.BlockSpec((T, hd), lambda b: (rb0 + b, C_SV // hd)),
                  cache_spec, cache_spec,
                  pl.BlockSpec((SB_BK, SB_BK), lambda b: (0, 0))],
        out_specs=pl.BlockSpec((T, hd), lambda b: (b, 0)),
        out_shape=jax.ShapeDtypeStruct((nb * T, hd), BF16),
        scratch_shapes=[pltpu.VMEM((T, hd), F32)],
        compiler_params=_cparams(("arbitrary",)),
        name="sb_sample",
    )(z_main, z_kv, z_kv, cache_k, cache_v, umat)


def _kv_out_kernel(*refs, depth, nb, nj, rt, n_front):
    z_refs = refs[:depth]
    k_hbm, v_hbm, kbuf, vbuf, sem = refs[depth:]
    l = pl.program_id(0)
    b = pl.program_id(1)
    j = pl.program_id(2)
    x = z_refs[0][...]
    for i in range(1, depth):
        x = jnp.where(l == i, z_refs[i][...], x)
    hd = SB_HEADS * SB_DH
    kbuf[...] = x[:, C_SK:C_SK + hd].reshape(rt, SB_HEADS, SB_DH)
    vbuf[...] = x[:, C_SV:C_SV + hd].reshape(rt, SB_HEADS, SB_DH)

    def copies(src_lo, n, dst_lo):
        return [pltpu.make_async_copy(buf.at[pl.ds(src_lo, n)], out.at[l, b, pl.ds(dst_lo, n)], sem.at[s])
                for s, (buf, out) in enumerate(((kbuf, k_hbm), (vbuf, v_hbm)))]

    @pl.when(j == 0)
    def _():
        cps = copies(n_front, rt - n_front, 0)
        for cp in cps:
            cp.start()
        for cp in cps:
            cp.wait()

    @pl.when(j > 0)
    def _():
        cps = copies(0, rt, j * rt - n_front)
        for cp in cps:
            cp.start()
        for cp in cps:
            cp.wait()


def kv_state_outputs(z_kvs, *, nb, tp, n_front):
    depth = len(z_kvs)
    rt = _pick(tp, (640, 512, 384, 256, 128))
    assert rt > n_front
    nj = tp // rt
    w = z_kvs[0].shape[1]
    last = nb * nj - 1

    def zspec(i):
        return pl.BlockSpec((rt, w), lambda l, b, j: (jnp.where(l == i, b * nj + j, jnp.where(l < i, 0, last)), 0))

    shape = jax.ShapeDtypeStruct((depth, nb, tp - n_front, SB_HEADS, SB_DH), F32)
    return pl.pallas_call(
        functools.partial(_kv_out_kernel, depth=depth, nb=nb, nj=nj, rt=rt, n_front=n_front),
        grid=(depth, nb, nj),
        in_specs=[zspec(i) for i in range(depth)],
        out_specs=[pl.BlockSpec(memory_space=pl.ANY), pl.BlockSpec(memory_space=pl.ANY)],
        out_shape=[shape, shape],
        scratch_shapes=[pltpu.VMEM((rt, SB_HEADS, SB_DH), F32), pltpu.VMEM((rt, SB_HEADS, SB_DH), F32),
                        pltpu.SemaphoreType.DMA((2,))],
        compiler_params=_cparams(("arbitrary", "arbitrary", "arbitrary")),
        name="kv_state_out",
    )(*z_kvs)


def _merge_kernel(agp_ref, amp_ref, asp_ref, ags_ref, ams_ref, ass_ref, g0_ref, g1_ref, g2_ref, x_ref,
                  wg_ref, wm_ref, ws_ref, wo_ref, lg_ref, lb_ref, o_ref, *, alpha, np_tiles):
    is_p = pl.program_id(0) < np_tiles
    ag = jnp.where(is_p, agp_ref[...], ags_ref[...])
    am = jnp.where(is_p, amp_ref[...], ams_ref[...])
    asb = jnp.where(is_p, asp_ref[...], ass_ref[...])
    merged = (_sigmoid(g0_ref[...]) * _dot(ag, wg_ref[...])
              + _sigmoid(g1_ref[...]) * _dot(am, wm_ref[...])
              + _sigmoid(g2_ref[...]) * _dot(asb, ws_ref[...]))
    out = _dot(merged.astype(BF16), wo_ref[...])
    o_ref[...] = _ln_math(alpha * x_ref[...] + out, lg_ref[...], lb_ref[...])


def merge_ln1(acts_p, acts_s, z_main, x, wg, wm, ws, wo, lg, lb, alpha):
    n, d = x.shape
    n_p = acts_p[0].shape[0]
    n_s = acts_s[0].shape[0]
    tm = _pick(np.gcd(n_p, n_s), (256, 128, 64))
    assert n_p + n_s == n
    np_tiles = n_p // tm
    row = lambda i: (i, 0)
    row_p = lambda i: (jnp.minimum(i, np_tiles - 1), 0)
    row_s = lambda i: (jnp.maximum(i - np_tiles, 0), 0)
    const = lambda i: (0, 0)
    gcb = C_GATE // d
    return pl.pallas_call(
        functools.partial(_merge_kernel, alpha=alpha, np_tiles=np_tiles),
        grid=(n // tm,),
        in_specs=[pl.BlockSpec((tm, d), row_p), pl.BlockSpec((tm, d), row_p), pl.BlockSpec((tm, d), row_p),
                  pl.BlockSpec((tm, d), row_s), pl.BlockSpec((tm, d), row_s), pl.BlockSpec((tm, d), row_s),
                  pl.BlockSpec((tm, d), lambda i: (i, gcb)),
                  pl.BlockSpec((tm, d), lambda i: (i, gcb + 1)),
                  pl.BlockSpec((tm, d), lambda i: (i, gcb + 2)),
                  pl.BlockSpec((tm, d), row),
                  pl.BlockSpec((d, d), const), pl.BlockSpec((d, d), const),
                  pl.BlockSpec((d, d), const), pl.BlockSpec((d, d), const),
                  pl.BlockSpec((1, d), const), pl.BlockSpec((1, d), const)],
        out_specs=pl.BlockSpec((tm, d), row),
        out_shape=jax.ShapeDtypeStruct((n, d), F32),
        compiler_params=_cparams(("arbitrary",)),
        name="merge_ln1",
    )(*acts_p, *acts_s, z_main, z_main, z_main, x, wg, wm, ws, wo, lg.reshape(1, d), lb.reshape(1, d))


R_ROWS = 8 + N_EXPERTS


def _router_kernel(x_ref, w_ref, b_ref, u_ref, idx_ref, gate_ref, cnt_ref):
    i = pl.program_id(0)
    tm = x_ref.shape[0]

    @pl.when(i == 0)
    def _():
        cnt_ref[...] = jnp.zeros_like(cnt_ref)

    logits = _dot_nt(w_ref[...].astype(BF16), x_ref[...].astype(BF16)) + b_ref[...]

    g = [logits[j:j + 1, :] for j in range(N_GROUPS)]
    gmax = jnp.maximum(jnp.maximum(g[0], g[1]), jnp.maximum(g[2], g[3]))
    gsel = jnp.where(g[0] == gmax, 0, jnp.where(g[1] == gmax, 1, jnp.where(g[2] == gmax, 2, 3)))
    gden = (jnp.exp(g[0] - gmax) + jnp.exp(g[1] - gmax)) + (jnp.exp(g[2] - gmax) + jnp.exp(g[3] - gmax))
    gprob = 1.0 / gden
    e_in = jnp.where(gsel == 0, logits[8:16, :],
                     jnp.where(gsel == 1, logits[16:24, :],
                               jnp.where(gsel == 2, logits[24:32, :], logits[32:40, :])))
    ridx = lax.broadcasted_iota(jnp.int32, (EXPERTS_PER_GROUP, tm), 0)
    v1 = jnp.max(e_in, axis=0, keepdims=True)
    i1 = jnp.min(jnp.where(e_in == v1, ridx, EXPERTS_PER_GROUP), axis=0, keepdims=True)
    e2 = jnp.where(ridx == i1, -jnp.inf, e_in)
    v2 = jnp.max(e2, axis=0, keepdims=True)
    i2 = jnp.min(jnp.where(e2 == v2, ridx, EXPERTS_PER_GROUP), axis=0, keepdims=True)
    t = jnp.exp(v2 - v1)
    p1 = 1.0 / (1.0 + t)
    gate_ref[0:1, :] = p1 * gprob
    gate_ref[1:2, :] = (t * p1) * gprob
    gate_ref[2:8, :] = jnp.zeros((6, tm), F32)
    eid1 = gsel * EXPERTS_PER_GROUP + i1
    eid2 = gsel * EXPERTS_PER_GROUP + i2

    eidx = lax.broadcasted_iota(jnp.int32, (N_EXPERTS, tm), 0)
    oh1 = eidx == eid1
    oh2 = eidx == eid2
    oh1f = jnp.where(oh1, 1.0, 0.0)
    oh2f = jnp.where(oh2, 1.0, 0.0)
    umat = u_ref[...]
    pre1 = _dot(oh1f.astype(BF16), umat)
    pre2 = _dot(oh2f.astype(BF16), umat)
    cnt1 = jnp.sum(oh1f, axis=1, keepdims=True)
    cnt2 = jnp.sum(oh2f, axis=1, keepdims=True)
    base = cnt_ref[:, 0:1]
    rank1 = jnp.sum(jnp.where(oh1, base + pre1, 0.0), axis=0, keepdims=True)
    rank2 = jnp.sum(jnp.where(oh2, base + cnt1 + pre2, 0.0), axis=0, keepdims=True)
    idx_ref[0:1, :] = eid1
    idx_ref[1:2, :] = eid2
    idx_ref[2:3, :] = rank1.astype(jnp.int32)
    idx_ref[3:4, :] = rank2.astype(jnp.int32)
    idx_ref[4:8, :] = jnp.zeros((4, tm), jnp.int32)
    cnt_ref[...] = cnt_ref[...] + (cnt1 + cnt2)


def moe_router(x, w_r, b_r, umat):
    n, d = x.shape
    tm = umat.shape[0]
    return pl.pallas_call(
        _router_kernel,
        grid=(n // tm,),
        in_specs=[pl.BlockSpec((tm, d), lambda i: (i, 0)),
                  pl.BlockSpec((R_ROWS, d), lambda i: (0, 0)),
                  pl.BlockSpec((R_ROWS, 1), lambda i: (0, 0)),
                  pl.BlockSpec((tm, tm), lambda i: (0, 0))],
        out_specs=[pl.BlockSpec((8, tm), lambda i: (0, i)),
                   pl.BlockSpec((8, tm), lambda i: (0, i)),
                   pl.BlockSpec((N_EXPERTS, 128), lambda i: (0, 0))],
        out_shape=[jax.ShapeDtypeStruct((8, n), jnp.int32),
                   jax.ShapeDtypeStruct((8, n), F32),
                   jax.ShapeDtypeStruct((N_EXPERTS, 128), F32)],
        compiler_params=_cparams(("arbitrary",)),
        name="moe_router",
    )(x, w_r, b_r, umat)


def _row_copy(src_hbm, dst_vmem, src_row, dst_row, sem):
    return pltpu.make_async_copy(src_hbm.at[pl.ds(src_row, 1), :], dst_vmem.at[pl.ds(dst_row, 1), :], sem)


GATHER_UNROLL = 8


def _start_row_gather(idx_ref, base, src_hbm, dst_vmem, nrows, sem):
    def start(r, carry):
        _row_copy(src_hbm, dst_vmem, idx_ref[base + r], r, sem).start()
        return carry

    lax.fori_loop(0, nrows, start, 0, unroll=GATHER_UNROLL)


def _wait_row_gather(src_hbm, dst_vmem, nrows, sem):
    def wait(r, carry):
        _row_copy(src_hbm, dst_vmem, 0, r, sem).wait()
        return carry

    lax.fori_loop(0, nrows, wait, 0, unroll=GATHER_UNROLL)


def _moe_kernel(be_ref, tok_ref, nblk_ref, x_hbm, wg_ref, wu_ref, wd_ref, o_ref,
                xbuf, wgb, wub, wdb, sem):
    i = pl.program_id(0)
    nblk = nblk_ref[0]
    slot = i % 2

    @pl.when((i == 0) & (nblk > 0))
    def _():
        _start_row_gather(tok_ref, 0, x_hbm, xbuf.at[0], MOE_BLK, sem.at[0])

    @pl.when(i + 1 < nblk)
    def _():
        _start_row_gather(tok_ref, (i + 1) * MOE_BLK, x_hbm, xbuf.at[1 - slot], MOE_BLK, sem.at[1 - slot])

    prev_e = be_ref[jnp.maximum(i - 1, 0)]

    @pl.when((i == 0) | (be_ref[i] != prev_e))
    def _():
        wgb[...] = wg_ref[0, 0].astype(BF16)
        wub[...] = wu_ref[0, 0].astype(BF16)
        wdb[...] = wd_ref[0, 0].astype(BF16)

    @pl.when(i < nblk)
    def _():
        _wait_row_gather(x_hbm, xbuf.at[slot], MOE_BLK, sem.at[slot])
        xb = xbuf[slot].astype(BF16)
        a = _dot(xb, wgb[...])
        hmid = (a * _sigmoid(a)) * _dot(xb, wub[...])
        o_ref[...] = _dot(hmid.astype(BF16), wdb[...])

    @pl.when(i >= nblk)
    def _():
        o_ref[...] = jnp.zeros_like(o_ref)


def moe_experts(x, block_e, tok_pad, nblk, w_gate, w_up, w_down, *, layer):
    n, d = x.shape
    nb = block_e.shape[0]
    wmap = lambda i, be, tok, nblk: (layer, be[i], 0, 0)
    grid_spec = pltpu.PrefetchScalarGridSpec(
        num_scalar_prefetch=3,
        grid=(nb,),
        in_specs=[pl.BlockSpec(memory_space=pl.ANY),
                  pl.BlockSpec((1, 1, d, D_EXPERT), wmap),
                  pl.BlockSpec((1, 1, d, D_EXPERT), wmap),
                  pl.BlockSpec((1, 1, D_EXPERT, d), wmap)],
        out_specs=pl.BlockSpec((MOE_BLK, d), lambda i, be, tok, nblk: (i, 0)),
        scratch_shapes=[pltpu.VMEM((2, MOE_BLK, d), F32),
                        pltpu.VMEM((d, D_EXPERT), BF16),
                        pltpu.VMEM((d, D_EXPERT), BF16),
                        pltpu.VMEM((D_EXPERT, d), BF16),
                        pltpu.SemaphoreType.DMA((2,))],
    )
    return pl.pallas_call(
        _moe_kernel,
        grid_spec=grid_spec,
        out_shape=jax.ShapeDtypeStruct((nb * MOE_BLK, d), F32),
        compiler_params=_cparams(("arbitrary",)),
        name="moe_experts",
    )(block_e, tok_pad, nblk, x, w_gate, w_up, w_down)


def _combine_kernel(dest_ref, yb_hbm, x_ref, gate_ref, lg_ref, lb_ref, o_ref, ybuf, sem, *, alpha, n):
    i = pl.program_id(0)
    nt = pl.num_programs(0)
    tm = x_ref.shape[0]
    slot = i % 2

    def start(tile, s):
        for k in range(TOP_K):
            _start_row_gather(dest_ref, k * n + tile * tm, yb_hbm, ybuf.at[s, k], tm, sem.at[s, k])

    @pl.when(i == 0)
    def _():
        start(0, 0)

    @pl.when(i + 1 < nt)
    def _():
        start(i + 1, 1 - slot)

    for k in range(TOP_K):
        _wait_row_gather(yb_hbm, ybuf.at[slot, k], tm, sem.at[slot, k])
    gate = gate_ref[...]
    y = ybuf[slot, 0] * gate[:, 0:1] + ybuf[slot, 1] * gate[:, 1:2]
    o_ref[...] = _ln_math(alpha * x_ref[...] + y, lg_ref[...], lb_ref[...])


def moe_combine_ln2(x, yb, dest, gate_cols, lg, lb, alpha):
    n, d = x.shape
    tm = _pick(n, (256, 128, 64))
    grid_spec = pltpu.PrefetchScalarGridSpec(
        num_scalar_prefetch=1,
        grid=(n // tm,),
        in_specs=[pl.BlockSpec(memory_space=pl.ANY),
                  pl.BlockSpec((tm, d), lambda i, dest: (i, 0)),
                  pl.BlockSpec((tm, TOP_K), lambda i, dest: (i, 0)),
                  pl.BlockSpec((1, d), lambda i, dest: (0, 0)),
                  pl.BlockSpec((1, d), lambda i, dest: (0, 0))],
        out_specs=pl.BlockSpec((tm, d), lambda i, dest: (i, 0)),
        scratch_shapes=[pltpu.VMEM((2, TOP_K, tm, d), F32), pltpu.SemaphoreType.DMA((2, TOP_K))],
    )
    return pl.pallas_call(
        functools.partial(_combine_kernel, alpha=alpha, n=n),
        grid_spec=grid_spec,
        out_shape=jax.ShapeDtypeStruct((n, d), F32),
        compiler_params=_cparams(("arbitrary",)),
        name="moe_combine_ln2",
    )(dest, yb, x, gate_cols, lg.reshape(1, d), lb.reshape(1, d))


def _tri_lower(n):
    t = np.arange(n)
    return (t[None, :] <= t[:, None]).astype(np.float32)


def _strict_upper(n):
    t = np.arange(n)
    return (t[:, None] < t[None, :]).astype(np.float32)


def kernel(x_prompt, x_sample, state_gla, state_ml_C, state_ml_n, state_ml_m, state_ml_conv, cache_sb_k, cache_sb_v, meta_tokens, ln_in_g, ln_in_b, w_in, gla_w_a2, gla_b_a, gla_norm_g, ml_conv_w, ml_conv_b, ml_wq, ml_wk, ml_wv, ml_b_i, ml_b_f, ml_norm_g, w_br_gla, w_br_ml, w_br_sb, w_out, ln1_g, ln1_b, w_rg, b_rg, w_re, b_re, w_e_gate, w_e_up, w_e_down, ln2_g, ln2_b):
    D = D_MODEL
    B, SEQ, _ = x_prompt.shape
    DB, DS, _ = x_sample.shape
    depth = w_in.shape[0]
    n_cache = cache_sb_k.shape[2]
    assert DS == CHUNK and SEQ % 128 == 0 and x_prompt.shape[2] == D
    alpha = (2.0 * depth) ** 0.25
    TP = PAD_FRONT + N_META + SEQ
    NP = B * TP
    NS = DB * DS
    N = NP + NS
    nchunk_p = TP // CHUNK

    meta = jnp.broadcast_to(meta_tokens.astype(F32)[None], (B, N_META, D))
    xp = jnp.concatenate([jnp.zeros((B, PAD_FRONT, D), F32), meta, x_prompt], axis=1).reshape(NP, D)
    x = jnp.concatenate([xp, x_sample.reshape(NS, D)], axis=0)
    x = layer_norm_rows(x, ln_in_g, ln_in_b)

    lvl = jnp.asarray(_gla_level_matrix(), BF16)
    tri = jnp.asarray(_tri_lower(CHUNK), BF16)
    u_sb = jnp.asarray(_strict_upper(SB_BK).T, BF16)
    tm_r = _pick(N, (256, 128, 64))
    u_r = jnp.asarray(_strict_upper(tm_r), BF16)

    A = N * TOP_K
    NB = -(-(A + N_EXPERTS * (MOE_BLK - 1)) // MOE_BLK)

    outs_p = [[] for _ in range(5)]
    outs_s = [[] for _ in range(7)]
    z_kvs = []
    for l in range(depth):
        wi = w_in[l]
        w_main = jnp.concatenate([wi[:, 0:3072], wi[:, 3088:4112], wi[:, 4120:5144], wi[:, 5144:6168],
                                  wi[:, 8216:]], axis=1).astype(BF16)
        w_kv = wi[:, 6168:8216].astype(BF16)
        w_small = jnp.concatenate([wi[:, 3072:3088], wi[:, 4112:4120],
                                   jnp.zeros((D, 128 - 24), F32)], axis=1).astype(BF16)
        z_main = in_projection(x, w_main, 1024, "in_proj_main")
        z_kv, kvb = in_projection(x, w_kv, 1024, "in_proj_kv", with_bf16=True)
        z_small = in_projection(x, w_small, 128, "in_proj_small")
        z_kvs.append(z_kv)

        wa = jnp.concatenate([gla_w_a2[l], jnp.zeros((128 - GLA_RANK, GLA_HEADS * GLA_DK), F32)], axis=0).astype(BF16)
        ba = gla_b_a[l].reshape(1, -1)
        ng = gla_norm_g[l].reshape(1, -1)
        s0_p = jnp.zeros((B, GLA_HEADS, GLA_DV, GLA_DK), F32)
        s0_s = jnp.swapaxes(state_gla[l].astype(F32), -1, -2)
        a_gla_p, sT_p = gla_branch(z_main, z_small, wa, ba, ng, lvl, s0_p,
                                   row0=0, nb=B, nchunk=nchunk_p, n_pad=PAD_FRONT)
        a_gla_s, sT_s = gla_branch(z_main, z_small, wa, ba, ng, lvl, s0_s,
                                   row0=NP, nb=DB, nchunk=1, n_pad=0)

        gbias = jnp.zeros((1, 128), F32).at[0, L_MI:L_MI + ML_HEADS].set(ml_b_i[l]).at[0, L_MF:L_MF + ML_HEADS].set(ml_b_f[l])
        cw = ml_conv_w[l]
        cb = ml_conv_b[l].reshape(1, -1)
        wq = ml_wq[l].astype(BF16)
        wk = ml_wk[l].astype(BF16)
        wv = ml_wv[l].astype(BF16)
        mng = ml_norm_g[l].reshape(1, -1)
        W = ML_HEADS * ML_DH
        conv0_p = jnp.zeros((B, 8, W), F32)
        conv0_s = jnp.concatenate([jnp.zeros((DB, 8 - (ML_CONV - 1), W), F32), state_ml_conv[l].astype(F32)], axis=1)
        m0_s = jnp.zeros((DB, 1, 128), F32).at[:, 0, L_MF:L_MF + ML_HEADS].set(state_ml_m[l].astype(F32))
        a_ml_p, c_p, n_p, m_p = mlstm_branch(
            z_main, z_small, gbias, cw, cb, wq, wk, wv, mng, tri, conv0_p,
            jnp.zeros((B, ML_HEADS, ML_DH, ML_DH), F32), jnp.zeros((B, ML_HEADS, ML_DH), F32),
            jnp.zeros((B, 1, 128), F32), row0=0, nb=B, nchunk=nchunk_p, n_pad=PAD_FRONT)
        a_ml_s, c_s, n_s, m_s = mlstm_branch(
            z_main, z_small, gbias, cw, cb, wq, wk, wv, mng, tri, conv0_s,
            state_ml_C[l].astype(F32), state_ml_n[l].astype(F32), m0_s,
            row0=NP, nb=DB, nchunk=1, n_pad=0)

        a_sb_p = sb_attention_prompt(z_main, kvb, u_sb, nb=B, tp=TP, bq=128, n_pad=PAD_FRONT)
        a_sb_s = sb_attention_sample(z_main, z_kv, cache_sb_k.astype(F32), cache_sb_v.astype(F32),
                                     u_sb, layer=l, row0=NP, nb=DB)

        x1 = merge_ln1((a_gla_p, a_ml_p, a_sb_p), (a_gla_s, a_ml_s, a_sb_s), z_main, x,
                       w_br_gla[l].astype(BF16), w_br_ml[l].astype(BF16), w_br_sb[l].astype(BF16),
                       w_out[l].astype(BF16), ln1_g[l], ln1_b[l], alpha)

        w_r = jnp.concatenate([w_rg[l].T, jnp.zeros((8 - N_GROUPS, D), F32), w_re[l].T], axis=0)
        b_r = jnp.concatenate([b_rg[l], jnp.zeros((8 - N_GROUPS,), F32), b_re[l]]).reshape(R_ROWS, 1)
        idx, gates, cnt = moe_router(x1, w_r, b_r, u_r)
        counts = cnt[:, 0].astype(jnp.int32)
        padded = (counts + MOE_BLK - 1) // MOE_BLK * MOE_BLK
        ends_pad = jnp.cumsum(padded)
        start_pad = ends_pad - padded
        eids = jnp.arange(N_EXPERTS, dtype=jnp.int32)
        start_of = jnp.sum(jnp.where(idx[0:2, :, None] == eids, start_pad, 0), axis=-1)
        dest = (start_of + idx[2:4]).reshape(-1)
        tok = jnp.tile(jnp.arange(N, dtype=jnp.int32), TOP_K)
        tok_pad = jnp.zeros((NB * MOE_BLK,), jnp.int32).at[dest].set(tok)
        blk_lo = jnp.arange(NB, dtype=jnp.int32) * MOE_BLK
        block_e = jnp.minimum(jnp.sum((ends_pad[None, :] <= blk_lo[:, None]).astype(jnp.int32), axis=-1),
                              N_EXPERTS - 1)
        nblk = (ends_pad[-1:] // MOE_BLK).astype(jnp.int32)
        yb = moe_experts(x1, block_e, tok_pad, nblk, w_e_gate, w_e_up, w_e_down, layer=l)
        x = moe_combine_ln2(x1, yb, dest, gates[0:2].T, ln2_g[l], ln2_b[l], alpha)

        zp = z_main[:NP].reshape(B, TP, Z_MAIN)
        zs3 = z_main[NP:].reshape(DB, DS, Z_MAIN)
        zkv_s = z_kv[NP:].reshape(DB, DS, 2 * D)
        kv_shape = (SB_HEADS, SB_DH)
        outs_p[0].append(jnp.swapaxes(sT_p, -1, -2))
        outs_s[0].append(jnp.swapaxes(sT_s, -1, -2))
        outs_p[1].append(c_p); outs_s[1].append(c_s)
        outs_p[2].append(n_p); outs_s[2].append(n_s)
        outs_p[3].append(m_p[:, 0, L_MF:L_MF + ML_HEADS]); outs_s[3].append(m_s[:, 0, L_MF:L_MF + ML_HEADS])
        outs_p[4].append(zp[:, TP - (ML_CONV - 1):, C_MU:C_MU + W]); outs_s[4].append(zs3[:, DS - (ML_CONV - 1):, C_MU:C_MU + W])
        outs_s[5].append(zkv_s[:, :, C_SK:C_SK + D].reshape(DB, DS, *kv_shape))
        outs_s[6].append(zkv_s[:, :, C_SV:C_SV + D].reshape(DB, DS, *kv_shape))

    sbk_p, sbv_p = kv_state_outputs(z_kvs, nb=B, tp=TP, n_front=PAD_FRONT)
    sp = [jnp.stack(a, axis=0) for a in outs_p]
    ss = [jnp.stack(a, axis=0) for a in outs_s]
    y_prompt = x[:NP].reshape(B, TP, D)[:, PAD_FRONT + N_META:]
    y_sample = x[NP:].reshape(DB, DS, D)
    return (y_prompt, y_sample, sp[0], ss[0], sp[1], ss[1], sp[2], ss[2], sp[3], ss[3],
            sp[4], ss[4], sbk_p, ss[5], sbv_p, ss[6])
```

```python
import functools

import numpy as np
import jax
import jax.numpy as jnp
from jax import lax
from jax.experimental import pallas as pl
from jax.experimental.pallas import tpu as pltpu

F32 = jnp.float32
BF16 = jnp.bfloat16

D_MODEL = 1024
N_META = 16
CHUNK = 64
PAD_FRONT = 240
GLA_HEADS, GLA_DK, GLA_DV, GLA_RANK, GLA_TAU = 4, 128, 256, 16, 16.0
ML_HEADS, ML_DH, ML_CONV = 4, 256, 4
SB_HEADS, SB_DH = 8, 128
N_GROUPS, EXPERTS_PER_GROUP, TOP_K = 4, 8, 2
N_EXPERTS = N_GROUPS * EXPERTS_PER_GROUP
D_EXPERT = D_MODEL // 2
MOE_BLK = 256
LN_EPS = 1e-5
NEG_BIG = -1e30
SB_SKIP = 110.0
SB_BK = 256

C_GQ, C_GK, C_GV, C_GR = 0, 512, 1024, 2048
C_MU, C_MO = 3072, 4096
C_SQ = 5120
C_GATE = 6144
Z_MAIN = 9216
C_SK, C_SV = 0, 1024
SB_HG = 4
L_LR, L_MI, L_MF = 0, 16, 20

VMEM_LIMIT = 48 * 1024 * 1024


def _cparams(sem):
    return pltpu.CompilerParams(dimension_semantics=sem, vmem_limit_bytes=VMEM_LIMIT)


def _pick(n, cands):
    for c in cands:
        if n % c == 0:
            return c
    raise ValueError(f"no tile in {cands} divides {n}")


def _log_sigmoid(x):
    return jnp.minimum(x, 0.0) - jnp.log1p(jnp.exp(-jnp.abs(x)))


def _sigmoid(x):
    return 1.0 / (1.0 + jnp.exp(-x))


def _split2(x):
    hi = x.astype(BF16)
    lo = (x - hi.astype(F32)).astype(BF16)
    return hi, lo


def _split3(x):
    hi = x.astype(BF16)
    r = x - hi.astype(F32)
    mid = r.astype(BF16)
    lo = (r - mid.astype(F32)).astype(BF16)
    return hi, mid, lo


def _dot(a, b):
    return jnp.dot(a, b, preferred_element_type=F32)


def _dot_nt(a, b):
    return lax.dot_general(a, b, (((1,), (1,)), ((), ())), preferred_element_type=F32)


def _dot_tn(a, b):
    return lax.dot_general(a, b, (((0,), (0,)), ((), ())), preferred_element_type=F32)


def _ln_math(x, g, b):
    mu = jnp.mean(x, axis=-1, keepdims=True)
    xc = x - mu
    var = jnp.mean(xc * xc, axis=-1, keepdims=True)
    return xc * lax.rsqrt(var + LN_EPS) * g + b


def _ln_kernel(x_ref, g_ref, b_ref, o_ref, ob_ref):
    y = _ln_math(x_ref[...], g_ref[...], b_ref[...])
    o_ref[...] = y
    ob_ref[...] = y.astype(BF16)


def layer_norm_rows(x, g, b):
    n, d = x.shape
    tm = _pick(n, (512, 256, 128, 64))
    return pl.pallas_call(
        _ln_kernel,
        grid=(n // tm,),
        in_specs=[pl.BlockSpec((tm, d), lambda i: (i, 0)),
                  pl.BlockSpec((1, d), lambda i: (0, 0)),
                  pl.BlockSpec((1, d), lambda i: (0, 0))],
        out_specs=[pl.BlockSpec((tm, d), lambda i: (i, 0)), pl.BlockSpec((tm, d), lambda i: (i, 0))],
        out_shape=[jax.ShapeDtypeStruct((n, d), F32), jax.ShapeDtypeStruct((n, d), BF16)],
        compiler_params=_cparams(("arbitrary",)),
        name="ln_in",
    )(x, g.reshape(1, d), b.reshape(1, d))


def _mm_kernel(x_ref, w_ref, o_ref):
    o_ref[...] = _dot(x_ref[...], w_ref[...])


def _mm2_kernel(x_ref, w_ref, o_ref, ob_ref):
    z = _dot(x_ref[...], w_ref[...])
    o_ref[...] = z
    ob_ref[...] = z.astype(BF16)


def in_projection(x, w_bf16, tn, name, with_bf16=False):
    n, d = x.shape
    cols = w_bf16.shape[1]
    tm = _pick(n, (896, 768, 512, 384, 256, 128, 64))
    out_spec = pl.BlockSpec((tm, tn), lambda j, i: (i, j))
    return pl.pallas_call(
        _mm2_kernel if with_bf16 else _mm_kernel,
        grid=(cols // tn, n // tm),
        in_specs=[pl.BlockSpec((tm, d), lambda j, i: (i, 0)),
                  pl.BlockSpec((d, tn), lambda j, i: (0, j))],
        out_specs=[out_spec, out_spec] if with_bf16 else out_spec,
        out_shape=([jax.ShapeDtypeStruct((n, cols), F32), jax.ShapeDtypeStruct((n, cols), BF16)]
                   if with_bf16 else jax.ShapeDtypeStruct((n, cols), F32)),
        compiler_params=_cparams(("arbitrary", "arbitrary")),
        name=name,
    )(x, w_bf16)


GLA_LEVELS = (32, 16, 8, 4, 2, 1)


def _gla_level_matrix():
    L = CHUNK
    t = np.arange(L)[:, None]
    j = np.arange(L)[None, :]
    blocks = []
    for m in (L,) + GLA_LEVELS:
        blocks.append(((j <= t) & (j // m == t // m)).astype(np.float32))
    for m in (L,) + GLA_LEVELS:
        blocks.append(((j > t) & (j // m == t // m)).astype(np.float32))
    return np.concatenate(blocks, axis=0)


def _gla_kernel(q_ref, k_ref, v_ref, r_ref, lr_ref, wa_ref, ba_ref, ng_ref, lvl_ref, s0_ref,
                o_ref, s_ref, *, n_pad):
    c = pl.program_id(1)
    L = CHUNK

    @pl.when(c == 0)
    def _():
        s_ref[...] = s0_ref[...]

    row = c * L + lax.broadcasted_iota(jnp.int32, (L, 1), 0)
    valid = row >= n_pad
    x = _dot(lr_ref[...].astype(BF16), wa_ref[...]) + ba_ref[...]
    la = jnp.where(valid, _log_sigmoid(x) * (1.0 / GLA_TAU), 0.0)
    la_hi, la_lo = _split2(la)
    lvl = lvl_ref[...]
    ex = jnp.exp(_dot(lvl, la_hi) + _dot(lvl, la_lo))
    nlev = len(GLA_LEVELS) + 1

    ti = lax.broadcasted_iota(jnp.int32, (L, L), 0)
    si = lax.broadcasted_iota(jnp.int32, (L, L), 1)
    level_masks = []
    for m in GLA_LEVELS:
        sh = m.bit_length() - 1
        tb = lax.shift_right_logical(ti, sh)
        sb = lax.shift_right_logical(si, sh)
        level_masks.append((lax.shift_right_logical(tb, 1) == lax.shift_right_logical(sb, 1))
                           & ((tb & 1) == 1) & ((sb & 1) == 0))
    eye = ti == si

    outs = []
    states = []
    for h in range(GLA_HEADS):
        dk = slice(h * GLA_DK, (h + 1) * GLA_DK)
        dv = slice(h * GLA_DV, (h + 1) * GLA_DV)
        q = q_ref[:, dk] * (GLA_DK ** -0.5)
        k = jnp.where(valid, k_ref[:, dk], 0.0)
        v = jnp.where(valid, v_ref[:, dv], 0.0).astype(BF16)
        att = jnp.where(eye, _dot_nt(q.astype(BF16), k.astype(BF16)), 0.0)
        for lev in range(len(GLA_LEVELS)):
            qe = ex[(1 + lev) * L:(2 + lev) * L, dk]
            ke = ex[(nlev + 1 + lev) * L:(nlev + 2 + lev) * L, dk]
            a = _dot_nt((q * qe).astype(BF16), (k * ke).astype(BF16))
            att = jnp.where(level_masks[lev], a, att)
        eb = ex[0:L, dk]
        st = s_ref[0, h]
        o = _dot(att.astype(BF16), v) + _dot_nt((q * eb).astype(BF16), st.astype(BF16))
        ke_end = ex[nlev * L:(nlev + 1) * L, dk]
        states.append(st * eb[L - 1:L, :] + _dot_tn(v, (k * ke_end).astype(BF16)))
        mu = jnp.mean(o, axis=-1, keepdims=True)
        oc = o - mu
        var = jnp.mean(oc * oc, axis=-1, keepdims=True)
        on = oc * lax.rsqrt(var + LN_EPS) * ng_ref[...]
        r = r_ref[:, dv]
        outs.append((on * (r * _sigmoid(r))).astype(BF16))
    o_ref[...] = jnp.concatenate(outs, axis=-1)
    s_ref[0] = jnp.stack(states, axis=0)


def gla_branch(z_main, z_small, wa, ba, ng, lvl, s0t, *, row0, nb, nchunk, n_pad):
    L = CHUNK
    rb0 = row0 // L
    hk = GLA_HEADS * GLA_DK
    hv = GLA_HEADS * GLA_DV

    def rows(colblk):
        return lambda b, c: (rb0 + b * nchunk + c, colblk)

    return pl.pallas_call(
        functools.partial(_gla_kernel, n_pad=n_pad),
        grid=(nb, nchunk),
        in_specs=[pl.BlockSpec((L, hk), rows(C_GQ // hk)),
                  pl.BlockSpec((L, hk), rows(C_GK // hk)),
                  pl.BlockSpec((L, hv), rows(C_GV // hv)),
                  pl.BlockSpec((L, hv), rows(C_GR // hv)),
                  pl.BlockSpec((L, 128), rows(0)),
                  pl.BlockSpec((128, hk), lambda b, c: (0, 0)),
                  pl.BlockSpec((1, hk), lambda b, c: (0, 0)),
                  pl.BlockSpec((1, GLA_DV), lambda b, c: (0, 0)),
                  pl.BlockSpec(lvl.shape, lambda b, c: (0, 0)),
                  pl.BlockSpec((1, GLA_HEADS, GLA_DV, GLA_DK), lambda b, c: (b, 0, 0, 0))],
        out_specs=[pl.BlockSpec((L, hv), lambda b, c: (b * nchunk + c, 0)),
                   pl.BlockSpec((1, GLA_HEADS, GLA_DV, GLA_DK), lambda b, c: (b, 0, 0, 0))],
        out_shape=[jax.ShapeDtypeStruct((nb * nchunk * L, hv), BF16),
                   jax.ShapeDtypeStruct((nb, GLA_HEADS, GLA_DV, GLA_DK), F32)],
        compiler_params=_cparams(("arbitrary", "arbitrary")),
        name="gla",
    )(z_main, z_main, z_main, z_main, z_small, wa, ba, ng, lvl, s0t)


def _mlstm_kernel(u_ref, mo_ref, g_ref, gb_ref, cw_ref, cb_ref, wq_ref, wk_ref, wv_ref, ng_ref, tri_ref,
                  conv0_ref, c0_ref, n0_ref, m0_ref,
                  o_ref, c_ref, n_ref, m_ref, ext_ref, *, n_pad):
    c = pl.program_id(1)
    L = CHUNK
    W = ML_HEADS * ML_DH

    @pl.when(c == 0)
    def _():
        ext_ref[0:8, :] = conv0_ref[0]
        c_ref[...] = c0_ref[...]
        n_ref[...] = n0_ref[...]
        m_ref[...] = m0_ref[...]

    row = c * L + lax.broadcasted_iota(jnp.int32, (L, 1), 0)
    valid = row >= n_pad

    u = jnp.where(valid, u_ref[...], 0.0)
    ext_ref[8:8 + L, :] = u
    uc = cb_ref[...]
    for i in range(ML_CONV):
        uc = uc + ext_ref[8 - (ML_CONV - 1) + i:8 - (ML_CONV - 1) + i + L, :] * cw_ref[i:i + 1, :]
    ext_ref[0:8, :] = u[L - 8:L, :]
    ucs = (uc * _sigmoid(uc)).astype(BF16)
    ub = u_ref[...].astype(BF16)

    g = g_ref[...] + gb_ref[...]
    li = jnp.where(valid, g, NEG_BIG)
    lf = jnp.where(valid, _log_sigmoid(g), 0.0)
    tri = tri_ref[...]
    lf_hi, lf_lo = _split2(lf)
    b_col = _dot(tri, lf_hi) + _dot(tri, lf_lo)
    a_col = b_col + m_ref[0]
    li_row = li.T
    b_row = b_col.T

    ti = lax.broadcasted_iota(jnp.int32, (L, L), 0)
    si = lax.broadcasted_iota(jnp.int32, (L, L), 1)
    causal = si <= ti
    lane = lax.broadcasted_iota(jnp.int32, (1, 128), 1)
    m_vec = m_ref[0]
    m_out = m_vec
    outs, c_new, n_new = [], [], []

    for h in range(ML_HEADS):
        d = slice(h * ML_DH, (h + 1) * ML_DH)
        q = _dot(ucs[:, d], wq_ref[h]).astype(BF16)
        k = _dot(ucs[:, d], wk_ref[h]) * (ML_DH ** -0.5)
        v = _dot(ub[:, d], wv_ref[h]).astype(BF16)
        lf_l = L_MF + h
        li_l = L_MI + h
        bc = b_col[:, lf_l:lf_l + 1]
        ac = a_col[:, lf_l:lf_l + 1]
        lic = li[:, li_l:li_l + 1]
        dmat = jnp.where(causal, bc - b_row[lf_l:lf_l + 1, :] + li_row[li_l:li_l + 1, :], NEG_BIG)
        mt = jnp.maximum(ac, jnp.max(dmat, axis=-1, keepdims=True))
        w_intra = jnp.exp(dmat - mt)
        w_inter = jnp.exp(ac - mt)
        qk = _dot_nt(q, k.astype(BF16)) * w_intra
        cmat = c_ref[0, h]
        nrow = n_ref[0, h:h + 1, :]
        num = _dot(qk.astype(BF16), v) + w_inter * _dot(q, cmat.astype(BF16))
        qn = jnp.sum(q.astype(F32) * nrow.astype(BF16).astype(F32), axis=-1, keepdims=True)
        den = jnp.sum(qk, axis=-1, keepdims=True) + w_inter * qn
        hh = num / jnp.maximum(jnp.abs(den), jnp.exp(-mt))
        m_new = mt[L - 1:L, :]
        b_end = bc[L - 1:L, :]
        wk = jnp.exp(b_end - bc + lic - m_new)
        ws = jnp.exp(b_end + m_vec[:, lf_l:lf_l + 1] - m_new)
        kw = k * wk
        c_new.append(ws * cmat + _dot_tn(kw.astype(BF16), v))
        n_new.append(ws * nrow + jnp.sum(kw, axis=0, keepdims=True))
        m_out = jnp.where(lane == lf_l, m_new, m_out)
        mu = jnp.mean(hh, axis=-1, keepdims=True)
        hc = hh - mu
        var = jnp.mean(hc * hc, axis=-1, keepdims=True)
        hn = hc * lax.rsqrt(var + LN_EPS) * ng_ref[...]
        outs.append((hn * _sigmoid(mo_ref[:, d])).astype(BF16))
    o_ref[...] = jnp.concatenate(outs, axis=-1)
    c_ref[0] = jnp.stack(c_new, axis=0)
    n_ref[0] = jnp.concatenate(n_new, axis=0)
    m_ref[0] = m_out


def mlstm_branch(z_main, z_small, gbias, cw, cb, wq, wk, wv, ng, tri, conv0, c0, n0, m0,
                 *, row0, nb, nchunk, n_pad):
    L = CHUNK
    rb0 = row0 // L
    W = ML_HEADS * ML_DH

    def rows(colblk):
        return lambda b, c: (rb0 + b * nchunk + c, colblk)

    const2 = lambda b, c: (0, 0)
    const3 = lambda b, c: (0, 0, 0)
    return pl.pallas_call(
        functools.partial(_mlstm_kernel, n_pad=n_pad),
        grid=(nb, nchunk),
        in_specs=[pl.BlockSpec((L, W), rows(C_MU // W)),
                  pl.BlockSpec((L, W), rows(C_MO // W)),
                  pl.BlockSpec((L, 128), rows(0)),
                  pl.BlockSpec((1, 128), const2),
                  pl.BlockSpec((ML_CONV, W), const2),
                  pl.BlockSpec((1, W), const2),
                  pl.BlockSpec((ML_HEADS, ML_DH, ML_DH), const3),
                  pl.BlockSpec((ML_HEADS, ML_DH, ML_DH), const3),
                  pl.BlockSpec((ML_HEADS, ML_DH, ML_DH), const3),
                  pl.BlockSpec((1, ML_DH), const2),
                  pl.BlockSpec((L, L), const2),
                  pl.BlockSpec((1, 8, W), lambda b, c: (b, 0, 0)),
                  pl.BlockSpec((1, ML_HEADS, ML_DH, ML_DH), lambda b, c: (b, 0, 0, 0)),
                  pl.BlockSpec((1, ML_HEADS, ML_DH), lambda b, c: (b, 0, 0)),
                  pl.BlockSpec((1, 1, 128), lambda b, c: (b, 0, 0))],
        out_specs=[pl.BlockSpec((L, W), lambda b, c: (b * nchunk + c, 0)),
                   pl.BlockSpec((1, ML_HEADS, ML_DH, ML_DH), lambda b, c: (b, 0, 0, 0)),
                   pl.BlockSpec((1, ML_HEADS, ML_DH), lambda b, c: (b, 0, 0)),
                   pl.BlockSpec((1, 1, 128), lambda b, c: (b, 0, 0))],
        out_shape=[jax.ShapeDtypeStruct((nb * nchunk * L, W), BF16),
                   jax.ShapeDtypeStruct((nb, ML_HEADS, ML_DH, ML_DH), F32),
                   jax.ShapeDtypeStruct((nb, ML_HEADS, ML_DH), F32),
                   jax.ShapeDtypeStruct((nb, 1, 128), F32)],
        scratch_shapes=[pltpu.VMEM((8 + L, W), F32)],
        compiler_params=_cparams(("arbitrary", "arbitrary")),
        name="mlstm",
    )(z_main, z_main, z_small, gbias, cw, cb, wq, wk, wv, ng, tri, conv0, c0, n0, m0)


def _sb_heads(qs, ks, vs, vis, cs, umat):
    nh = len(qs)
    bq = qs[0].shape[0]
    lss, l1s, parts = [], [], []
    for h in range(nh):
        z = _dot_nt(qs[h], ks[h])
        t = jnp.log1p(jnp.exp(-jnp.abs(z)))
        lss.append(jnp.minimum(z, 0.0) - t)
        l1 = jnp.minimum(-z, 0.0) - t
        if vis is not None:
            l1 = jnp.where(vis, l1, 0.0)
        l1s.append(l1)
        parts.extend(_split2(l1))
    suf = _dot(jnp.concatenate(parts, axis=0), umat)
    pvs, new_cs = [], []
    for h in range(nh):
        suffix = suf[2 * h * bq:(2 * h + 1) * bq] + suf[(2 * h + 1) * bq:(2 * h + 2) * bq]
        att = jnp.exp(lss[h] + suffix + cs[h])
        if vis is not None:
            att = jnp.where(vis, att, 0.0)
        pvs.append(_dot(att.astype(BF16), vs[h]))
        new_cs.append(cs[h] + jnp.sum(l1s[h], axis=-1, keepdims=True))
    return jnp.concatenate(pvs, axis=-1), new_cs, jnp.max(functools.reduce(jnp.maximum, new_cs))


def _sb_prompt_kernel(q_ref, k_ref, v_ref, u_ref, o_ref, acc_ref, *, bq, n_pad):
    qi = pl.program_id(2)
    bk = SB_BK
    assert bq == bk and n_pad < bk
    qpos = lax.broadcasted_iota(jnp.int32, (bq, 1), 0)
    kio = lax.broadcasted_iota(jnp.int32, (1, bk), 1)
    umat = u_ref[...]
    heads = [slice(h * SB_DH, (h + 1) * SB_DH) for h in range(SB_HG)]
    qs = [(q_ref[:, hs] * (SB_DH ** -0.5)).astype(BF16) for hs in heads]

    def block(j, vis, cs):
        start = pl.multiple_of(j * bk, bk)
        return _sb_heads(qs, [k_ref[pl.ds(start, bk), hs] for hs in heads],
                         [v_ref[pl.ds(start, bk), hs] for hs in heads], vis, cs, umat)

    vis_diag = (kio < qpos) & (kio >= jnp.where(qi > 0, 0, n_pad))
    pv, cs, cmax = block(qi, vis_diag, [jnp.zeros((bq, 1), F32) for _ in heads])
    acc_ref[...] = pv

    def body(carry):
        j, cs, _ = carry
        pv, cs, cmax = block(j, None, list(cs))
        acc_ref[...] += pv
        return j - 1, tuple(cs), cmax

    def cond(carry):
        j, _, cmax = carry
        return (j >= 1) & (cmax > -SB_SKIP)

    j, cs, cmax = lax.while_loop(cond, body, (qi - 1, tuple(cs), cmax))

    @pl.when((j == 0) & (cmax > -SB_SKIP))
    def _():
        pv, _, _ = block(0, jnp.broadcast_to(kio >= n_pad, (bq, bk)), list(cs))
        acc_ref[...] += pv

    o_ref[...] = acc_ref[...].astype(BF16)


def sb_attention_prompt(z_main, kvb, umat, *, nb, tp, bq, n_pad):
    nq = tp // bq
    gw = SB_HG * SB_DH
    ng = SB_HEADS // SB_HG
    return pl.pallas_call(
        functools.partial(_sb_prompt_kernel, bq=bq, n_pad=n_pad),
        grid=(nb, ng, nq),
        in_specs=[pl.BlockSpec((bq, gw), lambda b, g, i: (b * nq + i, C_SQ // gw + g)),
                  pl.BlockSpec((tp, gw), lambda b, g, i: (b, C_SK // gw + g)),
                  pl.BlockSpec((tp, gw), lambda b, g, i: (b, C_SV // gw + g)),
                  pl.BlockSpec((SB_BK, SB_BK), lambda b, g, i: (0, 0))],
        out_specs=pl.BlockSpec((bq, gw), lambda b, g, i: (b * nq + i, g)),
        out_shape=jax.ShapeDtypeStruct((nb * tp, SB_HEADS * SB_DH), BF16),
        scratch_shapes=[pltpu.VMEM((bq, gw), F32)],
        compiler_params=_cparams(("arbitrary", "arbitrary", "arbitrary")),
        name="sb_prompt",
    )(z_main, kvb, kvb, umat)


def _sb_sample_kernel(q_ref, kn_ref, vn_ref, kc_ref, vc_ref, u_ref, o_ref, acc_ref, *, n_cache):
    T = CHUNK
    bk = SB_BK
    nfull = n_cache // bk
    rem = n_cache % bk
    umat = u_ref[...]
    heads = [slice(h * SB_DH, (h + 1) * SB_DH) for h in range(SB_HEADS)]
    qs = [(q_ref[:, hs] * (SB_DH ** -0.5)).astype(BF16) for hs in heads]
    vis_new = lax.broadcasted_iota(jnp.int32, (1, T), 1) < lax.broadcasted_iota(jnp.int32, (T, 1), 0)
    pv, cs, cmax = _sb_heads(qs, [kn_ref[:, hs].astype(BF16) for hs in heads],
                             [vn_ref[:, hs].astype(BF16) for hs in heads], vis_new,
                             [jnp.zeros((T, 1), F32) for _ in heads], umat[0:T, 0:T])
    acc_ref[...] = pv

    def cache_block(start, vis, cs):
        return _sb_heads(qs, [kc_ref[0, 0, pl.ds(start, bk), h, :].astype(BF16) for h in range(SB_HEADS)],
                         [vc_ref[0, 0, pl.ds(start, bk), h, :].astype(BF16) for h in range(SB_HEADS)],
                         vis, cs, umat)

    def body(carry):
        j, cs, _ = carry
        pv, cs, cmax = cache_block(pl.multiple_of(n_cache - bk * (j + 1), 8), None, list(cs))
        acc_ref[...] += pv
        return j + 1, tuple(cs), cmax

    def cond(carry):
        j, _, cmax = carry
        return (j < nfull) & (cmax > -SB_SKIP)

    _, cs, _ = lax.while_loop(cond, body, (jnp.int32(0), tuple(cs), cmax))
    if rem:
        vis_rem = jnp.broadcast_to(lax.broadcasted_iota(jnp.int32, (1, bk), 1) < rem, (T, bk))
        pv, _, _ = cache_block(0, vis_rem, list(cs))
        acc_ref[...] += pv
    o_ref[...] = acc_ref[...].astype(BF16)


def sb_attention_sample(z_main, z_kv, cache_k, cache_v, umat, *, layer, row0, nb):
    T = CHUNK
    hd = SB_HEADS * SB_DH
    n_cache = cache_k.shape[2]
    assert n_cache % 8 == 0 and n_cache >= SB_BK
    rb0 = row0 // T
    cache_spec = pl.BlockSpec((1, 1, n_cache, SB_HEADS, SB_DH), lambda b: (layer, b, 0, 0, 0))
    return pl.pallas_call(
        functools.partial(_sb_sample_kernel, n_cache=n_cache),
        grid=(nb,),
        in_specs=[pl.BlockSpec((T, hd), lambda b: (rb0 + b, C_SQ // hd)),
                  pl.BlockSpec((T, hd), lambda b: (rb0 + b, C_SK // hd)),
                  pl.BlockSpec((T, hd), lambda b: (rb0 + b, C_SV // hd)),
                  cache_spec, cache_spec,
                  pl.BlockSpec((SB_BK, SB_BK), lambda b: (0, 0))],
        out_specs=pl.BlockSpec((T, hd), lambda b: (b, 0)),
        out_shape=jax.ShapeDtypeStruct((nb * T, hd), BF16),
        scratch_shapes=[pltpu.VMEM((T, hd), F32)],
        compiler_params=_cparams(("arbitrary",)),
        name="sb_sample",
    )(z_main, z_kv, z_kv, cache_k, cache_v, umat)


def _kv_out_kernel(*refs, depth, nb, nj, rt, n_front):
    z_refs = refs[:depth]
    k_hbm, v_hbm, kbuf, vbuf, sem = refs[depth:]
    l = pl.program_id(0)
    b = pl.program_id(1)
    j = pl.program_id(2)
    x = z_refs[0][...]
    for i in range(1, depth):
        x = jnp.where(l == i, z_refs[i][...], x)
    hd = SB_HEADS * SB_DH
    kbuf[...] = x[:, C_SK:C_SK + hd].reshape(rt, SB_HEADS, SB_DH)
    vbuf[...] = x[:, C_SV:C_SV + hd].reshape(rt, SB_HEADS, SB_DH)

    def copies(src_lo, n, dst_lo):
        return [pltpu.make_async_copy(buf.at[pl.ds(src_lo, n)], out.at[l, b, pl.ds(dst_lo, n)], sem.at[s])
                for s, (buf, out) in enumerate(((kbuf, k_hbm), (vbuf, v_hbm)))]

    @pl.when(j == 0)
    def _():
        cps = copies(n_front, rt - n_front, 0)
        for cp in cps:
            cp.start()
        for cp in cps:
            cp.wait()

    @pl.when(j > 0)
    def _():
        cps = copies(0, rt, j * rt - n_front)
        for cp in cps:
            cp.start()
        for cp in cps:
            cp.wait()


def kv_state_outputs(z_kvs, *, nb, tp, n_front):
    depth = len(z_kvs)
    rt = _pick(tp, (768, 640, 512, 384, 256))
    assert rt > n_front
    nj = tp // rt
    w = z_kvs[0].shape[1]
    last = nb * nj - 1

    def zspec(i):
        return pl.BlockSpec((rt, w), lambda l, b, j: (jnp.where(l == i, b * nj + j, jnp.where(l < i, 0, last)), 0))

    shape = jax.ShapeDtypeStruct((depth, nb, tp - n_front, SB_HEADS, SB_DH), F32)
    return pl.pallas_call(
        functools.partial(_kv_out_kernel, depth=depth, nb=nb, nj=nj, rt=rt, n_front=n_front),
        grid=(depth, nb, nj),
        in_specs=[zspec(i) for i in range(depth)],
        out_specs=[pl.BlockSpec(memory_space=pl.ANY), pl.BlockSpec(memory_space=pl.ANY)],
        out_shape=[shape, shape],
        scratch_shapes=[pltpu.VMEM((rt, SB_HEADS, SB_DH), F32), pltpu.VMEM((rt, SB_HEADS, SB_DH), F32),
                        pltpu.SemaphoreType.DMA((2,))],
        compiler_params=_cparams(("arbitrary", "arbitrary", "arbitrary")),
        name="kv_state_out",
    )(*z_kvs)


def _merge_kernel(agp_ref, amp_ref, asp_ref, ags_ref, ams_ref, ass_ref, g0_ref, g1_ref, g2_ref, x_ref,
                  wg_ref, wm_ref, ws_ref, wo_ref, lg_ref, lb_ref, o_ref, *, alpha, np_tiles):
    is_p = pl.program_id(0) < np_tiles
    ag = jnp.where(is_p, agp_ref[...], ags_ref[...])
    am = jnp.where(is_p, amp_ref[...], ams_ref[...])
    asb = jnp.where(is_p, asp_ref[...], ass_ref[...])
    merged = (_sigmoid(g0_ref[...]) * _dot(ag, wg_ref[...])
              + _sigmoid(g1_ref[...]) * _dot(am, wm_ref[...])
              + _sigmoid(g2_ref[...]) * _dot(asb, ws_ref[...]))
    out = _dot(merged.astype(BF16), wo_ref[...])
    o_ref[...] = _ln_math(alpha * x_ref[...] + out, lg_ref[...], lb_ref[...])


def merge_ln1(acts_p, acts_s, z_main, x, wg, wm, ws, wo, lg, lb, alpha):
    n, d = x.shape
    n_p = acts_p[0].shape[0]
    n_s = acts_s[0].shape[0]
    tm = _pick(np.gcd(n_p, n_s), (256, 128, 64))
    assert n_p + n_s == n
    np_tiles = n_p // tm
    row = lambda i: (i, 0)
    row_p = lambda i: (jnp.minimum(i, np_tiles - 1), 0)
    row_s = lambda i: (jnp.maximum(i - np_tiles, 0), 0)
    const = lambda i: (0, 0)
    gcb = C_GATE // d
    return pl.pallas_call(
        functools.partial(_merge_kernel, alpha=alpha, np_tiles=np_tiles),
        grid=(n // tm,),
        in_specs=[pl.BlockSpec((tm, d), row_p), pl.BlockSpec((tm, d), row_p), pl.BlockSpec((tm, d), row_p),
                  pl.BlockSpec((tm, d), row_s), pl.BlockSpec((tm, d), row_s), pl.BlockSpec((tm, d), row_s),
                  pl.BlockSpec((tm, d), lambda i: (i, gcb)),
                  pl.BlockSpec((tm, d), lambda i: (i, gcb + 1)),
                  pl.BlockSpec((tm, d), lambda i: (i, gcb + 2)),
                  pl.BlockSpec((tm, d), row),
                  pl.BlockSpec((d, d), const), pl.BlockSpec((d, d), const),
                  pl.BlockSpec((d, d), const), pl.BlockSpec((d, d), const),
                  pl.BlockSpec((1, d), const), pl.BlockSpec((1, d), const)],
        out_specs=pl.BlockSpec((tm, d), row),
        out_shape=jax.ShapeDtypeStruct((n, d), F32),
        compiler_params=_cparams(("arbitrary",)),
        name="merge_ln1",
    )(*acts_p, *acts_s, z_main, z_main, z_main, x, wg, wm, ws, wo, lg.reshape(1, d), lb.reshape(1, d))


R_ROWS = 8 + N_EXPERTS


def _router_kernel(x_ref, w_ref, b_ref, u_ref, idx_ref, gate_ref, cnt_ref):
    i = pl.program_id(0)
    tm = x_ref.shape[0]

    @pl.when(i == 0)
    def _():
        cnt_ref[...] = jnp.zeros_like(cnt_ref)

    logits = _dot_nt(w_ref[...].astype(BF16), x_ref[...].astype(BF16)) + b_ref[...]

    g = [logits[j:j + 1, :] for j in range(N_GROUPS)]
    gmax = jnp.maximum(jnp.maximum(g[0], g[1]), jnp.maximum(g[2], g[3]))
    gsel = jnp.where(g[0] == gmax, 0, jnp.where(g[1] == gmax, 1, jnp.where(g[2] == gmax, 2, 3)))
    gden = (jnp.exp(g[0] - gmax) + jnp.exp(g[1] - gmax)) + (jnp.exp(g[2] - gmax) + jnp.exp(g[3] - gmax))
    gprob = 1.0 / gden
    e_in = jnp.where(gsel == 0, logits[8:16, :],
                     jnp.where(gsel == 1, logits[16:24, :],
                               jnp.where(gsel == 2, logits[24:32, :], logits[32:40, :])))
    ridx = lax.broadcasted_iota(jnp.int32, (EXPERTS_PER_GROUP, tm), 0)
    v1 = jnp.max(e_in, axis=0, keepdims=True)
    i1 = jnp.min(jnp.where(e_in == v1, ridx, EXPERTS_PER_GROUP), axis=0, keepdims=True)
    e2 = jnp.where(ridx == i1, -jnp.inf, e_in)
    v2 = jnp.max(e2, axis=0, keepdims=True)
    i2 = jnp.min(jnp.where(e2 == v2, ridx, EXPERTS_PER_GROUP), axis=0, keepdims=True)
    t = jnp.exp(v2 - v1)
    p1 = 1.0 / (1.0 + t)
    gate_ref[0:1, :] = p1 * gprob
    gate_ref[1:2, :] = (t * p1) * gprob
    gate_ref[2:8, :] = jnp.zeros((6, tm), F32)
    eid1 = gsel * EXPERTS_PER_GROUP + i1
    eid2 = gsel * EXPERTS_PER_GROUP + i2

    eidx = lax.broadcasted_iota(jnp.int32, (N_EXPERTS, tm), 0)
    oh1 = eidx == eid1
    oh2 = eidx == eid2
    oh1f = jnp.where(oh1, 1.0, 0.0)
    oh2f = jnp.where(oh2, 1.0, 0.0)
    umat = u_ref[...]
    pre1 = _dot(oh1f.astype(BF16), umat)
    pre2 = _dot(oh2f.astype(BF16), umat)
    cnt1 = jnp.sum(oh1f, axis=1, keepdims=True)
    cnt2 = jnp.sum(oh2f, axis=1, keepdims=True)
    base = cnt_ref[:, 0:1]
    rank1 = jnp.sum(jnp.where(oh1, base + pre1, 0.0), axis=0, keepdims=True)
    rank2 = jnp.sum(jnp.where(oh2, base + cnt1 + pre2, 0.0), axis=0, keepdims=True)
    idx_ref[0:1, :] = eid1
    idx_ref[1:2, :] = eid2
    idx_ref[2:3, :] = rank1.astype(jnp.int32)
    idx_ref[3:4, :] = rank2.astype(jnp.int32)
    idx_ref[4:8, :] = jnp.zeros((4, tm), jnp.int32)
    cnt_ref[...] = cnt_ref[...] + (cnt1 + cnt2)


def moe_router(x, w_r, b_r, umat):
    n, d = x.shape
    tm = umat.shape[0]
    return pl.pallas_call(
        _router_kernel,
        grid=(n // tm,),
        in_specs=[pl.BlockSpec((tm, d), lambda i: (i, 0)),
                  pl.BlockSpec((R_ROWS, d), lambda i: (0, 0)),
                  pl.BlockSpec((R_ROWS, 1), lambda i: (0, 0)),
                  pl.BlockSpec((tm, tm), lambda i: (0, 0))],
        out_specs=[pl.BlockSpec((8, tm), lambda i: (0, i)),
                   pl.BlockSpec((8, tm), lambda i: (0, i)),
                   pl.BlockSpec((N_EXPERTS, 128), lambda i: (0, 0))],
        out_shape=[jax.ShapeDtypeStruct((8, n), jnp.int32),
                   jax.ShapeDtypeStruct((8, n), F32),
                   jax.ShapeDtypeStruct((N_EXPERTS, 128), F32)],
        compiler_params=_cparams(("arbitrary",)),
        name="moe_router",
    )(x, w_r, b_r, umat)


def _row_copy(src_hbm, dst_vmem, src_row, dst_row, sem):
    return pltpu.make_async_copy(src_hbm.at[pl.ds(src_row, 1), :], dst_vmem.at[pl.ds(dst_row, 1), :], sem)


GATHER_UNROLL = 8


def _start_row_gather(idx_ref, base, src_hbm, dst_vmem, nrows, sem):
    def start(r, carry):
        _row_copy(src_hbm, dst_vmem, idx_ref[base + r], r, sem).start()
        return carry

    lax.fori_loop(0, nrows, start, 0, unroll=GATHER_UNROLL)


def _wait_row_gather(src_hbm, dst_vmem, nrows, sem):
    def wait(r, carry):
        _row_copy(src_hbm, dst_vmem, 0, r, sem).wait()
        return carry

    lax.fori_loop(0, nrows, wait, 0, unroll=GATHER_UNROLL)


def _moe_kernel(be_ref, tok_ref, nblk_ref, x_hbm, wg_ref, wu_ref, wd_ref, o_ref,
                xbuf, wgb, wub, wdb, sem):
    i = pl.program_id(0)
    nblk = nblk_ref[0]
    slot = i % 2

    @pl.when((i == 0) & (nblk > 0))
    def _():
        _start_row_gather(tok_ref, 0, x_hbm, xbuf.at[0], MOE_BLK, sem.at[0])

    @pl.when(i + 1 < nblk)
    def _():
        _start_row_gather(tok_ref, (i + 1) * MOE_BLK, x_hbm, xbuf.at[1 - slot], MOE_BLK, sem.at[1 - slot])

    prev_e = be_ref[jnp.maximum(i - 1, 0)]

    @pl.when((i == 0) | (be_ref[i] != prev_e))
    def _():
        wgb[...] = wg_ref[0, 0].astype(BF16)
        wub[...] = wu_ref[0, 0].astype(BF16)
        wdb[...] = wd_ref[0, 0].astype(BF16)

    @pl.when(i < nblk)
    def _():
        _wait_row_gather(x_hbm, xbuf.at[slot], MOE_BLK, sem.at[slot])
        xb = xbuf[slot].astype(BF16)
        a = _dot(xb, wgb[...])
        hmid = (a * _sigmoid(a)) * _dot(xb, wub[...])
        o_ref[...] = _dot(hmid.astype(BF16), wdb[...])

    @pl.when(i >= nblk)
    def _():
        o_ref[...] = jnp.zeros_like(o_ref)


def moe_experts(x, block_e, tok_pad, nblk, w_gate, w_up, w_down, *, layer):
    n, d = x.shape
    nb = block_e.shape[0]
    wmap = lambda i, be, tok, nblk: (layer, be[i], 0, 0)
    grid_spec = pltpu.PrefetchScalarGridSpec(
        num_scalar_prefetch=3,
        grid=(nb,),
        in_specs=[pl.BlockSpec(memory_space=pl.ANY),
                  pl.BlockSpec((1, 1, d, D_EXPERT), wmap),
                  pl.BlockSpec((1, 1, d, D_EXPERT), wmap),
                  pl.BlockSpec((1, 1, D_EXPERT, d), wmap)],
        out_specs=pl.BlockSpec((MOE_BLK, d), lambda i, be, tok, nblk: (i, 0)),
        scratch_shapes=[pltpu.VMEM((2, MOE_BLK, d), F32),
                        pltpu.VMEM((d, D_EXPERT), BF16),
                        pltpu.VMEM((d, D_EXPERT), BF16),
                        pltpu.VMEM((D_EXPERT, d), BF16),
                        pltpu.SemaphoreType.DMA((2,))],
    )
    return pl.pallas_call(
        _moe_kernel,
        grid_spec=grid_spec,
        out_shape=jax.ShapeDtypeStruct((nb * MOE_BLK, d), F32),
        compiler_params=_cparams(("arbitrary",)),
        name="moe_experts",
    )(block_e, tok_pad, nblk, x, w_gate, w_up, w_down)


def _combine_kernel(dest_ref, yb_hbm, x_ref, gate_ref, lg_ref, lb_ref, o_ref, ob_ref, ybuf, sem, *, alpha, n):
    i = pl.program_id(0)
    nt = pl.num_programs(0)
    tm = x_ref.shape[0]
    slot = i % 2

    def start(tile, s):
        for k in range(TOP_K):
            _start_row_gather(dest_ref, k * n + tile * tm, yb_hbm, ybuf.at[s, k], tm, sem.at[s, k])

    @pl.when(i == 0)
    def _():
        start(0, 0)

    @pl.when(i + 1 < nt)
    def _():
        start(i + 1, 1 - slot)

    for k in range(TOP_K):
        _wait_row_gather(yb_hbm, ybuf.at[slot, k], tm, sem.at[slot, k])
    gate = gate_ref[...]
    y = ybuf[slot, 0] * gate[:, 0:1] + ybuf[slot, 1] * gate[:, 1:2]
    out = _ln_math(alpha * x_ref[...] + y, lg_ref[...], lb_ref[...])
    o_ref[...] = out
    ob_ref[...] = out.astype(BF16)


def moe_combine_ln2(x, yb, dest, gate_cols, lg, lb, alpha):
    n, d = x.shape
    tm = _pick(n, (256, 128, 64))
    grid_spec = pltpu.PrefetchScalarGridSpec(
        num_scalar_prefetch=1,
        grid=(n // tm,),
        in_specs=[pl.BlockSpec(memory_space=pl.ANY),
                  pl.BlockSpec((tm, d), lambda i, dest: (i, 0)),
                  pl.BlockSpec((tm, TOP_K), lambda i, dest: (i, 0)),
                  pl.BlockSpec((1, d), lambda i, dest: (0, 0)),
                  pl.BlockSpec((1, d), lambda i, dest: (0, 0))],
        out_specs=[pl.BlockSpec((tm, d), lambda i, dest: (i, 0)), pl.BlockSpec((tm, d), lambda i, dest: (i, 0))],
        scratch_shapes=[pltpu.VMEM((2, TOP_K, tm, d), F32), pltpu.SemaphoreType.DMA((2, TOP_K))],
    )
    return pl.pallas_call(
        functools.partial(_combine_kernel, alpha=alpha, n=n),
        grid_spec=grid_spec,
        out_shape=[jax.ShapeDtypeStruct((n, d), F32), jax.ShapeDtypeStruct((n, d), BF16)],
        compiler_params=_cparams(("arbitrary",)),
        name="moe_combine_ln2",
    )(dest, yb, x, gate_cols, lg.reshape(1, d), lb.reshape(1, d))


def _tri_lower(n):
    t = np.arange(n)
    return (t[None, :] <= t[:, None]).astype(np.float32)


def _strict_upper(n):
    t = np.arange(n)
    return (t[:, None] < t[None, :]).astype(np.float32)


def kernel(x_prompt, x_sample, state_gla, state_ml_C, state_ml_n, state_ml_m, state_ml_conv, cache_sb_k, cache_sb_v, meta_tokens, ln_in_g, ln_in_b, w_in, gla_w_a2, gla_b_a, gla_norm_g, ml_conv_w, ml_conv_b, ml_wq, ml_wk, ml_wv, ml_b_i, ml_b_f, ml_norm_g, w_br_gla, w_br_ml, w_br_sb, w_out, ln1_g, ln1_b, w_rg, b_rg, w_re, b_re, w_e_gate, w_e_up, w_e_down, ln2_g, ln2_b):
    D = D_MODEL
    B, SEQ, _ = x_prompt.shape
    DB, DS, _ = x_sample.shape
    depth = w_in.shape[0]
    n_cache = cache_sb_k.shape[2]
    assert DS == CHUNK and SEQ % 128 == 0 and x_prompt.shape[2] == D
    alpha = (2.0 * depth) ** 0.25
    TP = PAD_FRONT + N_META + SEQ
    NP = B * TP
    NS = DB * DS
    N = NP + NS
    nchunk_p = TP // CHUNK

    meta = jnp.broadcast_to(meta_tokens.astype(F32)[None], (B, N_META, D))
    xp = jnp.concatenate([jnp.zeros((B, PAD_FRONT, D), F32), meta, x_prompt], axis=1).reshape(NP, D)
    x = jnp.concatenate([xp, x_sample.reshape(NS, D)], axis=0)
    x, xb = layer_norm_rows(x, ln_in_g, ln_in_b)

    lvl = jnp.asarray(_gla_level_matrix(), BF16)
    tri = jnp.asarray(_tri_lower(CHUNK), BF16)
    u_sb = jnp.asarray(_strict_upper(SB_BK).T, BF16)
    tm_r = _pick(N, (256, 128, 64))
    u_r = jnp.asarray(_strict_upper(tm_r), BF16)

    A = N * TOP_K
    NB = -(-(A + N_EXPERTS * (MOE_BLK - 1)) // MOE_BLK)

    outs_p = [[] for _ in range(5)]
    outs_s = [[] for _ in range(7)]
    z_kvs = []
    for l in range(depth):
        wi = w_in[l]
        w_main = jnp.concatenate([wi[:, 0:3072], wi[:, 3088:4112], wi[:, 4120:5144], wi[:, 5144:6168],
                                  wi[:, 8216:]], axis=1).astype(BF16)
        w_kv = wi[:, 6168:8216].astype(BF16)
        w_small = jnp.concatenate([wi[:, 3072:3088], wi[:, 4112:4120],
                                   jnp.zeros((D, 128 - 24), F32)], axis=1).astype(BF16)
        z_main = in_projection(xb, w_main, 1024, "in_proj_main")
        z_kv, kvb = in_projection(xb, w_kv, 1024, "in_proj_kv", with_bf16=True)
        z_small = in_projection(xb, w_small, 128, "in_proj_small")
        z_kvs.append(z_kv)

        wa = jnp.concatenate([gla_w_a2[l], jnp.zeros((128 - GLA_RANK, GLA_HEADS * GLA_DK), F32)], axis=0).astype(BF16)
        ba = gla_b_a[l].reshape(1, -1)
        ng = gla_norm_g[l].reshape(1, -1)
        s0_p = jnp.zeros((B, GLA_HEADS, GLA_DV, GLA_DK), F32)
        s0_s = jnp.swapaxes(state_gla[l].astype(F32), -1, -2)
        a_gla_p, sT_p = gla_branch(z_main, z_small, wa, ba, ng, lvl, s0_p,
                                   row0=0, nb=B, nchunk=nchunk_p, n_pad=PAD_FRONT)
        a_gla_s, sT_s = gla_branch(z_main, z_small, wa, ba, ng, lvl, s0_s,
                                   row0=NP, nb=DB, nchunk=1, n_pad=0)

        gbias = jnp.zeros((1, 128), F32).at[0, L_MI:L_MI + ML_HEADS].set(ml_b_i[l]).at[0, L_MF:L_MF + ML_HEADS].set(ml_b_f[l])
        cw = ml_conv_w[l]
        cb = ml_conv_b[l].reshape(1, -1)
        wq = ml_wq[l].astype(BF16)
        wk = ml_wk[l].astype(BF16)
        wv = ml_wv[l].astype(BF16)
        mng = ml_norm_g[l].reshape(1, -1)
        W = ML_HEADS * ML_DH
        conv0_p = jnp.zeros((B, 8, W), F32)
        conv0_s = jnp.concatenate([jnp.zeros((DB, 8 - (ML_CONV - 1), W), F32), state_ml_conv[l].astype(F32)], axis=1)
        m0_s = jnp.zeros((DB, 1, 128), F32).at[:, 0, L_MF:L_MF + ML_HEADS].set(state_ml_m[l].astype(F32))
        a_ml_p, c_p, n_p, m_p = mlstm_branch(
            z_main, z_small, gbias, cw, cb, wq, wk, wv, mng, tri, conv0_p,
            jnp.zeros((B, ML_HEADS, ML_DH, ML_DH), F32), jnp.zeros((B, ML_HEADS, ML_DH), F32),
            jnp.zeros((B, 1, 128), F32), row0=0, nb=B, nchunk=nchunk_p, n_pad=PAD_FRONT)
        a_ml_s, c_s, n_s, m_s = mlstm_branch(
            z_main, z_small, gbias, cw, cb, wq, wk, wv, mng, tri, conv0_s,
            state_ml_C[l].astype(F32), state_ml_n[l].astype(F32), m0_s,
            row0=NP, nb=DB, nchunk=1, n_pad=0)

        a_sb_p = sb_attention_prompt(z_main, kvb, u_sb, nb=B, tp=TP, bq=SB_BK, n_pad=PAD_FRONT)
        a_sb_s = sb_attention_sample(z_main, z_kv, cache_sb_k.astype(F32), cache_sb_v.astype(F32),
                                     u_sb, layer=l, row0=NP, nb=DB)

        x1 = merge_ln1((a_gla_p, a_ml_p, a_sb_p), (a_gla_s, a_ml_s, a_sb_s), z_main, x,
                       w_br_gla[l].astype(BF16), w_br_ml[l].astype(BF16), w_br_sb[l].astype(BF16),
                       w_out[l].astype(BF16), ln1_g[l], ln1_b[l], alpha)

        w_r = jnp.concatenate([w_rg[l].T, jnp.zeros((8 - N_GROUPS, D), F32), w_re[l].T], axis=0)
        b_r = jnp.concatenate([b_rg[l], jnp.zeros((8 - N_GROUPS,), F32), b_re[l]]).reshape(R_ROWS, 1)
        idx, gates, cnt = moe_router(x1, w_r, b_r, u_r)
        counts = cnt[:, 0].astype(jnp.int32)
        padded = (counts + MOE_BLK - 1) // MOE_BLK * MOE_BLK
        ends_pad = jnp.cumsum(padded)
        start_pad = ends_pad - padded
        eids = jnp.arange(N_EXPERTS, dtype=jnp.int32)
        start_of = jnp.sum(jnp.where(idx[0:2, :, None] == eids, start_pad, 0), axis=-1)
        dest = (start_of + idx[2:4]).reshape(-1)
        tok = jnp.tile(jnp.arange(N, dtype=jnp.int32), TOP_K)
        tok_pad = jnp.zeros((NB * MOE_BLK,), jnp.int32).at[dest].set(tok)
        blk_lo = jnp.arange(NB, dtype=jnp.int32) * MOE_BLK
        block_e = jnp.minimum(jnp.sum((ends_pad[None, :] <= blk_lo[:, None]).astype(jnp.int32), axis=-1),
                              N_EXPERTS - 1)
        nblk = (ends_pad[-1:] // MOE_BLK).astype(jnp.int32)
        yb = moe_experts(x1, block_e, tok_pad, nblk, w_e_gate, w_e_up, w_e_down, layer=l)
        x, xb = moe_combine_ln2(x1, yb, dest, gates[0:2].T, ln2_g[l], ln2_b[l], alpha)

        nct = ML_CONV - 1
        conv_p = jnp.stack([z_main[(b + 1) * TP - nct:(b + 1) * TP, C_MU:C_MU + W] for b in range(B)], axis=0)
        conv_s = z_main[NP:, C_MU:C_MU + W].reshape(DB, DS, W)[:, DS - nct:]
        zkv_s = z_kv[NP:].reshape(DB, DS, 2 * D)
        kv_shape = (SB_HEADS, SB_DH)
        outs_p[0].append(jnp.swapaxes(sT_p, -1, -2))
        outs_s[0].append(jnp.swapaxes(sT_s, -1, -2))
        outs_p[1].append(c_p); outs_s[1].append(c_s)
        outs_p[2].append(n_p); outs_s[2].append(n_s)
        outs_p[3].append(m_p[:, 0, L_MF:L_MF + ML_HEADS]); outs_s[3].append(m_s[:, 0, L_MF:L_MF + ML_HEADS])
        outs_p[4].append(conv_p); outs_s[4].append(conv_s)
        outs_s[5].append(zkv_s[:, :, C_SK:C_SK + D].reshape(DB, DS, *kv_shape))
        outs_s[6].append(zkv_s[:, :, C_SV:C_SV + D].reshape(DB, DS, *kv_shape))

    sbk_p, sbv_p = kv_state_outputs(z_kvs, nb=B, tp=TP, n_front=PAD_FRONT)
    sp = [jnp.stack(a, axis=0) for a in outs_p]
    ss = [jnp.stack(a, axis=0) for a in outs_s]
    y_prompt = x[:NP].reshape(B, TP, D)[:, PAD_FRONT + N_META:]
    y_sample = x[NP:].reshape(DB, DS, D)
    return (y_prompt, y_sample, sp[0], ss[0], sp[1], ss[1], sp[2], ss[2], sp[3], ss[3],
            sp[4], ss[4], sbk_p, ss[5], sbv_p, ss[6])
```

```python
import functools

import numpy as np
import jax
import jax.numpy as jnp
from jax import lax
from jax.experimental import pallas as pl
from jax.experimental.pallas import tpu as pltpu

F32 = jnp.float32
BF16 = jnp.bfloat16

D_MODEL = 1024
N_META = 16
CHUNK = 64
CHUNK_PROMPT = 128
PAD_FRONT = 240
GLA_HEADS, GLA_DK, GLA_DV, GLA_RANK, GLA_TAU = 4, 128, 256, 16, 16.0
ML_HEADS, ML_DH, ML_CONV = 4, 256, 4
SB_HEADS, SB_DH = 8, 128
N_GROUPS, EXPERTS_PER_GROUP, TOP_K = 4, 8, 2
N_EXPERTS = N_GROUPS * EXPERTS_PER_GROUP
D_EXPERT = D_MODEL // 2
MOE_BLK = 256
ROW_SUB = 8
LN_EPS = 1e-5
NEG_BIG = -1e30
SB_SKIP = 110.0
SB_BK = 256

C_GQ, C_GK, C_GV, C_GR = 0, 512, 1024, 2048
C_MU, C_MO = 3072, 4096
C_SQ = 5120
C_GATE = 6144
Z_MAIN = 9216
C_SK, C_SV = 0, 1024
SB_HG = 4
L_LR, L_MI, L_MF = 0, 16, 20

VMEM_LIMIT = 48 * 1024 * 1024


def _cparams(sem):
    return pltpu.CompilerParams(dimension_semantics=sem, vmem_limit_bytes=VMEM_LIMIT)


def _pick(n, cands):
    for c in cands:
        if n % c == 0:
            return c
    raise ValueError(f"no tile in {cands} divides {n}")


def _log_sigmoid(x):
    return jnp.minimum(x, 0.0) - jnp.log1p(jnp.exp(-jnp.abs(x)))


def _sigmoid(x):
    return 1.0 / (1.0 + jnp.exp(-x))


def _split2(x):
    hi = x.astype(BF16)
    lo = (x - hi.astype(F32)).astype(BF16)
    return hi, lo


def _split3(x):
    hi = x.astype(BF16)
    r = x - hi.astype(F32)
    mid = r.astype(BF16)
    lo = (r - mid.astype(F32)).astype(BF16)
    return hi, mid, lo


def _dot(a, b):
    return jnp.dot(a, b, preferred_element_type=F32)


def _dot_nt(a, b):
    return lax.dot_general(a, b, (((1,), (1,)), ((), ())), preferred_element_type=F32)


def _dot_tn(a, b):
    return lax.dot_general(a, b, (((0,), (0,)), ((), ())), preferred_element_type=F32)


def _ln_math(x, g, b):
    mu = jnp.mean(x, axis=-1, keepdims=True)
    xc = x - mu
    var = jnp.mean(xc * xc, axis=-1, keepdims=True)
    return xc * lax.rsqrt(var + LN_EPS) * g + b


def _ln_kernel(x_ref, g_ref, b_ref, o_ref, ob_ref):
    y = _ln_math(x_ref[...], g_ref[...], b_ref[...])
    o_ref[...] = y
    ob_ref[...] = y.astype(BF16)


def layer_norm_rows(x, g, b):
    n, d = x.shape
    tm = _pick(n, (512, 256, 128, 64))
    return pl.pallas_call(
        _ln_kernel,
        grid=(n // tm,),
        in_specs=[pl.BlockSpec((tm, d), lambda i: (i, 0)),
                  pl.BlockSpec((1, d), lambda i: (0, 0)),
                  pl.BlockSpec((1, d), lambda i: (0, 0))],
        out_specs=[pl.BlockSpec((tm, d), lambda i: (i, 0)), pl.BlockSpec((tm, d), lambda i: (i, 0))],
        out_shape=[jax.ShapeDtypeStruct((n, d), F32), jax.ShapeDtypeStruct((n, d), BF16)],
        compiler_params=_cparams(("arbitrary",)),
        name="ln_in",
    )(x, g.reshape(1, d), b.reshape(1, d))


def _mm_kernel(x_ref, w_ref, o_ref):
    o_ref[...] = _dot(x_ref[...], w_ref[...])


def _mm2_kernel(x_ref, w_ref, o_ref, ob_ref):
    z = _dot(x_ref[...], w_ref[...])
    o_ref[...] = z
    ob_ref[...] = z.astype(BF16)


def in_projection(x, w_bf16, tn, name, with_bf16=False):
    n, d = x.shape
    cols = w_bf16.shape[1]
    tm = _pick(n, (896, 768, 512, 384, 256, 128, 64))
    out_spec = pl.BlockSpec((tm, tn), lambda j, i: (i, j))
    return pl.pallas_call(
        _mm2_kernel if with_bf16 else _mm_kernel,
        grid=(cols // tn, n // tm),
        in_specs=[pl.BlockSpec((tm, d), lambda j, i: (i, 0)),
                  pl.BlockSpec((d, tn), lambda j, i: (0, j))],
        out_specs=[out_spec, out_spec] if with_bf16 else out_spec,
        out_shape=([jax.ShapeDtypeStruct((n, cols), F32), jax.ShapeDtypeStruct((n, cols), BF16)]
                   if with_bf16 else jax.ShapeDtypeStruct((n, cols), F32)),
        compiler_params=_cparams(("arbitrary", "arbitrary")),
        name=name,
    )(x, w_bf16)


def _gla_levels(L):
    return tuple(L >> i for i in range(1, L.bit_length()))


def _gla_level_matrix(L):
    t = np.arange(L)[:, None]
    j = np.arange(L)[None, :]
    blocks = []
    for m in (L,) + _gla_levels(L):
        blocks.append(((j <= t) & (j // m == t // m)).astype(np.float32))
    for m in (L,) + _gla_levels(L):
        blocks.append(((j > t) & (j // m == t // m)).astype(np.float32))
    return np.concatenate(blocks, axis=0)


def _gla_kernel(q_ref, k_ref, v_ref, r_ref, lr_ref, wa_ref, ba_ref, ng_ref, lvl_ref, s0_ref,
                o_ref, s_ref, *, n_pad, L):
    c = pl.program_id(1)
    GLA_LEVELS = _gla_levels(L)

    @pl.when(c == 0)
    def _():
        s_ref[...] = s0_ref[...]

    row = c * L + lax.broadcasted_iota(jnp.int32, (L, 1), 0)
    valid = row >= n_pad
    x = _dot(lr_ref[...].astype(BF16), wa_ref[...]) + ba_ref[...]
    la = jnp.where(valid, _log_sigmoid(x) * (1.0 / GLA_TAU), 0.0)
    la_hi, la_lo = _split2(la)
    lvl = lvl_ref[...]
    ex = jnp.exp(_dot(lvl, la_hi) + _dot(lvl, la_lo))
    nlev = len(GLA_LEVELS) + 1

    ti = lax.broadcasted_iota(jnp.int32, (L, L), 0)
    si = lax.broadcasted_iota(jnp.int32, (L, L), 1)
    level_masks = []
    for m in GLA_LEVELS:
        sh = m.bit_length() - 1
        tb = lax.shift_right_logical(ti, sh)
        sb = lax.shift_right_logical(si, sh)
        level_masks.append((lax.shift_right_logical(tb, 1) == lax.shift_right_logical(sb, 1))
                           & ((tb & 1) == 1) & ((sb & 1) == 0))
    eye = ti == si

    outs = []
    states = []
    for h in range(GLA_HEADS):
        dk = slice(h * GLA_DK, (h + 1) * GLA_DK)
        dv = slice(h * GLA_DV, (h + 1) * GLA_DV)
        q = q_ref[:, dk] * (GLA_DK ** -0.5)
        k = jnp.where(valid, k_ref[:, dk], 0.0)
        v = jnp.where(valid, v_ref[:, dv], 0.0).astype(BF16)
        att = jnp.where(eye, _dot_nt(q.astype(BF16), k.astype(BF16)), 0.0)
        for lev in range(len(GLA_LEVELS)):
            qe = ex[(1 + lev) * L:(2 + lev) * L, dk]
            ke = ex[(nlev + 1 + lev) * L:(nlev + 2 + lev) * L, dk]
            a = _dot_nt((q * qe).astype(BF16), (k * ke).astype(BF16))
            att = jnp.where(level_masks[lev], a, att)
        eb = ex[0:L, dk]
        st = s_ref[0, h]
        o = _dot(att.astype(BF16), v) + _dot_nt((q * eb).astype(BF16), st.astype(BF16))
        ke_end = ex[nlev * L:(nlev + 1) * L, dk]
        states.append(st * eb[L - 1:L, :] + _dot_tn(v, (k * ke_end).astype(BF16)))
        mu = jnp.mean(o, axis=-1, keepdims=True)
        oc = o - mu
        var = jnp.mean(oc * oc, axis=-1, keepdims=True)
        on = oc * lax.rsqrt(var + LN_EPS) * ng_ref[...]
        r = r_ref[:, dv]
        outs.append((on * (r * _sigmoid(r))).astype(BF16))
    o_ref[...] = jnp.concatenate(outs, axis=-1)
    s_ref[0] = jnp.stack(states, axis=0)


def gla_branch(z_main, z_small, wa, ba, ng, s0t, *, row0, nb, nchunk, chunk, n_pad):
    L = chunk
    assert row0 % L == 0
    lvl = jnp.asarray(_gla_level_matrix(L), BF16)
    rb0 = row0 // L
    hk = GLA_HEADS * GLA_DK
    hv = GLA_HEADS * GLA_DV

    def rows(colblk):
        return lambda b, c: (rb0 + b * nchunk + c, colblk)

    return pl.pallas_call(
        functools.partial(_gla_kernel, n_pad=n_pad, L=L),
        grid=(nb, nchunk),
        in_specs=[pl.BlockSpec((L, hk), rows(C_GQ // hk)),
                  pl.BlockSpec((L, hk), rows(C_GK // hk)),
                  pl.BlockSpec((L, hv), rows(C_GV // hv)),
                  pl.BlockSpec((L, hv), rows(C_GR // hv)),
                  pl.BlockSpec((L, 128), rows(0)),
                  pl.BlockSpec((128, hk), lambda b, c: (0, 0)),
                  pl.BlockSpec((1, hk), lambda b, c: (0, 0)),
                  pl.BlockSpec((1, GLA_DV), lambda b, c: (0, 0)),
                  pl.BlockSpec(lvl.shape, lambda b, c: (0, 0)),
                  pl.BlockSpec((1, GLA_HEADS, GLA_DV, GLA_DK), lambda b, c: (b, 0, 0, 0))],
        out_specs=[pl.BlockSpec((L, hv), lambda b, c: (b * nchunk + c, 0)),
                   pl.BlockSpec((1, GLA_HEADS, GLA_DV, GLA_DK), lambda b, c: (b, 0, 0, 0))],
        out_shape=[jax.ShapeDtypeStruct((nb * nchunk * L, hv), BF16),
                   jax.ShapeDtypeStruct((nb, GLA_HEADS, GLA_DV, GLA_DK), F32)],
        compiler_params=_cparams(("arbitrary", "arbitrary")),
        name="gla",
    )(z_main, z_main, z_main, z_main, z_small, wa, ba, ng, lvl, s0t)


def _mlstm_kernel(u_ref, mo_ref, g_ref, gb_ref, cw_ref, cb_ref, wq_ref, wk_ref, wv_ref, ng_ref, tri_ref,
                  conv0_ref, c0_ref, n0_ref, m0_ref,
                  o_ref, c_ref, n_ref, m_ref, ext_ref, *, n_pad, L):
    c = pl.program_id(1)
    W = ML_HEADS * ML_DH

    @pl.when(c == 0)
    def _():
        ext_ref[0:8, :] = conv0_ref[0]
        c_ref[...] = c0_ref[...]
        n_ref[...] = n0_ref[...]
        m_ref[...] = m0_ref[...]

    row = c * L + lax.broadcasted_iota(jnp.int32, (L, 1), 0)
    valid = row >= n_pad

    u = jnp.where(valid, u_ref[...], 0.0)
    ext_ref[8:8 + L, :] = u
    uc = cb_ref[...]
    for i in range(ML_CONV):
        uc = uc + ext_ref[8 - (ML_CONV - 1) + i:8 - (ML_CONV - 1) + i + L, :] * cw_ref[i:i + 1, :]
    ext_ref[0:8, :] = u[L - 8:L, :]
    ucs = (uc * _sigmoid(uc)).astype(BF16)
    ub = u_ref[...].astype(BF16)

    g = g_ref[...] + gb_ref[...]
    li = jnp.where(valid, g, NEG_BIG)
    lf = jnp.where(valid, _log_sigmoid(g), 0.0)
    tri = tri_ref[...]
    lf_hi, lf_lo = _split2(lf)
    b_col = _dot(tri, lf_hi) + _dot(tri, lf_lo)
    a_col = b_col + m_ref[0]
    li_row = li.T
    b_row = b_col.T

    ti = lax.broadcasted_iota(jnp.int32, (L, L), 0)
    si = lax.broadcasted_iota(jnp.int32, (L, L), 1)
    causal = si <= ti
    lane = lax.broadcasted_iota(jnp.int32, (1, 128), 1)
    m_vec = m_ref[0]
    m_out = m_vec
    outs, c_new, n_new = [], [], []

    for h in range(ML_HEADS):
        d = slice(h * ML_DH, (h + 1) * ML_DH)
        q = _dot(ucs[:, d], wq_ref[h]).astype(BF16)
        k = _dot(ucs[:, d], wk_ref[h]) * (ML_DH ** -0.5)
        v = _dot(ub[:, d], wv_ref[h]).astype(BF16)
        lf_l = L_MF + h
        li_l = L_MI + h
        bc = b_col[:, lf_l:lf_l + 1]
        ac = a_col[:, lf_l:lf_l + 1]
        lic = li[:, li_l:li_l + 1]
        dmat = jnp.where(causal, bc - b_row[lf_l:lf_l + 1, :] + li_row[li_l:li_l + 1, :], NEG_BIG)
        mt = jnp.maximum(ac, jnp.max(dmat, axis=-1, keepdims=True))
        w_intra = jnp.exp(dmat - mt)
        w_inter = jnp.exp(ac - mt)
        qk = _dot_nt(q, k.astype(BF16)) * w_intra
        cmat = c_ref[0, h]
        nrow = n_ref[0, h:h + 1, :]
        num = _dot(qk.astype(BF16), v) + w_inter * _dot(q, cmat.astype(BF16))
        qn = jnp.sum(q.astype(F32) * nrow.astype(BF16).astype(F32), axis=-1, keepdims=True)
        den = jnp.sum(qk, axis=-1, keepdims=True) + w_inter * qn
        hh = num / jnp.maximum(jnp.abs(den), jnp.exp(-mt))
        m_new = mt[L - 1:L, :]
        b_end = bc[L - 1:L, :]
        wk = jnp.exp(b_end - bc + lic - m_new)
        ws = jnp.exp(b_end + m_vec[:, lf_l:lf_l + 1] - m_new)
        kw = k * wk
        c_new.append(ws * cmat + _dot_tn(kw.astype(BF16), v))
        n_new.append(ws * nrow + jnp.sum(kw, axis=0, keepdims=True))
        m_out = jnp.where(lane == lf_l, m_new, m_out)
        mu = jnp.mean(hh, axis=-1, keepdims=True)
        hc = hh - mu
        var = jnp.mean(hc * hc, axis=-1, keepdims=True)
        hn = hc * lax.rsqrt(var + LN_EPS) * ng_ref[...]
        outs.append((hn * _sigmoid(mo_ref[:, d])).astype(BF16))
    o_ref[...] = jnp.concatenate(outs, axis=-1)
    c_ref[0] = jnp.stack(c_new, axis=0)
    n_ref[0] = jnp.concatenate(n_new, axis=0)
    m_ref[0] = m_out


def mlstm_branch(z_main, z_small, gbias, cw, cb, wq, wk, wv, ng, conv0, c0, n0, m0,
                 *, row0, nb, nchunk, chunk, n_pad):
    L = chunk
    assert row0 % L == 0
    tri = jnp.asarray(_tri_lower(L), BF16)
    rb0 = row0 // L
    W = ML_HEADS * ML_DH

    def rows(colblk):
        return lambda b, c: (rb0 + b * nchunk + c, colblk)

    const2 = lambda b, c: (0, 0)
    const3 = lambda b, c: (0, 0, 0)
    return pl.pallas_call(
        functools.partial(_mlstm_kernel, n_pad=n_pad, L=L),
        grid=(nb, nchunk),
        in_specs=[pl.BlockSpec((L, W), rows(C_MU // W)),
                  pl.BlockSpec((L, W), rows(C_MO // W)),
                  pl.BlockSpec((L, 128), rows(0)),
                  pl.BlockSpec((1, 128), const2),
                  pl.BlockSpec((ML_CONV, W), const2),
                  pl.BlockSpec((1, W), const2),
                  pl.BlockSpec((ML_HEADS, ML_DH, ML_DH), const3),
                  pl.BlockSpec((ML_HEADS, ML_DH, ML_DH), const3),
                  pl.BlockSpec((ML_HEADS, ML_DH, ML_DH), const3),
                  pl.BlockSpec((1, ML_DH), const2),
                  pl.BlockSpec((L, L), const2),
                  pl.BlockSpec((1, 8, W), lambda b, c: (b, 0, 0)),
                  pl.BlockSpec((1, ML_HEADS, ML_DH, ML_DH), lambda b, c: (b, 0, 0, 0)),
                  pl.BlockSpec((1, ML_HEADS, ML_DH), lambda b, c: (b, 0, 0)),
                  pl.BlockSpec((1, 1, 128), lambda b, c: (b, 0, 0))],
        out_specs=[pl.BlockSpec((L, W), lambda b, c: (b * nchunk + c, 0)),
                   pl.BlockSpec((1, ML_HEADS, ML_DH, ML_DH), lambda b, c: (b, 0, 0, 0)),
                   pl.BlockSpec((1, ML_HEADS, ML_DH), lambda b, c: (b, 0, 0)),
                   pl.BlockSpec((1, 1, 128), lambda b, c: (b, 0, 0))],
        out_shape=[jax.ShapeDtypeStruct((nb * nchunk * L, W), BF16),
                   jax.ShapeDtypeStruct((nb, ML_HEADS, ML_DH, ML_DH), F32),
                   jax.ShapeDtypeStruct((nb, ML_HEADS, ML_DH), F32),
                   jax.ShapeDtypeStruct((nb, 1, 128), F32)],
        scratch_shapes=[pltpu.VMEM((8 + L, W), F32)],
        compiler_params=_cparams(("arbitrary", "arbitrary")),
        name="mlstm",
    )(z_main, z_main, z_small, gbias, cw, cb, wq, wk, wv, ng, tri, conv0, c0, n0, m0)


def _sb_heads(qs, ks, vs, vis, cs, umat):
    nh = len(qs)
    bq = qs[0].shape[0]
    lss, l1s, parts = [], [], []
    for h in range(nh):
        z = _dot_nt(qs[h], ks[h])
        t = jnp.log1p(jnp.exp(-jnp.abs(z)))
        lss.append(jnp.minimum(z, 0.0) - t)
        l1 = jnp.minimum(-z, 0.0) - t
        if vis is not None:
            l1 = jnp.where(vis, l1, 0.0)
        l1s.append(l1)
        parts.extend(_split2(l1))
    suf = _dot(jnp.concatenate(parts, axis=0), umat)
    pvs, new_cs = [], []
    for h in range(nh):
        suffix = suf[2 * h * bq:(2 * h + 1) * bq] + suf[(2 * h + 1) * bq:(2 * h + 2) * bq]
        att = jnp.exp(lss[h] + suffix + cs[h])
        if vis is not None:
            att = jnp.where(vis, att, 0.0)
        pvs.append(_dot(att.astype(BF16), vs[h]))
        new_cs.append(cs[h] + jnp.sum(l1s[h], axis=-1, keepdims=True))
    return jnp.concatenate(pvs, axis=-1), new_cs, jnp.max(functools.reduce(jnp.maximum, new_cs))


def _sb_prompt_kernel(q_ref, k_ref, v_ref, u_ref, o_ref, acc_ref, *, bq, n_pad):
    qi = pl.program_id(2)
    bk = SB_BK
    assert bq == bk and n_pad < bk
    qpos = lax.broadcasted_iota(jnp.int32, (bq, 1), 0)
    kio = lax.broadcasted_iota(jnp.int32, (1, bk), 1)
    umat = u_ref[...]
    heads = [slice(h * SB_DH, (h + 1) * SB_DH) for h in range(SB_HG)]
    qs = [(q_ref[:, hs] * (SB_DH ** -0.5)).astype(BF16) for hs in heads]

    def block(j, vis, cs):
        start = pl.multiple_of(j * bk, bk)
        return _sb_heads(qs, [k_ref[pl.ds(start, bk), hs] for hs in heads],
                         [v_ref[pl.ds(start, bk), hs] for hs in heads], vis, cs, umat)

    vis_diag = (kio < qpos) & (kio >= jnp.where(qi > 0, 0, n_pad))
    pv, cs, cmax = block(qi, vis_diag, [jnp.zeros((bq, 1), F32) for _ in heads])
    acc_ref[...] = pv

    def body(carry):
        j, cs, _ = carry
        pv, cs, cmax = block(j, None, list(cs))
        acc_ref[...] += pv
        return j - 1, tuple(cs), cmax

    def cond(carry):
        j, _, cmax = carry
        return (j >= 1) & (cmax > -SB_SKIP)

    j, cs, cmax = lax.while_loop(cond, body, (qi - 1, tuple(cs), cmax))

    @pl.when((j == 0) & (cmax > -SB_SKIP))
    def _():
        pv, _, _ = block(0, jnp.broadcast_to(kio >= n_pad, (bq, bk)), list(cs))
        acc_ref[...] += pv

    o_ref[...] = acc_ref[...].astype(BF16)


def sb_attention_prompt(z_main, kvb, umat, *, nb, tp, bq, n_pad):
    nq = tp // bq
    gw = SB_HG * SB_DH
    ng = SB_HEADS // SB_HG
    return pl.pallas_call(
        functools.partial(_sb_prompt_kernel, bq=bq, n_pad=n_pad),
        grid=(nb, ng, nq),
        in_specs=[pl.BlockSpec((bq, gw), lambda b, g, i: (b * nq + i, C_SQ // gw + g)),
                  pl.BlockSpec((tp, gw), lambda b, g, i: (b, C_SK // gw + g)),
                  pl.BlockSpec((tp, gw), lambda b, g, i: (b, C_SV // gw + g)),
                  pl.BlockSpec((SB_BK, SB_BK), lambda b, g, i: (0, 0))],
        out_specs=pl.BlockSpec((bq, gw), lambda b, g, i: (b * nq + i, g)),
        out_shape=jax.ShapeDtypeStruct((nb * tp, SB_HEADS * SB_DH), BF16),
        scratch_shapes=[pltpu.VMEM((bq, gw), F32)],
        compiler_params=_cparams(("arbitrary", "arbitrary", "arbitrary")),
        name="sb_prompt",
    )(z_main, kvb, kvb, umat)


def _sb_sample_kernel(q_ref, kn_ref, vn_ref, kc_ref, vc_ref, u_ref, o_ref, acc_ref, *, n_cache):
    T = CHUNK
    bk = SB_BK
    nfull = n_cache // bk
    rem = n_cache % bk
    umat = u_ref[...]
    heads = [slice(h * SB_DH, (h + 1) * SB_DH) for h in range(SB_HEADS)]
    qs = [(q_ref[:, hs] * (SB_DH ** -0.5)).astype(BF16) for hs in heads]
    vis_new = lax.broadcasted_iota(jnp.int32, (1, T), 1) < lax.broadcasted_iota(jnp.int32, (T, 1), 0)
    pv, cs, cmax = _sb_heads(qs, [kn_ref[:, hs].astype(BF16) for hs in heads],
                             [vn_ref[:, hs].astype(BF16) for hs in heads], vis_new,
                             [jnp.zeros((T, 1), F32) for _ in heads], umat[0:T, 0:T])
    acc_ref[...] = pv

    def cache_block(start, vis, cs):
        return _sb_heads(qs, [kc_ref[0, 0, pl.ds(start, bk), h, :].astype(BF16) for h in range(SB_HEADS)],
                         [vc_ref[0, 0, pl.ds(start, bk), h, :].astype(BF16) for h in range(SB_HEADS)],
                         vis, cs, umat)

    def body(carry):
        j, cs, _ = carry
        pv, cs, cmax = cache_block(pl.multiple_of(n_cache - bk * (j + 1), 8), None, list(cs))
        acc_ref[...] += pv
        return j + 1, tuple(cs), cmax

    def cond(carry):
        j, _, cmax = carry
        return (j < nfull) & (cmax > -SB_SKIP)

    _, cs, _ = lax.while_loop(cond, body, (jnp.int32(0), tuple(cs), cmax))
    if rem:
        vis_rem = jnp.broadcast_to(lax.broadcasted_iota(jnp.int32, (1, bk), 1) < rem, (T, bk))
        pv, _, _ = cache_block(0, vis_rem, list(cs))
        acc_ref[...] += pv
    o_ref[...] = acc_ref[...].astype(BF16)


def sb_attention_sample(z_main, z_kv, cache_k, cache_v, umat, *, layer, row0, nb):
    T = CHUNK
    hd = SB_HEADS * SB_DH
    n_cache = cache_k.shape[2]
    assert n_cache % 8 == 0 and n_cache >= SB_BK
    rb0 = row0 // T
    cache_spec = pl.BlockSpec((1, 1, n_cache, SB_HEADS, SB_DH), lambda b: (layer, b, 0, 0, 0))
    return pl.pallas_call(
        functools.partial(_sb_sample_kernel, n_cache=n_cache),
        grid=(nb,),
        in_specs=[pl.BlockSpec((T, hd), lambda b: (rb0 + b, C_SQ // hd)),
                  pl.BlockSpec((T, hd), lambda b: (rb0 + b, C_SK // hd)),
                  pl.BlockSpec((T, hd), lambda b: (rb0 + b, C_SV // hd)),
                  cache_spec, cache_spec,
                  pl.BlockSpec((SB_BK, SB_BK), lambda b: (0, 0))],
        out_specs=pl.BlockSpec((T, hd), lambda b: (b, 0)),
        out_shape=jax.ShapeDtypeStruct((nb * T, hd), BF16),
        scratch_shapes=[pltpu.VMEM((T, hd), F32)],
        compiler_params=_cparams(("arbitrary",)),
        name="sb_sample",
    )(z_main, z_kv, z_kv, cache_k, cache_v, umat)


def _kv_out_kernel(*refs, depth, nb, nj, rt, n_front):
    z_refs = refs[:depth]
    k_hbm, v_hbm, kbuf, vbuf, sem = refs[depth:]
    l = pl.program_id(0)
    b = pl.program_id(1)
    j = pl.program_id(2)
    x = z_refs[0][...]
    for i in range(1, depth):
        x = jnp.where(l == i, z_refs[i][...], x)
    hd = SB_HEADS * SB_DH
    kbuf[...] = x[:, C_SK:C_SK + hd].reshape(rt, SB_HEADS, SB_DH)
    vbuf[...] = x[:, C_SV:C_SV + hd].reshape(rt, SB_HEADS, SB_DH)

    def copies(src_lo, n, dst_lo):
        return [pltpu.make_async_copy(buf.at[pl.ds(src_lo, n)], out.at[l, b, pl.ds(dst_lo, n)], sem.at[s])
                for s, (buf, out) in enumerate(((kbuf, k_hbm), (vbuf, v_hbm)))]

    @pl.when(j == 0)
    def _():
        cps = copies(n_front, rt - n_front, 0)
        for cp in cps:
            cp.start()
        for cp in cps:
            cp.wait()

    @pl.when(j > 0)
    def _():
        cps = copies(0, rt, j * rt - n_front)
        for cp in cps:
            cp.start()
        for cp in cps:
            cp.wait()


def kv_state_outputs(z_kvs, *, nb, tp, n_front):
    depth = len(z_kvs)
    rt = _pick(tp, (768, 640, 512, 384, 256))
    assert rt > n_front
    nj = tp // rt
    w = z_kvs[0].shape[1]
    last = nb * nj - 1

    def zspec(i):
        return pl.BlockSpec((rt, w), lambda l, b, j: (jnp.where(l == i, b * nj + j, jnp.where(l < i, 0, last)), 0))

    shape = jax.ShapeDtypeStruct((depth, nb, tp - n_front, SB_HEADS, SB_DH), F32)
    return pl.pallas_call(
        functools.partial(_kv_out_kernel, depth=depth, nb=nb, nj=nj, rt=rt, n_front=n_front),
        grid=(depth, nb, nj),
        in_specs=[zspec(i) for i in range(depth)],
        out_specs=[pl.BlockSpec(memory_space=pl.ANY), pl.BlockSpec(memory_space=pl.ANY)],
        out_shape=[shape, shape],
        scratch_shapes=[pltpu.VMEM((rt, SB_HEADS, SB_DH), F32), pltpu.VMEM((rt, SB_HEADS, SB_DH), F32),
                        pltpu.SemaphoreType.DMA((2,))],
        compiler_params=_cparams(("arbitrary", "arbitrary", "arbitrary")),
        name="kv_state_out",
    )(*z_kvs)


def _merge_kernel(agp_ref, amp_ref, asp_ref, ags_ref, ams_ref, ass_ref, g0_ref, g1_ref, g2_ref, x_ref,
                  wg_ref, wm_ref, ws_ref, wo_ref, lg_ref, lb_ref, o_ref, or_ref, *, alpha, np_tiles):
    is_p = pl.program_id(0) < np_tiles
    ag = jnp.where(is_p, agp_ref[...], ags_ref[...])
    am = jnp.where(is_p, amp_ref[...], ams_ref[...])
    asb = jnp.where(is_p, asp_ref[...], ass_ref[...])
    merged = (_sigmoid(g0_ref[...]) * _dot(ag, wg_ref[...])
              + _sigmoid(g1_ref[...]) * _dot(am, wm_ref[...])
              + _sigmoid(g2_ref[...]) * _dot(asb, ws_ref[...]))
    out = _dot(merged.astype(BF16), wo_ref[...])
    y = _ln_math(alpha * x_ref[...] + out, lg_ref[...], lb_ref[...])
    o_ref[...] = y
    or_ref[...] = y.reshape(or_ref.shape)


def merge_ln1(acts_p, acts_s, z_main, x, wg, wm, ws, wo, lg, lb, alpha):
    n, d = x.shape
    n_p = acts_p[0].shape[0]
    n_s = acts_s[0].shape[0]
    tm = _pick(np.gcd(n_p, n_s), (256, 128, 64))
    assert n_p + n_s == n
    np_tiles = n_p // tm
    row = lambda i: (i, 0)
    row_p = lambda i: (jnp.minimum(i, np_tiles - 1), 0)
    row_s = lambda i: (jnp.maximum(i - np_tiles, 0), 0)
    const = lambda i: (0, 0)
    gcb = C_GATE // d
    return pl.pallas_call(
        functools.partial(_merge_kernel, alpha=alpha, np_tiles=np_tiles),
        grid=(n // tm,),
        in_specs=[pl.BlockSpec((tm, d), row_p), pl.BlockSpec((tm, d), row_p), pl.BlockSpec((tm, d), row_p),
                  pl.BlockSpec((tm, d), row_s), pl.BlockSpec((tm, d), row_s), pl.BlockSpec((tm, d), row_s),
                  pl.BlockSpec((tm, d), lambda i: (i, gcb)),
                  pl.BlockSpec((tm, d), lambda i: (i, gcb + 1)),
                  pl.BlockSpec((tm, d), lambda i: (i, gcb + 2)),
                  pl.BlockSpec((tm, d), row),
                  pl.BlockSpec((d, d), const), pl.BlockSpec((d, d), const),
                  pl.BlockSpec((d, d), const), pl.BlockSpec((d, d), const),
                  pl.BlockSpec((1, d), const), pl.BlockSpec((1, d), const)],
        out_specs=[pl.BlockSpec((tm, d), row), pl.BlockSpec((tm, ROW_SUB, d // ROW_SUB), lambda i: (i, 0, 0))],
        out_shape=[jax.ShapeDtypeStruct((n, d), F32), jax.ShapeDtypeStruct((n, ROW_SUB, d // ROW_SUB), F32)],
        compiler_params=_cparams(("arbitrary",)),
        name="merge_ln1",
    )(*acts_p, *acts_s, z_main, z_main, z_main, x, wg, wm, ws, wo, lg.reshape(1, d), lb.reshape(1, d))


R_ROWS = 8 + N_EXPERTS


def _router_kernel(x_ref, w_ref, b_ref, u_ref, idx_ref, gate_ref, cnt_ref):
    i = pl.program_id(0)
    tm = x_ref.shape[0]

    @pl.when(i == 0)
    def _():
        cnt_ref[...] = jnp.zeros_like(cnt_ref)

    logits = _dot_nt(w_ref[...].astype(BF16), x_ref[...].astype(BF16)) + b_ref[...]

    g = [logits[j:j + 1, :] for j in range(N_GROUPS)]
    gmax = jnp.maximum(jnp.maximum(g[0], g[1]), jnp.maximum(g[2], g[3]))
    gsel = jnp.where(g[0] == gmax, 0, jnp.where(g[1] == gmax, 1, jnp.where(g[2] == gmax, 2, 3)))
    gden = (jnp.exp(g[0] - gmax) + jnp.exp(g[1] - gmax)) + (jnp.exp(g[2] - gmax) + jnp.exp(g[3] - gmax))
    gprob = 1.0 / gden
    e_in = jnp.where(gsel == 0, logits[8:16, :],
                     jnp.where(gsel == 1, logits[16:24, :],
                               jnp.where(gsel == 2, logits[24:32, :], logits[32:40, :])))
    ridx = lax.broadcasted_iota(jnp.int32, (EXPERTS_PER_GROUP, tm), 0)
    v1 = jnp.max(e_in, axis=0, keepdims=True)
    i1 = jnp.min(jnp.where(e_in == v1, ridx, EXPERTS_PER_GROUP), axis=0, keepdims=True)
    e2 = jnp.where(ridx == i1, -jnp.inf, e_in)
    v2 = jnp.max(e2, axis=0, keepdims=True)
    i2 = jnp.min(jnp.where(e2 == v2, ridx, EXPERTS_PER_GROUP), axis=0, keepdims=True)
    t = jnp.exp(v2 - v1)
    p1 = 1.0 / (1.0 + t)
    gate_ref[0:1, :] = p1 * gprob
    gate_ref[1:2, :] = (t * p1) * gprob
    gate_ref[2:8, :] = jnp.zeros((6, tm), F32)
    eid1 = gsel * EXPERTS_PER_GROUP + i1
    eid2 = gsel * EXPERTS_PER_GROUP + i2

    eidx = lax.broadcasted_iota(jnp.int32, (N_EXPERTS, tm), 0)
    oh1 = eidx == eid1
    oh2 = eidx == eid2
    oh1f = jnp.where(oh1, 1.0, 0.0)
    oh2f = jnp.where(oh2, 1.0, 0.0)
    umat = u_ref[...]
    pre1 = _dot(oh1f.astype(BF16), umat)
    pre2 = _dot(oh2f.astype(BF16), umat)
    cnt1 = jnp.sum(oh1f, axis=1, keepdims=True)
    cnt2 = jnp.sum(oh2f, axis=1, keepdims=True)
    base = cnt_ref[:, 0:1]
    rank1 = jnp.sum(jnp.where(oh1, base + pre1, 0.0), axis=0, keepdims=True)
    rank2 = jnp.sum(jnp.where(oh2, base + cnt1 + pre2, 0.0), axis=0, keepdims=True)
    idx_ref[0:1, :] = eid1
    idx_ref[1:2, :] = eid2
    idx_ref[2:3, :] = rank1.astype(jnp.int32)
    idx_ref[3:4, :] = rank2.astype(jnp.int32)
    idx_ref[4:8, :] = jnp.zeros((4, tm), jnp.int32)
    cnt_ref[...] = cnt_ref[...] + (cnt1 + cnt2)


def moe_router(x, w_r, b_r, umat):
    n, d = x.shape
    tm = umat.shape[0]
    return pl.pallas_call(
        _router_kernel,
        grid=(n // tm,),
        in_specs=[pl.BlockSpec((tm, d), lambda i: (i, 0)),
                  pl.BlockSpec((R_ROWS, d), lambda i: (0, 0)),
                  pl.BlockSpec((R_ROWS, 1), lambda i: (0, 0)),
                  pl.BlockSpec((tm, tm), lambda i: (0, 0))],
        out_specs=[pl.BlockSpec((8, tm), lambda i: (0, i)),
                   pl.BlockSpec((8, tm), lambda i: (0, i)),
                   pl.BlockSpec((N_EXPERTS, 128), lambda i: (0, 0))],
        out_shape=[jax.ShapeDtypeStruct((8, n), jnp.int32),
                   jax.ShapeDtypeStruct((8, n), F32),
                   jax.ShapeDtypeStruct((N_EXPERTS, 128), F32)],
        compiler_params=_cparams(("arbitrary",)),
        name="moe_router",
    )(x, w_r, b_r, umat)


def _row_copy(src_hbm, dst_vmem, src_row, dst_row, sem):
    return pltpu.make_async_copy(src_hbm.at[pl.ds(src_row, 1)], dst_vmem.at[pl.ds(dst_row, 1)], sem)


GATHER_UNROLL = 8


def _start_row_gather(idx_ref, base, src_hbm, dst_vmem, nrows, sem):
    def start(r, carry):
        _row_copy(src_hbm, dst_vmem, idx_ref[base + r], r, sem).start()
        return carry

    lax.fori_loop(0, nrows, start, 0, unroll=GATHER_UNROLL)


def _wait_row_gather(src_hbm, dst_vmem, nrows, sem):
    def wait(r, carry):
        _row_copy(src_hbm, dst_vmem, 0, r, sem).wait()
        return carry

    lax.fori_loop(0, nrows, wait, 0, unroll=GATHER_UNROLL)


def _moe_kernel(be_ref, tok_ref, nblk_ref, x_hbm, wg_ref, wu_ref, wd_ref, o_ref,
                xbuf, wgb, wub, wdb, sem):
    i = pl.program_id(0)
    nblk = nblk_ref[0]
    slot = i % 2

    @pl.when((i == 0) & (nblk > 0))
    def _():
        _start_row_gather(tok_ref, 0, x_hbm, xbuf.at[0], MOE_BLK, sem.at[0])

    @pl.when(i + 1 < nblk)
    def _():
        _start_row_gather(tok_ref, (i + 1) * MOE_BLK, x_hbm, xbuf.at[1 - slot], MOE_BLK, sem.at[1 - slot])

    prev_e = be_ref[jnp.maximum(i - 1, 0)]

    @pl.when((i == 0) | (be_ref[i] != prev_e))
    def _():
        wgb[...] = wg_ref[0, 0].astype(BF16)
        wub[...] = wu_ref[0, 0].astype(BF16)
        wdb[...] = wd_ref[0, 0].astype(BF16)

    @pl.when(i < nblk)
    def _():
        _wait_row_gather(x_hbm, xbuf.at[slot], MOE_BLK, sem.at[slot])
        xb = xbuf[slot].reshape(MOE_BLK, D_MODEL).astype(BF16)
        a = _dot(xb, wgb[...])
        hmid = (a * _sigmoid(a)) * _dot(xb, wub[...])
        o_ref[...] = _dot(hmid.astype(BF16), wdb[...]).reshape(o_ref.shape)

    @pl.when(i >= nblk)
    def _():
        o_ref[...] = jnp.zeros_like(o_ref)


def moe_experts(x, block_e, tok_pad, nblk, w_gate, w_up, w_down, *, layer):
    n = x.shape[0]
    d = D_MODEL
    rs = (ROW_SUB, d // ROW_SUB)
    nb = block_e.shape[0]
    wmap = lambda i, be, tok, nblk: (layer, be[i], 0, 0)
    grid_spec = pltpu.PrefetchScalarGridSpec(
        num_scalar_prefetch=3,
        grid=(nb,),
        in_specs=[pl.BlockSpec(memory_space=pl.ANY),
                  pl.BlockSpec((1, 1, d, D_EXPERT), wmap),
                  pl.BlockSpec((1, 1, d, D_EXPERT), wmap),
                  pl.BlockSpec((1, 1, D_EXPERT, d), wmap)],
        out_specs=pl.BlockSpec((MOE_BLK,) + rs, lambda i, be, tok, nblk: (i, 0, 0)),
        scratch_shapes=[pltpu.VMEM((2, MOE_BLK) + rs, F32),
                        pltpu.VMEM((d, D_EXPERT), BF16),
                        pltpu.VMEM((d, D_EXPERT), BF16),
                        pltpu.VMEM((D_EXPERT, d), BF16),
                        pltpu.SemaphoreType.DMA((2,))],
    )
    return pl.pallas_call(
        _moe_kernel,
        grid_spec=grid_spec,
        out_shape=jax.ShapeDtypeStruct((nb * MOE_BLK,) + rs, F32),
        compiler_params=_cparams(("arbitrary",)),
        name="moe_experts",
    )(block_e, tok_pad, nblk, x, w_gate, w_up, w_down)


def _combine_kernel(dest_ref, yb_hbm, x_ref, gate_ref, lg_ref, lb_ref, o_ref, ob_ref, ybuf, sem, *, alpha, n):
    i = pl.program_id(0)
    nt = pl.num_programs(0)
    tm = x_ref.shape[0]
    slot = i % 2

    def start(tile, s):
        for k in range(TOP_K):
            _start_row_gather(dest_ref, k * n + tile * tm, yb_hbm, ybuf.at[s, k], tm, sem.at[s, k])

    @pl.when(i == 0)
    def _():
        start(0, 0)

    @pl.when(i + 1 < nt)
    def _():
        start(i + 1, 1 - slot)

    for k in range(TOP_K):
        _wait_row_gather(yb_hbm, ybuf.at[slot, k], tm, sem.at[slot, k])
    gate = gate_ref[...]
    d = x_ref.shape[1]
    y = ybuf[slot, 0].reshape(tm, d) * gate[:, 0:1] + ybuf[slot, 1].reshape(tm, d) * gate[:, 1:2]
    out = _ln_math(alpha * x_ref[...] + y, lg_ref[...], lb_ref[...])
    o_ref[...] = out
    ob_ref[...] = out.astype(BF16)


def moe_combine_ln2(x, yb, dest, gate_cols, lg, lb, alpha):
    n, d = x.shape
    tm = _pick(n, (256, 128, 64))
    grid_spec = pltpu.PrefetchScalarGridSpec(
        num_scalar_prefetch=1,
        grid=(n // tm,),
        in_specs=[pl.BlockSpec(memory_space=pl.ANY),
                  pl.BlockSpec((tm, d), lambda i, dest: (i, 0)),
                  pl.BlockSpec((tm, TOP_K), lambda i, dest: (i, 0)),
                  pl.BlockSpec((1, d), lambda i, dest: (0, 0)),
                  pl.BlockSpec((1, d), lambda i, dest: (0, 0))],
        out_specs=[pl.BlockSpec((tm, d), lambda i, dest: (i, 0)), pl.BlockSpec((tm, d), lambda i, dest: (i, 0))],
        scratch_shapes=[pltpu.VMEM((2, TOP_K, tm, ROW_SUB, d // ROW_SUB), F32),
                        pltpu.SemaphoreType.DMA((2, TOP_K))],
    )
    return pl.pallas_call(
        functools.partial(_combine_kernel, alpha=alpha, n=n),
        grid_spec=grid_spec,
        out_shape=[jax.ShapeDtypeStruct((n, d), F32), jax.ShapeDtypeStruct((n, d), BF16)],
        compiler_params=_cparams(("arbitrary",)),
        name="moe_combine_ln2",
    )(dest, yb, x, gate_cols, lg.reshape(1, d), lb.reshape(1, d))


def _tri_lower(n):
    t = np.arange(n)
    return (t[None, :] <= t[:, None]).astype(np.float32)


def _strict_upper(n):
    t = np.arange(n)
    return (t[:, None] < t[None, :]).astype(np.float32)


def kernel(x_prompt, x_sample, state_gla, state_ml_C, state_ml_n, state_ml_m, state_ml_conv, cache_sb_k, cache_sb_v, meta_tokens, ln_in_g, ln_in_b, w_in, gla_w_a2, gla_b_a, gla_norm_g, ml_conv_w, ml_conv_b, ml_wq, ml_wk, ml_wv, ml_b_i, ml_b_f, ml_norm_g, w_br_gla, w_br_ml, w_br_sb, w_out, ln1_g, ln1_b, w_rg, b_rg, w_re, b_re, w_e_gate, w_e_up, w_e_down, ln2_g, ln2_b):
    D = D_MODEL
    B, SEQ, _ = x_prompt.shape
    DB, DS, _ = x_sample.shape
    depth = w_in.shape[0]
    n_cache = cache_sb_k.shape[2]
    assert DS == CHUNK and SEQ % 128 == 0 and x_prompt.shape[2] == D
    alpha = (2.0 * depth) ** 0.25
    TP = PAD_FRONT + N_META + SEQ
    NP = B * TP
    NS = DB * DS
    N = NP + NS
    assert TP % CHUNK_PROMPT == 0 and NP % CHUNK == 0
    nchunk_p = TP // CHUNK_PROMPT

    meta = jnp.broadcast_to(meta_tokens.astype(F32)[None], (B, N_META, D))
    xp = jnp.concatenate([jnp.zeros((B, PAD_FRONT, D), F32), meta, x_prompt], axis=1).reshape(NP, D)
    x = jnp.concatenate([xp, x_sample.reshape(NS, D)], axis=0)
    x, xb = layer_norm_rows(x, ln_in_g, ln_in_b)

    u_sb = jnp.asarray(_strict_upper(SB_BK).T, BF16)
    tm_r = _pick(N, (256, 128, 64))
    u_r = jnp.asarray(_strict_upper(tm_r), BF16)

    A = N * TOP_K
    NB = -(-(A + N_EXPERTS * (MOE_BLK - 1)) // MOE_BLK)

    outs_p = [[] for _ in range(5)]
    outs_s = [[] for _ in range(7)]
    z_kvs = []
    for l in range(depth):
        wi = w_in[l]
        w_main = jnp.concatenate([wi[:, 0:3072], wi[:, 3088:4112], wi[:, 4120:5144], wi[:, 5144:6168],
                                  wi[:, 8216:]], axis=1).astype(BF16)
        w_kv = wi[:, 6168:8216].astype(BF16)
        w_small = jnp.concatenate([wi[:, 3072:3088], wi[:, 4112:4120],
                                   jnp.zeros((D, 128 - 24), F32)], axis=1).astype(BF16)
        z_main = in_projection(xb, w_main, 1024, "in_proj_main")
        z_kv, kvb = in_projection(xb, w_kv, 1024, "in_proj_kv", with_bf16=True)
        z_small = in_projection(xb, w_small, 128, "in_proj_small")
        z_kvs.append(z_kv)

        wa = jnp.concatenate([gla_w_a2[l], jnp.zeros((128 - GLA_RANK, GLA_HEADS * GLA_DK), F32)], axis=0).astype(BF16)
        ba = gla_b_a[l].reshape(1, -1)
        ng = gla_norm_g[l].reshape(1, -1)
        s0_p = jnp.zeros((B, GLA_HEADS, GLA_DV, GLA_DK), F32)
        s0_s = jnp.swapaxes(state_gla[l].astype(F32), -1, -2)
        a_gla_p, sT_p = gla_branch(z_main, z_small, wa, ba, ng, s0_p,
                                   row0=0, nb=B, nchunk=nchunk_p, chunk=CHUNK_PROMPT, n_pad=PAD_FRONT)
        a_gla_s, sT_s = gla_branch(z_main, z_small, wa, ba, ng, s0_s,
                                   row0=NP, nb=DB, nchunk=1, chunk=CHUNK, n_pad=0)

        gbias = jnp.zeros((1, 128), F32).at[0, L_MI:L_MI + ML_HEADS].set(ml_b_i[l]).at[0, L_MF:L_MF + ML_HEADS].set(ml_b_f[l])
        cw = ml_conv_w[l]
        cb = ml_conv_b[l].reshape(1, -1)
        wq = ml_wq[l].astype(BF16)
        wk = ml_wk[l].astype(BF16)
        wv = ml_wv[l].astype(BF16)
        mng = ml_norm_g[l].reshape(1, -1)
        W = ML_HEADS * ML_DH
        conv0_p = jnp.zeros((B, 8, W), F32)
        conv0_s = jnp.concatenate([jnp.zeros((DB, 8 - (ML_CONV - 1), W), F32), state_ml_conv[l].astype(F32)], axis=1)
        m0_s = jnp.zeros((DB, 1, 128), F32).at[:, 0, L_MF:L_MF + ML_HEADS].set(state_ml_m[l].astype(F32))
        a_ml_p, c_p, n_p, m_p = mlstm_branch(
            z_main, z_small, gbias, cw, cb, wq, wk, wv, mng, conv0_p,
            jnp.zeros((B, ML_HEADS, ML_DH, ML_DH), F32), jnp.zeros((B, ML_HEADS, ML_DH), F32),
            jnp.zeros((B, 1, 128), F32), row0=0, nb=B, nchunk=nchunk_p, chunk=CHUNK_PROMPT, n_pad=PAD_FRONT)
        a_ml_s, c_s, n_s, m_s = mlstm_branch(
            z_main, z_small, gbias, cw, cb, wq, wk, wv, mng, conv0_s,
            state_ml_C[l].astype(F32), state_ml_n[l].astype(F32), m0_s,
            row0=NP, nb=DB, nchunk=1, chunk=CHUNK, n_pad=0)

        a_sb_p = sb_attention_prompt(z_main, kvb, u_sb, nb=B, tp=TP, bq=SB_BK, n_pad=PAD_FRONT)
        a_sb_s = sb_attention_sample(z_main, z_kv, cache_sb_k.astype(F32), cache_sb_v.astype(F32),
                                     u_sb, layer=l, row0=NP, nb=DB)

        x1, x1r = merge_ln1((a_gla_p, a_ml_p, a_sb_p), (a_gla_s, a_ml_s, a_sb_s), z_main, x,
                       w_br_gla[l].astype(BF16), w_br_ml[l].astype(BF16), w_br_sb[l].astype(BF16),
                       w_out[l].astype(BF16), ln1_g[l], ln1_b[l], alpha)

        w_r = jnp.concatenate([w_rg[l].T, jnp.zeros((8 - N_GROUPS, D), F32), w_re[l].T], axis=0)
        b_r = jnp.concatenate([b_rg[l], jnp.zeros((8 - N_GROUPS,), F32), b_re[l]]).reshape(R_ROWS, 1)
        idx, gates, cnt = moe_router(x1, w_r, b_r, u_r)
        counts = cnt[:, 0].astype(jnp.int32)
        padded = (counts + MOE_BLK - 1) // MOE_BLK * MOE_BLK
        ends_pad = jnp.cumsum(padded)
        start_pad = ends_pad - padded
        eids = jnp.arange(N_EXPERTS, dtype=jnp.int32)
        start_of = jnp.sum(jnp.where(idx[0:2, :, None] == eids, start_pad, 0), axis=-1)
        dest = (start_of + idx[2:4]).reshape(-1)
        tok = jnp.tile(jnp.arange(N, dtype=jnp.int32), TOP_K)
        tok_pad = jnp.zeros((NB * MOE_BLK,), jnp.int32).at[dest].set(tok)
        blk_lo = jnp.arange(NB, dtype=jnp.int32) * MOE_BLK
        block_e = jnp.minimum(jnp.sum((ends_pad[None, :] <= blk_lo[:, None]).astype(jnp.int32), axis=-1),
                              N_EXPERTS - 1)
        nblk = (ends_pad[-1:] // MOE_BLK).astype(jnp.int32)
        yb = moe_experts(x1r, block_e, tok_pad, nblk, w_e_gate, w_e_up, w_e_down, layer=l)
        x, xb = moe_combine_ln2(x1, yb, dest, gates[0:2].T, ln2_g[l], ln2_b[l], alpha)

        nct = ML_CONV - 1
        conv_p = jnp.stack([z_main[(b + 1) * TP - nct:(b + 1) * TP, C_MU:C_MU + W] for b in range(B)], axis=0)
        conv_s = z_main[NP:, C_MU:C_MU + W].reshape(DB, DS, W)[:, DS - nct:]
        zkv_s = z_kv[NP:].reshape(DB, DS, 2 * D)
        kv_shape = (SB_HEADS, SB_DH)
        outs_p[0].append(jnp.swapaxes(sT_p, -1, -2))
        outs_s[0].append(jnp.swapaxes(sT_s, -1, -2))
        outs_p[1].append(c_p); outs_s[1].append(c_s)
        outs_p[2].append(n_p); outs_s[2].append(n_s)
        outs_p[3].append(m_p[:, 0, L_MF:L_MF + ML_HEADS]); outs_s[3].append(m_s[:, 0, L_MF:L_MF + ML_HEADS])
        outs_p[4].append(conv_p); outs_s[4].append(conv_s)
        outs_s[5].append(zkv_s[:, :, C_SK:C_SK + D].reshape(DB, DS, *kv_shape))
        outs_s[6].append(zkv_s[:, :, C_SV:C_SV + D].reshape(DB, DS, *kv_shape))

    sbk_p, sbv_p = kv_state_outputs(z_kvs, nb=B, tp=TP, n_front=PAD_FRONT)
    sp = [jnp.stack(a, axis=0) for a in outs_p]
    ss = [jnp.stack(a, axis=0) for a in outs_s]
    y_prompt = x[:NP].reshape(B, TP, D)[:, PAD_FRONT + N_META:]
    y_sample = x[NP:].reshape(DB, DS, D)
    return (y_prompt, y_sample, sp[0], ss[0], sp[1], ss[1], sp[2], ss[2], sp[3], ss[3],
            sp[4], ss[4], sbk_p, ss[5], sbv_p, ss[6])
```

```python
import functools

import numpy as np
import jax
import jax.numpy as jnp
from jax import lax
from jax.experimental import pallas as pl
from jax.experimental.pallas import tpu as pltpu

F32 = jnp.float32
BF16 = jnp.bfloat16

D_MODEL = 1024
N_META = 16
CHUNK = 64
CHUNK_PROMPT = 128
PAD_FRONT = 240
GLA_HEADS, GLA_DK, GLA_DV, GLA_RANK, GLA_TAU = 4, 128, 256, 16, 16.0
ML_HEADS, ML_DH, ML_CONV = 4, 256, 4
SB_HEADS, SB_DH = 8, 128
N_GROUPS, EXPERTS_PER_GROUP, TOP_K = 4, 8, 2
N_EXPERTS = N_GROUPS * EXPERTS_PER_GROUP
D_EXPERT = D_MODEL // 2
MOE_BLK = 256
ROW_SUB = 8
LN_EPS = 1e-5
NEG_BIG = -1e30
SB_SKIP = 110.0
SB_BK = 256

C_GQ, C_GK, C_GV, C_GR = 0, 512, 1024, 2048
C_MU, C_MO = 3072, 4096
C_SQ = 5120
C_GATE = 6144
Z_MAIN = 9216
C_SK, C_SV = 0, 1024
SB_HG = 4
L_LR, L_MI, L_MF = 0, 16, 20

VMEM_LIMIT = 48 * 1024 * 1024


def _cparams(sem):
    return pltpu.CompilerParams(dimension_semantics=sem, vmem_limit_bytes=VMEM_LIMIT)


def _pick(n, cands):
    for c in cands:
        if n % c == 0:
            return c
    raise ValueError(f"no tile in {cands} divides {n}")


def _log_sigmoid(x):
    return jnp.minimum(x, 0.0) - jnp.log1p(jnp.exp(-jnp.abs(x)))


def _sigmoid(x):
    return 1.0 / (1.0 + jnp.exp(-x))


def _split2(x):
    hi = x.astype(BF16)
    lo = (x - hi.astype(F32)).astype(BF16)
    return hi, lo


def _split3(x):
    hi = x.astype(BF16)
    r = x - hi.astype(F32)
    mid = r.astype(BF16)
    lo = (r - mid.astype(F32)).astype(BF16)
    return hi, mid, lo


def _dot(a, b):
    return jnp.dot(a, b, preferred_element_type=F32)


def _dot_nt(a, b):
    return lax.dot_general(a, b, (((1,), (1,)), ((), ())), preferred_element_type=F32)


def _dot_tn(a, b):
    return lax.dot_general(a, b, (((0,), (0,)), ((), ())), preferred_element_type=F32)


def _ln_math(x, g, b):
    mu = jnp.mean(x, axis=-1, keepdims=True)
    xc = x - mu
    var = jnp.mean(xc * xc, axis=-1, keepdims=True)
    return xc * lax.rsqrt(var + LN_EPS) * g + b


def _ln_kernel(x_ref, g_ref, b_ref, o_ref, ob_ref):
    y = _ln_math(x_ref[...], g_ref[...], b_ref[...])
    o_ref[...] = y
    ob_ref[...] = y.astype(BF16)


def layer_norm_rows(x, g, b):
    n, d = x.shape
    tm = _pick(n, (512, 256, 128, 64))
    return pl.pallas_call(
        _ln_kernel,
        grid=(n // tm,),
        in_specs=[pl.BlockSpec((tm, d), lambda i: (i, 0)),
                  pl.BlockSpec((1, d), lambda i: (0, 0)),
                  pl.BlockSpec((1, d), lambda i: (0, 0))],
        out_specs=[pl.BlockSpec((tm, d), lambda i: (i, 0)), pl.BlockSpec((tm, d), lambda i: (i, 0))],
        out_shape=[jax.ShapeDtypeStruct((n, d), F32), jax.ShapeDtypeStruct((n, d), BF16)],
        compiler_params=_cparams(("arbitrary",)),
        name="ln_in",
    )(x, g.reshape(1, d), b.reshape(1, d))


def _mm_kernel(x_ref, w_ref, o_ref):
    o_ref[...] = _dot(x_ref[...], w_ref[...])


def _mm2_kernel(x_ref, w_ref, o_ref, ob_ref):
    z = _dot(x_ref[...], w_ref[...])
    o_ref[...] = z
    ob_ref[...] = z.astype(BF16)


def in_projection(x, w_bf16, tn, name, with_bf16=False):
    n, d = x.shape
    cols = w_bf16.shape[1]
    tm = _pick(n, (896, 768, 512, 384, 256, 128, 64))
    out_spec = pl.BlockSpec((tm, tn), lambda j, i: (i, j))
    return pl.pallas_call(
        _mm2_kernel if with_bf16 else _mm_kernel,
        grid=(cols // tn, n // tm),
        in_specs=[pl.BlockSpec((tm, d), lambda j, i: (i, 0)),
                  pl.BlockSpec((d, tn), lambda j, i: (0, j))],
        out_specs=[out_spec, out_spec] if with_bf16 else out_spec,
        out_shape=([jax.ShapeDtypeStruct((n, cols), F32), jax.ShapeDtypeStruct((n, cols), BF16)]
                   if with_bf16 else jax.ShapeDtypeStruct((n, cols), F32)),
        compiler_params=_cparams(("arbitrary", "arbitrary")),
        name=name,
    )(x, w_bf16)


def _gla_levels(L):
    return tuple(L >> i for i in range(1, L.bit_length()))


def _gla_level_matrix(L):
    t = np.arange(L)[:, None]
    j = np.arange(L)[None, :]
    blocks = []
    for m in (L,) + _gla_levels(L):
        blocks.append(((j <= t) & (j // m == t // m)).astype(np.float32))
    for m in (L,) + _gla_levels(L):
        blocks.append(((j > t) & (j // m == t // m)).astype(np.float32))
    return np.concatenate(blocks, axis=0)


def _gla_kernel(q_ref, k_ref, v_ref, r_ref, lr_ref, wa_ref, ba_ref, ng_ref, lvl_ref, s0_ref,
                o_ref, s_ref, *, n_pad, L, phase):
    c = pl.program_id(1)
    GLA_LEVELS = _gla_levels(L)

    if phase == "init":
        @pl.when(c == 0)
        def _():
            s_ref[...] = s0_ref[...]
        return

    row = c * L + lax.broadcasted_iota(jnp.int32, (L, 1), 0)
    valid = row >= n_pad
    x = _dot(lr_ref[...].astype(BF16), wa_ref[...]) + ba_ref[...]
    la = jnp.where(valid, _log_sigmoid(x) * (1.0 / GLA_TAU), 0.0)
    la_hi, la_lo = _split2(la)
    lvl = lvl_ref[...]
    ex = jnp.exp(_dot(lvl, la_hi) + _dot(lvl, la_lo))
    nlev = len(GLA_LEVELS) + 1

    ti = lax.broadcasted_iota(jnp.int32, (L, L), 0)
    si = lax.broadcasted_iota(jnp.int32, (L, L), 1)
    level_masks = []
    for m in GLA_LEVELS:
        sh = m.bit_length() - 1
        tb = lax.shift_right_logical(ti, sh)
        sb = lax.shift_right_logical(si, sh)
        level_masks.append((lax.shift_right_logical(tb, 1) == lax.shift_right_logical(sb, 1))
                           & ((tb & 1) == 1) & ((sb & 1) == 0))
    eye = ti == si

    outs = []
    states = []
    for h in range(GLA_HEADS):
        dk = slice(h * GLA_DK, (h + 1) * GLA_DK)
        dv = slice(h * GLA_DV, (h + 1) * GLA_DV)
        q = q_ref[:, dk] * (GLA_DK ** -0.5)
        k = jnp.where(valid, k_ref[:, dk], 0.0)
        v = jnp.where(valid, v_ref[:, dv], 0.0).astype(BF16)
        att = jnp.where(eye, _dot_nt(q.astype(BF16), k.astype(BF16)), 0.0)
        for lev in range(len(GLA_LEVELS)):
            qe = ex[(1 + lev) * L:(2 + lev) * L, dk]
            ke = ex[(nlev + 1 + lev) * L:(nlev + 2 + lev) * L, dk]
            a = _dot_nt((q * qe).astype(BF16), (k * ke).astype(BF16))
            att = jnp.where(level_masks[lev], a, att)
        eb = ex[0:L, dk]
        st = s_ref[0, h]
        o = _dot(att.astype(BF16), v) + _dot_nt((q * eb).astype(BF16), st.astype(BF16))
        ke_end = ex[nlev * L:(nlev + 1) * L, dk]
        states.append(st * eb[L - 1:L, :] + _dot_tn(v, (k * ke_end).astype(BF16)))
        mu = jnp.mean(o, axis=-1, keepdims=True)
        oc = o - mu
        var = jnp.mean(oc * oc, axis=-1, keepdims=True)
        on = oc * lax.rsqrt(var + LN_EPS) * ng_ref[...]
        r = r_ref[:, dv]
        outs.append((on * (r * _sigmoid(r))).astype(BF16))
    o_ref[...] = jnp.concatenate(outs, axis=-1)
    s_ref[0] = jnp.stack(states, axis=0)


def _gla_call_parts(z_main, z_small, wa, ba, ng, s0t, *, row0, nb, nchunk, chunk):
    L = chunk
    assert row0 % L == 0
    lvl = jnp.asarray(_gla_level_matrix(L), BF16)
    rb0 = row0 // L
    hk = GLA_HEADS * GLA_DK
    hv = GLA_HEADS * GLA_DV

    def rows(colblk):
        return lambda b, c: (rb0 + b * nchunk + c, colblk)

    return dict(
        args=(z_main, z_main, z_main, z_main, z_small, wa, ba, ng, lvl, s0t),
        in_specs=[pl.BlockSpec((L, hk), rows(C_GQ // hk)),
                  pl.BlockSpec((L, hk), rows(C_GK // hk)),
                  pl.BlockSpec((L, hv), rows(C_GV // hv)),
                  pl.BlockSpec((L, hv), rows(C_GR // hv)),
                  pl.BlockSpec((L, 128), rows(0)),
                  pl.BlockSpec((128, hk), lambda b, c: (0, 0)),
                  pl.BlockSpec((1, hk), lambda b, c: (0, 0)),
                  pl.BlockSpec((1, GLA_DV), lambda b, c: (0, 0)),
                  pl.BlockSpec(lvl.shape, lambda b, c: (0, 0)),
                  pl.BlockSpec((1, GLA_HEADS, GLA_DV, GLA_DK), lambda b, c: (b, 0, 0, 0))],
        out_specs=[pl.BlockSpec((L, hv), lambda b, c: (b * nchunk + c, 0)),
                   pl.BlockSpec((1, GLA_HEADS, GLA_DV, GLA_DK), lambda b, c: (b, 0, 0, 0))],
        out_shape=[jax.ShapeDtypeStruct((nb * nchunk * L, hv), BF16),
                   jax.ShapeDtypeStruct((nb, GLA_HEADS, GLA_DV, GLA_DK), F32)],
        scratch=[])


def _mlstm_kernel(u_ref, mo_ref, g_ref, gb_ref, cw_ref, cb_ref, wq_ref, wk_ref, wv_ref, ng_ref, tri_ref,
                  conv0_ref, c0_ref, n0_ref, m0_ref,
                  o_ref, c_ref, n_ref, m_ref, ext_ref, *, n_pad, L, phase):
    c = pl.program_id(1)
    W = ML_HEADS * ML_DH

    if phase == "init":
        @pl.when(c == 0)
        def _():
            ext_ref[0:8, :] = conv0_ref[0]
            c_ref[...] = c0_ref[...]
            n_ref[...] = n0_ref[...]
            m_ref[...] = m0_ref[...]
        return

    row = c * L + lax.broadcasted_iota(jnp.int32, (L, 1), 0)
    valid = row >= n_pad

    u = jnp.where(valid, u_ref[...], 0.0)
    ext_ref[8:8 + L, :] = u
    uc = cb_ref[...]
    for i in range(ML_CONV):
        uc = uc + ext_ref[8 - (ML_CONV - 1) + i:8 - (ML_CONV - 1) + i + L, :] * cw_ref[i:i + 1, :]
    ext_ref[0:8, :] = u[L - 8:L, :]
    ucs = (uc * _sigmoid(uc)).astype(BF16)
    ub = u_ref[...].astype(BF16)

    g = g_ref[...] + gb_ref[...]
    li = jnp.where(valid, g, NEG_BIG)
    lf = jnp.where(valid, _log_sigmoid(g), 0.0)
    tri = tri_ref[...]
    lf_hi, lf_lo = _split2(lf)
    b_col = _dot(tri, lf_hi) + _dot(tri, lf_lo)
    a_col = b_col + m_ref[0]
    li_row = li.T
    b_row = b_col.T

    ti = lax.broadcasted_iota(jnp.int32, (L, L), 0)
    si = lax.broadcasted_iota(jnp.int32, (L, L), 1)
    causal = si <= ti
    lane = lax.broadcasted_iota(jnp.int32, (1, 128), 1)
    m_vec = m_ref[0]
    m_out = m_vec
    outs, c_new, n_new = [], [], []

    for h in range(ML_HEADS):
        d = slice(h * ML_DH, (h + 1) * ML_DH)
        q = _dot(ucs[:, d], wq_ref[h]).astype(BF16)
        k = _dot(ucs[:, d], wk_ref[h]) * (ML_DH ** -0.5)
        v = _dot(ub[:, d], wv_ref[h]).astype(BF16)
        lf_l = L_MF + h
        li_l = L_MI + h
        bc = b_col[:, lf_l:lf_l + 1]
        ac = a_col[:, lf_l:lf_l + 1]
        lic = li[:, li_l:li_l + 1]
        dmat = jnp.where(causal, bc - b_row[lf_l:lf_l + 1, :] + li_row[li_l:li_l + 1, :], NEG_BIG)
        mt = jnp.maximum(ac, jnp.max(dmat, axis=-1, keepdims=True))
        w_intra = jnp.exp(dmat - mt)
        w_inter = jnp.exp(ac - mt)
        qk = _dot_nt(q, k.astype(BF16)) * w_intra
        cmat = c_ref[0, h]
        nrow = n_ref[0, h:h + 1, :]
        num = _dot(qk.astype(BF16), v) + w_inter * _dot(q, cmat.astype(BF16))
        qn = jnp.sum(q.astype(F32) * nrow.astype(BF16).astype(F32), axis=-1, keepdims=True)
        den = jnp.sum(qk, axis=-1, keepdims=True) + w_inter * qn
        hh = num / jnp.maximum(jnp.abs(den), jnp.exp(-mt))
        m_new = mt[L - 1:L, :]
        b_end = bc[L - 1:L, :]
        wk = jnp.exp(b_end - bc + lic - m_new)
        ws = jnp.exp(b_end + m_vec[:, lf_l:lf_l + 1] - m_new)
        kw = k * wk
        c_new.append(ws * cmat + _dot_tn(kw.astype(BF16), v))
        n_new.append(ws * nrow + jnp.sum(kw, axis=0, keepdims=True))
        m_out = jnp.where(lane == lf_l, m_new, m_out)
        mu = jnp.mean(hh, axis=-1, keepdims=True)
        hc = hh - mu
        var = jnp.mean(hc * hc, axis=-1, keepdims=True)
        hn = hc * lax.rsqrt(var + LN_EPS) * ng_ref[...]
        outs.append((hn * _sigmoid(mo_ref[:, d])).astype(BF16))
    o_ref[...] = jnp.concatenate(outs, axis=-1)
    c_ref[0] = jnp.stack(c_new, axis=0)
    n_ref[0] = jnp.concatenate(n_new, axis=0)
    m_ref[0] = m_out


def _mlstm_call_parts(z_main, z_small, gbias, cw, cb, wq, wk, wv, ng, conv0, c0, n0, m0,
                      *, row0, nb, nchunk, chunk):
    L = chunk
    assert row0 % L == 0
    tri = jnp.asarray(_tri_lower(L), BF16)
    rb0 = row0 // L
    W = ML_HEADS * ML_DH

    def rows(colblk):
        return lambda b, c: (rb0 + b * nchunk + c, colblk)

    const2 = lambda b, c: (0, 0)
    const3 = lambda b, c: (0, 0, 0)
    return dict(
        args=(z_main, z_main, z_small, gbias, cw, cb, wq, wk, wv, ng, tri, conv0, c0, n0, m0),
        in_specs=[pl.BlockSpec((L, W), rows(C_MU // W)),
                  pl.BlockSpec((L, W), rows(C_MO // W)),
                  pl.BlockSpec((L, 128), rows(0)),
                  pl.BlockSpec((1, 128), const2),
                  pl.BlockSpec((ML_CONV, W), const2),
                  pl.BlockSpec((1, W), const2),
                  pl.BlockSpec((ML_HEADS, ML_DH, ML_DH), const3),
                  pl.BlockSpec((ML_HEADS, ML_DH, ML_DH), const3),
                  pl.BlockSpec((ML_HEADS, ML_DH, ML_DH), const3),
                  pl.BlockSpec((1, ML_DH), const2),
                  pl.BlockSpec((L, L), const2),
                  pl.BlockSpec((1, 8, W), lambda b, c: (b, 0, 0)),
                  pl.BlockSpec((1, ML_HEADS, ML_DH, ML_DH), lambda b, c: (b, 0, 0, 0)),
                  pl.BlockSpec((1, ML_HEADS, ML_DH), lambda b, c: (b, 0, 0)),
                  pl.BlockSpec((1, 1, 128), lambda b, c: (b, 0, 0))],
        out_specs=[pl.BlockSpec((L, W), lambda b, c: (b * nchunk + c, 0)),
                   pl.BlockSpec((1, ML_HEADS, ML_DH, ML_DH), lambda b, c: (b, 0, 0, 0)),
                   pl.BlockSpec((1, ML_HEADS, ML_DH), lambda b, c: (b, 0, 0)),
                   pl.BlockSpec((1, 1, 128), lambda b, c: (b, 0, 0))],
        out_shape=[jax.ShapeDtypeStruct((nb * nchunk * L, W), BF16),
                   jax.ShapeDtypeStruct((nb, ML_HEADS, ML_DH, ML_DH), F32),
                   jax.ShapeDtypeStruct((nb, ML_HEADS, ML_DH), F32),
                   jax.ShapeDtypeStruct((nb, 1, 128), F32)],
        scratch=[pltpu.VMEM((8 + L, W), F32)])


def _recurrent_kernel(*refs, n_gla_in, n_ml_in, n_pad, L):
    g_in = refs[:n_gla_in]
    m_in = refs[n_gla_in:n_gla_in + n_ml_in]
    g_out = refs[n_gla_in + n_ml_in:n_gla_in + n_ml_in + 2]
    m_rest = refs[n_gla_in + n_ml_in + 2:]
    for phase in ("init", "body"):
        _gla_kernel(*g_in, *g_out, n_pad=n_pad, L=L, phase=phase)
        _mlstm_kernel(*m_in, *m_rest, n_pad=n_pad, L=L, phase=phase)


def recurrent_branches(gla_parts, ml_parts, *, nb, nchunk, chunk, n_pad):
    return pl.pallas_call(
        functools.partial(_recurrent_kernel, n_gla_in=len(gla_parts["args"]), n_ml_in=len(ml_parts["args"]),
                          n_pad=n_pad, L=chunk),
        grid=(nb, nchunk),
        in_specs=gla_parts["in_specs"] + ml_parts["in_specs"],
        out_specs=gla_parts["out_specs"] + ml_parts["out_specs"],
        out_shape=gla_parts["out_shape"] + ml_parts["out_shape"],
        scratch_shapes=gla_parts["scratch"] + ml_parts["scratch"],
        compiler_params=_cparams(("arbitrary", "arbitrary")),
        name="gla_mlstm",
    )(*gla_parts["args"], *ml_parts["args"])


def _sb_heads(qs, ks, vs, vis, cs, umat):
    nh = len(qs)
    bq = qs[0].shape[0]
    lss, l1s, parts = [], [], []
    for h in range(nh):
        z = _dot_nt(qs[h], ks[h])
        t = jnp.log(1.0 + jnp.exp(-jnp.abs(z)))
        lss.append(jnp.minimum(z, 0.0) - t)
        l1 = jnp.minimum(-z, 0.0) - t
        if vis is not None:
            l1 = jnp.where(vis, l1, 0.0)
        l1s.append(l1)
        parts.extend(_split2(l1))
    suf = _dot(jnp.concatenate(parts, axis=0), umat)
    pvs, new_cs = [], []
    for h in range(nh):
        suffix = suf[2 * h * bq:(2 * h + 1) * bq] + suf[(2 * h + 1) * bq:(2 * h + 2) * bq]
        att = jnp.exp(lss[h] + suffix + cs[h])
        if vis is not None:
            att = jnp.where(vis, att, 0.0)
        pvs.append(_dot(att.astype(BF16), vs[h]))
        new_cs.append(cs[h] + jnp.sum(l1s[h], axis=-1, keepdims=True))
    return jnp.concatenate(pvs, axis=-1), new_cs, jnp.max(functools.reduce(jnp.maximum, new_cs))


def _sb_prompt_kernel(q_ref, k_ref, v_ref, u_ref, o_ref, acc_ref, *, bq, n_pad):
    qi = pl.program_id(2)
    bk = SB_BK
    assert bq == bk and n_pad < bk
    qpos = lax.broadcasted_iota(jnp.int32, (bq, 1), 0)
    kio = lax.broadcasted_iota(jnp.int32, (1, bk), 1)
    umat = u_ref[...]
    heads = [slice(h * SB_DH, (h + 1) * SB_DH) for h in range(SB_HG)]
    qs = [(q_ref[:, hs] * (SB_DH ** -0.5)).astype(BF16) for hs in heads]

    def block(j, vis, cs):
        start = pl.multiple_of(j * bk, bk)
        return _sb_heads(qs, [k_ref[pl.ds(start, bk), hs] for hs in heads],
                         [v_ref[pl.ds(start, bk), hs] for hs in heads], vis, cs, umat)

    vis_diag = (kio < qpos) & (kio >= jnp.where(qi > 0, 0, n_pad))
    pv, cs, cmax = block(qi, vis_diag, [jnp.zeros((bq, 1), F32) for _ in heads])
    acc_ref[...] = pv

    def body(carry):
        j, cs, _ = carry
        pv, cs, cmax = block(j, None, list(cs))
        acc_ref[...] += pv
        return j - 1, tuple(cs), cmax

    def cond(carry):
        j, _, cmax = carry
        return (j >= 1) & (cmax > -SB_SKIP)

    j, cs, cmax = lax.while_loop(cond, body, (qi - 1, tuple(cs), cmax))

    @pl.when((j == 0) & (cmax > -SB_SKIP))
    def _():
        pv, _, _ = block(0, jnp.broadcast_to(kio >= n_pad, (bq, bk)), list(cs))
        acc_ref[...] += pv

    o_ref[...] = acc_ref[...].astype(BF16)


def sb_attention_prompt(z_main, kvb, umat, *, nb, tp, bq, n_pad):
    nq = tp // bq
    gw = SB_HG * SB_DH
    ng = SB_HEADS // SB_HG
    return pl.pallas_call(
        functools.partial(_sb_prompt_kernel, bq=bq, n_pad=n_pad),
        grid=(nb, ng, nq),
        in_specs=[pl.BlockSpec((bq, gw), lambda b, g, i: (b * nq + i, C_SQ // gw + g)),
                  pl.BlockSpec((tp, gw), lambda b, g, i: (b, C_SK // gw + g)),
                  pl.BlockSpec((tp, gw), lambda b, g, i: (b, C_SV // gw + g)),
                  pl.BlockSpec((SB_BK, SB_BK), lambda b, g, i: (0, 0))],
        out_specs=pl.BlockSpec((bq, gw), lambda b, g, i: (b * nq + i, g)),
        out_shape=jax.ShapeDtypeStruct((nb * tp, SB_HEADS * SB_DH), BF16),
        scratch_shapes=[pltpu.VMEM((bq, gw), F32)],
        compiler_params=_cparams(("arbitrary", "arbitrary", "arbitrary")),
        name="sb_prompt",
    )(z_main, kvb, kvb, umat)


def _sb_sample_kernel(q_ref, kn_ref, vn_ref, kc_ref, vc_ref, u_ref, o_ref, acc_ref, *, n_cache):
    T = CHUNK
    bk = SB_BK
    nfull = n_cache // bk
    rem = n_cache % bk
    umat = u_ref[...]
    heads = [slice(h * SB_DH, (h + 1) * SB_DH) for h in range(SB_HEADS)]
    qs = [(q_ref[:, hs] * (SB_DH ** -0.5)).astype(BF16) for hs in heads]
    vis_new = lax.broadcasted_iota(jnp.int32, (1, T), 1) < lax.broadcasted_iota(jnp.int32, (T, 1), 0)
    pv, cs, cmax = _sb_heads(qs, [kn_ref[:, hs].astype(BF16) for hs in heads],
                             [vn_ref[:, hs].astype(BF16) for hs in heads], vis_new,
                             [jnp.zeros((T, 1), F32) for _ in heads], umat[0:T, 0:T])
    acc_ref[...] = pv

    def cache_block(start, vis, cs):
        return _sb_heads(qs, [kc_ref[0, 0, pl.ds(start, bk), h, :].astype(BF16) for h in range(SB_HEADS)],
                         [vc_ref[0, 0, pl.ds(start, bk), h, :].astype(BF16) for h in range(SB_HEADS)],
                         vis, cs, umat)

    def body(carry):
        j, cs, _ = carry
        pv, cs, cmax = cache_block(pl.multiple_of(n_cache - bk * (j + 1), 8), None, list(cs))
        acc_ref[...] += pv
        return j + 1, tuple(cs), cmax

    def cond(carry):
        j, _, cmax = carry
        return (j < nfull) & (cmax > -SB_SKIP)

    _, cs, _ = lax.while_loop(cond, body, (jnp.int32(0), tuple(cs), cmax))
    if rem:
        vis_rem = jnp.broadcast_to(lax.broadcasted_iota(jnp.int32, (1, bk), 1) < rem, (T, bk))
        pv, _, _ = cache_block(0, vis_rem, list(cs))
        acc_ref[...] += pv
    o_ref[...] = acc_ref[...].astype(BF16)


def sb_attention_sample(z_main, z_kv, cache_k, cache_v, umat, *, layer, row0, nb):
    T = CHUNK
    hd = SB_HEADS * SB_DH
    n_cache = cache_k.shape[2]
    assert n_cache % 8 == 0 and n_cache >= SB_BK
    rb0 = row0 // T
    cache_spec = pl.BlockSpec((1, 1, n_cache, SB_HEADS, SB_DH), lambda b: (layer, b, 0, 0, 0))
    return pl.pallas_call(
        functools.partial(_sb_sample_kernel, n_cache=n_cache),
        grid=(nb,),
        in_specs=[pl.BlockSpec((T, hd), lambda b: (rb0 + b, C_SQ // hd)),
                  pl.BlockSpec((T, hd), lambda b: (rb0 + b, C_SK // hd)),
                  pl.BlockSpec((T, hd), lambda b: (rb0 + b, C_SV // hd)),
                  cache_spec, cache_spec,
                  pl.BlockSpec((SB_BK, SB_BK), lambda b: (0, 0))],
        out_specs=pl.BlockSpec((T, hd), lambda b: (b, 0)),
        out_shape=jax.ShapeDtypeStruct((nb * T, hd), BF16),
        scratch_shapes=[pltpu.VMEM((T, hd), F32)],
        compiler_params=_cparams(("arbitrary",)),
        name="sb_sample",
    )(z_main, z_kv, z_kv, cache_k, cache_v, umat)


def _kv_out_kernel(*refs, depth, nb, nj, rt, n_front):
    z_refs = refs[:depth]
    k_hbm, v_hbm, kbuf, vbuf, sem = refs[depth:]
    l = pl.program_id(0)
    b = pl.program_id(1)
    j = pl.program_id(2)
    x = z_refs[0][...]
    for i in range(1, depth):
        x = jnp.where(l == i, z_refs[i][...], x)
    hd = SB_HEADS * SB_DH
    kbuf[...] = x[:, C_SK:C_SK + hd].reshape(rt, SB_HEADS, SB_DH)
    vbuf[...] = x[:, C_SV:C_SV + hd].reshape(rt, SB_HEADS, SB_DH)

    def copies(src_lo, n, dst_lo):
        return [pltpu.make_async_copy(buf.at[pl.ds(src_lo, n)], out.at[l, b, pl.ds(dst_lo, n)], sem.at[s])
                for s, (buf, out) in enumerate(((kbuf, k_hbm), (vbuf, v_hbm)))]

    @pl.when(j == 0)
    def _():
        cps = copies(n_front, rt - n_front, 0)
        for cp in cps:
            cp.start()
        for cp in cps:
            cp.wait()

    @pl.when(j > 0)
    def _():
        cps = copies(0, rt, j * rt - n_front)
        for cp in cps:
            cp.start()
        for cp in cps:
            cp.wait()


def kv_state_outputs(z_kvs, *, nb, tp, n_front):
    depth = len(z_kvs)
    rt = _pick(tp, (768, 640, 512, 384, 256))
    assert rt > n_front
    nj = tp // rt
    w = z_kvs[0].shape[1]
    last = nb * nj - 1

    def zspec(i):
        return pl.BlockSpec((rt, w), lambda l, b, j: (jnp.where(l == i, b * nj + j, jnp.where(l < i, 0, last)), 0))

    shape = jax.ShapeDtypeStruct((depth, nb, tp - n_front, SB_HEADS, SB_DH), F32)
    return pl.pallas_call(
        functools.partial(_kv_out_kernel, depth=depth, nb=nb, nj=nj, rt=rt, n_front=n_front),
        grid=(depth, nb, nj),
        in_specs=[zspec(i) for i in range(depth)],
        out_specs=[pl.BlockSpec(memory_space=pl.ANY), pl.BlockSpec(memory_space=pl.ANY)],
        out_shape=[shape, shape],
        scratch_shapes=[pltpu.VMEM((rt, SB_HEADS, SB_DH), F32), pltpu.VMEM((rt, SB_HEADS, SB_DH), F32),
                        pltpu.SemaphoreType.DMA((2,))],
        compiler_params=_cparams(("arbitrary", "arbitrary", "arbitrary")),
        name="kv_state_out",
    )(*z_kvs)


def _merge_kernel(agp_ref, amp_ref, asp_ref, ags_ref, ams_ref, ass_ref, g0_ref, g1_ref, g2_ref, x_ref,
                  wg_ref, wm_ref, ws_ref, wo_ref, lg_ref, lb_ref, o_ref, or_ref, *, alpha, np_tiles):
    is_p = pl.program_id(0) < np_tiles
    ag = jnp.where(is_p, agp_ref[...], ags_ref[...])
    am = jnp.where(is_p, amp_ref[...], ams_ref[...])
    asb = jnp.where(is_p, asp_ref[...], ass_ref[...])
    merged = (_sigmoid(g0_ref[...]) * _dot(ag, wg_ref[...])
              + _sigmoid(g1_ref[...]) * _dot(am, wm_ref[...])
              + _sigmoid(g2_ref[...]) * _dot(asb, ws_ref[...]))
    out = _dot(merged.astype(BF16), wo_ref[...])
    y = _ln_math(alpha * x_ref[...] + out, lg_ref[...], lb_ref[...])
    o_ref[...] = y
    or_ref[...] = y.reshape(or_ref.shape)


def merge_ln1(acts_p, acts_s, z_main, x, wg, wm, ws, wo, lg, lb, alpha):
    n, d = x.shape
    n_p = acts_p[0].shape[0]
    n_s = acts_s[0].shape[0]
    tm = _pick(np.gcd(n_p, n_s), (256, 128, 64))
    assert n_p + n_s == n
    np_tiles = n_p // tm
    row = lambda i: (i, 0)
    row_p = lambda i: (jnp.minimum(i, np_tiles - 1), 0)
    row_s = lambda i: (jnp.maximum(i - np_tiles, 0), 0)
    const = lambda i: (0, 0)
    gcb = C_GATE // d
    return pl.pallas_call(
        functools.partial(_merge_kernel, alpha=alpha, np_tiles=np_tiles),
        grid=(n // tm,),
        in_specs=[pl.BlockSpec((tm, d), row_p), pl.BlockSpec((tm, d), row_p), pl.BlockSpec((tm, d), row_p),
                  pl.BlockSpec((tm, d), row_s), pl.BlockSpec((tm, d), row_s), pl.BlockSpec((tm, d), row_s),
                  pl.BlockSpec((tm, d), lambda i: (i, gcb)),
                  pl.BlockSpec((tm, d), lambda i: (i, gcb + 1)),
                  pl.BlockSpec((tm, d), lambda i: (i, gcb + 2)),
                  pl.BlockSpec((tm, d), row),
                  pl.BlockSpec((d, d), const), pl.BlockSpec((d, d), const),
                  pl.BlockSpec((d, d), const), pl.BlockSpec((d, d), const),
                  pl.BlockSpec((1, d), const), pl.BlockSpec((1, d), const)],
        out_specs=[pl.BlockSpec((tm, d), row), pl.BlockSpec((tm, ROW_SUB, d // ROW_SUB), lambda i: (i, 0, 0))],
        out_shape=[jax.ShapeDtypeStruct((n, d), F32), jax.ShapeDtypeStruct((n, ROW_SUB, d // ROW_SUB), F32)],
        compiler_params=_cparams(("arbitrary",)),
        name="merge_ln1",
    )(*acts_p, *acts_s, z_main, z_main, z_main, x, wg, wm, ws, wo, lg.reshape(1, d), lb.reshape(1, d))


R_ROWS = 8 + N_EXPERTS


def _router_kernel(x_ref, w_ref, b_ref, u_ref, idx_ref, gate_ref, cnt_ref):
    i = pl.program_id(0)
    tm = x_ref.shape[0]

    @pl.when(i == 0)
    def _():
        cnt_ref[...] = jnp.zeros_like(cnt_ref)

    logits = _dot_nt(w_ref[...].astype(BF16), x_ref[...].astype(BF16)) + b_ref[...]

    g = [logits[j:j + 1, :] for j in range(N_GROUPS)]
    gmax = jnp.maximum(jnp.maximum(g[0], g[1]), jnp.maximum(g[2], g[3]))
    gsel = jnp.where(g[0] == gmax, 0, jnp.where(g[1] == gmax, 1, jnp.where(g[2] == gmax, 2, 3)))
    gden = (jnp.exp(g[0] - gmax) + jnp.exp(g[1] - gmax)) + (jnp.exp(g[2] - gmax) + jnp.exp(g[3] - gmax))
    gprob = 1.0 / gden
    e_in = jnp.where(gsel == 0, logits[8:16, :],
                     jnp.where(gsel == 1, logits[16:24, :],
                               jnp.where(gsel == 2, logits[24:32, :], logits[32:40, :])))
    ridx = lax.broadcasted_iota(jnp.int32, (EXPERTS_PER_GROUP, tm), 0)
    v1 = jnp.max(e_in, axis=0, keepdims=True)
    i1 = jnp.min(jnp.where(e_in == v1, ridx, EXPERTS_PER_GROUP), axis=0, keepdims=True)
    e2 = jnp.where(ridx == i1, -jnp.inf, e_in)
    v2 = jnp.max(e2, axis=0, keepdims=True)
    i2 = jnp.min(jnp.where(e2 == v2, ridx, EXPERTS_PER_GROUP), axis=0, keepdims=True)
    t = jnp.exp(v2 - v1)
    p1 = 1.0 / (1.0 + t)
    gate_ref[0:1, :] = p1 * gprob
    gate_ref[1:2, :] = (t * p1) * gprob
    gate_ref[2:8, :] = jnp.zeros((6, tm), F32)
    eid1 = gsel * EXPERTS_PER_GROUP + i1
    eid2 = gsel * EXPERTS_PER_GROUP + i2

    eidx = lax.broadcasted_iota(jnp.int32, (N_EXPERTS, tm), 0)
    oh1 = eidx == eid1
    oh2 = eidx == eid2
    oh1f = jnp.where(oh1, 1.0, 0.0)
    oh2f = jnp.where(oh2, 1.0, 0.0)
    umat = u_ref[...]
    pre1 = _dot(oh1f.astype(BF16), umat)
    pre2 = _dot(oh2f.astype(BF16), umat)
    cnt1 = jnp.sum(oh1f, axis=1, keepdims=True)
    cnt2 = jnp.sum(oh2f, axis=1, keepdims=True)
    base = cnt_ref[:, 0:1]
    rank1 = jnp.sum(jnp.where(oh1, base + pre1, 0.0), axis=0, keepdims=True)
    rank2 = jnp.sum(jnp.where(oh2, base + cnt1 + pre2, 0.0), axis=0, keepdims=True)
    idx_ref[0:1, :] = eid1
    idx_ref[1:2, :] = eid2
    idx_ref[2:3, :] = rank1.astype(jnp.int32)
    idx_ref[3:4, :] = rank2.astype(jnp.int32)
    idx_ref[4:8, :] = jnp.zeros((4, tm), jnp.int32)
    cnt_ref[...] = cnt_ref[...] + (cnt1 + cnt2)


def moe_router(x, w_r, b_r, umat):
    n, d = x.shape
    tm = umat.shape[0]
    return pl.pallas_call(
        _router_kernel,
        grid=(n // tm,),
        in_specs=[pl.BlockSpec((tm, d), lambda i: (i, 0)),
                  pl.BlockSpec((R_ROWS, d), lambda i: (0, 0)),
                  pl.BlockSpec((R_ROWS, 1), lambda i: (0, 0)),
                  pl.BlockSpec((tm, tm), lambda i: (0, 0))],
        out_specs=[pl.BlockSpec((8, tm), lambda i: (0, i)),
                   pl.BlockSpec((8, tm), lambda i: (0, i)),
                   pl.BlockSpec((N_EXPERTS, 128), lambda i: (0, 0))],
        out_shape=[jax.ShapeDtypeStruct((8, n), jnp.int32),
                   jax.ShapeDtypeStruct((8, n), F32),
                   jax.ShapeDtypeStruct((N_EXPERTS, 128), F32)],
        compiler_params=_cparams(("arbitrary",)),
        name="moe_router",
    )(x, w_r, b_r, umat)


def _row_copy(src_hbm, dst_vmem, src_row, dst_row, sem):
    return pltpu.make_async_copy(src_hbm.at[pl.ds(src_row, 1)], dst_vmem.at[pl.ds(dst_row, 1)], sem)


GATHER_UNROLL = 8


def _start_row_gather(idx_ref, base, src_hbm, dst_vmem, nrows, sem):
    def start(r, carry):
        _row_copy(src_hbm, dst_vmem, idx_ref[base + r], r, sem).start()
        return carry

    lax.fori_loop(0, nrows, start, 0, unroll=GATHER_UNROLL)


def _wait_row_gather(src_hbm, dst_vmem, nrows, sem):
    def wait(r, carry):
        _row_copy(src_hbm, dst_vmem, 0, r, sem).wait()
        return carry

    lax.fori_loop(0, nrows, wait, 0, unroll=GATHER_UNROLL)


def _moe_kernel(be_ref, tok_ref, x_hbm, wg_ref, wu_ref, wd_ref, o_ref,
                xbuf, wgb, wub, wdb, sem):
    i = pl.program_id(0)
    nb = pl.num_programs(0)
    slot = i % 2

    @pl.when(i == 0)
    def _():
        _start_row_gather(tok_ref, 0, x_hbm, xbuf.at[0], MOE_BLK, sem.at[0])

    prev_e = be_ref[jnp.maximum(i - 1, 0)]

    @pl.when((i == 0) | (be_ref[i] != prev_e))
    def _():
        wgb[...] = wg_ref[0, 0].astype(BF16)
        wub[...] = wu_ref[0, 0].astype(BF16)
        wdb[...] = wd_ref[0, 0].astype(BF16)

    _wait_row_gather(x_hbm, xbuf.at[slot], MOE_BLK, sem.at[slot])
    xb = xbuf[slot].reshape(MOE_BLK, D_MODEL).astype(BF16)
    nxt = jnp.minimum(i + 1, nb - 1) * MOE_BLK
    for r in range(MOE_BLK):
        _row_copy(x_hbm, xbuf.at[1 - slot], tok_ref[nxt + r], r, sem.at[1 - slot]).start()
    a = _dot(xb, wgb[...])
    hmid = (a * _sigmoid(a)) * _dot(xb, wub[...])
    o_ref[...] = _dot(hmid.astype(BF16), wdb[...]).reshape(o_ref.shape)

    @pl.when(i == nb - 1)
    def _():
        _wait_row_gather(x_hbm, xbuf.at[1 - slot], MOE_BLK, sem.at[1 - slot])


def moe_experts(x, block_e, tok_pad, w_gate, w_up, w_down, *, layer):
    n = x.shape[0]
    d = D_MODEL
    rs = (ROW_SUB, d // ROW_SUB)
    nb = block_e.shape[0]
    wmap = lambda i, be, tok: (layer, be[i], 0, 0)
    grid_spec = pltpu.PrefetchScalarGridSpec(
        num_scalar_prefetch=2,
        grid=(nb,),
        in_specs=[pl.BlockSpec(memory_space=pl.ANY),
                  pl.BlockSpec((1, 1, d, D_EXPERT), wmap),
                  pl.BlockSpec((1, 1, d, D_EXPERT), wmap),
                  pl.BlockSpec((1, 1, D_EXPERT, d), wmap)],
        out_specs=pl.BlockSpec((MOE_BLK,) + rs, lambda i, be, tok: (i, 0, 0)),
        scratch_shapes=[pltpu.VMEM((2, MOE_BLK) + rs, F32),
                        pltpu.VMEM((d, D_EXPERT), BF16),
                        pltpu.VMEM((d, D_EXPERT), BF16),
                        pltpu.VMEM((D_EXPERT, d), BF16),
                        pltpu.SemaphoreType.DMA((2,))],
    )
    return pl.pallas_call(
        _moe_kernel,
        grid_spec=grid_spec,
        out_shape=jax.ShapeDtypeStruct((nb * MOE_BLK,) + rs, F32),
        compiler_params=_cparams(("arbitrary",)),
        name="moe_experts",
    )(block_e, tok_pad, x, w_gate, w_up, w_down)


def _combine_kernel(dest_ref, yb_hbm, x_ref, gate_ref, lg_ref, lb_ref, o_ref, ob_ref, ybuf, sem, *, alpha, n):
    i = pl.program_id(0)
    nt = pl.num_programs(0)
    tm = x_ref.shape[0]
    slot = i % 2

    @pl.when(i == 0)
    def _():
        for k in range(TOP_K):
            _start_row_gather(dest_ref, k * n, yb_hbm, ybuf.at[0, k], tm, sem.at[0, k])

    for k in range(TOP_K):
        _wait_row_gather(yb_hbm, ybuf.at[slot, k], tm, sem.at[slot, k])
    gate = gate_ref[...]
    d = x_ref.shape[1]
    y0 = ybuf[slot, 0].reshape(tm, d)
    y1 = ybuf[slot, 1].reshape(tm, d)
    nxt = jnp.minimum(i + 1, nt - 1) * tm
    for k in range(TOP_K):
        for r in range(tm):
            _row_copy(yb_hbm, ybuf.at[1 - slot, k], dest_ref[k * n + nxt + r], r, sem.at[1 - slot, k]).start()
    y = y0 * gate[:, 0:1] + y1 * gate[:, 1:2]
    out = _ln_math(alpha * x_ref[...] + y, lg_ref[...], lb_ref[...])
    o_ref[...] = out
    ob_ref[...] = out.astype(BF16)

    @pl.when(i == nt - 1)
    def _():
        for k in range(TOP_K):
            _wait_row_gather(yb_hbm, ybuf.at[1 - slot, k], tm, sem.at[1 - slot, k])


def moe_combine_ln2(x, yb, dest, gate_cols, lg, lb, alpha):
    n, d = x.shape
    tm = _pick(n, (256, 128, 64))
    grid_spec = pltpu.PrefetchScalarGridSpec(
        num_scalar_prefetch=1,
        grid=(n // tm,),
        in_specs=[pl.BlockSpec(memory_space=pl.ANY),
                  pl.BlockSpec((tm, d), lambda i, dest: (i, 0)),
                  pl.BlockSpec((tm, TOP_K), lambda i, dest: (i, 0)),
                  pl.BlockSpec((1, d), lambda i, dest: (0, 0)),
                  pl.BlockSpec((1, d), lambda i, dest: (0, 0))],
        out_specs=[pl.BlockSpec((tm, d), lambda i, dest: (i, 0)), pl.BlockSpec((tm, d), lambda i, dest: (i, 0))],
        scratch_shapes=[pltpu.VMEM((2, TOP_K, tm, ROW_SUB, d // ROW_SUB), F32),
                        pltpu.SemaphoreType.DMA((2, TOP_K))],
    )
    return pl.pallas_call(
        functools.partial(_combine_kernel, alpha=alpha, n=n),
        grid_spec=grid_spec,
        out_shape=[jax.ShapeDtypeStruct((n, d), F32), jax.ShapeDtypeStruct((n, d), BF16)],
        compiler_params=_cparams(("arbitrary",)),
        name="moe_combine_ln2",
    )(dest, yb, x, gate_cols, lg.reshape(1, d), lb.reshape(1, d))


def _tri_lower(n):
    t = np.arange(n)
    return (t[None, :] <= t[:, None]).astype(np.float32)


def _strict_upper(n):
    t = np.arange(n)
    return (t[:, None] < t[None, :]).astype(np.float32)


def kernel(x_prompt, x_sample, state_gla, state_ml_C, state_ml_n, state_ml_m, state_ml_conv, cache_sb_k, cache_sb_v, meta_tokens, ln_in_g, ln_in_b, w_in, gla_w_a2, gla_b_a, gla_norm_g, ml_conv_w, ml_conv_b, ml_wq, ml_wk, ml_wv, ml_b_i, ml_b_f, ml_norm_g, w_br_gla, w_br_ml, w_br_sb, w_out, ln1_g, ln1_b, w_rg, b_rg, w_re, b_re, w_e_gate, w_e_up, w_e_down, ln2_g, ln2_b):
    D = D_MODEL
    B, SEQ, _ = x_prompt.shape
    DB, DS, _ = x_sample.shape
    depth = w_in.shape[0]
    n_cache = cache_sb_k.shape[2]
    assert DS == CHUNK and SEQ % 128 == 0 and x_prompt.shape[2] == D
    alpha = (2.0 * depth) ** 0.25
    TP = PAD_FRONT + N_META + SEQ
    NP = B * TP
    NS = DB * DS
    N = NP + NS
    assert TP % CHUNK_PROMPT == 0 and NP % CHUNK == 0
    nchunk_p = TP // CHUNK_PROMPT

    meta = jnp.broadcast_to(meta_tokens.astype(F32)[None], (B, N_META, D))
    xp = jnp.concatenate([jnp.zeros((B, PAD_FRONT, D), F32), meta, x_prompt], axis=1).reshape(NP, D)
    x = jnp.concatenate([xp, x_sample.reshape(NS, D)], axis=0)
    x, xb = layer_norm_rows(x, ln_in_g, ln_in_b)

    u_sb = jnp.asarray(_strict_upper(SB_BK).T, BF16)
    tm_r = _pick(N, (256, 128, 64))
    u_r = jnp.asarray(_strict_upper(tm_r), BF16)

    A = N * TOP_K
    NB = -(-(A + N_EXPERTS * (MOE_BLK - 1)) // MOE_BLK)

    outs_p = [[] for _ in range(5)]
    outs_s = [[] for _ in range(7)]
    z_kvs = []
    for l in range(depth):
        wi = w_in[l]
        w_main = jnp.concatenate([wi[:, 0:3072], wi[:, 3088:4112], wi[:, 4120:5144], wi[:, 5144:6168],
                                  wi[:, 8216:]], axis=1).astype(BF16)
        w_kv = wi[:, 6168:8216].astype(BF16)
        w_small = jnp.concatenate([wi[:, 3072:3088], wi[:, 4112:4120],
                                   jnp.zeros((D, 128 - 24), F32)], axis=1).astype(BF16)
        z_main = in_projection(xb, w_main, 1024, "in_proj_main")
        z_kv, kvb = in_projection(xb, w_kv, 1024, "in_proj_kv", with_bf16=True)
        z_small = in_projection(xb, w_small, 128, "in_proj_small")
        z_kvs.append(z_kv)

        wa = jnp.concatenate([gla_w_a2[l], jnp.zeros((128 - GLA_RANK, GLA_HEADS * GLA_DK), F32)], axis=0).astype(BF16)
        ba = gla_b_a[l].reshape(1, -1)
        ng = gla_norm_g[l].reshape(1, -1)
        s0_p = jnp.zeros((B, GLA_HEADS, GLA_DV, GLA_DK), F32)
        s0_s = jnp.swapaxes(state_gla[l].astype(F32), -1, -2)
        geo_p = dict(row0=0, nb=B, nchunk=nchunk_p, chunk=CHUNK_PROMPT)
        geo_s = dict(row0=NP, nb=DB, nchunk=1, chunk=CHUNK)
        gla_p = _gla_call_parts(z_main, z_small, wa, ba, ng, s0_p, **geo_p)
        gla_s = _gla_call_parts(z_main, z_small, wa, ba, ng, s0_s, **geo_s)

        gbias = jnp.zeros((1, 128), F32).at[0, L_MI:L_MI + ML_HEADS].set(ml_b_i[l]).at[0, L_MF:L_MF + ML_HEADS].set(ml_b_f[l])
        cw = ml_conv_w[l]
        cb = ml_conv_b[l].reshape(1, -1)
        wq = ml_wq[l].astype(BF16)
        wk = ml_wk[l].astype(BF16)
        wv = ml_wv[l].astype(BF16)
        mng = ml_norm_g[l].reshape(1, -1)
        W = ML_HEADS * ML_DH
        conv0_p = jnp.zeros((B, 8, W), F32)
        conv0_s = jnp.concatenate([jnp.zeros((DB, 8 - (ML_CONV - 1), W), F32), state_ml_conv[l].astype(F32)], axis=1)
        m0_s = jnp.zeros((DB, 1, 128), F32).at[:, 0, L_MF:L_MF + ML_HEADS].set(state_ml_m[l].astype(F32))
        ml_p = _mlstm_call_parts(
            z_main, z_small, gbias, cw, cb, wq, wk, wv, mng, conv0_p,
            jnp.zeros((B, ML_HEADS, ML_DH, ML_DH), F32), jnp.zeros((B, ML_HEADS, ML_DH), F32),
            jnp.zeros((B, 1, 128), F32), **geo_p)
        ml_s = _mlstm_call_parts(
            z_main, z_small, gbias, cw, cb, wq, wk, wv, mng, conv0_s,
            state_ml_C[l].astype(F32), state_ml_n[l].astype(F32), m0_s, **geo_s)
        a_gla_p, sT_p, a_ml_p, c_p, n_p, m_p = recurrent_branches(
            gla_p, ml_p, nb=B, nchunk=nchunk_p, chunk=CHUNK_PROMPT, n_pad=PAD_FRONT)
        a_gla_s, sT_s, a_ml_s, c_s, n_s, m_s = recurrent_branches(
            gla_s, ml_s, nb=DB, nchunk=1, chunk=CHUNK, n_pad=0)

        a_sb_p = sb_attention_prompt(z_main, kvb, u_sb, nb=B, tp=TP, bq=SB_BK, n_pad=PAD_FRONT)
        a_sb_s = sb_attention_sample(z_main, z_kv, cache_sb_k.astype(F32), cache_sb_v.astype(F32),
                                     u_sb, layer=l, row0=NP, nb=DB)

        x1, x1r = merge_ln1((a_gla_p, a_ml_p, a_sb_p), (a_gla_s, a_ml_s, a_sb_s), z_main, x,
                       w_br_gla[l].astype(BF16), w_br_ml[l].astype(BF16), w_br_sb[l].astype(BF16),
                       w_out[l].astype(BF16), ln1_g[l], ln1_b[l], alpha)

        w_r = jnp.concatenate([w_rg[l].T, jnp.zeros((8 - N_GROUPS, D), F32), w_re[l].T], axis=0)
        b_r = jnp.concatenate([b_rg[l], jnp.zeros((8 - N_GROUPS,), F32), b_re[l]]).reshape(R_ROWS, 1)
        idx, gates, cnt = moe_router(x1, w_r, b_r, u_r)
        counts = cnt[:, 0].astype(jnp.int32)
        padded = (counts + MOE_BLK - 1) // MOE_BLK * MOE_BLK
        ends_pad = jnp.cumsum(padded)
        start_pad = ends_pad - padded
        eids = jnp.arange(N_EXPERTS, dtype=jnp.int32)
        start_of = jnp.sum(jnp.where(idx[0:2, :, None] == eids, start_pad, 0), axis=-1)
        dest = (start_of + idx[2:4]).reshape(-1)
        tok = jnp.tile(jnp.arange(N, dtype=jnp.int32), TOP_K)
        tok_pad = jnp.zeros((NB * MOE_BLK,), jnp.int32).at[dest].set(tok)
        blk_lo = jnp.arange(NB, dtype=jnp.int32) * MOE_BLK
        block_e = jnp.minimum(jnp.sum((ends_pad[None, :] <= blk_lo[:, None]).astype(jnp.int32), axis=-1),
                              N_EXPERTS - 1)
        yb = moe_experts(x1r, block_e, tok_pad, w_e_gate, w_e_up, w_e_down, layer=l)
        x, xb = moe_combine_ln2(x1, yb, dest, gates[0:2].T, ln2_g[l], ln2_b[l], alpha)

        nct = ML_CONV - 1
        conv_p = jnp.stack([z_main[(b + 1) * TP - nct:(b + 1) * TP, C_MU:C_MU + W] for b in range(B)], axis=0)
        conv_s = z_main[NP:, C_MU:C_MU + W].reshape(DB, DS, W)[:, DS - nct:]
        zkv_s = z_kv[NP:].reshape(DB, DS, 2 * D)
        kv_shape = (SB_HEADS, SB_DH)
        outs_p[0].append(jnp.swapaxes(sT_p, -1, -2))
        outs_s[0].append(jnp.swapaxes(sT_s, -1, -2))
        outs_p[1].append(c_p); outs_s[1].append(c_s)
        outs_p[2].append(n_p); outs_s[2].append(n_s)
        outs_p[3].append(m_p[:, 0, L_MF:L_MF + ML_HEADS]); outs_s[3].append(m_s[:, 0, L_MF:L_MF + ML_HEADS])
        outs_p[4].append(conv_p); outs_s[4].append(conv_s)
        outs_s[5].append(zkv_s[:, :, C_SK:C_SK + D].reshape(DB, DS, *kv_shape))
        outs_s[6].append(zkv_s[:, :, C_SV:C_SV + D].reshape(DB, DS, *kv_shape))

    sbk_p, sbv_p = kv_state_outputs(z_kvs, nb=B, tp=TP, n_front=PAD_FRONT)
    sp = [jnp.stack(a, axis=0) for a in outs_p]
    ss = [jnp.stack(a, axis=0) for a in outs_s]
    y_prompt = x[:NP].reshape(B, TP, D)[:, PAD_FRONT + N_META:]
    y_sample = x[NP:].reshape(DB, DS, D)
    return (y_prompt, y_sample, sp[0], ss[0], sp[1], ss[1], sp[2], ss[2], sp[3], ss[3],
            sp[4], ss[4], sbk_p, ss[5], sbv_p, ss[6])
```

```python
import functools

import numpy as np
import jax
import jax.numpy as jnp
from jax import lax
from jax.experimental import pallas as pl
from jax.experimental.pallas import tpu as pltpu

F32 = jnp.float32
BF16 = jnp.bfloat16

D_MODEL = 1024
N_META = 16
CHUNK = 64
CHUNK_PROMPT = 128
PAD_FRONT = 240
GLA_HEADS, GLA_DK, GLA_DV, GLA_RANK, GLA_TAU = 4, 128, 256, 16, 16.0
ML_HEADS, ML_DH, ML_CONV = 4, 256, 4
SB_HEADS, SB_DH = 8, 128
N_GROUPS, EXPERTS_PER_GROUP, TOP_K = 4, 8, 2
N_EXPERTS = N_GROUPS * EXPERTS_PER_GROUP
D_EXPERT = D_MODEL // 2
MOE_BLK = 256
ROW_SUB = 8
LN_EPS = 1e-5
NEG_BIG = -1e30
SB_SKIP = 110.0
SB_BK = 256

C_GQ, C_GK, C_GV, C_GR = 0, 512, 1024, 2048
C_MU, C_MO = 3072, 4096
C_SQ = 5120
C_GATE = 6144
Z_MAIN = 9216
C_SK, C_SV = 0, 1024
SB_HG = 4
L_LR, L_MI, L_MF = 0, 16, 20

VMEM_LIMIT = 48 * 1024 * 1024


def _cparams(sem):
    return pltpu.CompilerParams(dimension_semantics=sem, vmem_limit_bytes=VMEM_LIMIT)


def _pick(n, cands):
    for c in cands:
        if n % c == 0:
            return c
    raise ValueError(f"no tile in {cands} divides {n}")


def _log_sigmoid(x):
    return jnp.minimum(x, 0.0) - jnp.log1p(jnp.exp(-jnp.abs(x)))


def _sigmoid(x):
    return 1.0 / (1.0 + jnp.exp(-x))


def _split2(x):
    hi = x.astype(BF16)
    lo = (x - hi.astype(F32)).astype(BF16)
    return hi, lo


def _split3(x):
    hi = x.astype(BF16)
    r = x - hi.astype(F32)
    mid = r.astype(BF16)
    lo = (r - mid.astype(F32)).astype(BF16)
    return hi, mid, lo


def _dot(a, b):
    return jnp.dot(a, b, preferred_element_type=F32)


def _dot_nt(a, b):
    return lax.dot_general(a, b, (((1,), (1,)), ((), ())), preferred_element_type=F32)


def _dot_tn(a, b):
    return lax.dot_general(a, b, (((0,), (0,)), ((), ())), preferred_element_type=F32)


def _ln_math(x, g, b):
    mu = jnp.mean(x, axis=-1, keepdims=True)
    xc = x - mu
    var = jnp.mean(xc * xc, axis=-1, keepdims=True)
    return xc * lax.rsqrt(var + LN_EPS) * g + b


def _ln_kernel(x_ref, g_ref, b_ref, o_ref, ob_ref):
    y = _ln_math(x_ref[...], g_ref[...], b_ref[...])
    o_ref[...] = y
    ob_ref[...] = y.astype(BF16)


def layer_norm_rows(x, g, b):
    n, d = x.shape
    tm = _pick(n, (512, 256, 128, 64))
    return pl.pallas_call(
        _ln_kernel,
        grid=(n // tm,),
        in_specs=[pl.BlockSpec((tm, d), lambda i: (i, 0)),
                  pl.BlockSpec((1, d), lambda i: (0, 0)),
                  pl.BlockSpec((1, d), lambda i: (0, 0))],
        out_specs=[pl.BlockSpec((tm, d), lambda i: (i, 0)), pl.BlockSpec((tm, d), lambda i: (i, 0))],
        out_shape=[jax.ShapeDtypeStruct((n, d), F32), jax.ShapeDtypeStruct((n, d), BF16)],
        compiler_params=_cparams(("arbitrary",)),
        name="ln_in",
    )(x, g.reshape(1, d), b.reshape(1, d))


def _mm_kernel(x_ref, w_ref, o_ref):
    o_ref[...] = _dot(x_ref[...], w_ref[...])


def _mm2_kernel(x_ref, w_ref, o_ref, ob_ref):
    z = _dot(x_ref[...], w_ref[...])
    o_ref[...] = z
    ob_ref[...] = z.astype(BF16)


def in_projection(x, w_bf16, tn, name, with_bf16=False):
    n, d = x.shape
    cols = w_bf16.shape[1]
    tm = _pick(n, (896, 768, 512, 384, 256, 128, 64))
    out_spec = pl.BlockSpec((tm, tn), lambda j, i: (i, j))
    return pl.pallas_call(
        _mm2_kernel if with_bf16 else _mm_kernel,
        grid=(cols // tn, n // tm),
        in_specs=[pl.BlockSpec((tm, d), lambda j, i: (i, 0)),
                  pl.BlockSpec((d, tn), lambda j, i: (0, j))],
        out_specs=[out_spec, out_spec] if with_bf16 else out_spec,
        out_shape=([jax.ShapeDtypeStruct((n, cols), F32), jax.ShapeDtypeStruct((n, cols), BF16)]
                   if with_bf16 else jax.ShapeDtypeStruct((n, cols), F32)),
        compiler_params=_cparams(("arbitrary", "arbitrary")),
        name=name,
    )(x, w_bf16)


def _gla_levels(L):
    return tuple(L >> i for i in range(1, L.bit_length()))


def _gla_level_matrix(L):
    t = np.arange(L)[:, None]
    j = np.arange(L)[None, :]
    blocks = []
    for m in (L,) + _gla_levels(L):
        blocks.append(((j <= t) & (j // m == t // m)).astype(np.float32))
    for m in (L,) + _gla_levels(L):
        blocks.append(((j > t) & (j // m == t // m)).astype(np.float32))
    return np.concatenate(blocks, axis=0)


def _gla_kernel(q_ref, k_ref, v_ref, r_ref, lr_ref, wa_ref, ba_ref, ng_ref, lvl_ref, s0_ref,
                o_ref, s_ref, *, n_pad, L, phase):
    c = pl.program_id(1)
    GLA_LEVELS = _gla_levels(L)

    if phase == "init":
        @pl.when(c == 0)
        def _():
            s_ref[...] = s0_ref[...]
        return

    row = c * L + lax.broadcasted_iota(jnp.int32, (L, 1), 0)
    valid = row >= n_pad
    x = _dot(lr_ref[...].astype(BF16), wa_ref[...]) + ba_ref[...]
    la = jnp.where(valid, _log_sigmoid(x) * (1.0 / GLA_TAU), 0.0)
    la_hi, la_lo = _split2(la)
    lvl = lvl_ref[...]
    ex = jnp.exp(_dot(lvl, la_hi) + _dot(lvl, la_lo))
    nlev = len(GLA_LEVELS) + 1

    ti = lax.broadcasted_iota(jnp.int32, (L, L), 0)
    si = lax.broadcasted_iota(jnp.int32, (L, L), 1)
    level_masks = []
    for m in GLA_LEVELS:
        sh = m.bit_length() - 1
        tb = lax.shift_right_logical(ti, sh)
        sb = lax.shift_right_logical(si, sh)
        level_masks.append((lax.shift_right_logical(tb, 1) == lax.shift_right_logical(sb, 1))
                           & ((tb & 1) == 1) & ((sb & 1) == 0))
    eye = ti == si

    outs = []
    states = []
    for h in range(GLA_HEADS):
        dk = slice(h * GLA_DK, (h + 1) * GLA_DK)
        dv = slice(h * GLA_DV, (h + 1) * GLA_DV)
        q = q_ref[:, dk] * (GLA_DK ** -0.5)
        k = jnp.where(valid, k_ref[:, dk], 0.0)
        v = jnp.where(valid, v_ref[:, dv], 0.0).astype(BF16)
        att = jnp.where(eye, _dot_nt(q.astype(BF16), k.astype(BF16)), 0.0)
        for lev in range(len(GLA_LEVELS)):
            qe = ex[(1 + lev) * L:(2 + lev) * L, dk]
            ke = ex[(nlev + 1 + lev) * L:(nlev + 2 + lev) * L, dk]
            a = _dot_nt((q * qe).astype(BF16), (k * ke).astype(BF16))
            att = jnp.where(level_masks[lev], a, att)
        eb = ex[0:L, dk]
        st = s_ref[0, h]
        o = _dot(att.astype(BF16), v) + _dot_nt((q * eb).astype(BF16), st.astype(BF16))
        ke_end = ex[nlev * L:(nlev + 1) * L, dk]
        states.append(st * eb[L - 1:L, :] + _dot_tn(v, (k * ke_end).astype(BF16)))
        mu = jnp.mean(o, axis=-1, keepdims=True)
        oc = o - mu
        var = jnp.mean(oc * oc, axis=-1, keepdims=True)
        on = oc * lax.rsqrt(var + LN_EPS) * ng_ref[...]
        r = r_ref[:, dv]
        outs.append((on * (r * _sigmoid(r))).astype(BF16))
    o_ref[...] = jnp.concatenate(outs, axis=-1)
    s_ref[0] = jnp.stack(states, axis=0)


def _gla_call_parts(z_main, z_small, wa, ba, ng, s0t, *, row0, nb, nchunk, chunk):
    L = chunk
    assert row0 % L == 0
    lvl = jnp.asarray(_gla_level_matrix(L), BF16)
    rb0 = row0 // L
    hk = GLA_HEADS * GLA_DK
    hv = GLA_HEADS * GLA_DV

    def rows(colblk):
        return lambda b, c: (rb0 + b * nchunk + c, colblk)

    return dict(
        args=(z_main, z_main, z_main, z_main, z_small, wa, ba, ng, lvl, s0t),
        in_specs=[pl.BlockSpec((L, hk), rows(C_GQ // hk)),
                  pl.BlockSpec((L, hk), rows(C_GK // hk)),
                  pl.BlockSpec((L, hv), rows(C_GV // hv)),
                  pl.BlockSpec((L, hv), rows(C_GR // hv)),
                  pl.BlockSpec((L, 128), rows(0)),
                  pl.BlockSpec((128, hk), lambda b, c: (0, 0)),
                  pl.BlockSpec((1, hk), lambda b, c: (0, 0)),
                  pl.BlockSpec((1, GLA_DV), lambda b, c: (0, 0)),
                  pl.BlockSpec(lvl.shape, lambda b, c: (0, 0)),
                  pl.BlockSpec((1, GLA_HEADS, GLA_DV, GLA_DK), lambda b, c: (b, 0, 0, 0))],
        out_specs=[pl.BlockSpec((L, hv), lambda b, c: (b * nchunk + c, 0)),
                   pl.BlockSpec((1, GLA_HEADS, GLA_DV, GLA_DK), lambda b, c: (b, 0, 0, 0))],
        out_shape=[jax.ShapeDtypeStruct((nb * nchunk * L, hv), BF16),
                   jax.ShapeDtypeStruct((nb, GLA_HEADS, GLA_DV, GLA_DK), F32)],
        scratch=[])


def _mlstm_kernel(u_ref, mo_ref, g_ref, gb_ref, cw_ref, cb_ref, wq_ref, wk_ref, wv_ref, ng_ref, tri_ref,
                  conv0_ref, c0_ref, n0_ref, m0_ref,
                  o_ref, c_ref, n_ref, m_ref, ext_ref, *, n_pad, L, phase):
    c = pl.program_id(1)
    W = ML_HEADS * ML_DH

    if phase == "init":
        @pl.when(c == 0)
        def _():
            ext_ref[0:8, :] = conv0_ref[0]
            c_ref[...] = c0_ref[...]
            n_ref[...] = n0_ref[...]
            m_ref[...] = m0_ref[...]
        return

    row = c * L + lax.broadcasted_iota(jnp.int32, (L, 1), 0)
    valid = row >= n_pad

    u = jnp.where(valid, u_ref[...], 0.0)
    ext_ref[8:8 + L, :] = u
    uc = cb_ref[...]
    for i in range(ML_CONV):
        uc = uc + ext_ref[8 - (ML_CONV - 1) + i:8 - (ML_CONV - 1) + i + L, :] * cw_ref[i:i + 1, :]
    ext_ref[0:8, :] = u[L - 8:L, :]
    ucs = (uc * _sigmoid(uc)).astype(BF16)
    ub = u_ref[...].astype(BF16)

    g = g_ref[...] + gb_ref[...]
    li = jnp.where(valid, g, NEG_BIG)
    lf = jnp.where(valid, _log_sigmoid(g), 0.0)
    tri = tri_ref[...]
    lf_hi, lf_lo = _split2(lf)
    b_col = _dot(tri, lf_hi) + _dot(tri, lf_lo)
    a_col = b_col + m_ref[0]
    li_row = li.T
    b_row = b_col.T

    ti = lax.broadcasted_iota(jnp.int32, (L, L), 0)
    si = lax.broadcasted_iota(jnp.int32, (L, L), 1)
    causal = si <= ti
    lane = lax.broadcasted_iota(jnp.int32, (1, 128), 1)
    m_vec = m_ref[0]
    m_out = m_vec
    outs, c_new, n_new = [], [], []

    for h in range(ML_HEADS):
        d = slice(h * ML_DH, (h + 1) * ML_DH)
        q = _dot(ucs[:, d], wq_ref[h]).astype(BF16)
        k = _dot(ucs[:, d], wk_ref[h]) * (ML_DH ** -0.5)
        v = _dot(ub[:, d], wv_ref[h]).astype(BF16)
        lf_l = L_MF + h
        li_l = L_MI + h
        bc = b_col[:, lf_l:lf_l + 1]
        ac = a_col[:, lf_l:lf_l + 1]
        lic = li[:, li_l:li_l + 1]
        dmat = jnp.where(causal, bc - b_row[lf_l:lf_l + 1, :] + li_row[li_l:li_l + 1, :], NEG_BIG)
        mt = jnp.maximum(ac, jnp.max(dmat, axis=-1, keepdims=True))
        w_intra = jnp.exp(dmat - mt)
        w_inter = jnp.exp(ac - mt)
        qk = _dot_nt(q, k.astype(BF16)) * w_intra
        cmat = c_ref[0, h]
        nrow = n_ref[0, h:h + 1, :]
        num = _dot(qk.astype(BF16), v) + w_inter * _dot(q, cmat.astype(BF16))
        qn = jnp.sum(q.astype(F32) * nrow.astype(BF16).astype(F32), axis=-1, keepdims=True)
        den = jnp.sum(qk, axis=-1, keepdims=True) + w_inter * qn
        hh = num / jnp.maximum(jnp.abs(den), jnp.exp(-mt))
        m_new = mt[L - 1:L, :]
        b_end = bc[L - 1:L, :]
        wk = jnp.exp(b_end - bc + lic - m_new)
        ws = jnp.exp(b_end + m_vec[:, lf_l:lf_l + 1] - m_new)
        kw = k * wk
        c_new.append(ws * cmat + _dot_tn(kw.astype(BF16), v))
        n_new.append(ws * nrow + jnp.sum(kw, axis=0, keepdims=True))
        m_out = jnp.where(lane == lf_l, m_new, m_out)
        mu = jnp.mean(hh, axis=-1, keepdims=True)
        hc = hh - mu
        var = jnp.mean(hc * hc, axis=-1, keepdims=True)
        hn = hc * lax.rsqrt(var + LN_EPS) * ng_ref[...]
        outs.append((hn * _sigmoid(mo_ref[:, d])).astype(BF16))
    o_ref[...] = jnp.concatenate(outs, axis=-1)
    c_ref[0] = jnp.stack(c_new, axis=0)
    n_ref[0] = jnp.concatenate(n_new, axis=0)
    m_ref[0] = m_out


def _mlstm_call_parts(z_main, z_small, gbias, cw, cb, wq, wk, wv, ng, conv0, c0, n0, m0,
                      *, row0, nb, nchunk, chunk):
    L = chunk
    assert row0 % L == 0
    tri = jnp.asarray(_tri_lower(L), BF16)
    rb0 = row0 // L
    W = ML_HEADS * ML_DH

    def rows(colblk):
        return lambda b, c: (rb0 + b * nchunk + c, colblk)

    const2 = lambda b, c: (0, 0)
    const3 = lambda b, c: (0, 0, 0)
    return dict(
        args=(z_main, z_main, z_small, gbias, cw, cb, wq, wk, wv, ng, tri, conv0, c0, n0, m0),
        in_specs=[pl.BlockSpec((L, W), rows(C_MU // W)),
                  pl.BlockSpec((L, W), rows(C_MO // W)),
                  pl.BlockSpec((L, 128), rows(0)),
                  pl.BlockSpec((1, 128), const2),
                  pl.BlockSpec((ML_CONV, W), const2),
                  pl.BlockSpec((1, W), const2),
                  pl.BlockSpec((ML_HEADS, ML_DH, ML_DH), const3),
                  pl.BlockSpec((ML_HEADS, ML_DH, ML_DH), const3),
                  pl.BlockSpec((ML_HEADS, ML_DH, ML_DH), const3),
                  pl.BlockSpec((1, ML_DH), const2),
                  pl.BlockSpec((L, L), const2),
                  pl.BlockSpec((1, 8, W), lambda b, c: (b, 0, 0)),
                  pl.BlockSpec((1, ML_HEADS, ML_DH, ML_DH), lambda b, c: (b, 0, 0, 0)),
                  pl.BlockSpec((1, ML_HEADS, ML_DH), lambda b, c: (b, 0, 0)),
                  pl.BlockSpec((1, 1, 128), lambda b, c: (b, 0, 0))],
        out_specs=[pl.BlockSpec((L, W), lambda b, c: (b * nchunk + c, 0)),
                   pl.BlockSpec((1, ML_HEADS, ML_DH, ML_DH), lambda b, c: (b, 0, 0, 0)),
                   pl.BlockSpec((1, ML_HEADS, ML_DH), lambda b, c: (b, 0, 0)),
                   pl.BlockSpec((1, 1, 128), lambda b, c: (b, 0, 0))],
        out_shape=[jax.ShapeDtypeStruct((nb * nchunk * L, W), BF16),
                   jax.ShapeDtypeStruct((nb, ML_HEADS, ML_DH, ML_DH), F32),
                   jax.ShapeDtypeStruct((nb, ML_HEADS, ML_DH), F32),
                   jax.ShapeDtypeStruct((nb, 1, 128), F32)],
        scratch=[pltpu.VMEM((8 + L, W), F32)])


def _recurrent_kernel(*refs, n_gla_in, n_ml_in, n_pad, L):
    g_in = refs[:n_gla_in]
    m_in = refs[n_gla_in:n_gla_in + n_ml_in]
    g_out = refs[n_gla_in + n_ml_in:n_gla_in + n_ml_in + 2]
    m_rest = refs[n_gla_in + n_ml_in + 2:]
    for phase in ("init", "body"):
        _gla_kernel(*g_in, *g_out, n_pad=n_pad, L=L, phase=phase)
        _mlstm_kernel(*m_in, *m_rest, n_pad=n_pad, L=L, phase=phase)


def recurrent_branches(gla_parts, ml_parts, *, nb, nchunk, chunk, n_pad):
    return pl.pallas_call(
        functools.partial(_recurrent_kernel, n_gla_in=len(gla_parts["args"]), n_ml_in=len(ml_parts["args"]),
                          n_pad=n_pad, L=chunk),
        grid=(nb, nchunk),
        in_specs=gla_parts["in_specs"] + ml_parts["in_specs"],
        out_specs=gla_parts["out_specs"] + ml_parts["out_specs"],
        out_shape=gla_parts["out_shape"] + ml_parts["out_shape"],
        scratch_shapes=gla_parts["scratch"] + ml_parts["scratch"],
        compiler_params=_cparams(("arbitrary", "arbitrary")),
        name="gla_mlstm",
    )(*gla_parts["args"], *ml_parts["args"])


def _sb_heads(qs, ks, vs, vis, cs, umat):
    nh = len(qs)
    bq = qs[0].shape[0]
    lss, l1s, parts = [], [], []
    for h in range(nh):
        z = _dot_nt(qs[h], ks[h])
        t = jnp.log(1.0 + jnp.exp(-jnp.abs(z)))
        lss.append(jnp.minimum(z, 0.0) - t)
        l1 = jnp.minimum(-z, 0.0) - t
        if vis is not None:
            l1 = jnp.where(vis, l1, 0.0)
        l1s.append(l1)
        parts.extend(_split2(l1))
    suf = _dot(jnp.concatenate(parts, axis=0), umat)
    pvs, new_cs = [], []
    for h in range(nh):
        suffix = suf[2 * h * bq:(2 * h + 1) * bq] + suf[(2 * h + 1) * bq:(2 * h + 2) * bq]
        att = jnp.exp(lss[h] + suffix + cs[h])
        if vis is not None:
            att = jnp.where(vis, att, 0.0)
        pvs.append(_dot(att.astype(BF16), vs[h]))
        new_cs.append(cs[h] + jnp.sum(l1s[h], axis=-1, keepdims=True))
    return jnp.concatenate(pvs, axis=-1), new_cs, jnp.max(functools.reduce(jnp.maximum, new_cs))


def _sb_prompt_kernel(q_ref, k_ref, v_ref, u_ref, o_ref, acc_ref, *, bq, n_pad):
    qi = pl.program_id(2)
    bk = SB_BK
    assert bq == bk and n_pad < bk
    qpos = lax.broadcasted_iota(jnp.int32, (bq, 1), 0)
    kio = lax.broadcasted_iota(jnp.int32, (1, bk), 1)
    umat = u_ref[...]
    heads = [slice(h * SB_DH, (h + 1) * SB_DH) for h in range(SB_HG)]
    qs = [(q_ref[:, hs] * (SB_DH ** -0.5)).astype(BF16) for hs in heads]

    def block(j, vis, cs):
        start = pl.multiple_of(j * bk, bk)
        return _sb_heads(qs, [k_ref[pl.ds(start, bk), hs] for hs in heads],
                         [v_ref[pl.ds(start, bk), hs] for hs in heads], vis, cs, umat)

    vis_diag = (kio < qpos) & (kio >= jnp.where(qi > 0, 0, n_pad))
    pv, cs, cmax = block(qi, vis_diag, [jnp.zeros((bq, 1), F32) for _ in heads])
    acc_ref[...] = pv

    def body(carry):
        j, cs, _ = carry
        pv, cs, cmax = block(j, None, list(cs))
        acc_ref[...] += pv
        return j - 1, tuple(cs), cmax

    def cond(carry):
        j, _, cmax = carry
        return (j >= 1) & (cmax > -SB_SKIP)

    j, cs, cmax = lax.while_loop(cond, body, (qi - 1, tuple(cs), cmax))

    @pl.when((j == 0) & (cmax > -SB_SKIP))
    def _():
        pv, _, _ = block(0, jnp.broadcast_to(kio >= n_pad, (bq, bk)), list(cs))
        acc_ref[...] += pv

    o_ref[...] = acc_ref[...].astype(BF16)


def sb_attention_prompt(z_main, kvb, umat, *, nb, tp, bq, n_pad):
    nq = tp // bq
    gw = SB_HG * SB_DH
    ng = SB_HEADS // SB_HG
    return pl.pallas_call(
        functools.partial(_sb_prompt_kernel, bq=bq, n_pad=n_pad),
        grid=(nb, ng, nq),
        in_specs=[pl.BlockSpec((bq, gw), lambda b, g, i: (b * nq + i, C_SQ // gw + g)),
                  pl.BlockSpec((tp, gw), lambda b, g, i: (b, C_SK // gw + g)),
                  pl.BlockSpec((tp, gw), lambda b, g, i: (b, C_SV // gw + g)),
                  pl.BlockSpec((SB_BK, SB_BK), lambda b, g, i: (0, 0))],
        out_specs=pl.BlockSpec((bq, gw), lambda b, g, i: (b * nq + i, g)),
        out_shape=jax.ShapeDtypeStruct((nb * tp, SB_HEADS * SB_DH), BF16),
        scratch_shapes=[pltpu.VMEM((bq, gw), F32)],
        compiler_params=_cparams(("arbitrary", "arbitrary", "arbitrary")),
        name="sb_prompt",
    )(z_main, kvb, kvb, umat)


def _sb_sample_kernel(q_ref, kn_ref, vn_ref, kc_ref, vc_ref, u_ref, o_ref, acc_ref, *, n_cache):
    T = CHUNK
    bk = SB_BK
    nfull = n_cache // bk
    rem = n_cache % bk
    umat = u_ref[...]
    heads = [slice(h * SB_DH, (h + 1) * SB_DH) for h in range(SB_HEADS)]
    qs = [(q_ref[:, hs] * (SB_DH ** -0.5)).astype(BF16) for hs in heads]
    vis_new = lax.broadcasted_iota(jnp.int32, (1, T), 1) < lax.broadcasted_iota(jnp.int32, (T, 1), 0)
    pv, cs, cmax = _sb_heads(qs, [kn_ref[:, hs].astype(BF16) for hs in heads],
                             [vn_ref[:, hs].astype(BF16) for hs in heads], vis_new,
                             [jnp.zeros((T, 1), F32) for _ in heads], umat[0:T, 0:T])
    acc_ref[...] = pv

    def cache_block(start, vis, cs):
        return _sb_heads(qs, [kc_ref[0, 0, pl.ds(start, bk), h, :].astype(BF16) for h in range(SB_HEADS)],
                         [vc_ref[0, 0, pl.ds(start, bk), h, :].astype(BF16) for h in range(SB_HEADS)],
                         vis, cs, umat)

    def body(carry):
        j, cs, _ = carry
        pv, cs, cmax = cache_block(pl.multiple_of(n_cache - bk * (j + 1), 8), None, list(cs))
        acc_ref[...] += pv
        return j + 1, tuple(cs), cmax

    def cond(carry):
        j, _, cmax = carry
        return (j < nfull) & (cmax > -SB_SKIP)

    _, cs, cmax = lax.while_loop(cond, body, (jnp.int32(0), tuple(cs), cmax))
    if rem:
        @pl.when(cmax > -SB_SKIP)
        def _():
            vis_rem = jnp.broadcast_to(lax.broadcasted_iota(jnp.int32, (1, bk), 1) < rem, (T, bk))
            pv, _, _ = cache_block(0, vis_rem, list(cs))
            acc_ref[...] += pv
    o_ref[...] = acc_ref[...].astype(BF16)


def sb_attention_sample(z_main, z_kv, cache_k, cache_v, umat, *, layer, row0, nb):
    T = CHUNK
    hd = SB_HEADS * SB_DH
    n_cache = cache_k.shape[2]
    assert n_cache % 8 == 0 and n_cache >= SB_BK
    rb0 = row0 // T
    cache_spec = pl.BlockSpec((1, 1, n_cache, SB_HEADS, SB_DH), lambda b: (layer, b, 0, 0, 0))
    return pl.pallas_call(
        functools.partial(_sb_sample_kernel, n_cache=n_cache),
        grid=(nb,),
        in_specs=[pl.BlockSpec((T, hd), lambda b: (rb0 + b, C_SQ // hd)),
                  pl.BlockSpec((T, hd), lambda b: (rb0 + b, C_SK // hd)),
                  pl.BlockSpec((T, hd), lambda b: (rb0 + b, C_SV // hd)),
                  cache_spec, cache_spec,
                  pl.BlockSpec((SB_BK, SB_BK), lambda b: (0, 0))],
        out_specs=pl.BlockSpec((T, hd), lambda b: (b, 0)),
        out_shape=jax.ShapeDtypeStruct((nb * T, hd), BF16),
        scratch_shapes=[pltpu.VMEM((T, hd), F32)],
        compiler_params=_cparams(("arbitrary",)),
        name="sb_sample",
    )(z_main, z_kv, z_kv, cache_k, cache_v, umat)


def _kv_out_kernel(*refs, depth, nb, nj, rt, n_front):
    z_refs = refs[:depth]
    k_hbm, v_hbm, kbuf, vbuf, sem = refs[depth:]
    l = pl.program_id(0)
    b = pl.program_id(1)
    j = pl.program_id(2)
    lin = (l * nb + b) * nj + j
    total = depth * nb * nj
    slot = lin % 2

    def copies(s, first, dst_lo):
        src_lo, n = (n_front, rt - n_front) if first else (0, rt)
        return [pltpu.make_async_copy(buf.at[s, pl.ds(src_lo, n)], out.at[l, b, pl.ds(dst_lo, n)], sem.at[s, c])
                for c, (buf, out) in enumerate(((kbuf, k_hbm), (vbuf, v_hbm)))]

    def wait_step(step, s):
        jj = step % nj

        @pl.when(jj == 0)
        def _():
            for cp in copies(s, True, 0):
                cp.wait()

        @pl.when(jj != 0)
        def _():
            for cp in copies(s, False, 0):
                cp.wait()

    @pl.when(lin >= 2)
    def _():
        wait_step(lin - 2, slot)

    x = z_refs[0][...]
    for i in range(1, depth):
        x = jnp.where(l == i, z_refs[i][...], x)
    hd = SB_HEADS * SB_DH
    kbuf[slot] = x[:, C_SK:C_SK + hd].reshape(rt, SB_HEADS, SB_DH)
    vbuf[slot] = x[:, C_SV:C_SV + hd].reshape(rt, SB_HEADS, SB_DH)

    @pl.when(j == 0)
    def _():
        for cp in copies(slot, True, 0):
            cp.start()

    @pl.when(j > 0)
    def _():
        for cp in copies(slot, False, j * rt - n_front):
            cp.start()

    @pl.when(lin == total - 1)
    def _():
        if total >= 2:
            wait_step(lin - 1, 1 - slot)
        wait_step(lin, slot)


def kv_state_outputs(z_kvs, *, nb, tp, n_front):
    depth = len(z_kvs)
    rt = _pick(tp, (768, 640, 512, 384, 256))
    assert rt > n_front
    nj = tp // rt
    w = z_kvs[0].shape[1]
    last = nb * nj - 1

    def zspec(i):
        return pl.BlockSpec((rt, w), lambda l, b, j: (jnp.where(l == i, b * nj + j, jnp.where(l < i, 0, last)), 0))

    shape = jax.ShapeDtypeStruct((depth, nb, tp - n_front, SB_HEADS, SB_DH), F32)
    return pl.pallas_call(
        functools.partial(_kv_out_kernel, depth=depth, nb=nb, nj=nj, rt=rt, n_front=n_front),
        grid=(depth, nb, nj),
        in_specs=[zspec(i) for i in range(depth)],
        out_specs=[pl.BlockSpec(memory_space=pl.ANY), pl.BlockSpec(memory_space=pl.ANY)],
        out_shape=[shape, shape],
        scratch_shapes=[pltpu.VMEM((2, rt, SB_HEADS, SB_DH), F32), pltpu.VMEM((2, rt, SB_HEADS, SB_DH), F32),
                        pltpu.SemaphoreType.DMA((2, 2))],
        compiler_params=_cparams(("arbitrary", "arbitrary", "arbitrary")),
        name="kv_state_out",
    )(*z_kvs)


def _merge_kernel(agp_ref, amp_ref, asp_ref, ags_ref, ams_ref, ass_ref, g0_ref, g1_ref, g2_ref, x_ref,
                  wg_ref, wm_ref, ws_ref, wo_ref, lg_ref, lb_ref, o_ref, or_ref, *, alpha, np_tiles):
    is_p = pl.program_id(0) < np_tiles
    ag = jnp.where(is_p, agp_ref[...], ags_ref[...])
    am = jnp.where(is_p, amp_ref[...], ams_ref[...])
    asb = jnp.where(is_p, asp_ref[...], ass_ref[...])
    merged = (_sigmoid(g0_ref[...]) * _dot(ag, wg_ref[...])
              + _sigmoid(g1_ref[...]) * _dot(am, wm_ref[...])
              + _sigmoid(g2_ref[...]) * _dot(asb, ws_ref[...]))
    out = _dot(merged.astype(BF16), wo_ref[...])
    y = _ln_math(alpha * x_ref[...] + out, lg_ref[...], lb_ref[...])
    o_ref[...] = y
    or_ref[...] = y.reshape(or_ref.shape)


def merge_ln1(acts_p, acts_s, z_main, x, wg, wm, ws, wo, lg, lb, alpha):
    n, d = x.shape
    n_p = acts_p[0].shape[0]
    n_s = acts_s[0].shape[0]
    tm = _pick(np.gcd(n_p, n_s), (256, 128, 64))
    assert n_p + n_s == n
    np_tiles = n_p // tm
    row = lambda i: (i, 0)
    row_p = lambda i: (jnp.minimum(i, np_tiles - 1), 0)
    row_s = lambda i: (jnp.maximum(i - np_tiles, 0), 0)
    const = lambda i: (0, 0)
    wspec = pl.BlockSpec((d, d), const)
    gcb = C_GATE // d
    return pl.pallas_call(
        functools.partial(_merge_kernel, alpha=alpha, np_tiles=np_tiles),
        grid=(n // tm,),
        in_specs=[pl.BlockSpec((tm, d), row_p), pl.BlockSpec((tm, d), row_p), pl.BlockSpec((tm, d), row_p),
                  pl.BlockSpec((tm, d), row_s), pl.BlockSpec((tm, d), row_s), pl.BlockSpec((tm, d), row_s),
                  pl.BlockSpec((tm, d), lambda i: (i, gcb)),
                  pl.BlockSpec((tm, d), lambda i: (i, gcb + 1)),
                  pl.BlockSpec((tm, d), lambda i: (i, gcb + 2)),
                  pl.BlockSpec((tm, d), row),
                  wspec, wspec, wspec, wspec,
                  pl.BlockSpec((1, d), const), pl.BlockSpec((1, d), const)],
        out_specs=[pl.BlockSpec((tm, d), row), pl.BlockSpec((tm, ROW_SUB, d // ROW_SUB), lambda i: (i, 0, 0))],
        out_shape=[jax.ShapeDtypeStruct((n, d), F32), jax.ShapeDtypeStruct((n, ROW_SUB, d // ROW_SUB), F32)],
        compiler_params=_cparams(("arbitrary",)),
        name="merge_ln1",
    )(*acts_p, *acts_s, z_main, z_main, z_main, x, wg, wm, ws, wo, lg.reshape(1, d), lb.reshape(1, d))


R_ROWS = 8 + N_EXPERTS


def _router_kernel(x_ref, w_ref, b_ref, u_ref, idx_ref, gate_ref, cnt_ref):
    i = pl.program_id(0)
    tm = x_ref.shape[0]

    @pl.when(i == 0)
    def _():
        cnt_ref[...] = jnp.zeros_like(cnt_ref)

    logits = _dot_nt(w_ref[...].astype(BF16), x_ref[...].astype(BF16)) + b_ref[...]

    g = [logits[j:j + 1, :] for j in range(N_GROUPS)]
    gmax = jnp.maximum(jnp.maximum(g[0], g[1]), jnp.maximum(g[2], g[3]))
    gsel = jnp.where(g[0] == gmax, 0, jnp.where(g[1] == gmax, 1, jnp.where(g[2] == gmax, 2, 3)))
    gden = (jnp.exp(g[0] - gmax) + jnp.exp(g[1] - gmax)) + (jnp.exp(g[2] - gmax) + jnp.exp(g[3] - gmax))
    gprob = 1.0 / gden
    e_in = jnp.where(gsel == 0, logits[8:16, :],
                     jnp.where(gsel == 1, logits[16:24, :],
                               jnp.where(gsel == 2, logits[24:32, :], logits[32:40, :])))
    ridx = lax.broadcasted_iota(jnp.int32, (EXPERTS_PER_GROUP, tm), 0)
    v1 = jnp.max(e_in, axis=0, keepdims=True)
    i1 = jnp.min(jnp.where(e_in == v1, ridx, EXPERTS_PER_GROUP), axis=0, keepdims=True)
    e2 = jnp.where(ridx == i1, -jnp.inf, e_in)
    v2 = jnp.max(e2, axis=0, keepdims=True)
    i2 = jnp.min(jnp.where(e2 == v2, ridx, EXPERTS_PER_GROUP), axis=0, keepdims=True)
    t = jnp.exp(v2 - v1)
    p1 = 1.0 / (1.0 + t)
    gate_ref[0:1, :] = p1 * gprob
    gate_ref[1:2, :] = (t * p1) * gprob
    gate_ref[2:8, :] = jnp.zeros((6, tm), F32)
    eid1 = gsel * EXPERTS_PER_GROUP + i1
    eid2 = gsel * EXPERTS_PER_GROUP + i2

    eidx = lax.broadcasted_iota(jnp.int32, (N_EXPERTS, tm), 0)
    oh1 = eidx == eid1
    oh2 = eidx == eid2
    oh1f = jnp.where(oh1, 1.0, 0.0)
    oh2f = jnp.where(oh2, 1.0, 0.0)
    umat = u_ref[...]
    pre1 = _dot(oh1f.astype(BF16), umat)
    pre2 = _dot(oh2f.astype(BF16), umat)
    cnt1 = jnp.sum(oh1f, axis=1, keepdims=True)
    cnt2 = jnp.sum(oh2f, axis=1, keepdims=True)
    base = cnt_ref[:, 0:1]
    rank1 = jnp.sum(jnp.where(oh1, base + pre1, 0.0), axis=0, keepdims=True)
    rank2 = jnp.sum(jnp.where(oh2, base + cnt1 + pre2, 0.0), axis=0, keepdims=True)
    idx_ref[0:1, :] = eid1
    idx_ref[1:2, :] = eid2
    idx_ref[2:3, :] = rank1.astype(jnp.int32)
    idx_ref[3:4, :] = rank2.astype(jnp.int32)
    idx_ref[4:8, :] = jnp.zeros((4, tm), jnp.int32)
    cnt_ref[...] = cnt_ref[...] + (cnt1 + cnt2)


def moe_router(x, w_r, b_r, umat):
    n, d = x.shape
    tm = umat.shape[0]
    return pl.pallas_call(
        _router_kernel,
        grid=(n // tm,),
        in_specs=[pl.BlockSpec((tm, d), lambda i: (i, 0)),
                  pl.BlockSpec((R_ROWS, d), lambda i: (0, 0)),
                  pl.BlockSpec((R_ROWS, 1), lambda i: (0, 0)),
                  pl.BlockSpec((tm, tm), lambda i: (0, 0))],
        out_specs=[pl.BlockSpec((8, tm), lambda i: (0, i)),
                   pl.BlockSpec((8, tm), lambda i: (0, i)),
                   pl.BlockSpec((N_EXPERTS, 128), lambda i: (0, 0))],
        out_shape=[jax.ShapeDtypeStruct((8, n), jnp.int32),
                   jax.ShapeDtypeStruct((8, n), F32),
                   jax.ShapeDtypeStruct((N_EXPERTS, 128), F32)],
        compiler_params=_cparams(("arbitrary",)),
        name="moe_router",
    )(x, w_r, b_r, umat)


def _row_copy(src_hbm, dst_vmem, src_row, dst_row, sem):
    return pltpu.make_async_copy(src_hbm.at[pl.ds(src_row, 1)], dst_vmem.at[pl.ds(dst_row, 1)], sem)


GATHER_UNROLL = 8


def _start_row_gather(idx_ref, base, src_hbm, dst_vmem, nrows, sem):
    def start(r, carry):
        _row_copy(src_hbm, dst_vmem, idx_ref[base + r], r, sem).start()
        return carry

    lax.fori_loop(0, nrows, start, 0, unroll=GATHER_UNROLL)


def _wait_row_gather(src_hbm, dst_vmem, nrows, sem):
    def wait(r, carry):
        _row_copy(src_hbm, dst_vmem, 0, r, sem).wait()
        return carry

    lax.fori_loop(0, nrows, wait, 0, unroll=GATHER_UNROLL)


def _moe_kernel(be_ref, tok_ref, nblk_ref, x_hbm, wg_ref, wu_ref, wd_ref, o_ref,
                xbuf, wgb, wub, wdb, sem):
    i = pl.program_id(0)
    nblk = nblk_ref[0]
    slot = i % 2

    @pl.when((i == 0) & (nblk > 0))
    def _():
        _start_row_gather(tok_ref, 0, x_hbm, xbuf.at[0], MOE_BLK, sem.at[0])

    @pl.when(i + 1 < nblk)
    def _():
        _start_row_gather(tok_ref, (i + 1) * MOE_BLK, x_hbm, xbuf.at[1 - slot], MOE_BLK, sem.at[1 - slot])

    prev_e = be_ref[jnp.maximum(i - 1, 0)]

    @pl.when((i == 0) | (be_ref[i] != prev_e))
    def _():
        wgb[...] = wg_ref[0, 0].astype(BF16)
        wub[...] = wu_ref[0, 0].astype(BF16)
        wdb[...] = wd_ref[0, 0].astype(BF16)

    @pl.when(i < nblk)
    def _():
        _wait_row_gather(x_hbm, xbuf.at[slot], MOE_BLK, sem.at[slot])
        xb = xbuf[slot].reshape(MOE_BLK, D_MODEL).astype(BF16)
        a = _dot(xb, wgb[...])
        hmid = (a * _sigmoid(a)) * _dot(xb, wub[...])
        o_ref[...] = _dot(hmid.astype(BF16), wdb[...]).reshape(o_ref.shape)

    @pl.when(i >= nblk)
    def _():
        o_ref[...] = jnp.zeros_like(o_ref)


def moe_experts(x, block_e, tok_pad, nblk, w_gate, w_up, w_down, *, layer):
    n = x.shape[0]
    d = D_MODEL
    rs = (ROW_SUB, d // ROW_SUB)
    nb = block_e.shape[0]
    wmap = lambda i, be, tok, nblk: (layer, be[i], 0, 0)
    grid_spec = pltpu.PrefetchScalarGridSpec(
        num_scalar_prefetch=3,
        grid=(nb,),
        in_specs=[pl.BlockSpec(memory_space=pl.ANY),
                  pl.BlockSpec((1, 1, d, D_EXPERT), wmap),
                  pl.BlockSpec((1, 1, d, D_EXPERT), wmap),
                  pl.BlockSpec((1, 1, D_EXPERT, d), wmap)],
        out_specs=pl.BlockSpec((MOE_BLK,) + rs, lambda i, be, tok, nblk: (i, 0, 0)),
        scratch_shapes=[pltpu.VMEM((2, MOE_BLK) + rs, F32),
                        pltpu.VMEM((d, D_EXPERT), BF16),
                        pltpu.VMEM((d, D_EXPERT), BF16),
                        pltpu.VMEM((D_EXPERT, d), BF16),
                        pltpu.SemaphoreType.DMA((2,))],
    )
    return pl.pallas_call(
        _moe_kernel,
        grid_spec=grid_spec,
        out_shape=jax.ShapeDtypeStruct((nb * MOE_BLK,) + rs, F32),
        compiler_params=_cparams(("arbitrary",)),
        name="moe_experts",
    )(block_e, tok_pad, nblk, x, w_gate, w_up, w_down)


def _combine_kernel(dest_ref, yb_hbm, x_ref, gate_ref, lg_ref, lb_ref, o_ref, ob_ref, ybuf, sem, *, alpha, n):
    i = pl.program_id(0)
    nt = pl.num_programs(0)
    tm = x_ref.shape[0]
    slot = i % 2

    def start(tile, s):
        for k in range(TOP_K):
            _start_row_gather(dest_ref, k * n + tile * tm, yb_hbm, ybuf.at[s, k], tm, sem.at[s, k])

    @pl.when(i == 0)
    def _():
        start(0, 0)

    @pl.when(i + 1 < nt)
    def _():
        start(i + 1, 1 - slot)

    for k in range(TOP_K):
        _wait_row_gather(yb_hbm, ybuf.at[slot, k], tm, sem.at[slot, k])
    gate = gate_ref[...]
    d = x_ref.shape[1]
    y = ybuf[slot, 0].reshape(tm, d) * gate[:, 0:1] + ybuf[slot, 1].reshape(tm, d) * gate[:, 1:2]
    out = _ln_math(alpha * x_ref[...] + y, lg_ref[...], lb_ref[...])
    o_ref[...] = out
    ob_ref[...] = out.astype(BF16)


def moe_combine_ln2(x, yb, dest, gate_cols, lg, lb, alpha):
    n, d = x.shape
    tm = _pick(n, (256, 128, 64))
    grid_spec = pltpu.PrefetchScalarGridSpec(
        num_scalar_prefetch=1,
        grid=(n // tm,),
        in_specs=[pl.BlockSpec(memory_space=pl.ANY),
                  pl.BlockSpec((tm, d), lambda i, dest: (i, 0)),
                  pl.BlockSpec((tm, TOP_K), lambda i, dest: (i, 0)),
                  pl.BlockSpec((1, d), lambda i, dest: (0, 0)),
                  pl.BlockSpec((1, d), lambda i, dest: (0, 0))],
        out_specs=[pl.BlockSpec((tm, d), lambda i, dest: (i, 0)), pl.BlockSpec((tm, d), lambda i, dest: (i, 0))],
        scratch_shapes=[pltpu.VMEM((2, TOP_K, tm, ROW_SUB, d // ROW_SUB), F32),
                        pltpu.SemaphoreType.DMA((2, TOP_K))],
    )
    return pl.pallas_call(
        functools.partial(_combine_kernel, alpha=alpha, n=n),
        grid_spec=grid_spec,
        out_shape=[jax.ShapeDtypeStruct((n, d), F32), jax.ShapeDtypeStruct((n, d), BF16)],
        compiler_params=_cparams(("arbitrary",)),
        name="moe_combine_ln2",
    )(dest, yb, x, gate_cols, lg.reshape(1, d), lb.reshape(1, d))


def _tri_lower(n):
    t = np.arange(n)
    return (t[None, :] <= t[:, None]).astype(np.float32)


def _strict_upper(n):
    t = np.arange(n)
    return (t[:, None] < t[None, :]).astype(np.float32)


def kernel(x_prompt, x_sample, state_gla, state_ml_C, state_ml_n, state_ml_m, state_ml_conv, cache_sb_k, cache_sb_v, meta_tokens, ln_in_g, ln_in_b, w_in, gla_w_a2, gla_b_a, gla_norm_g, ml_conv_w, ml_conv_b, ml_wq, ml_wk, ml_wv, ml_b_i, ml_b_f, ml_norm_g, w_br_gla, w_br_ml, w_br_sb, w_out, ln1_g, ln1_b, w_rg, b_rg, w_re, b_re, w_e_gate, w_e_up, w_e_down, ln2_g, ln2_b):
    D = D_MODEL
    B, SEQ, _ = x_prompt.shape
    DB, DS, _ = x_sample.shape
    depth = w_in.shape[0]
    n_cache = cache_sb_k.shape[2]
    assert DS == CHUNK and SEQ % 128 == 0 and x_prompt.shape[2] == D
    alpha = (2.0 * depth) ** 0.25
    TP = PAD_FRONT + N_META + SEQ
    NP = B * TP
    NS = DB * DS
    N = NP + NS
    assert TP % CHUNK_PROMPT == 0 and NP % CHUNK == 0
    nchunk_p = TP // CHUNK_PROMPT

    meta = jnp.broadcast_to(meta_tokens.astype(F32)[None], (B, N_META, D))
    xp = jnp.concatenate([jnp.zeros((B, PAD_FRONT, D), F32), meta, x_prompt], axis=1).reshape(NP, D)
    x = jnp.concatenate([xp, x_sample.reshape(NS, D)], axis=0)
    x, xb = layer_norm_rows(x, ln_in_g, ln_in_b)

    u_sb = jnp.asarray(_strict_upper(SB_BK).T, BF16)
    tm_r = _pick(N, (256, 128, 64))
    u_r = jnp.asarray(_strict_upper(tm_r), BF16)

    A = N * TOP_K
    NB = -(-(A + N_EXPERTS * (MOE_BLK - 1)) // MOE_BLK)

    outs_p = [[] for _ in range(5)]
    outs_s = [[] for _ in range(7)]
    z_kvs = []
    for l in range(depth):
        wi = w_in[l]
        w_main = jnp.concatenate([wi[:, 0:3072], wi[:, 3088:4112], wi[:, 4120:5144], wi[:, 5144:6168],
                                  wi[:, 8216:]], axis=1).astype(BF16)
        w_kv = wi[:, 6168:8216].astype(BF16)
        w_small = jnp.concatenate([wi[:, 3072:3088], wi[:, 4112:4120],
                                   jnp.zeros((D, 128 - 24), F32)], axis=1).astype(BF16)
        z_main = in_projection(xb, w_main, 1024, "in_proj_main")
        z_kv, kvb = in_projection(xb, w_kv, 1024, "in_proj_kv", with_bf16=True)
        z_small = in_projection(xb, w_small, 128, "in_proj_small")
        z_kvs.append(z_kv)

        wa = jnp.concatenate([gla_w_a2[l], jnp.zeros((128 - GLA_RANK, GLA_HEADS * GLA_DK), F32)], axis=0).astype(BF16)
        ba = gla_b_a[l].reshape(1, -1)
        ng = gla_norm_g[l].reshape(1, -1)
        s0_p = jnp.zeros((B, GLA_HEADS, GLA_DV, GLA_DK), F32)
        s0_s = jnp.swapaxes(state_gla[l].astype(F32), -1, -2)
        geo_p = dict(row0=0, nb=B, nchunk=nchunk_p, chunk=CHUNK_PROMPT)
        geo_s = dict(row0=NP, nb=DB, nchunk=1, chunk=CHUNK)
        gla_p = _gla_call_parts(z_main, z_small, wa, ba, ng, s0_p, **geo_p)
        gla_s = _gla_call_parts(z_main, z_small, wa, ba, ng, s0_s, **geo_s)

        gbias = jnp.zeros((1, 128), F32).at[0, L_MI:L_MI + ML_HEADS].set(ml_b_i[l]).at[0, L_MF:L_MF + ML_HEADS].set(ml_b_f[l])
        cw = ml_conv_w[l]
        cb = ml_conv_b[l].reshape(1, -1)
        wq = ml_wq[l].astype(BF16)
        wk = ml_wk[l].astype(BF16)
        wv = ml_wv[l].astype(BF16)
        mng = ml_norm_g[l].reshape(1, -1)
        W = ML_HEADS * ML_DH
        conv0_p = jnp.zeros((B, 8, W), F32)
        conv0_s = jnp.concatenate([jnp.zeros((DB, 8 - (ML_CONV - 1), W), F32), state_ml_conv[l].astype(F32)], axis=1)
        m0_s = jnp.zeros((DB, 1, 128), F32).at[:, 0, L_MF:L_MF + ML_HEADS].set(state_ml_m[l].astype(F32))
        ml_p = _mlstm_call_parts(
            z_main, z_small, gbias, cw, cb, wq, wk, wv, mng, conv0_p,
            jnp.zeros((B, ML_HEADS, ML_DH, ML_DH), F32), jnp.zeros((B, ML_HEADS, ML_DH), F32),
            jnp.zeros((B, 1, 128), F32), **geo_p)
        ml_s = _mlstm_call_parts(
            z_main, z_small, gbias, cw, cb, wq, wk, wv, mng, conv0_s,
            state_ml_C[l].astype(F32), state_ml_n[l].astype(F32), m0_s, **geo_s)
        a_gla_p, sT_p, a_ml_p, c_p, n_p, m_p = recurrent_branches(
            gla_p, ml_p, nb=B, nchunk=nchunk_p, chunk=CHUNK_PROMPT, n_pad=PAD_FRONT)
        a_gla_s, sT_s, a_ml_s, c_s, n_s, m_s = recurrent_branches(
            gla_s, ml_s, nb=DB, nchunk=1, chunk=CHUNK, n_pad=0)

        a_sb_p = sb_attention_prompt(z_main, kvb, u_sb, nb=B, tp=TP, bq=SB_BK, n_pad=PAD_FRONT)
        a_sb_s = sb_attention_sample(z_main, z_kv, cache_sb_k.astype(F32), cache_sb_v.astype(F32),
                                     u_sb, layer=l, row0=NP, nb=DB)

        x1, x1r = merge_ln1((a_gla_p, a_ml_p, a_sb_p), (a_gla_s, a_ml_s, a_sb_s), z_main, x,
                       w_br_gla[l].astype(BF16), w_br_ml[l].astype(BF16), w_br_sb[l].astype(BF16),
                       w_out[l].astype(BF16), ln1_g[l], ln1_b[l], alpha)

        w_r = jnp.concatenate([w_rg[l].T, jnp.zeros((8 - N_GROUPS, D), F32), w_re[l].T], axis=0)
        b_r = jnp.concatenate([b_rg[l], jnp.zeros((8 - N_GROUPS,), F32), b_re[l]]).reshape(R_ROWS, 1)
        idx, gates, cnt = moe_router(x1, w_r, b_r, u_r)
        counts = cnt[:, 0].astype(jnp.int32)
        padded = (counts + MOE_BLK - 1) // MOE_BLK * MOE_BLK
        ends_pad = jnp.cumsum(padded)
        start_pad = ends_pad - padded
        eids = jnp.arange(N_EXPERTS, dtype=jnp.int32)
        start_of = jnp.sum(jnp.where(idx[0:2, :, None] == eids, start_pad, 0), axis=-1)
        dest = (start_of + idx[2:4]).reshape(-1)
        tok = jnp.tile(jnp.arange(N, dtype=jnp.int32), TOP_K)
        tok_pad = jnp.zeros((NB * MOE_BLK,), jnp.int32).at[dest].set(tok)
        blk_lo = jnp.arange(NB, dtype=jnp.int32) * MOE_BLK
        block_e = jnp.minimum(jnp.sum((ends_pad[None, :] <= blk_lo[:, None]).astype(jnp.int32), axis=-1),
                              N_EXPERTS - 1)
        nblk = (ends_pad[-1:] // MOE_BLK).astype(jnp.int32)
        yb = moe_experts(x1r, block_e, tok_pad, nblk, w_e_gate, w_e_up, w_e_down, layer=l)
        x, xb = moe_combine_ln2(x1, yb, dest, gates[0:2].T, ln2_g[l], ln2_b[l], alpha)

        nct = ML_CONV - 1
        conv_p = jnp.stack([z_main[(b + 1) * TP - nct:(b + 1) * TP, C_MU:C_MU + W] for b in range(B)], axis=0)
        conv_s = z_main[NP:, C_MU:C_MU + W].reshape(DB, DS, W)[:, DS - nct:]
        zkv_s = z_kv[NP:].reshape(DB, DS, 2 * D)
        kv_shape = (SB_HEADS, SB_DH)
        outs_p[0].append(jnp.swapaxes(sT_p, -1, -2))
        outs_s[0].append(jnp.swapaxes(sT_s, -1, -2))
        outs_p[1].append(c_p); outs_s[1].append(c_s)
        outs_p[2].append(n_p); outs_s[2].append(n_s)
        outs_p[3].append(m_p[:, 0, L_MF:L_MF + ML_HEADS]); outs_s[3].append(m_s[:, 0, L_MF:L_MF + ML_HEADS])
        outs_p[4].append(conv_p); outs_s[4].append(conv_s)
        outs_s[5].append(zkv_s[:, :, C_SK:C_SK + D].reshape(DB, DS, *kv_shape))
        outs_s[6].append(zkv_s[:, :, C_SV:C_SV + D].reshape(DB, DS, *kv_shape))

    sbk_p, sbv_p = kv_state_outputs(z_kvs, nb=B, tp=TP, n_front=PAD_FRONT)
    sp = [jnp.stack(a, axis=0) for a in outs_p]
    ss = [jnp.stack(a, axis=0) for a in outs_s]
    y_prompt = x[:NP].reshape(B, TP, D)[:, PAD_FRONT + N_META:]
    y_sample = x[NP:].reshape(DB, DS, D)
    return (y_prompt, y_sample, sp[0], ss[0], sp[1], ss[1], sp[2], ss[2], sp[3], ss[3],
            sp[4], ss[4], sbk_p, ss[5], sbv_p, ss[6])
```

```python
import functools

import numpy as np
import jax
import jax.numpy as jnp
from jax import lax
from jax.experimental import pallas as pl
from jax.experimental.pallas import tpu as pltpu

F32 = jnp.float32
BF16 = jnp.bfloat16

D_MODEL = 1024
N_META = 16
CHUNK = 64
CHUNK_PROMPT = 256
PAD_FRONT = 240
GLA_HEADS, GLA_DK, GLA_DV, GLA_RANK, GLA_TAU = 4, 128, 256, 16, 16.0
ML_HEADS, ML_DH, ML_CONV = 4, 256, 4
SB_HEADS, SB_DH = 8, 128
N_GROUPS, EXPERTS_PER_GROUP, TOP_K = 4, 8, 2
N_EXPERTS = N_GROUPS * EXPERTS_PER_GROUP
D_EXPERT = D_MODEL // 2
MOE_BLK = 256
ROW_SUB = 8
LN_EPS = 1e-5
NEG_BIG = -1e30
SB_SKIP = 110.0
SB_BK = 256

C_GQ, C_GK, C_GV, C_GR = 0, 512, 1024, 2048
C_MU, C_MO = 3072, 4096
C_SQ = 5120
C_GATE = 6144
Z_MAIN = 9216
C_SK, C_SV = 0, 1024
SB_HG = 4
L_LR, L_MI, L_MF = 0, 16, 20

VMEM_LIMIT = 48 * 1024 * 1024


def _cparams(sem):
    return pltpu.CompilerParams(dimension_semantics=sem, vmem_limit_bytes=VMEM_LIMIT)


def _pick(n, cands):
    for c in cands:
        if n % c == 0:
            return c
    raise ValueError(f"no tile in {cands} divides {n}")


def _log_sigmoid(x):
    return jnp.minimum(x, 0.0) - jnp.log1p(jnp.exp(-jnp.abs(x)))


def _sigmoid(x):
    return 1.0 / (1.0 + jnp.exp(-x))


def _split2(x):
    hi = x.astype(BF16)
    lo = (x - hi.astype(F32)).astype(BF16)
    return hi, lo


def _split3(x):
    hi = x.astype(BF16)
    r = x - hi.astype(F32)
    mid = r.astype(BF16)
    lo = (r - mid.astype(F32)).astype(BF16)
    return hi, mid, lo


def _dot(a, b):
    return jnp.dot(a, b, preferred_element_type=F32)


def _dot_nt(a, b):
    return lax.dot_general(a, b, (((1,), (1,)), ((), ())), preferred_element_type=F32)


def _dot_tn(a, b):
    return lax.dot_general(a, b, (((0,), (0,)), ((), ())), preferred_element_type=F32)


def _ln_math(x, g, b):
    mu = jnp.mean(x, axis=-1, keepdims=True)
    xc = x - mu
    var = jnp.mean(xc * xc, axis=-1, keepdims=True)
    return xc * lax.rsqrt(var + LN_EPS) * g + b


def _ln_kernel(x_ref, g_ref, b_ref, o_ref, ob_ref):
    y = _ln_math(x_ref[...], g_ref[...], b_ref[...])
    o_ref[...] = y
    ob_ref[...] = y.astype(BF16)


def layer_norm_rows(x, g, b):
    n, d = x.shape
    tm = _pick(n, (512, 256, 128, 64))
    return pl.pallas_call(
        _ln_kernel,
        grid=(n // tm,),
        in_specs=[pl.BlockSpec((tm, d), lambda i: (i, 0)),
                  pl.BlockSpec((1, d), lambda i: (0, 0)),
                  pl.BlockSpec((1, d), lambda i: (0, 0))],
        out_specs=[pl.BlockSpec((tm, d), lambda i: (i, 0)), pl.BlockSpec((tm, d), lambda i: (i, 0))],
        out_shape=[jax.ShapeDtypeStruct((n, d), F32), jax.ShapeDtypeStruct((n, d), BF16)],
        compiler_params=_cparams(("arbitrary",)),
        name="ln_in",
    )(x, g.reshape(1, d), b.reshape(1, d))


def _mm_kernel(x_ref, w_ref, o_ref):
    o_ref[...] = _dot(x_ref[...], w_ref[...])


def _mm2_kernel(x_ref, w_ref, o_ref, ob_ref):
    z = _dot(x_ref[...], w_ref[...])
    o_ref[...] = z
    ob_ref[...] = z.astype(BF16)


def in_projection(x, w_bf16, tn, name, with_bf16=False):
    n, d = x.shape
    cols = w_bf16.shape[1]
    tm = _pick(n, (896, 768, 512, 384, 256, 128, 64))
    out_spec = pl.BlockSpec((tm, tn), lambda j, i: (i, j))
    return pl.pallas_call(
        _mm2_kernel if with_bf16 else _mm_kernel,
        grid=(cols // tn, n // tm),
        in_specs=[pl.BlockSpec((tm, d), lambda j, i: (i, 0)),
                  pl.BlockSpec((d, tn), lambda j, i: (0, j))],
        out_specs=[out_spec, out_spec] if with_bf16 else out_spec,
        out_shape=([jax.ShapeDtypeStruct((n, cols), F32), jax.ShapeDtypeStruct((n, cols), BF16)]
                   if with_bf16 else jax.ShapeDtypeStruct((n, cols), F32)),
        compiler_params=_cparams(("arbitrary", "arbitrary")),
        name=name,
    )(x, w_bf16)


def _gla_levels(L):
    return tuple(L >> i for i in range(1, L.bit_length()))


def _gla_level_matrix(L):
    t = np.arange(L)[:, None]
    j = np.arange(L)[None, :]
    blocks = []
    for m in (L,) + _gla_levels(L):
        blocks.append(((j <= t) & (j // m == t // m)).astype(np.float32))
    for m in (L,) + _gla_levels(L):
        blocks.append(((j > t) & (j // m == t // m)).astype(np.float32))
    return np.concatenate(blocks, axis=0)


def _gla_kernel(q_ref, k_ref, v_ref, r_ref, lr_ref, wa_ref, ba_ref, ng_ref, lvl_ref, s0_ref,
                o_ref, s_ref, *, n_pad, L, phase):
    c = pl.program_id(1)
    GLA_LEVELS = _gla_levels(L)

    if phase == "init":
        @pl.when(c == 0)
        def _():
            s_ref[...] = s0_ref[...]
        return

    row = c * L + lax.broadcasted_iota(jnp.int32, (L, 1), 0)
    valid = row >= n_pad
    x = _dot(lr_ref[...].astype(BF16), wa_ref[...]) + ba_ref[...]
    la = jnp.where(valid, _log_sigmoid(x) * (1.0 / GLA_TAU), 0.0)
    la_hi, la_lo = _split2(la)
    lvl = lvl_ref[...]
    ex = jnp.exp(_dot(lvl, la_hi) + _dot(lvl, la_lo))
    nlev = len(GLA_LEVELS) + 1

    ti = lax.broadcasted_iota(jnp.int32, (L, L), 0)
    si = lax.broadcasted_iota(jnp.int32, (L, L), 1)
    level_masks = []
    for m in GLA_LEVELS:
        sh = m.bit_length() - 1
        tb = lax.shift_right_logical(ti, sh)
        sb = lax.shift_right_logical(si, sh)
        level_masks.append((lax.shift_right_logical(tb, 1) == lax.shift_right_logical(sb, 1))
                           & ((tb & 1) == 1) & ((sb & 1) == 0))
    eye = ti == si

    outs = []
    states = []
    for h in range(GLA_HEADS):
        dk = slice(h * GLA_DK, (h + 1) * GLA_DK)
        dv = slice(h * GLA_DV, (h + 1) * GLA_DV)
        q = q_ref[:, dk] * (GLA_DK ** -0.5)
        k = jnp.where(valid, k_ref[:, dk], 0.0)
        v = jnp.where(valid, v_ref[:, dv], 0.0).astype(BF16)
        att = jnp.where(eye, _dot_nt(q.astype(BF16), k.astype(BF16)), 0.0)
        for lev in range(len(GLA_LEVELS)):
            qe = ex[(1 + lev) * L:(2 + lev) * L, dk]
            ke = ex[(nlev + 1 + lev) * L:(nlev + 2 + lev) * L, dk]
            a = _dot_nt((q * qe).astype(BF16), (k * ke).astype(BF16))
            att = jnp.where(level_masks[lev], a, att)
        eb = ex[0:L, dk]
        st = s_ref[0, h]
        o = _dot(att.astype(BF16), v) + _dot_nt((q * eb).astype(BF16), st.astype(BF16))
        ke_end = ex[nlev * L:(nlev + 1) * L, dk]
        states.append(st * eb[L - 1:L, :] + _dot_tn(v, (k * ke_end).astype(BF16)))
        mu = jnp.mean(o, axis=-1, keepdims=True)
        oc = o - mu
        var = jnp.mean(oc * oc, axis=-1, keepdims=True)
        on = oc * lax.rsqrt(var + LN_EPS) * ng_ref[...]
        r = r_ref[:, dv]
        outs.append((on * (r * _sigmoid(r))).astype(BF16))
    o_ref[...] = jnp.concatenate(outs, axis=-1)
    s_ref[0] = jnp.stack(states, axis=0)


def _gla_call_parts(z_main, z_small, wa, ba, ng, s0t, *, row0, nb, nchunk, chunk):
    L = chunk
    assert row0 % L == 0
    lvl = jnp.asarray(_gla_level_matrix(L), BF16)
    rb0 = row0 // L
    hk = GLA_HEADS * GLA_DK
    hv = GLA_HEADS * GLA_DV

    def rows(colblk):
        return lambda b, c: (rb0 + b * nchunk + c, colblk)

    return dict(
        args=(z_main, z_main, z_main, z_main, z_small, wa, ba, ng, lvl, s0t),
        in_specs=[pl.BlockSpec((L, hk), rows(C_GQ // hk)),
                  pl.BlockSpec((L, hk), rows(C_GK // hk)),
                  pl.BlockSpec((L, hv), rows(C_GV // hv)),
                  pl.BlockSpec((L, hv), rows(C_GR // hv)),
                  pl.BlockSpec((L, 128), rows(0)),
                  pl.BlockSpec((128, hk), lambda b, c: (0, 0)),
                  pl.BlockSpec((1, hk), lambda b, c: (0, 0)),
                  pl.BlockSpec((1, GLA_DV), lambda b, c: (0, 0)),
                  pl.BlockSpec(lvl.shape, lambda b, c: (0, 0)),
                  pl.BlockSpec((1, GLA_HEADS, GLA_DV, GLA_DK), lambda b, c: (b, 0, 0, 0))],
        out_specs=[pl.BlockSpec((L, hv), lambda b, c: (b * nchunk + c, 0)),
                   pl.BlockSpec((1, GLA_HEADS, GLA_DV, GLA_DK), lambda b, c: (b, 0, 0, 0))],
        out_shape=[jax.ShapeDtypeStruct((nb * nchunk * L, hv), BF16),
                   jax.ShapeDtypeStruct((nb, GLA_HEADS, GLA_DV, GLA_DK), F32)],
        scratch=[])


def _mlstm_kernel(u_ref, mo_ref, g_ref, gb_ref, cw_ref, cb_ref, wqk_ref, wv_ref, ng_ref, tri_ref,
                  conv0_ref, c0_ref, n0_ref, m0_ref,
                  o_ref, c_ref, n_ref, m_ref, ext_ref, *, n_pad, L, phase):
    c = pl.program_id(1)
    W = ML_HEADS * ML_DH

    if phase == "init":
        @pl.when(c == 0)
        def _():
            ext_ref[0:8, :] = conv0_ref[0]
            c_ref[...] = c0_ref[...]
            n_ref[...] = n0_ref[...]
            m_ref[...] = m0_ref[...]
        return

    row = c * L + lax.broadcasted_iota(jnp.int32, (L, 1), 0)
    valid = row >= n_pad

    u = jnp.where(valid, u_ref[...], 0.0)
    ext_ref[8:8 + L, :] = u
    uc = cb_ref[...]
    for i in range(ML_CONV):
        uc = uc + ext_ref[8 - (ML_CONV - 1) + i:8 - (ML_CONV - 1) + i + L, :] * cw_ref[i:i + 1, :]
    ext_ref[0:8, :] = u[L - 8:L, :]
    ucs = (uc * _sigmoid(uc)).astype(BF16)
    ub = u_ref[...].astype(BF16)

    g = g_ref[...] + gb_ref[...]
    li = jnp.where(valid, g, NEG_BIG)
    lf = jnp.where(valid, _log_sigmoid(g), 0.0)
    tri = tri_ref[...]
    lf_hi, lf_lo = _split2(lf)
    b_col = _dot(tri, lf_hi) + _dot(tri, lf_lo)
    a_col = b_col + m_ref[0]
    li_row = li.T
    b_row = b_col.T

    ti = lax.broadcasted_iota(jnp.int32, (L, L), 0)
    si = lax.broadcasted_iota(jnp.int32, (L, L), 1)
    causal = si <= ti
    lane = lax.broadcasted_iota(jnp.int32, (1, 128), 1)
    m_vec = m_ref[0]
    m_out = m_vec
    outs, c_new, n_new = [], [], []

    for h in range(ML_HEADS):
        d = slice(h * ML_DH, (h + 1) * ML_DH)
        qk_proj = _dot(ucs[:, d], wqk_ref[h])
        q = qk_proj[:, 0:ML_DH].astype(BF16)
        k = qk_proj[:, ML_DH:2 * ML_DH] * (ML_DH ** -0.5)
        v = _dot(ub[:, d], wv_ref[h]).astype(BF16)
        lf_l = L_MF + h
        li_l = L_MI + h
        bc = b_col[:, lf_l:lf_l + 1]
        ac = a_col[:, lf_l:lf_l + 1]
        lic = li[:, li_l:li_l + 1]
        dmat = jnp.where(causal, bc - b_row[lf_l:lf_l + 1, :] + li_row[li_l:li_l + 1, :], NEG_BIG)
        mt = jnp.maximum(ac, jnp.max(dmat, axis=-1, keepdims=True))
        w_intra = jnp.exp(dmat - mt)
        w_inter = jnp.exp(ac - mt)
        qk = _dot_nt(q, k.astype(BF16)) * w_intra
        cmat = c_ref[0, h]
        nrow = n_ref[0, h:h + 1, :]
        num = _dot(qk.astype(BF16), v) + w_inter * _dot(q, cmat.astype(BF16))
        qn = jnp.sum(q.astype(F32) * nrow.astype(BF16).astype(F32), axis=-1, keepdims=True)
        den = jnp.sum(qk, axis=-1, keepdims=True) + w_inter * qn
        hh = num / jnp.maximum(jnp.abs(den), jnp.exp(-mt))
        m_new = mt[L - 1:L, :]
        b_end = bc[L - 1:L, :]
        wk = jnp.exp(b_end - bc + lic - m_new)
        ws = jnp.exp(b_end + m_vec[:, lf_l:lf_l + 1] - m_new)
        kw = k * wk
        c_new.append(ws * cmat + _dot_tn(kw.astype(BF16), v))
        n_new.append(ws * nrow + jnp.sum(kw, axis=0, keepdims=True))
        m_out = jnp.where(lane == lf_l, m_new, m_out)
        mu = jnp.mean(hh, axis=-1, keepdims=True)
        hc = hh - mu
        var = jnp.mean(hc * hc, axis=-1, keepdims=True)
        hn = hc * lax.rsqrt(var + LN_EPS) * ng_ref[...]
        outs.append((hn * _sigmoid(mo_ref[:, d])).astype(BF16))
    o_ref[...] = jnp.concatenate(outs, axis=-1)
    c_ref[0] = jnp.stack(c_new, axis=0)
    n_ref[0] = jnp.concatenate(n_new, axis=0)
    m_ref[0] = m_out


def _mlstm_call_parts(z_main, z_small, gbias, cw, cb, wqk, wv, ng, conv0, c0, n0, m0,
                      *, row0, nb, nchunk, chunk):
    L = chunk
    assert row0 % L == 0
    tri = jnp.asarray(_tri_lower(L), BF16)
    rb0 = row0 // L
    W = ML_HEADS * ML_DH

    def rows(colblk):
        return lambda b, c: (rb0 + b * nchunk + c, colblk)

    const2 = lambda b, c: (0, 0)
    const3 = lambda b, c: (0, 0, 0)
    return dict(
        args=(z_main, z_main, z_small, gbias, cw, cb, wqk, wv, ng, tri, conv0, c0, n0, m0),
        in_specs=[pl.BlockSpec((L, W), rows(C_MU // W)),
                  pl.BlockSpec((L, W), rows(C_MO // W)),
                  pl.BlockSpec((L, 128), rows(0)),
                  pl.BlockSpec((1, 128), const2),
                  pl.BlockSpec((ML_CONV, W), const2),
                  pl.BlockSpec((1, W), const2),
                  pl.BlockSpec((ML_HEADS, ML_DH, 2 * ML_DH), const3),
                  pl.BlockSpec((ML_HEADS, ML_DH, ML_DH), const3),
                  pl.BlockSpec((1, ML_DH), const2),
                  pl.BlockSpec((L, L), const2),
                  pl.BlockSpec((1, 8, W), lambda b, c: (b, 0, 0)),
                  pl.BlockSpec((1, ML_HEADS, ML_DH, ML_DH), lambda b, c: (b, 0, 0, 0)),
                  pl.BlockSpec((1, ML_HEADS, ML_DH), lambda b, c: (b, 0, 0)),
                  pl.BlockSpec((1, 1, 128), lambda b, c: (b, 0, 0))],
        out_specs=[pl.BlockSpec((L, W), lambda b, c: (b * nchunk + c, 0)),
                   pl.BlockSpec((1, ML_HEADS, ML_DH, ML_DH), lambda b, c: (b, 0, 0, 0)),
                   pl.BlockSpec((1, ML_HEADS, ML_DH), lambda b, c: (b, 0, 0)),
                   pl.BlockSpec((1, 1, 128), lambda b, c: (b, 0, 0))],
        out_shape=[jax.ShapeDtypeStruct((nb * nchunk * L, W), BF16),
                   jax.ShapeDtypeStruct((nb, ML_HEADS, ML_DH, ML_DH), F32),
                   jax.ShapeDtypeStruct((nb, ML_HEADS, ML_DH), F32),
                   jax.ShapeDtypeStruct((nb, 1, 128), F32)],
        scratch=[pltpu.VMEM((8 + L, W), F32)])


def _recurrent_kernel(*refs, n_gla_in, n_ml_in, n_pad, L):
    g_in = refs[:n_gla_in]
    m_in = refs[n_gla_in:n_gla_in + n_ml_in]
    g_out = refs[n_gla_in + n_ml_in:n_gla_in + n_ml_in + 2]
    m_rest = refs[n_gla_in + n_ml_in + 2:]
    for phase in ("init", "body"):
        _gla_kernel(*g_in, *g_out, n_pad=n_pad, L=L, phase=phase)
        _mlstm_kernel(*m_in, *m_rest, n_pad=n_pad, L=L, phase=phase)


def recurrent_branches(gla_parts, ml_parts, *, nb, nchunk, chunk, n_pad):
    return pl.pallas_call(
        functools.partial(_recurrent_kernel, n_gla_in=len(gla_parts["args"]), n_ml_in=len(ml_parts["args"]),
                          n_pad=n_pad, L=chunk),
        grid=(nb, nchunk),
        in_specs=gla_parts["in_specs"] + ml_parts["in_specs"],
        out_specs=gla_parts["out_specs"] + ml_parts["out_specs"],
        out_shape=gla_parts["out_shape"] + ml_parts["out_shape"],
        scratch_shapes=gla_parts["scratch"] + ml_parts["scratch"],
        compiler_params=_cparams(("arbitrary", "arbitrary")),
        name="gla_mlstm",
    )(*gla_parts["args"], *ml_parts["args"])


def _sb_heads(qs, ks, vs, vis, cs, umat):
    nh = len(qs)
    bq = qs[0].shape[0]
    lss, l1s, parts = [], [], []
    for h in range(nh):
        z = _dot_nt(qs[h], ks[h])
        t = jnp.log(1.0 + jnp.exp(-jnp.abs(z)))
        lss.append(jnp.minimum(z, 0.0) - t)
        l1 = jnp.minimum(-z, 0.0) - t
        if vis is not None:
            l1 = jnp.where(vis, l1, 0.0)
        l1s.append(l1)
        parts.extend(_split2(l1))
    suf = _dot(jnp.concatenate(parts, axis=0), umat)
    pvs, new_cs = [], []
    for h in range(nh):
        suffix = suf[2 * h * bq:(2 * h + 1) * bq] + suf[(2 * h + 1) * bq:(2 * h + 2) * bq]
        att = jnp.exp(lss[h] + suffix + cs[h])
        if vis is not None:
            att = jnp.where(vis, att, 0.0)
        pvs.append(_dot(att.astype(BF16), vs[h]))
        new_cs.append(cs[h] + jnp.sum(l1s[h], axis=-1, keepdims=True))
    return jnp.concatenate(pvs, axis=-1), new_cs, jnp.max(functools.reduce(jnp.maximum, new_cs))


def _sb_prompt_kernel(q_ref, k_ref, v_ref, u_ref, o_ref, acc_ref, *, bq, n_pad):
    qi = pl.program_id(2)
    bk = SB_BK
    assert bq == bk and n_pad < bk
    qpos = lax.broadcasted_iota(jnp.int32, (bq, 1), 0)
    kio = lax.broadcasted_iota(jnp.int32, (1, bk), 1)
    umat = u_ref[...]
    heads = [slice(h * SB_DH, (h + 1) * SB_DH) for h in range(SB_HG)]
    qs = [(q_ref[:, hs] * (SB_DH ** -0.5)).astype(BF16) for hs in heads]

    def block(j, vis, cs):
        start = pl.multiple_of(j * bk, bk)
        return _sb_heads(qs, [k_ref[pl.ds(start, bk), hs] for hs in heads],
                         [v_ref[pl.ds(start, bk), hs] for hs in heads], vis, cs, umat)

    vis_diag = (kio < qpos) & (kio >= jnp.where(qi > 0, 0, n_pad))
    pv, cs, cmax = block(qi, vis_diag, [jnp.zeros((bq, 1), F32) for _ in heads])
    acc_ref[...] = pv

    def body(carry):
        j, cs, _ = carry
        pv, cs, cmax = block(j, None, list(cs))
        acc_ref[...] += pv
        return j - 1, tuple(cs), cmax

    def cond(carry):
        j, _, cmax = carry
        return (j >= 1) & (cmax > -SB_SKIP)

    j, cs, cmax = lax.while_loop(cond, body, (qi - 1, tuple(cs), cmax))

    @pl.when((j == 0) & (cmax > -SB_SKIP))
    def _():
        pv, _, _ = block(0, jnp.broadcast_to(kio >= n_pad, (bq, bk)), list(cs))
        acc_ref[...] += pv

    o_ref[...] = acc_ref[...].astype(BF16)


def sb_attention_prompt(z_main, kvb, umat, *, nb, tp, bq, n_pad):
    nq = tp // bq
    gw = SB_HG * SB_DH
    ng = SB_HEADS // SB_HG
    return pl.pallas_call(
        functools.partial(_sb_prompt_kernel, bq=bq, n_pad=n_pad),
        grid=(nb, ng, nq),
        in_specs=[pl.BlockSpec((bq, gw), lambda b, g, i: (b * nq + i, C_SQ // gw + g)),
                  pl.BlockSpec((tp, gw), lambda b, g, i: (b, C_SK // gw + g)),
                  pl.BlockSpec((tp, gw), lambda b, g, i: (b, C_SV // gw + g)),
                  pl.BlockSpec((SB_BK, SB_BK), lambda b, g, i: (0, 0))],
        out_specs=pl.BlockSpec((bq, gw), lambda b, g, i: (b * nq + i, g)),
        out_shape=jax.ShapeDtypeStruct((nb * tp, SB_HEADS * SB_DH), BF16),
        scratch_shapes=[pltpu.VMEM((bq, gw), F32)],
        compiler_params=_cparams(("arbitrary", "arbitrary", "arbitrary")),
        name="sb_prompt",
    )(z_main, kvb, kvb, umat)


def _sb_sample_kernel(q_ref, kn_ref, vn_ref, kc_ref, vc_ref, u_ref, o_ref, acc_ref, *, n_cache):
    T = CHUNK
    bk = SB_BK
    nfull = n_cache // bk
    rem = n_cache % bk
    umat = u_ref[...]
    heads = [slice(h * SB_DH, (h + 1) * SB_DH) for h in range(SB_HEADS)]
    qs = [(q_ref[:, hs] * (SB_DH ** -0.5)).astype(BF16) for hs in heads]
    vis_new = lax.broadcasted_iota(jnp.int32, (1, T), 1) < lax.broadcasted_iota(jnp.int32, (T, 1), 0)
    pv, cs, cmax = _sb_heads(qs, [kn_ref[:, hs].astype(BF16) for hs in heads],
                             [vn_ref[:, hs].astype(BF16) for hs in heads], vis_new,
                             [jnp.zeros((T, 1), F32) for _ in heads], umat[0:T, 0:T])
    acc_ref[...] = pv

    def cache_block(start, vis, cs):
        return _sb_heads(qs, [kc_ref[0, 0, pl.ds(start, bk), h, :].astype(BF16) for h in range(SB_HEADS)],
                         [vc_ref[0, 0, pl.ds(start, bk), h, :].astype(BF16) for h in range(SB_HEADS)],
                         vis, cs, umat)

    def body(carry):
        j, cs, _ = carry
        pv, cs, cmax = cache_block(pl.multiple_of(n_cache - bk * (j + 1), 8), None, list(cs))
        acc_ref[...] += pv
        return j + 1, tuple(cs), cmax

    def cond(carry):
        j, _, cmax = carry
        return (j < nfull) & (cmax > -SB_SKIP)

    _, cs, cmax = lax.while_loop(cond, body, (jnp.int32(0), tuple(cs), cmax))
    if rem:
        @pl.when(cmax > -SB_SKIP)
        def _():
            vis_rem = jnp.broadcast_to(lax.broadcasted_iota(jnp.int32, (1, bk), 1) < rem, (T, bk))
            pv, _, _ = cache_block(0, vis_rem, list(cs))
            acc_ref[...] += pv
    o_ref[...] = acc_ref[...].astype(BF16)


def sb_attention_sample(z_main, z_kv, cache_k, cache_v, umat, *, layer, row0, nb):
    T = CHUNK
    hd = SB_HEADS * SB_DH
    n_cache = cache_k.shape[2]
    assert n_cache % 8 == 0 and n_cache >= SB_BK
    rb0 = row0 // T
    cache_spec = pl.BlockSpec((1, 1, n_cache, SB_HEADS, SB_DH), lambda b: (layer, b, 0, 0, 0))
    return pl.pallas_call(
        functools.partial(_sb_sample_kernel, n_cache=n_cache),
        grid=(nb,),
        in_specs=[pl.BlockSpec((T, hd), lambda b: (rb0 + b, C_SQ // hd)),
                  pl.BlockSpec((T, hd), lambda b: (rb0 + b, C_SK // hd)),
                  pl.BlockSpec((T, hd), lambda b: (rb0 + b, C_SV // hd)),
                  cache_spec, cache_spec,
                  pl.BlockSpec((SB_BK, SB_BK), lambda b: (0, 0))],
        out_specs=pl.BlockSpec((T, hd), lambda b: (b, 0)),
        out_shape=jax.ShapeDtypeStruct((nb * T, hd), BF16),
        scratch_shapes=[pltpu.VMEM((T, hd), F32)],
        compiler_params=_cparams(("arbitrary",)),
        name="sb_sample",
    )(z_main, z_kv, z_kv, cache_k, cache_v, umat)


def _kv_out_kernel(*refs, depth, nb, nj, rt, n_front):
    z_refs = refs[:depth]
    k_hbm, v_hbm, kbuf, vbuf, sem = refs[depth:]
    l = pl.program_id(0)
    b = pl.program_id(1)
    j = pl.program_id(2)
    lin = (l * nb + b) * nj + j
    total = depth * nb * nj
    slot = lin % 2

    def copies(s, first, dst_lo):
        src_lo, n = (n_front, rt - n_front) if first else (0, rt)
        return [pltpu.make_async_copy(buf.at[s, pl.ds(src_lo, n)], out.at[l, b, pl.ds(dst_lo, n)], sem.at[s, c])
                for c, (buf, out) in enumerate(((kbuf, k_hbm), (vbuf, v_hbm)))]

    def wait_step(step, s):
        jj = step % nj

        @pl.when(jj == 0)
        def _():
            for cp in copies(s, True, 0):
                cp.wait()

        @pl.when(jj != 0)
        def _():
            for cp in copies(s, False, 0):
                cp.wait()

    @pl.when(lin >= 2)
    def _():
        wait_step(lin - 2, slot)

    x = z_refs[0][...]
    for i in range(1, depth):
        x = jnp.where(l == i, z_refs[i][...], x)
    hd = SB_HEADS * SB_DH
    kbuf[slot] = x[:, C_SK:C_SK + hd].reshape(rt, SB_HEADS, SB_DH)
    vbuf[slot] = x[:, C_SV:C_SV + hd].reshape(rt, SB_HEADS, SB_DH)

    @pl.when(j == 0)
    def _():
        for cp in copies(slot, True, 0):
            cp.start()

    @pl.when(j > 0)
    def _():
        for cp in copies(slot, False, j * rt - n_front):
            cp.start()

    @pl.when(lin == total - 1)
    def _():
        if total >= 2:
            wait_step(lin - 1, 1 - slot)
        wait_step(lin, slot)


def kv_state_outputs(z_kvs, *, nb, tp, n_front):
    depth = len(z_kvs)
    rt = _pick(tp, (768, 640, 512, 384, 256))
    assert rt > n_front
    nj = tp // rt
    w = z_kvs[0].shape[1]
    last = nb * nj - 1

    def zspec(i):
        return pl.BlockSpec((rt, w), lambda l, b, j: (jnp.where(l == i, b * nj + j, jnp.where(l < i, 0, last)), 0))

    shape = jax.ShapeDtypeStruct((depth, nb, tp - n_front, SB_HEADS, SB_DH), F32)
    return pl.pallas_call(
        functools.partial(_kv_out_kernel, depth=depth, nb=nb, nj=nj, rt=rt, n_front=n_front),
        grid=(depth, nb, nj),
        in_specs=[zspec(i) for i in range(depth)],
        out_specs=[pl.BlockSpec(memory_space=pl.ANY), pl.BlockSpec(memory_space=pl.ANY)],
        out_shape=[shape, shape],
        scratch_shapes=[pltpu.VMEM((2, rt, SB_HEADS, SB_DH), F32), pltpu.VMEM((2, rt, SB_HEADS, SB_DH), F32),
                        pltpu.SemaphoreType.DMA((2, 2))],
        compiler_params=_cparams(("arbitrary", "arbitrary", "arbitrary")),
        name="kv_state_out",
    )(*z_kvs)


def _merge_kernel(agp_ref, amp_ref, asp_ref, ags_ref, ams_ref, ass_ref, g0_ref, g1_ref, g2_ref, x_ref,
                  wg_ref, wm_ref, ws_ref, wo_ref, lg_ref, lb_ref, o_ref, or_ref, *, alpha, np_tiles):
    is_p = pl.program_id(0) < np_tiles
    ag = jnp.where(is_p, agp_ref[...], ags_ref[...])
    am = jnp.where(is_p, amp_ref[...], ams_ref[...])
    asb = jnp.where(is_p, asp_ref[...], ass_ref[...])
    merged = (_sigmoid(g0_ref[...]) * _dot(ag, wg_ref[...])
              + _sigmoid(g1_ref[...]) * _dot(am, wm_ref[...])
              + _sigmoid(g2_ref[...]) * _dot(asb, ws_ref[...]))
    out = _dot(merged.astype(BF16), wo_ref[...])
    y = _ln_math(alpha * x_ref[...] + out, lg_ref[...], lb_ref[...])
    o_ref[...] = y
    or_ref[...] = y.reshape(or_ref.shape)


def merge_ln1(acts_p, acts_s, z_main, x, wg, wm, ws, wo, lg, lb, alpha):
    n, d = x.shape
    n_p = acts_p[0].shape[0]
    n_s = acts_s[0].shape[0]
    tm = _pick(np.gcd(n_p, n_s), (256, 128, 64))
    assert n_p + n_s == n
    np_tiles = n_p // tm
    row = lambda i: (i, 0)
    row_p = lambda i: (jnp.minimum(i, np_tiles - 1), 0)
    row_s = lambda i: (jnp.maximum(i - np_tiles, 0), 0)
    const = lambda i: (0, 0)
    wspec = pl.BlockSpec((d, d), const)
    gcb = C_GATE // d
    return pl.pallas_call(
        functools.partial(_merge_kernel, alpha=alpha, np_tiles=np_tiles),
        grid=(n // tm,),
        in_specs=[pl.BlockSpec((tm, d), row_p), pl.BlockSpec((tm, d), row_p), pl.BlockSpec((tm, d), row_p),
                  pl.BlockSpec((tm, d), row_s), pl.BlockSpec((tm, d), row_s), pl.BlockSpec((tm, d), row_s),
                  pl.BlockSpec((tm, d), lambda i: (i, gcb)),
                  pl.BlockSpec((tm, d), lambda i: (i, gcb + 1)),
                  pl.BlockSpec((tm, d), lambda i: (i, gcb + 2)),
                  pl.BlockSpec((tm, d), row),
                  wspec, wspec, wspec, wspec,
                  pl.BlockSpec((1, d), const), pl.BlockSpec((1, d), const)],
        out_specs=[pl.BlockSpec((tm, d), row), pl.BlockSpec((tm, ROW_SUB, d // ROW_SUB), lambda i: (i, 0, 0))],
        out_shape=[jax.ShapeDtypeStruct((n, d), F32), jax.ShapeDtypeStruct((n, ROW_SUB, d // ROW_SUB), F32)],
        compiler_params=_cparams(("arbitrary",)),
        name="merge_ln1",
    )(*acts_p, *acts_s, z_main, z_main, z_main, x, wg, wm, ws, wo, lg.reshape(1, d), lb.reshape(1, d))


R_ROWS = 8 + N_EXPERTS


def _router_kernel(x_ref, w_ref, b_ref, u_ref, idx_ref, gate_ref, cnt_ref):
    i = pl.program_id(0)
    tm = x_ref.shape[0]

    @pl.when(i == 0)
    def _():
        cnt_ref[...] = jnp.zeros_like(cnt_ref)

    logits = _dot_nt(w_ref[...].astype(BF16), x_ref[...].astype(BF16)) + b_ref[...]

    g = [logits[j:j + 1, :] for j in range(N_GROUPS)]
    gmax = jnp.maximum(jnp.maximum(g[0], g[1]), jnp.maximum(g[2], g[3]))
    gsel = jnp.where(g[0] == gmax, 0, jnp.where(g[1] == gmax, 1, jnp.where(g[2] == gmax, 2, 3)))
    gden = (jnp.exp(g[0] - gmax) + jnp.exp(g[1] - gmax)) + (jnp.exp(g[2] - gmax) + jnp.exp(g[3] - gmax))
    gprob = 1.0 / gden
    e_in = jnp.where(gsel == 0, logits[8:16, :],
                     jnp.where(gsel == 1, logits[16:24, :],
                               jnp.where(gsel == 2, logits[24:32, :], logits[32:40, :])))
    ridx = lax.broadcasted_iota(jnp.int32, (EXPERTS_PER_GROUP, tm), 0)
    v1 = jnp.max(e_in, axis=0, keepdims=True)
    i1 = jnp.min(jnp.where(e_in == v1, ridx, EXPERTS_PER_GROUP), axis=0, keepdims=True)
    e2 = jnp.where(ridx == i1, -jnp.inf, e_in)
    v2 = jnp.max(e2, axis=0, keepdims=True)
    i2 = jnp.min(jnp.where(e2 == v2, ridx, EXPERTS_PER_GROUP), axis=0, keepdims=True)
    t = jnp.exp(v2 - v1)
    p1 = 1.0 / (1.0 + t)
    gate_ref[0:1, :] = p1 * gprob
    gate_ref[1:2, :] = (t * p1) * gprob
    gate_ref[2:8, :] = jnp.zeros((6, tm), F32)
    eid1 = gsel * EXPERTS_PER_GROUP + i1
    eid2 = gsel * EXPERTS_PER_GROUP + i2

    eidx = lax.broadcasted_iota(jnp.int32, (N_EXPERTS, tm), 0)
    oh1 = eidx == eid1
    oh2 = eidx == eid2
    oh1f = jnp.where(oh1, 1.0, 0.0)
    oh2f = jnp.where(oh2, 1.0, 0.0)
    umat = u_ref[...]
    pre1 = _dot(oh1f.astype(BF16), umat)
    pre2 = _dot(oh2f.astype(BF16), umat)
    cnt1 = jnp.sum(oh1f, axis=1, keepdims=True)
    cnt2 = jnp.sum(oh2f, axis=1, keepdims=True)
    base = cnt_ref[:, 0:1]
    rank1 = jnp.sum(jnp.where(oh1, base + pre1, 0.0), axis=0, keepdims=True)
    rank2 = jnp.sum(jnp.where(oh2, base + cnt1 + pre2, 0.0), axis=0, keepdims=True)
    idx_ref[0:1, :] = eid1
    idx_ref[1:2, :] = eid2
    idx_ref[2:3, :] = rank1.astype(jnp.int32)
    idx_ref[3:4, :] = rank2.astype(jnp.int32)
    idx_ref[4:8, :] = jnp.zeros((4, tm), jnp.int32)
    cnt_ref[...] = cnt_ref[...] + (cnt1 + cnt2)


def moe_router(x, w_r, b_r, umat):
    n, d = x.shape
    tm = umat.shape[0]
    return pl.pallas_call(
        _router_kernel,
        grid=(n // tm,),
        in_specs=[pl.BlockSpec((tm, d), lambda i: (i, 0)),
                  pl.BlockSpec((R_ROWS, d), lambda i: (0, 0)),
                  pl.BlockSpec((R_ROWS, 1), lambda i: (0, 0)),
                  pl.BlockSpec((tm, tm), lambda i: (0, 0))],
        out_specs=[pl.BlockSpec((8, tm), lambda i: (0, i)),
                   pl.BlockSpec((8, tm), lambda i: (0, i)),
                   pl.BlockSpec((N_EXPERTS, 128), lambda i: (0, 0))],
        out_shape=[jax.ShapeDtypeStruct((8, n), jnp.int32),
                   jax.ShapeDtypeStruct((8, n), F32),
                   jax.ShapeDtypeStruct((N_EXPERTS, 128), F32)],
        compiler_params=_cparams(("arbitrary",)),
        name="moe_router",
    )(x, w_r, b_r, umat)


def _row_copy(src_hbm, dst_vmem, src_row, dst_row, sem):
    return pltpu.make_async_copy(src_hbm.at[pl.ds(src_row, 1)], dst_vmem.at[pl.ds(dst_row, 1)], sem)


GATHER_UNROLL = 8


def _start_row_gather(idx_ref, base, src_hbm, dst_vmem, nrows, sem):
    def start(r, carry):
        _row_copy(src_hbm, dst_vmem, idx_ref[base + r], r, sem).start()
        return carry

    lax.fori_loop(0, nrows, start, 0, unroll=GATHER_UNROLL)


def _wait_row_gather(src_hbm, dst_vmem, nrows, sem):
    def wait(r, carry):
        _row_copy(src_hbm, dst_vmem, 0, r, sem).wait()
        return carry

    lax.fori_loop(0, nrows, wait, 0, unroll=GATHER_UNROLL)


def _moe_kernel(be_ref, tok_ref, nblk_ref, x_hbm, wg_ref, wu_ref, wd_ref, o_ref,
                xbuf, wgb, wub, wdb, sem):
    i = pl.program_id(0)
    nblk = nblk_ref[0]
    slot = i % 2

    @pl.when((i == 0) & (nblk > 0))
    def _():
        _start_row_gather(tok_ref, 0, x_hbm, xbuf.at[0], MOE_BLK, sem.at[0])

    @pl.when(i + 1 < nblk)
    def _():
        _start_row_gather(tok_ref, (i + 1) * MOE_BLK, x_hbm, xbuf.at[1 - slot], MOE_BLK, sem.at[1 - slot])

    prev_e = be_ref[jnp.maximum(i - 1, 0)]

    @pl.when((i == 0) | (be_ref[i] != prev_e))
    def _():
        wgb[...] = wg_ref[0, 0].astype(BF16)
        wub[...] = wu_ref[0, 0].astype(BF16)
        wdb[...] = wd_ref[0, 0].astype(BF16)

    @pl.when(i < nblk)
    def _():
        _wait_row_gather(x_hbm, xbuf.at[slot], MOE_BLK, sem.at[slot])
        xb = xbuf[slot].reshape(MOE_BLK, D_MODEL).astype(BF16)
        a = _dot(xb, wgb[...])
        hmid = (a * _sigmoid(a)) * _dot(xb, wub[...])
        o_ref[...] = _dot(hmid.astype(BF16), wdb[...]).reshape(o_ref.shape)

    @pl.when(i >= nblk)
    def _():
        o_ref[...] = jnp.zeros_like(o_ref)


def moe_experts(x, block_e, tok_pad, nblk, w_gate, w_up, w_down, *, layer):
    n = x.shape[0]
    d = D_MODEL
    rs = (ROW_SUB, d // ROW_SUB)
    nb = block_e.shape[0]
    wmap = lambda i, be, tok, nblk: (layer, be[i], 0, 0)
    grid_spec = pltpu.PrefetchScalarGridSpec(
        num_scalar_prefetch=3,
        grid=(nb,),
        in_specs=[pl.BlockSpec(memory_space=pl.ANY),
                  pl.BlockSpec((1, 1, d, D_EXPERT), wmap),
                  pl.BlockSpec((1, 1, d, D_EXPERT), wmap),
                  pl.BlockSpec((1, 1, D_EXPERT, d), wmap)],
        out_specs=pl.BlockSpec((MOE_BLK,) + rs, lambda i, be, tok, nblk: (i, 0, 0)),
        scratch_shapes=[pltpu.VMEM((2, MOE_BLK) + rs, F32),
                        pltpu.VMEM((d, D_EXPERT), BF16),
                        pltpu.VMEM((d, D_EXPERT), BF16),
                        pltpu.VMEM((D_EXPERT, d), BF16),
                        pltpu.SemaphoreType.DMA((2,))],
    )
    return pl.pallas_call(
        _moe_kernel,
        grid_spec=grid_spec,
        out_shape=jax.ShapeDtypeStruct((nb * MOE_BLK,) + rs, F32),
        compiler_params=_cparams(("arbitrary",)),
        name="moe_experts",
    )(block_e, tok_pad, nblk, x, w_gate, w_up, w_down)


def _combine_kernel(dest_ref, yb_hbm, x_ref, gate_ref, lg_ref, lb_ref, o_ref, ob_ref, ybuf, sem, *, alpha, n):
    i = pl.program_id(0)
    nt = pl.num_programs(0)
    tm = x_ref.shape[0]
    slot = i % 2

    def start(tile, s):
        for k in range(TOP_K):
            _start_row_gather(dest_ref, k * n + tile * tm, yb_hbm, ybuf.at[s, k], tm, sem.at[s, k])

    @pl.when(i == 0)
    def _():
        start(0, 0)

    @pl.when(i + 1 < nt)
    def _():
        start(i + 1, 1 - slot)

    for k in range(TOP_K):
        _wait_row_gather(yb_hbm, ybuf.at[slot, k], tm, sem.at[slot, k])
    gate = gate_ref[...]
    d = x_ref.shape[1]
    y = ybuf[slot, 0].reshape(tm, d) * gate[:, 0:1] + ybuf[slot, 1].reshape(tm, d) * gate[:, 1:2]
    out = _ln_math(alpha * x_ref[...] + y, lg_ref[...], lb_ref[...])
    o_ref[...] = out
    ob_ref[...] = out.astype(BF16)


def moe_combine_ln2(x, yb, dest, gate_cols, lg, lb, alpha):
    n, d = x.shape
    tm = _pick(n, (256, 128, 64))
    grid_spec = pltpu.PrefetchScalarGridSpec(
        num_scalar_prefetch=1,
        grid=(n // tm,),
        in_specs=[pl.BlockSpec(memory_space=pl.ANY),
                  pl.BlockSpec((tm, d), lambda i, dest: (i, 0)),
                  pl.BlockSpec((tm, TOP_K), lambda i, dest: (i, 0)),
                  pl.BlockSpec((1, d), lambda i, dest: (0, 0)),
                  pl.BlockSpec((1, d), lambda i, dest: (0, 0))],
        out_specs=[pl.BlockSpec((tm, d), lambda i, dest: (i, 0)), pl.BlockSpec((tm, d), lambda i, dest: (i, 0))],
        scratch_shapes=[pltpu.VMEM((2, TOP_K, tm, ROW_SUB, d // ROW_SUB), F32),
                        pltpu.SemaphoreType.DMA((2, TOP_K))],
    )
    return pl.pallas_call(
        functools.partial(_combine_kernel, alpha=alpha, n=n),
        grid_spec=grid_spec,
        out_shape=[jax.ShapeDtypeStruct((n, d), F32), jax.ShapeDtypeStruct((n, d), BF16)],
        compiler_params=_cparams(("arbitrary",)),
        name="moe_combine_ln2",
    )(dest, yb, x, gate_cols, lg.reshape(1, d), lb.reshape(1, d))


def _tri_lower(n):
    t = np.arange(n)
    return (t[None, :] <= t[:, None]).astype(np.float32)


def _strict_upper(n):
    t = np.arange(n)
    return (t[:, None] < t[None, :]).astype(np.float32)


def kernel(x_prompt, x_sample, state_gla, state_ml_C, state_ml_n, state_ml_m, state_ml_conv, cache_sb_k, cache_sb_v, meta_tokens, ln_in_g, ln_in_b, w_in, gla_w_a2, gla_b_a, gla_norm_g, ml_conv_w, ml_conv_b, ml_wq, ml_wk, ml_wv, ml_b_i, ml_b_f, ml_norm_g, w_br_gla, w_br_ml, w_br_sb, w_out, ln1_g, ln1_b, w_rg, b_rg, w_re, b_re, w_e_gate, w_e_up, w_e_down, ln2_g, ln2_b):
    D = D_MODEL
    B, SEQ, _ = x_prompt.shape
    DB, DS, _ = x_sample.shape
    depth = w_in.shape[0]
    n_cache = cache_sb_k.shape[2]
    assert DS == CHUNK and SEQ % 128 == 0 and x_prompt.shape[2] == D
    alpha = (2.0 * depth) ** 0.25
    TP = PAD_FRONT + N_META + SEQ
    NP = B * TP
    NS = DB * DS
    N = NP + NS
    assert TP % CHUNK_PROMPT == 0 and NP % CHUNK == 0
    nchunk_p = TP // CHUNK_PROMPT

    meta = jnp.broadcast_to(meta_tokens.astype(F32)[None], (B, N_META, D))
    xp = jnp.concatenate([jnp.zeros((B, PAD_FRONT, D), F32), meta, x_prompt], axis=1).reshape(NP, D)
    x = jnp.concatenate([xp, x_sample.reshape(NS, D)], axis=0)
    x, xb = layer_norm_rows(x, ln_in_g, ln_in_b)

    u_sb = jnp.asarray(_strict_upper(SB_BK).T, BF16)
    tm_r = _pick(N, (256, 128, 64))
    u_r = jnp.asarray(_strict_upper(tm_r), BF16)

    A = N * TOP_K
    NB = -(-(A + N_EXPERTS * (MOE_BLK - 1)) // MOE_BLK)

    outs_p = [[] for _ in range(5)]
    outs_s = [[] for _ in range(7)]
    z_kvs = []
    for l in range(depth):
        wi = w_in[l]
        w_main = jnp.concatenate([wi[:, 0:3072], wi[:, 3088:4112], wi[:, 4120:5144], wi[:, 5144:6168],
                                  wi[:, 8216:]], axis=1).astype(BF16)
        w_kv = wi[:, 6168:8216].astype(BF16)
        w_small = jnp.concatenate([wi[:, 3072:3088], wi[:, 4112:4120],
                                   jnp.zeros((D, 128 - 24), F32)], axis=1).astype(BF16)
        z_main = in_projection(xb, w_main, 1024, "in_proj_main")
        z_kv, kvb = in_projection(xb, w_kv, 1024, "in_proj_kv", with_bf16=True)
        z_small = in_projection(xb, w_small, 128, "in_proj_small")
        z_kvs.append(z_kv)

        wa = jnp.concatenate([gla_w_a2[l], jnp.zeros((128 - GLA_RANK, GLA_HEADS * GLA_DK), F32)], axis=0).astype(BF16)
        ba = gla_b_a[l].reshape(1, -1)
        ng = gla_norm_g[l].reshape(1, -1)
        s0_p = jnp.zeros((B, GLA_HEADS, GLA_DV, GLA_DK), F32)
        s0_s = jnp.swapaxes(state_gla[l].astype(F32), -1, -2)
        geo_p = dict(row0=0, nb=B, nchunk=nchunk_p, chunk=CHUNK_PROMPT)
        geo_s = dict(row0=NP, nb=DB, nchunk=1, chunk=CHUNK)
        gla_p = _gla_call_parts(z_main, z_small, wa, ba, ng, s0_p, **geo_p)
        gla_s = _gla_call_parts(z_main, z_small, wa, ba, ng, s0_s, **geo_s)

        gbias = jnp.zeros((1, 128), F32).at[0, L_MI:L_MI + ML_HEADS].set(ml_b_i[l]).at[0, L_MF:L_MF + ML_HEADS].set(ml_b_f[l])
        cw = ml_conv_w[l]
        cb = ml_conv_b[l].reshape(1, -1)
        wqk = jnp.concatenate([ml_wq[l], ml_wk[l]], axis=-1).astype(BF16)
        wv = ml_wv[l].astype(BF16)
        mng = ml_norm_g[l].reshape(1, -1)
        W = ML_HEADS * ML_DH
        conv0_p = jnp.zeros((B, 8, W), F32)
        conv0_s = jnp.concatenate([jnp.zeros((DB, 8 - (ML_CONV - 1), W), F32), state_ml_conv[l].astype(F32)], axis=1)
        m0_s = jnp.zeros((DB, 1, 128), F32).at[:, 0, L_MF:L_MF + ML_HEADS].set(state_ml_m[l].astype(F32))
        ml_p = _mlstm_call_parts(
            z_main, z_small, gbias, cw, cb, wqk, wv, mng, conv0_p,
            jnp.zeros((B, ML_HEADS, ML_DH, ML_DH), F32), jnp.zeros((B, ML_HEADS, ML_DH), F32),
            jnp.zeros((B, 1, 128), F32), **geo_p)
        ml_s = _mlstm_call_parts(
            z_main, z_small, gbias, cw, cb, wqk, wv, mng, conv0_s,
            state_ml_C[l].astype(F32), state_ml_n[l].astype(F32), m0_s, **geo_s)
        a_gla_p, sT_p, a_ml_p, c_p, n_p, m_p = recurrent_branches(
            gla_p, ml_p, nb=B, nchunk=nchunk_p, chunk=CHUNK_PROMPT, n_pad=PAD_FRONT)
        a_gla_s, sT_s, a_ml_s, c_s, n_s, m_s = recurrent_branches(
            gla_s, ml_s, nb=DB, nchunk=1, chunk=CHUNK, n_pad=0)

        a_sb_p = sb_attention_prompt(z_main, kvb, u_sb, nb=B, tp=TP, bq=SB_BK, n_pad=PAD_FRONT)
        a_sb_s = sb_attention_sample(z_main, z_kv, cache_sb_k.astype(F32), cache_sb_v.astype(F32),
                                     u_sb, layer=l, row0=NP, nb=DB)

        x1, x1r = merge_ln1((a_gla_p, a_ml_p, a_sb_p), (a_gla_s, a_ml_s, a_sb_s), z_main, x,
                       w_br_gla[l].astype(BF16), w_br_ml[l].astype(BF16), w_br_sb[l].astype(BF16),
                       w_out[l].astype(BF16), ln1_g[l], ln1_b[l], alpha)

        w_r = jnp.concatenate([w_rg[l].T, jnp.zeros((8 - N_GROUPS, D), F32), w_re[l].T], axis=0)
        b_r = jnp.concatenate([b_rg[l], jnp.zeros((8 - N_GROUPS,), F32), b_re[l]]).reshape(R_ROWS, 1)
        idx, gates, cnt = moe_router(x1, w_r, b_r, u_r)
        counts = cnt[:, 0].astype(jnp.int32)
        padded = (counts + MOE_BLK - 1) // MOE_BLK * MOE_BLK
        ends_pad = jnp.cumsum(padded)
        start_pad = ends_pad - padded
        eids = jnp.arange(N_EXPERTS, dtype=jnp.int32)
        start_of = jnp.sum(jnp.where(idx[0:2, :, None] == eids, start_pad, 0), axis=-1)
        dest = (start_of + idx[2:4]).reshape(-1)
        tok = jnp.tile(jnp.arange(N, dtype=jnp.int32), TOP_K)
        tok_pad = jnp.zeros((NB * MOE_BLK,), jnp.int32).at[dest].set(tok)
        blk_lo = jnp.arange(NB, dtype=jnp.int32) * MOE_BLK
        block_e = jnp.minimum(jnp.sum((ends_pad[None, :] <= blk_lo[:, None]).astype(jnp.int32), axis=-1),
                              N_EXPERTS - 1)
        nblk = (ends_pad[-1:] // MOE_BLK).astype(jnp.int32)
        yb = moe_experts(x1r, block_e, tok_pad, nblk, w_e_gate, w_e_up, w_e_down, layer=l)
        x, xb = moe_combine_ln2(x1, yb, dest, gates[0:2].T, ln2_g[l], ln2_b[l], alpha)

        nct = ML_CONV - 1
        conv_p = jnp.stack([z_main[(b + 1) * TP - nct:(b + 1) * TP, C_MU:C_MU + W] for b in range(B)], axis=0)
        conv_s = z_main[NP:, C_MU:C_MU + W].reshape(DB, DS, W)[:, DS - nct:]
        zkv_s = z_kv[NP:].reshape(DB, DS, 2 * D)
        kv_shape = (SB_HEADS, SB_DH)
        outs_p[0].append(jnp.swapaxes(sT_p, -1, -2))
        outs_s[0].append(jnp.swapaxes(sT_s, -1, -2))
        outs_p[1].append(c_p); outs_s[1].append(c_s)
        outs_p[2].append(n_p); outs_s[2].append(n_s)
        outs_p[3].append(m_p[:, 0, L_MF:L_MF + ML_HEADS]); outs_s[3].append(m_s[:, 0, L_MF:L_MF + ML_HEADS])
        outs_p[4].append(conv_p); outs_s[4].append(conv_s)
        outs_s[5].append(zkv_s[:, :, C_SK:C_SK + D].reshape(DB, DS, *kv_shape))
        outs_s[6].append(zkv_s[:, :, C_SV:C_SV + D].reshape(DB, DS, *kv_shape))

    sbk_p, sbv_p = kv_state_outputs(z_kvs, nb=B, tp=TP, n_front=PAD_FRONT)
    sp = [jnp.stack(a, axis=0) for a in outs_p]
    ss = [jnp.stack(a, axis=0) for a in outs_s]
    y_prompt = x[:NP].reshape(B, TP, D)[:, PAD_FRONT + N_META:]
    y_sample = x[NP:].reshape(DB, DS, D)
    return (y_prompt, y_sample, sp[0], ss[0], sp[1], ss[1], sp[2], ss[2], sp[3], ss[3],
            sp[4], ss[4], sbk_p, ss[5], sbv_p, ss[6])
```

```python
import functools

import numpy as np
import jax
import jax.numpy as jnp
from jax import lax
from jax.experimental import pallas as pl
from jax.experimental.pallas import tpu as pltpu

F32 = jnp.float32
BF16 = jnp.bfloat16

D_MODEL = 1024
N_META = 16
CHUNK = 64
CHUNK_PROMPT = 256
PAD_FRONT = 240
GLA_HEADS, GLA_DK, GLA_DV, GLA_RANK, GLA_TAU = 4, 128, 256, 16, 16.0
ML_HEADS, ML_DH, ML_CONV = 4, 256, 4
SB_HEADS, SB_DH = 8, 128
N_GROUPS, EXPERTS_PER_GROUP, TOP_K = 4, 8, 2
N_EXPERTS = N_GROUPS * EXPERTS_PER_GROUP
D_EXPERT = D_MODEL // 2
MOE_BLK = 256
ROW_SUB = 8
LN_EPS = 1e-5
NEG_BIG = -1e30
SB_SKIP = 110.0
SB_BK = 256

C_GQ, C_GK, C_GV, C_GR = 0, 512, 1024, 2048
C_MU, C_MO = 3072, 4096
C_SQ = 5120
C_GATE = 6144
C_SK, C_SV = 0, 1024
SB_HG = 4
L_MI, L_MF = 16, 20

VMEM_LIMIT = 48 * 1024 * 1024


def _cparams(sem):
    return pltpu.CompilerParams(dimension_semantics=sem, vmem_limit_bytes=VMEM_LIMIT)


def _pick(n, cands):
    for c in cands:
        if n % c == 0:
            return c
    raise ValueError(f"no tile in {cands} divides {n}")


def _log_sigmoid(x):
    return jnp.minimum(x, 0.0) - jnp.log1p(jnp.exp(-jnp.abs(x)))


def _sigmoid(x):
    return 1.0 / (1.0 + jnp.exp(-x))


def _split2(x):
    hi = x.astype(BF16)
    lo = (x - hi.astype(F32)).astype(BF16)
    return hi, lo


def _dot(a, b):
    return jnp.dot(a, b, preferred_element_type=F32)


def _dot_nt(a, b):
    return lax.dot_general(a, b, (((1,), (1,)), ((), ())), preferred_element_type=F32)


def _dot_tn(a, b):
    return lax.dot_general(a, b, (((0,), (0,)), ((), ())), preferred_element_type=F32)


def _ln_math(x, g, b):
    mu = jnp.mean(x, axis=-1, keepdims=True)
    xc = x - mu
    var = jnp.mean(xc * xc, axis=-1, keepdims=True)
    return xc * lax.rsqrt(var + LN_EPS) * g + b


def _ln_kernel(x_ref, g_ref, b_ref, o_ref, ob_ref):
    y = _ln_math(x_ref[...], g_ref[...], b_ref[...])
    o_ref[...] = y
    ob_ref[...] = y.astype(BF16)


def layer_norm_rows(x, g, b):
    n, d = x.shape
    tm = _pick(n, (512, 256, 128, 64))
    return pl.pallas_call(
        _ln_kernel,
        grid=(n // tm,),
        in_specs=[pl.BlockSpec((tm, d), lambda i: (i, 0)),
                  pl.BlockSpec((1, d), lambda i: (0, 0)),
                  pl.BlockSpec((1, d), lambda i: (0, 0))],
        out_specs=[pl.BlockSpec((tm, d), lambda i: (i, 0)), pl.BlockSpec((tm, d), lambda i: (i, 0))],
        out_shape=[jax.ShapeDtypeStruct((n, d), F32), jax.ShapeDtypeStruct((n, d), BF16)],
        compiler_params=_cparams(("arbitrary",)),
        name="ln_in",
    )(x, g.reshape(1, d), b.reshape(1, d))


def _mm_kernel(x_ref, w_ref, o_ref):
    o_ref[...] = _dot(x_ref[...], w_ref[...])


def _mm2_kernel(x_ref, w_ref, o_ref, ob_ref):
    z = _dot(x_ref[...], w_ref[...])
    o_ref[...] = z
    ob_ref[...] = z.astype(BF16)


def in_projection(x, w_bf16, tn, name, with_bf16=False):
    n, d = x.shape
    cols = w_bf16.shape[1]
    tm = _pick(n, (896, 768, 512, 384, 256, 128, 64))
    out_spec = pl.BlockSpec((tm, tn), lambda j, i: (i, j))
    return pl.pallas_call(
        _mm2_kernel if with_bf16 else _mm_kernel,
        grid=(cols // tn, n // tm),
        in_specs=[pl.BlockSpec((tm, d), lambda j, i: (i, 0)),
                  pl.BlockSpec((d, tn), lambda j, i: (0, j))],
        out_specs=[out_spec, out_spec] if with_bf16 else out_spec,
        out_shape=([jax.ShapeDtypeStruct((n, cols), F32), jax.ShapeDtypeStruct((n, cols), BF16)]
                   if with_bf16 else jax.ShapeDtypeStruct((n, cols), F32)),
        compiler_params=_cparams(("arbitrary", "arbitrary")),
        name=name,
    )(x, w_bf16)


def _gla_levels(L):
    return tuple(L >> i for i in range(1, L.bit_length()))


def _gla_level_matrix(L):
    t = np.arange(L)[:, None]
    j = np.arange(L)[None, :]
    blocks = []
    for m in (L,) + _gla_levels(L):
        blocks.append(((j <= t) & (j // m == t // m)).astype(np.float32))
    for m in (L,) + _gla_levels(L):
        blocks.append(((j > t) & (j // m == t // m)).astype(np.float32))
    return np.concatenate(blocks, axis=0)


def _gla_kernel(q_ref, k_ref, v_ref, r_ref, lr_ref, wa_ref, ba_ref, ng_ref, lvl_ref, s0_ref,
                o_ref, s_ref, *, n_pad, L, phase):
    c = pl.program_id(1)
    GLA_LEVELS = _gla_levels(L)

    if phase == "init":
        @pl.when(c == 0)
        def _():
            s_ref[...] = s0_ref[...]
        return

    row = c * L + lax.broadcasted_iota(jnp.int32, (L, 1), 0)
    valid = row >= n_pad
    x = _dot(lr_ref[...].astype(BF16), wa_ref[...]) + ba_ref[...]
    la = jnp.where(valid, _log_sigmoid(x) * (1.0 / GLA_TAU), 0.0)
    la_hi, la_lo = _split2(la)
    lvl = lvl_ref[...]
    ex = jnp.exp(_dot(lvl, la_hi) + _dot(lvl, la_lo))
    nlev = len(GLA_LEVELS) + 1

    ti = lax.broadcasted_iota(jnp.int32, (L, L), 0)
    si = lax.broadcasted_iota(jnp.int32, (L, L), 1)
    level_masks = []
    for m in GLA_LEVELS:
        sh = m.bit_length() - 1
        tb = lax.shift_right_logical(ti, sh)
        sb = lax.shift_right_logical(si, sh)
        level_masks.append((lax.shift_right_logical(tb, 1) == lax.shift_right_logical(sb, 1))
                           & ((tb & 1) == 1) & ((sb & 1) == 0))
    eye = ti == si

    outs = []
    states = []
    for h in range(GLA_HEADS):
        dk = slice(h * GLA_DK, (h + 1) * GLA_DK)
        dv = slice(h * GLA_DV, (h + 1) * GLA_DV)
        q = q_ref[:, dk] * (GLA_DK ** -0.5)
        k = jnp.where(valid, k_ref[:, dk], 0.0)
        v = jnp.where(valid, v_ref[:, dv], 0.0).astype(BF16)
        att = jnp.where(eye, _dot_nt(q.astype(BF16), k.astype(BF16)), 0.0)
        for lev in range(len(GLA_LEVELS)):
            qe = ex[(1 + lev) * L:(2 + lev) * L, dk]
            ke = ex[(nlev + 1 + lev) * L:(nlev + 2 + lev) * L, dk]
            a = _dot_nt((q * qe).astype(BF16), (k * ke).astype(BF16))
            att = jnp.where(level_masks[lev], a, att)
        eb = ex[0:L, dk]
        st = s_ref[0, h]
        o = _dot(att.astype(BF16), v) + _dot_nt((q * eb).astype(BF16), st.astype(BF16))
        ke_end = ex[nlev * L:(nlev + 1) * L, dk]
        states.append(st * eb[L - 1:L, :] + _dot_tn(v, (k * ke_end).astype(BF16)))
        mu = jnp.mean(o, axis=-1, keepdims=True)
        oc = o - mu
        var = jnp.mean(oc * oc, axis=-1, keepdims=True)
        on = oc * lax.rsqrt(var + LN_EPS) * ng_ref[...]
        r = r_ref[:, dv]
        outs.append((on * (r * _sigmoid(r))).astype(BF16))
    o_ref[...] = jnp.concatenate(outs, axis=-1)
    s_ref[0] = jnp.stack(states, axis=0)


def _gla_call_parts(z_main, z_small, wa, ba, ng, s0t, *, row0, nb, nchunk, chunk):
    L = chunk
    assert row0 % L == 0
    lvl = jnp.asarray(_gla_level_matrix(L), BF16)
    rb0 = row0 // L
    hk = GLA_HEADS * GLA_DK
    hv = GLA_HEADS * GLA_DV

    def rows(colblk):
        return lambda b, c: (rb0 + b * nchunk + c, colblk)

    return dict(
        args=(z_main, z_main, z_main, z_main, z_small, wa, ba, ng, lvl, s0t),
        in_specs=[pl.BlockSpec((L, hk), rows(C_GQ // hk)),
                  pl.BlockSpec((L, hk), rows(C_GK // hk)),
                  pl.BlockSpec((L, hv), rows(C_GV // hv)),
                  pl.BlockSpec((L, hv), rows(C_GR // hv)),
                  pl.BlockSpec((L, 128), rows(0)),
                  pl.BlockSpec((128, hk), lambda b, c: (0, 0)),
                  pl.BlockSpec((1, hk), lambda b, c: (0, 0)),
                  pl.BlockSpec((1, GLA_DV), lambda b, c: (0, 0)),
                  pl.BlockSpec(lvl.shape, lambda b, c: (0, 0)),
                  pl.BlockSpec((1, GLA_HEADS, GLA_DV, GLA_DK), lambda b, c: (b, 0, 0, 0))],
        out_specs=[pl.BlockSpec((L, hv), lambda b, c: (b * nchunk + c, 0)),
                   pl.BlockSpec((1, GLA_HEADS, GLA_DV, GLA_DK), lambda b, c: (b, 0, 0, 0))],
        out_shape=[jax.ShapeDtypeStruct((nb * nchunk * L, hv), BF16),
                   jax.ShapeDtypeStruct((nb, GLA_HEADS, GLA_DV, GLA_DK), F32)],
        scratch=[])


def _mlstm_kernel(u_ref, mo_ref, g_ref, gb_ref, cw_ref, cb_ref, wqk_ref, wv_ref, ng_ref, tri_ref,
                  conv0_ref, c0_ref, n0_ref, m0_ref,
                  o_ref, c_ref, n_ref, m_ref, ext_ref, *, n_pad, L, phase):
    c = pl.program_id(1)
    W = ML_HEADS * ML_DH

    if phase == "init":
        @pl.when(c == 0)
        def _():
            ext_ref[0:8, :] = conv0_ref[0]
            c_ref[...] = c0_ref[0]
            n_ref[...] = n0_ref[...]
            m_ref[...] = m0_ref[...]
        return

    row = c * L + lax.broadcasted_iota(jnp.int32, (L, 1), 0)
    valid = row >= n_pad

    u = jnp.where(valid, u_ref[...], 0.0)
    ext_ref[8:8 + L, :] = u
    uc = cb_ref[...]
    for i in range(ML_CONV):
        uc = uc + ext_ref[8 - (ML_CONV - 1) + i:8 - (ML_CONV - 1) + i + L, :] * cw_ref[i:i + 1, :]
    ext_ref[0:8, :] = u[L - 8:L, :]
    ucs = (uc * _sigmoid(uc)).astype(BF16)
    ub = u_ref[...].astype(BF16)

    g = g_ref[...] + gb_ref[...]
    li = jnp.where(valid, g, NEG_BIG)
    lf = jnp.where(valid, _log_sigmoid(g), 0.0)
    tri = tri_ref[...]
    lf_hi, lf_lo = _split2(lf)
    b_col = _dot(tri, lf_hi) + _dot(tri, lf_lo)
    a_col = b_col + m_ref[0]
    li_row = li.T
    b_row = b_col.T

    ti = lax.broadcasted_iota(jnp.int32, (L, L), 0)
    si = lax.broadcasted_iota(jnp.int32, (L, L), 1)
    causal = si <= ti
    lane = lax.broadcasted_iota(jnp.int32, (1, 128), 1)
    m_vec = m_ref[0]
    m_out = m_vec
    outs, c_new, n_new = [], [], []

    for h in range(ML_HEADS):
        d = slice(h * ML_DH, (h + 1) * ML_DH)
        qk_proj = _dot(ucs[:, d], wqk_ref[h])
        q = qk_proj[:, 0:ML_DH].astype(BF16)
        k = qk_proj[:, ML_DH:2 * ML_DH] * (ML_DH ** -0.5)
        v = _dot(ub[:, d], wv_ref[h]).astype(BF16)
        lf_l = L_MF + h
        li_l = L_MI + h
        bc = b_col[:, lf_l:lf_l + 1]
        ac = a_col[:, lf_l:lf_l + 1]
        lic = li[:, li_l:li_l + 1]
        dmat = jnp.where(causal, bc - b_row[lf_l:lf_l + 1, :] + li_row[li_l:li_l + 1, :], NEG_BIG)
        mt = jnp.maximum(ac, jnp.max(dmat, axis=-1, keepdims=True))
        w_intra = jnp.exp(dmat - mt)
        w_inter = jnp.exp(ac - mt)
        qk = _dot_nt(q, k.astype(BF16)) * w_intra
        cmat = c_ref[0, h]
        nrow = n_ref[0, h:h + 1, :]
        num = _dot(qk.astype(BF16), v) + w_inter * _dot(q, cmat.astype(BF16))
        qn = jnp.sum(q.astype(F32) * nrow.astype(BF16).astype(F32), axis=-1, keepdims=True)
        den = jnp.sum(qk, axis=-1, keepdims=True) + w_inter * qn
        hh = num / jnp.maximum(jnp.abs(den), jnp.exp(-mt))
        m_new = mt[L - 1:L, :]
        b_end = bc[L - 1:L, :]
        wk = jnp.exp(b_end - bc + lic - m_new)
        ws = jnp.exp(b_end + m_vec[:, lf_l:lf_l + 1] - m_new)
        kw = k * wk
        c_new.append(ws * cmat + _dot_tn(kw.astype(BF16), v))
        n_new.append(ws * nrow + jnp.sum(kw, axis=0, keepdims=True))
        m_out = jnp.where(lane == lf_l, m_new, m_out)
        mu = jnp.mean(hh, axis=-1, keepdims=True)
        hc = hh - mu
        var = jnp.mean(hc * hc, axis=-1, keepdims=True)
        hn = hc * lax.rsqrt(var + LN_EPS) * ng_ref[...]
        outs.append((hn * _sigmoid(mo_ref[:, d])).astype(BF16))
    o_ref[...] = jnp.concatenate(outs, axis=-1)
    c_ref[0] = jnp.stack(c_new, axis=0)
    n_ref[0] = jnp.concatenate(n_new, axis=0)
    m_ref[0] = m_out


def _mlstm_call_parts(z_main, z_small, gbias, cw, cb, wqk, wv, ng, conv0, c0, n0, m0,
                      *, c0_layer, row0, nb, nchunk, chunk):
    L = chunk
    assert row0 % L == 0
    tri = jnp.asarray(_tri_lower(L), BF16)
    rb0 = row0 // L
    W = ML_HEADS * ML_DH

    def rows(colblk):
        return lambda b, c: (rb0 + b * nchunk + c, colblk)

    const2 = lambda b, c: (0, 0)
    const3 = lambda b, c: (0, 0, 0)
    return dict(
        args=(z_main, z_main, z_small, gbias, cw, cb, wqk, wv, ng, tri, conv0, c0, n0, m0),
        in_specs=[pl.BlockSpec((L, W), rows(C_MU // W)),
                  pl.BlockSpec((L, W), rows(C_MO // W)),
                  pl.BlockSpec((L, 128), rows(0)),
                  pl.BlockSpec((1, 128), const2),
                  pl.BlockSpec((ML_CONV, W), const2),
                  pl.BlockSpec((1, W), const2),
                  pl.BlockSpec((ML_HEADS, ML_DH, 2 * ML_DH), const3),
                  pl.BlockSpec((ML_HEADS, ML_DH, ML_DH), const3),
                  pl.BlockSpec((1, ML_DH), const2),
                  pl.BlockSpec((L, L), const2),
                  pl.BlockSpec((1, 8, W), lambda b, c: (b, 0, 0)),
                  pl.BlockSpec((1, 1, ML_HEADS, ML_DH, ML_DH), lambda b, c: (c0_layer, b, 0, 0, 0)),
                  pl.BlockSpec((1, ML_HEADS, ML_DH), lambda b, c: (b, 0, 0)),
                  pl.BlockSpec((1, 1, 128), lambda b, c: (b, 0, 0))],
        out_specs=[pl.BlockSpec((L, W), lambda b, c: (b * nchunk + c, 0)),
                   pl.BlockSpec((1, ML_HEADS, ML_DH, ML_DH), lambda b, c: (b, 0, 0, 0)),
                   pl.BlockSpec((1, ML_HEADS, ML_DH), lambda b, c: (b, 0, 0)),
                   pl.BlockSpec((1, 1, 128), lambda b, c: (b, 0, 0))],
        out_shape=[jax.ShapeDtypeStruct((nb * nchunk * L, W), BF16),
                   jax.ShapeDtypeStruct((nb, ML_HEADS, ML_DH, ML_DH), F32),
                   jax.ShapeDtypeStruct((nb, ML_HEADS, ML_DH), F32),
                   jax.ShapeDtypeStruct((nb, 1, 128), F32)],
        scratch=[pltpu.VMEM((8 + L, W), F32)])


def _recurrent_kernel(*refs, n_gla_in, n_ml_in, n_pad, L):
    g_in = refs[:n_gla_in]
    m_in = refs[n_gla_in:n_gla_in + n_ml_in]
    g_out = refs[n_gla_in + n_ml_in:n_gla_in + n_ml_in + 2]
    m_rest = refs[n_gla_in + n_ml_in + 2:]
    for phase in ("init", "body"):
        _gla_kernel(*g_in, *g_out, n_pad=n_pad, L=L, phase=phase)
        _mlstm_kernel(*m_in, *m_rest, n_pad=n_pad, L=L, phase=phase)


def recurrent_branches(gla_parts, ml_parts, *, nb, nchunk, chunk, n_pad):
    return pl.pallas_call(
        functools.partial(_recurrent_kernel, n_gla_in=len(gla_parts["args"]), n_ml_in=len(ml_parts["args"]),
                          n_pad=n_pad, L=chunk),
        grid=(nb, nchunk),
        in_specs=gla_parts["in_specs"] + ml_parts["in_specs"],
        out_specs=gla_parts["out_specs"] + ml_parts["out_specs"],
        out_shape=gla_parts["out_shape"] + ml_parts["out_shape"],
        scratch_shapes=gla_parts["scratch"] + ml_parts["scratch"],
        compiler_params=_cparams(("arbitrary", "arbitrary")),
        name="gla_mlstm",
    )(*gla_parts["args"], *ml_parts["args"])


def _sb_heads(qs, ks, vs, vis, cs, umat):
    nh = len(qs)
    bq = qs[0].shape[0]
    lss, l1s, parts = [], [], []
    for h in range(nh):
        z = _dot_nt(qs[h], ks[h])
        t = jnp.log(1.0 + jnp.exp(-jnp.abs(z)))
        lss.append(jnp.minimum(z, 0.0) - t)
        l1 = jnp.minimum(-z, 0.0) - t
        if vis is not None:
            l1 = jnp.where(vis, l1, 0.0)
        l1s.append(l1)
        parts.extend(_split2(l1))
    suf = _dot(jnp.concatenate(parts, axis=0), umat)
    pvs, new_cs = [], []
    for h in range(nh):
        suffix = suf[2 * h * bq:(2 * h + 1) * bq] + suf[(2 * h + 1) * bq:(2 * h + 2) * bq]
        att = jnp.exp(lss[h] + suffix + cs[h])
        if vis is not None:
            att = jnp.where(vis, att, 0.0)
        pvs.append(_dot(att.astype(BF16), vs[h]))
        new_cs.append(cs[h] + jnp.sum(l1s[h], axis=-1, keepdims=True))
    return jnp.concatenate(pvs, axis=-1), new_cs, jnp.max(functools.reduce(jnp.maximum, new_cs))


def _sb_prompt_kernel(q_ref, k_ref, v_ref, u_ref, o_ref, acc_ref, *, bq, n_pad):
    qi = pl.program_id(2)
    bk = SB_BK
    assert bq == bk and n_pad < bk
    qpos = lax.broadcasted_iota(jnp.int32, (bq, 1), 0)
    kio = lax.broadcasted_iota(jnp.int32, (1, bk), 1)
    umat = u_ref[...]
    heads = [slice(h * SB_DH, (h + 1) * SB_DH) for h in range(SB_HG)]
    qs = [(q_ref[:, hs] * (SB_DH ** -0.5)).astype(BF16) for hs in heads]

    def block(j, vis, cs):
        start = pl.multiple_of(j * bk, bk)
        return _sb_heads(qs, [k_ref[pl.ds(start, bk), hs] for hs in heads],
                         [v_ref[pl.ds(start, bk), hs] for hs in heads], vis, cs, umat)

    vis_diag = (kio < qpos) & (kio >= jnp.where(qi > 0, 0, n_pad))
    pv, cs, cmax = block(qi, vis_diag, [jnp.zeros((bq, 1), F32) for _ in heads])
    acc_ref[...] = pv

    def body(carry):
        j, cs, _ = carry
        pv, cs, cmax = block(j, None, list(cs))
        acc_ref[...] += pv
        return j - 1, tuple(cs), cmax

    def cond(carry):
        j, _, cmax = carry
        return (j >= 1) & (cmax > -SB_SKIP)

    j, cs, cmax = lax.while_loop(cond, body, (qi - 1, tuple(cs), cmax))

    @pl.when((j == 0) & (cmax > -SB_SKIP))
    def _():
        pv, _, _ = block(0, jnp.broadcast_to(kio >= n_pad, (bq, bk)), list(cs))
        acc_ref[...] += pv

    o_ref[...] = acc_ref[...].astype(BF16)


def sb_attention_prompt(z_main, kvb, umat, *, nb, tp, bq, n_pad):
    nq = tp // bq
    gw = SB_HG * SB_DH
    ng = SB_HEADS // SB_HG
    return pl.pallas_call(
        functools.partial(_sb_prompt_kernel, bq=bq, n_pad=n_pad),
        grid=(nb, ng, nq),
        in_specs=[pl.BlockSpec((bq, gw), lambda b, g, i: (b * nq + i, C_SQ // gw + g)),
                  pl.BlockSpec((tp, gw), lambda b, g, i: (b, C_SK // gw + g)),
                  pl.BlockSpec((tp, gw), lambda b, g, i: (b, C_SV // gw + g)),
                  pl.BlockSpec((SB_BK, SB_BK), lambda b, g, i: (0, 0))],
        out_specs=pl.BlockSpec((bq, gw), lambda b, g, i: (b * nq + i, g)),
        out_shape=jax.ShapeDtypeStruct((nb * tp, SB_HEADS * SB_DH), BF16),
        scratch_shapes=[pltpu.VMEM((bq, gw), F32)],
        compiler_params=_cparams(("arbitrary", "arbitrary", "arbitrary")),
        name="sb_prompt",
    )(z_main, kvb, kvb, umat)


def _sb_sample_kernel(q_ref, kn_ref, vn_ref, kc_ref, vc_ref, u_ref, o_ref, acc_ref, *, n_cache):
    T = CHUNK
    bk = SB_BK
    nfull = n_cache // bk
    rem = n_cache % bk
    umat = u_ref[...]
    heads = [slice(h * SB_DH, (h + 1) * SB_DH) for h in range(SB_HEADS)]
    qs = [(q_ref[:, hs] * (SB_DH ** -0.5)).astype(BF16) for hs in heads]
    vis_new = lax.broadcasted_iota(jnp.int32, (1, T), 1) < lax.broadcasted_iota(jnp.int32, (T, 1), 0)
    pv, cs, cmax = _sb_heads(qs, [kn_ref[:, hs].astype(BF16) for hs in heads],
                             [vn_ref[:, hs].astype(BF16) for hs in heads], vis_new,
                             [jnp.zeros((T, 1), F32) for _ in heads], umat[0:T, 0:T])
    acc_ref[...] = pv

    def cache_block(start, vis, cs):
        return _sb_heads(qs, [kc_ref[0, 0, pl.ds(start, bk), h, :].astype(BF16) for h in range(SB_HEADS)],
                         [vc_ref[0, 0, pl.ds(start, bk), h, :].astype(BF16) for h in range(SB_HEADS)],
                         vis, cs, umat)

    def body(carry):
        j, cs, _ = carry
        pv, cs, cmax = cache_block(pl.multiple_of(n_cache - bk * (j + 1), 8), None, list(cs))
        acc_ref[...] += pv
        return j + 1, tuple(cs), cmax

    def cond(carry):
        j, _, cmax = carry
        return (j < nfull) & (cmax > -SB_SKIP)

    _, cs, cmax = lax.while_loop(cond, body, (jnp.int32(0), tuple(cs), cmax))
    if rem:
        @pl.when(cmax > -SB_SKIP)
        def _():
            vis_rem = jnp.broadcast_to(lax.broadcasted_iota(jnp.int32, (1, bk), 1) < rem, (T, bk))
            pv, _, _ = cache_block(0, vis_rem, list(cs))
            acc_ref[...] += pv
    o_ref[...] = acc_ref[...].astype(BF16)


def sb_attention_sample(z_main, z_kv, cache_k, cache_v, umat, *, layer, row0, nb):
    T = CHUNK
    hd = SB_HEADS * SB_DH
    n_cache = cache_k.shape[2]
    assert n_cache % 8 == 0 and n_cache >= SB_BK
    rb0 = row0 // T
    cache_spec = pl.BlockSpec((1, 1, n_cache, SB_HEADS, SB_DH), lambda b: (layer, b, 0, 0, 0))
    return pl.pallas_call(
        functools.partial(_sb_sample_kernel, n_cache=n_cache),
        grid=(nb,),
        in_specs=[pl.BlockSpec((T, hd), lambda b: (rb0 + b, C_SQ // hd)),
                  pl.BlockSpec((T, hd), lambda b: (rb0 + b, C_SK // hd)),
                  pl.BlockSpec((T, hd), lambda b: (rb0 + b, C_SV // hd)),
                  cache_spec, cache_spec,
                  pl.BlockSpec((SB_BK, SB_BK), lambda b: (0, 0))],
        out_specs=pl.BlockSpec((T, hd), lambda b: (b, 0)),
        out_shape=jax.ShapeDtypeStruct((nb * T, hd), BF16),
        scratch_shapes=[pltpu.VMEM((T, hd), F32)],
        compiler_params=_cparams(("arbitrary",)),
        name="sb_sample",
    )(z_main, z_kv, z_kv, cache_k, cache_v, umat)


def _kv_out_kernel(*refs, depth, nb, nj, rt, n_front):
    z_refs = refs[:depth]
    k_hbm, v_hbm, kbuf, vbuf, sem = refs[depth:]
    l = pl.program_id(0)
    b = pl.program_id(1)
    j = pl.program_id(2)
    lin = (l * nb + b) * nj + j
    total = depth * nb * nj
    slot = lin % 2

    def copies(s, first, dst_lo):
        src_lo, n = (n_front, rt - n_front) if first else (0, rt)
        return [pltpu.make_async_copy(buf.at[s, pl.ds(src_lo, n)], out.at[l, b, pl.ds(dst_lo, n)], sem.at[s, c])
                for c, (buf, out) in enumerate(((kbuf, k_hbm), (vbuf, v_hbm)))]

    def wait_step(step, s):
        jj = step % nj

        @pl.when(jj == 0)
        def _():
            for cp in copies(s, True, 0):
                cp.wait()

        @pl.when(jj != 0)
        def _():
            for cp in copies(s, False, 0):
                cp.wait()

    @pl.when(lin >= 2)
    def _():
        wait_step(lin - 2, slot)

    x = z_refs[0][...]
    for i in range(1, depth):
        x = jnp.where(l == i, z_refs[i][...], x)
    hd = SB_HEADS * SB_DH
    kbuf[slot] = x[:, C_SK:C_SK + hd].reshape(rt, SB_HEADS, SB_DH)
    vbuf[slot] = x[:, C_SV:C_SV + hd].reshape(rt, SB_HEADS, SB_DH)

    @pl.when(j == 0)
    def _():
        for cp in copies(slot, True, 0):
            cp.start()

    @pl.when(j > 0)
    def _():
        for cp in copies(slot, False, j * rt - n_front):
            cp.start()

    @pl.when(lin == total - 1)
    def _():
        if total >= 2:
            wait_step(lin - 1, 1 - slot)
        wait_step(lin, slot)


def kv_state_outputs(z_kvs, *, nb, tp, n_front):
    depth = len(z_kvs)
    rt = _pick(tp, (768, 640, 512, 384, 256))
    assert rt > n_front
    nj = tp // rt
    w = z_kvs[0].shape[1]
    last = nb * nj - 1

    def zspec(i):
        return pl.BlockSpec((rt, w), lambda l, b, j: (jnp.where(l == i, b * nj + j, jnp.where(l < i, 0, last)), 0))

    shape = jax.ShapeDtypeStruct((depth, nb, tp - n_front, SB_HEADS, SB_DH), F32)
    return pl.pallas_call(
        functools.partial(_kv_out_kernel, depth=depth, nb=nb, nj=nj, rt=rt, n_front=n_front),
        grid=(depth, nb, nj),
        in_specs=[zspec(i) for i in range(depth)],
        out_specs=[pl.BlockSpec(memory_space=pl.ANY), pl.BlockSpec(memory_space=pl.ANY)],
        out_shape=[shape, shape],
        scratch_shapes=[pltpu.VMEM((2, rt, SB_HEADS, SB_DH), F32), pltpu.VMEM((2, rt, SB_HEADS, SB_DH), F32),
                        pltpu.SemaphoreType.DMA((2, 2))],
        compiler_params=_cparams(("arbitrary", "arbitrary", "arbitrary")),
        name="kv_state_out",
    )(*z_kvs)


def _merge_kernel(agp_ref, amp_ref, asp_ref, ags_ref, ams_ref, ass_ref, g0_ref, g1_ref, g2_ref, x_ref,
                  wg_ref, wm_ref, ws_ref, wo_ref, lg_ref, lb_ref, o_ref, or_ref, *, alpha, np_tiles):
    is_p = pl.program_id(0) < np_tiles
    ag = jnp.where(is_p, agp_ref[...], ags_ref[...])
    am = jnp.where(is_p, amp_ref[...], ams_ref[...])
    asb = jnp.where(is_p, asp_ref[...], ass_ref[...])
    merged = (_sigmoid(g0_ref[...]) * _dot(ag, wg_ref[...])
              + _sigmoid(g1_ref[...]) * _dot(am, wm_ref[...])
              + _sigmoid(g2_ref[...]) * _dot(asb, ws_ref[...]))
    out = _dot(merged.astype(BF16), wo_ref[...])
    y = _ln_math(alpha * x_ref[...] + out, lg_ref[...], lb_ref[...])
    o_ref[...] = y
    or_ref[...] = y.reshape(or_ref.shape)


def merge_ln1(acts_p, acts_s, z_main, x, wg, wm, ws, wo, lg, lb, alpha):
    n, d = x.shape
    n_p = acts_p[0].shape[0]
    n_s = acts_s[0].shape[0]
    tm = _pick(np.gcd(n_p, n_s), (256, 128, 64))
    assert n_p + n_s == n
    np_tiles = n_p // tm
    row = lambda i: (i, 0)
    row_p = lambda i: (jnp.minimum(i, np_tiles - 1), 0)
    row_s = lambda i: (jnp.maximum(i - np_tiles, 0), 0)
    const = lambda i: (0, 0)
    wspec = pl.BlockSpec((d, d), const)
    gcb = C_GATE // d
    return pl.pallas_call(
        functools.partial(_merge_kernel, alpha=alpha, np_tiles=np_tiles),
        grid=(n // tm,),
        in_specs=[pl.BlockSpec((tm, d), row_p), pl.BlockSpec((tm, d), row_p), pl.BlockSpec((tm, d), row_p),
                  pl.BlockSpec((tm, d), row_s), pl.BlockSpec((tm, d), row_s), pl.BlockSpec((tm, d), row_s),
                  pl.BlockSpec((tm, d), lambda i: (i, gcb)),
                  pl.BlockSpec((tm, d), lambda i: (i, gcb + 1)),
                  pl.BlockSpec((tm, d), lambda i: (i, gcb + 2)),
                  pl.BlockSpec((tm, d), row),
                  wspec, wspec, wspec, wspec,
                  pl.BlockSpec((1, d), const), pl.BlockSpec((1, d), const)],
        out_specs=[pl.BlockSpec((tm, d), row), pl.BlockSpec((tm, ROW_SUB, d // ROW_SUB), lambda i: (i, 0, 0))],
        out_shape=[jax.ShapeDtypeStruct((n, d), F32), jax.ShapeDtypeStruct((n, ROW_SUB, d // ROW_SUB), F32)],
        compiler_params=_cparams(("arbitrary",)),
        name="merge_ln1",
    )(*acts_p, *acts_s, z_main, z_main, z_main, x, wg, wm, ws, wo, lg.reshape(1, d), lb.reshape(1, d))


R_ROWS = 8 + N_EXPERTS


def _router_kernel(x_ref, w_ref, b_ref, u_ref, idx_ref, gate_ref, cnt_ref):
    i = pl.program_id(0)
    tm = x_ref.shape[0]

    @pl.when(i == 0)
    def _():
        cnt_ref[...] = jnp.zeros_like(cnt_ref)

    logits = _dot_nt(w_ref[...].astype(BF16), x_ref[...].astype(BF16)) + b_ref[...]

    g = [logits[j:j + 1, :] for j in range(N_GROUPS)]
    gmax = jnp.maximum(jnp.maximum(g[0], g[1]), jnp.maximum(g[2], g[3]))
    gsel = jnp.where(g[0] == gmax, 0, jnp.where(g[1] == gmax, 1, jnp.where(g[2] == gmax, 2, 3)))
    gden = (jnp.exp(g[0] - gmax) + jnp.exp(g[1] - gmax)) + (jnp.exp(g[2] - gmax) + jnp.exp(g[3] - gmax))
    gprob = 1.0 / gden
    e_in = jnp.where(gsel == 0, logits[8:16, :],
                     jnp.where(gsel == 1, logits[16:24, :],
                               jnp.where(gsel == 2, logits[24:32, :], logits[32:40, :])))
    ridx = lax.broadcasted_iota(jnp.int32, (EXPERTS_PER_GROUP, tm), 0)
    v1 = jnp.max(e_in, axis=0, keepdims=True)
    i1 = jnp.min(jnp.where(e_in == v1, ridx, EXPERTS_PER_GROUP), axis=0, keepdims=True)
    e2 = jnp.where(ridx == i1, -jnp.inf, e_in)
    v2 = jnp.max(e2, axis=0, keepdims=True)
    i2 = jnp.min(jnp.where(e2 == v2, ridx, EXPERTS_PER_GROUP), axis=0, keepdims=True)
    t = jnp.exp(v2 - v1)
    p1 = 1.0 / (1.0 + t)
    gate_ref[0:1, :] = p1 * gprob
    gate_ref[1:2, :] = (t * p1) * gprob
    gate_ref[2:8, :] = jnp.zeros((6, tm), F32)
    eid1 = gsel * EXPERTS_PER_GROUP + i1
    eid2 = gsel * EXPERTS_PER_GROUP + i2

    eidx = lax.broadcasted_iota(jnp.int32, (N_EXPERTS, tm), 0)
    oh1 = eidx == eid1
    oh2 = eidx == eid2
    oh1f = jnp.where(oh1, 1.0, 0.0)
    oh2f = jnp.where(oh2, 1.0, 0.0)
    umat = u_ref[...]
    pre1 = _dot(oh1f.astype(BF16), umat)
    pre2 = _dot(oh2f.astype(BF16), umat)
    cnt1 = jnp.sum(oh1f, axis=1, keepdims=True)
    cnt2 = jnp.sum(oh2f, axis=1, keepdims=True)
    base = cnt_ref[:, 0:1]
    rank1 = jnp.sum(jnp.where(oh1, base + pre1, 0.0), axis=0, keepdims=True)
    rank2 = jnp.sum(jnp.where(oh2, base + cnt1 + pre2, 0.0), axis=0, keepdims=True)
    idx_ref[0:1, :] = eid1
    idx_ref[1:2, :] = eid2
    idx_ref[2:3, :] = rank1.astype(jnp.int32)
    idx_ref[3:4, :] = rank2.astype(jnp.int32)
    idx_ref[4:8, :] = jnp.zeros((4, tm), jnp.int32)
    cnt_ref[...] = cnt_ref[...] + (cnt1 + cnt2)


def moe_router(x, w_r, b_r, umat):
    n, d = x.shape
    tm = umat.shape[0]
    return pl.pallas_call(
        _router_kernel,
        grid=(n // tm,),
        in_specs=[pl.BlockSpec((tm, d), lambda i: (i, 0)),
                  pl.BlockSpec((R_ROWS, d), lambda i: (0, 0)),
                  pl.BlockSpec((R_ROWS, 1), lambda i: (0, 0)),
                  pl.BlockSpec((tm, tm), lambda i: (0, 0))],
        out_specs=[pl.BlockSpec((8, tm), lambda i: (0, i)),
                   pl.BlockSpec((8, tm), lambda i: (0, i)),
                   pl.BlockSpec((N_EXPERTS, 128), lambda i: (0, 0))],
        out_shape=[jax.ShapeDtypeStruct((8, n), jnp.int32),
                   jax.ShapeDtypeStruct((8, n), F32),
                   jax.ShapeDtypeStruct((N_EXPERTS, 128), F32)],
        compiler_params=_cparams(("arbitrary",)),
        name="moe_router",
    )(x, w_r, b_r, umat)


def _row_copy(src_hbm, dst_vmem, src_row, dst_row, sem):
    return pltpu.make_async_copy(src_hbm.at[pl.ds(src_row, 1)], dst_vmem.at[pl.ds(dst_row, 1)], sem)


GATHER_UNROLL = 8


def _start_row_gather(idx_ref, base, src_hbm, dst_vmem, nrows, sem):
    def start(r, carry):
        _row_copy(src_hbm, dst_vmem, idx_ref[base + r], r, sem).start()
        return carry

    lax.fori_loop(0, nrows, start, 0, unroll=GATHER_UNROLL)


def _wait_row_gather(src_hbm, dst_vmem, nrows, sem):
    def wait(r, carry):
        _row_copy(src_hbm, dst_vmem, 0, r, sem).wait()
        return carry

    lax.fori_loop(0, nrows, wait, 0, unroll=GATHER_UNROLL)


def _moe_kernel(be_ref, tok_ref, nblk_ref, x_hbm, wg_ref, wu_ref, wd_ref, o_ref,
                xbuf, wgb, wub, wdb, sem):
    i = pl.program_id(0)
    nblk = nblk_ref[0]
    slot = i % 2

    @pl.when((i == 0) & (nblk > 0))
    def _():
        _start_row_gather(tok_ref, 0, x_hbm, xbuf.at[0], MOE_BLK, sem.at[0])

    @pl.when(i + 1 < nblk)
    def _():
        _start_row_gather(tok_ref, (i + 1) * MOE_BLK, x_hbm, xbuf.at[1 - slot], MOE_BLK, sem.at[1 - slot])

    prev_e = be_ref[jnp.maximum(i - 1, 0)]

    @pl.when((i == 0) | (be_ref[i] != prev_e))
    def _():
        wgb[...] = wg_ref[0, 0].astype(BF16)
        wub[...] = wu_ref[0, 0].astype(BF16)
        wdb[...] = wd_ref[0, 0].astype(BF16)

    @pl.when(i < nblk)
    def _():
        _wait_row_gather(x_hbm, xbuf.at[slot], MOE_BLK, sem.at[slot])
        xb = xbuf[slot].reshape(MOE_BLK, D_MODEL).astype(BF16)
        a = _dot(xb, wgb[...])
        hmid = (a * _sigmoid(a)) * _dot(xb, wub[...])
        o_ref[...] = _dot(hmid.astype(BF16), wdb[...]).reshape(o_ref.shape)

    @pl.when(i >= nblk)
    def _():
        o_ref[...] = jnp.zeros_like(o_ref)


def moe_experts(x, block_e, tok_pad, nblk, w_gate, w_up, w_down, *, layer):
    n = x.shape[0]
    d = D_MODEL
    rs = (ROW_SUB, d // ROW_SUB)
    nb = block_e.shape[0]
    wmap = lambda i, be, tok, nblk: (layer, be[i], 0, 0)
    grid_spec = pltpu.PrefetchScalarGridSpec(
        num_scalar_prefetch=3,
        grid=(nb,),
        in_specs=[pl.BlockSpec(memory_space=pl.ANY),
                  pl.BlockSpec((1, 1, d, D_EXPERT), wmap),
                  pl.BlockSpec((1, 1, d, D_EXPERT), wmap),
                  pl.BlockSpec((1, 1, D_EXPERT, d), wmap)],
        out_specs=pl.BlockSpec((MOE_BLK,) + rs, lambda i, be, tok, nblk: (i, 0, 0)),
        scratch_shapes=[pltpu.VMEM((2, MOE_BLK) + rs, F32),
                        pltpu.VMEM((d, D_EXPERT), BF16),
                        pltpu.VMEM((d, D_EXPERT), BF16),
                        pltpu.VMEM((D_EXPERT, d), BF16),
                        pltpu.SemaphoreType.DMA((2,))],
    )
    return pl.pallas_call(
        _moe_kernel,
        grid_spec=grid_spec,
        out_shape=jax.ShapeDtypeStruct((nb * MOE_BLK,) + rs, F32),
        compiler_params=_cparams(("arbitrary",)),
        name="moe_experts",
    )(block_e, tok_pad, nblk, x, w_gate, w_up, w_down)


def _combine_kernel(dest_ref, yb_hbm, x_ref, gate_ref, lg_ref, lb_ref, o_ref, ob_ref, ybuf, sem, *, alpha, n):
    i = pl.program_id(0)
    nt = pl.num_programs(0)
    tm = x_ref.shape[0]
    slot = i % 2

    def start(tile, s):
        for k in range(TOP_K):
            _start_row_gather(dest_ref, k * n + tile * tm, yb_hbm, ybuf.at[s, k], tm, sem.at[s, k])

    @pl.when(i == 0)
    def _():
        start(0, 0)

    @pl.when(i + 1 < nt)
    def _():
        start(i + 1, 1 - slot)

    for k in range(TOP_K):
        _wait_row_gather(yb_hbm, ybuf.at[slot, k], tm, sem.at[slot, k])
    gate = gate_ref[...]
    d = x_ref.shape[1]
    y = ybuf[slot, 0].reshape(tm, d) * gate[:, 0:1] + ybuf[slot, 1].reshape(tm, d) * gate[:, 1:2]
    out = _ln_math(alpha * x_ref[...] + y, lg_ref[...], lb_ref[...])
    o_ref[...] = out
    ob_ref[...] = out.astype(BF16)


def moe_combine_ln2(x, yb, dest, gate_cols, lg, lb, alpha):
    n, d = x.shape
    tm = _pick(n, (256, 128, 64))
    grid_spec = pltpu.PrefetchScalarGridSpec(
        num_scalar_prefetch=1,
        grid=(n // tm,),
        in_specs=[pl.BlockSpec(memory_space=pl.ANY),
                  pl.BlockSpec((tm, d), lambda i, dest: (i, 0)),
                  pl.BlockSpec((tm, TOP_K), lambda i, dest: (i, 0)),
                  pl.BlockSpec((1, d), lambda i, dest: (0, 0)),
                  pl.BlockSpec((1, d), lambda i, dest: (0, 0))],
        out_specs=[pl.BlockSpec((tm, d), lambda i, dest: (i, 0)), pl.BlockSpec((tm, d), lambda i, dest: (i, 0))],
        scratch_shapes=[pltpu.VMEM((2, TOP_K, tm, ROW_SUB, d // ROW_SUB), F32),
                        pltpu.SemaphoreType.DMA((2, TOP_K))],
    )
    return pl.pallas_call(
        functools.partial(_combine_kernel, alpha=alpha, n=n),
        grid_spec=grid_spec,
        out_shape=[jax.ShapeDtypeStruct((n, d), F32), jax.ShapeDtypeStruct((n, d), BF16)],
        compiler_params=_cparams(("arbitrary",)),
        name="moe_combine_ln2",
    )(dest, yb, x, gate_cols, lg.reshape(1, d), lb.reshape(1, d))


def _tri_lower(n):
    t = np.arange(n)
    return (t[None, :] <= t[:, None]).astype(np.float32)


def _strict_upper(n):
    t = np.arange(n)
    return (t[:, None] < t[None, :]).astype(np.float32)


def kernel(x_prompt, x_sample, state_gla, state_ml_C, state_ml_n, state_ml_m, state_ml_conv, cache_sb_k, cache_sb_v, meta_tokens, ln_in_g, ln_in_b, w_in, gla_w_a2, gla_b_a, gla_norm_g, ml_conv_w, ml_conv_b, ml_wq, ml_wk, ml_wv, ml_b_i, ml_b_f, ml_norm_g, w_br_gla, w_br_ml, w_br_sb, w_out, ln1_g, ln1_b, w_rg, b_rg, w_re, b_re, w_e_gate, w_e_up, w_e_down, ln2_g, ln2_b):
    D = D_MODEL
    B, SEQ, _ = x_prompt.shape
    DB, DS, _ = x_sample.shape
    depth = w_in.shape[0]
    n_cache = cache_sb_k.shape[2]
    assert DS == CHUNK and SEQ % 128 == 0 and x_prompt.shape[2] == D
    alpha = (2.0 * depth) ** 0.25
    TP = PAD_FRONT + N_META + SEQ
    NP = B * TP
    NS = DB * DS
    N = NP + NS
    assert TP % CHUNK_PROMPT == 0 and NP % CHUNK == 0
    nchunk_p = TP // CHUNK_PROMPT

    pieces = []
    for b in range(B):
        pieces += [jnp.zeros((PAD_FRONT, D), F32), meta_tokens.astype(F32), x_prompt[b].astype(F32)]
    x = jnp.concatenate(pieces + [x_sample.reshape(NS, D).astype(F32)], axis=0)
    x, xb = layer_norm_rows(x, ln_in_g, ln_in_b)

    u_sb = jnp.asarray(_strict_upper(SB_BK).T, BF16)
    tm_r = _pick(N, (256, 128, 64))
    u_r = jnp.asarray(_strict_upper(tm_r), BF16)

    A = N * TOP_K
    NB = -(-(A + N_EXPERTS * (MOE_BLK - 1)) // MOE_BLK)

    outs_p = [[] for _ in range(5)]
    outs_s = [[] for _ in range(7)]
    z_kvs = []
    for l in range(depth):
        wi = w_in[l]
        w_main = jnp.concatenate([wi[:, 0:3072], wi[:, 3088:4112], wi[:, 4120:5144], wi[:, 5144:6168],
                                  wi[:, 8216:]], axis=1).astype(BF16)
        w_kv = wi[:, 6168:8216].astype(BF16)
        w_small = jnp.concatenate([wi[:, 3072:3088], wi[:, 4112:4120],
                                   jnp.zeros((D, 128 - 24), F32)], axis=1).astype(BF16)
        z_main = in_projection(xb, w_main, 1024, "in_proj_main")
        z_kv, kvb = in_projection(xb, w_kv, 1024, "in_proj_kv", with_bf16=True)
        z_small = in_projection(xb, w_small, 128, "in_proj_small")
        z_kvs.append(z_kv)

        wa = jnp.concatenate([gla_w_a2[l], jnp.zeros((128 - GLA_RANK, GLA_HEADS * GLA_DK), F32)], axis=0).astype(BF16)
        ba = gla_b_a[l].reshape(1, -1)
        ng = gla_norm_g[l].reshape(1, -1)
        s0_p = jnp.zeros((B, GLA_HEADS, GLA_DV, GLA_DK), F32)
        s0_s = jnp.swapaxes(state_gla[l].astype(F32), -1, -2)
        geo_p = dict(row0=0, nb=B, nchunk=nchunk_p, chunk=CHUNK_PROMPT)
        geo_s = dict(row0=NP, nb=DB, nchunk=1, chunk=CHUNK)
        gla_p = _gla_call_parts(z_main, z_small, wa, ba, ng, s0_p, **geo_p)
        gla_s = _gla_call_parts(z_main, z_small, wa, ba, ng, s0_s, **geo_s)

        gbias = jnp.zeros((1, 128), F32).at[0, L_MI:L_MI + ML_HEADS].set(ml_b_i[l]).at[0, L_MF:L_MF + ML_HEADS].set(ml_b_f[l])
        cw = ml_conv_w[l]
        cb = ml_conv_b[l].reshape(1, -1)
        wqk = jnp.concatenate([ml_wq[l], ml_wk[l]], axis=-1).astype(BF16)
        wv = ml_wv[l].astype(BF16)
        mng = ml_norm_g[l].reshape(1, -1)
        W = ML_HEADS * ML_DH
        conv0_p = jnp.zeros((B, 8, W), F32)
        conv0_s = jnp.concatenate([jnp.zeros((DB, 8 - (ML_CONV - 1), W), F32), state_ml_conv[l].astype(F32)], axis=1)
        m0_s = jnp.zeros((DB, 1, 128), F32).at[:, 0, L_MF:L_MF + ML_HEADS].set(state_ml_m[l].astype(F32))
        ml_p = _mlstm_call_parts(
            z_main, z_small, gbias, cw, cb, wqk, wv, mng, conv0_p,
            jnp.zeros((1, B, ML_HEADS, ML_DH, ML_DH), F32), jnp.zeros((B, ML_HEADS, ML_DH), F32),
            jnp.zeros((B, 1, 128), F32), c0_layer=0, **geo_p)
        ml_s = _mlstm_call_parts(
            z_main, z_small, gbias, cw, cb, wqk, wv, mng, conv0_s,
            state_ml_C.astype(F32), state_ml_n[l].astype(F32), m0_s, c0_layer=l, **geo_s)
        a_gla_p, sT_p, a_ml_p, c_p, n_p, m_p = recurrent_branches(
            gla_p, ml_p, nb=B, nchunk=nchunk_p, chunk=CHUNK_PROMPT, n_pad=PAD_FRONT)
        a_gla_s, sT_s, a_ml_s, c_s, n_s, m_s = recurrent_branches(
            gla_s, ml_s, nb=DB, nchunk=1, chunk=CHUNK, n_pad=0)

        a_sb_p = sb_attention_prompt(z_main, kvb, u_sb, nb=B, tp=TP, bq=SB_BK, n_pad=PAD_FRONT)
        a_sb_s = sb_attention_sample(z_main, z_kv, cache_sb_k.astype(F32), cache_sb_v.astype(F32),
                                     u_sb, layer=l, row0=NP, nb=DB)

        x1, x1r = merge_ln1((a_gla_p, a_ml_p, a_sb_p), (a_gla_s, a_ml_s, a_sb_s), z_main, x,
                       w_br_gla[l].astype(BF16), w_br_ml[l].astype(BF16), w_br_sb[l].astype(BF16),
                       w_out[l].astype(BF16), ln1_g[l], ln1_b[l], alpha)

        w_r = jnp.concatenate([w_rg[l].T, jnp.zeros((8 - N_GROUPS, D), F32), w_re[l].T], axis=0)
        b_r = jnp.concatenate([b_rg[l], jnp.zeros((8 - N_GROUPS,), F32), b_re[l]]).reshape(R_ROWS, 1)
        idx, gates, cnt = moe_router(x1, w_r, b_r, u_r)
        counts = cnt[:, 0].astype(jnp.int32)
        padded = (counts + MOE_BLK - 1) // MOE_BLK * MOE_BLK
        ends_pad = jnp.cumsum(padded)
        start_pad = ends_pad - padded
        eids = jnp.arange(N_EXPERTS, dtype=jnp.int32)
        start_of = jnp.sum(jnp.where(idx[0:2, :, None] == eids, start_pad, 0), axis=-1)
        dest = (start_of + idx[2:4]).reshape(-1)
        tok = jnp.tile(jnp.arange(N, dtype=jnp.int32), TOP_K)
        tok_pad = jnp.zeros((NB * MOE_BLK,), jnp.int32).at[dest].set(tok)
        blk_lo = jnp.arange(NB, dtype=jnp.int32) * MOE_BLK
        block_e = jnp.minimum(jnp.sum((ends_pad[None, :] <= blk_lo[:, None]).astype(jnp.int32), axis=-1),
                              N_EXPERTS - 1)
        nblk = (ends_pad[-1:] // MOE_BLK).astype(jnp.int32)
        yb = moe_experts(x1r, block_e, tok_pad, nblk, w_e_gate, w_e_up, w_e_down, layer=l)
        x, xb = moe_combine_ln2(x1, yb, dest, gates[0:2].T, ln2_g[l], ln2_b[l], alpha)

        nct = ML_CONV - 1
        conv_p = jnp.stack([z_main[(b + 1) * TP - nct:(b + 1) * TP, C_MU:C_MU + W] for b in range(B)], axis=0)
        conv_s = z_main[NP:, C_MU:C_MU + W].reshape(DB, DS, W)[:, DS - nct:]
        zkv_s = z_kv[NP:].reshape(DB, DS, 2 * D)
        kv_shape = (SB_HEADS, SB_DH)
        outs_p[0].append(jnp.swapaxes(sT_p, -1, -2))
        outs_s[0].append(jnp.swapaxes(sT_s, -1, -2))
        outs_p[1].append(c_p); outs_s[1].append(c_s)
        outs_p[2].append(n_p); outs_s[2].append(n_s)
        outs_p[3].append(m_p[:, 0, L_MF:L_MF + ML_HEADS]); outs_s[3].append(m_s[:, 0, L_MF:L_MF + ML_HEADS])
        outs_p[4].append(conv_p); outs_s[4].append(conv_s)
        outs_s[5].append(zkv_s[:, :, C_SK:C_SK + D].reshape(DB, DS, *kv_shape))
        outs_s[6].append(zkv_s[:, :, C_SV:C_SV + D].reshape(DB, DS, *kv_shape))

    sbk_p, sbv_p = kv_state_outputs(z_kvs, nb=B, tp=TP, n_front=PAD_FRONT)
    sp = [jnp.stack(a, axis=0) for a in outs_p]
    ss = [jnp.stack(a, axis=0) for a in outs_s]
    y_prompt = jnp.stack([x[b * TP + PAD_FRONT + N_META:(b + 1) * TP] for b in range(B)], axis=0)
    y_sample = x[NP:].reshape(DB, DS, D)
    return (y_prompt, y_sample, sp[0], ss[0], sp[1], ss[1], sp[2], ss[2], sp[3], ss[3],
            sp[4], ss[4], sbk_p, ss[5], sbv_p, ss[6])
```

```python
import functools

import numpy as np
import jax
import jax.numpy as jnp
from jax import lax
from jax.experimental import pallas as pl
from jax.experimental.pallas import tpu as pltpu

F32 = jnp.float32
BF16 = jnp.bfloat16

D_MODEL = 1024
N_META = 16
CHUNK = 64
CHUNK_PROMPT = 256
PAD_FRONT = 240
GLA_HEADS, GLA_DK, GLA_DV, GLA_RANK, GLA_TAU = 4, 128, 256, 16, 16.0
ML_HEADS, ML_DH, ML_CONV = 4, 256, 4
SB_HEADS, SB_DH = 8, 128
N_GROUPS, EXPERTS_PER_GROUP, TOP_K = 4, 8, 2
N_EXPERTS = N_GROUPS * EXPERTS_PER_GROUP
D_EXPERT = D_MODEL // 2
MOE_BLK = 256
ROW_SUB = 8
LN_EPS = 1e-5
NEG_BIG = -1e30
SB_SKIP = 110.0
SB_BK = 256

C_GQ, C_GK, C_GV, C_GR = 0, 512, 1024, 2048
C_MU, C_MO = 3072, 4096
C_SQ = 5120
C_GATE = 6144
C_SK, C_SV = 0, 1024
SB_HG = 4
L_MI, L_MF = 16, 20

VMEM_LIMIT = 48 * 1024 * 1024


def _cparams(sem):
    return pltpu.CompilerParams(dimension_semantics=sem, vmem_limit_bytes=VMEM_LIMIT)


def _pick(n, cands):
    for c in cands:
        if n % c == 0:
            return c
    raise ValueError(f"no tile in {cands} divides {n}")


def _log_sigmoid(x):
    return jnp.minimum(x, 0.0) - jnp.log1p(jnp.exp(-jnp.abs(x)))


def _sigmoid(x):
    return 1.0 / (1.0 + jnp.exp(-x))


def _split2(x):
    hi = x.astype(BF16)
    lo = (x - hi.astype(F32)).astype(BF16)
    return hi, lo


def _dot(a, b):
    return jnp.dot(a, b, preferred_element_type=F32)


def _dot_nt(a, b):
    return lax.dot_general(a, b, (((1,), (1,)), ((), ())), preferred_element_type=F32)


def _dot_tn(a, b):
    return lax.dot_general(a, b, (((0,), (0,)), ((), ())), preferred_element_type=F32)


def _ln_math(x, g, b):
    mu = jnp.mean(x, axis=-1, keepdims=True)
    xc = x - mu
    var = jnp.mean(xc * xc, axis=-1, keepdims=True)
    return xc * lax.rsqrt(var + LN_EPS) * g + b


def _ln_kernel(x_ref, g_ref, b_ref, o_ref, ob_ref):
    y = _ln_math(x_ref[...], g_ref[...], b_ref[...])
    o_ref[...] = y
    ob_ref[...] = y.astype(BF16)


def layer_norm_rows(x, g, b):
    n, d = x.shape
    tm = _pick(n, (512, 256, 128, 64))
    return pl.pallas_call(
        _ln_kernel,
        grid=(n // tm,),
        in_specs=[pl.BlockSpec((tm, d), lambda i: (i, 0)),
                  pl.BlockSpec((1, d), lambda i: (0, 0)),
                  pl.BlockSpec((1, d), lambda i: (0, 0))],
        out_specs=[pl.BlockSpec((tm, d), lambda i: (i, 0)), pl.BlockSpec((tm, d), lambda i: (i, 0))],
        out_shape=[jax.ShapeDtypeStruct((n, d), F32), jax.ShapeDtypeStruct((n, d), BF16)],
        compiler_params=_cparams(("arbitrary",)),
        name="ln_in",
    )(x, g.reshape(1, d), b.reshape(1, d))


def _mm_kernel(x_ref, w_ref, o_ref):
    o_ref[...] = _dot(x_ref[...], w_ref[...])


def _mm2_kernel(x_ref, w_ref, o_ref, ob_ref):
    z = _dot(x_ref[...], w_ref[...])
    o_ref[...] = z
    ob_ref[...] = z.astype(BF16)


def in_projection(x, w_bf16, tn, name, with_bf16=False):
    n, d = x.shape
    cols = w_bf16.shape[1]
    tm = _pick(n, (896, 768, 512, 384, 256, 128, 64))
    out_spec = pl.BlockSpec((tm, tn), lambda j, i: (i, j))
    return pl.pallas_call(
        _mm2_kernel if with_bf16 else _mm_kernel,
        grid=(cols // tn, n // tm),
        in_specs=[pl.BlockSpec((tm, d), lambda j, i: (i, 0)),
                  pl.BlockSpec((d, tn), lambda j, i: (0, j))],
        out_specs=[out_spec, out_spec] if with_bf16 else out_spec,
        out_shape=([jax.ShapeDtypeStruct((n, cols), F32), jax.ShapeDtypeStruct((n, cols), BF16)]
                   if with_bf16 else jax.ShapeDtypeStruct((n, cols), F32)),
        compiler_params=_cparams(("arbitrary", "arbitrary")),
        name=name,
    )(x, w_bf16)


def _gla_levels(L):
    return tuple(L >> i for i in range(1, L.bit_length()))


def _gla_level_matrix(L):
    t = np.arange(L)[:, None]
    j = np.arange(L)[None, :]
    blocks = []
    for m in (L,) + _gla_levels(L):
        blocks.append(((j <= t) & (j // m == t // m)).astype(np.float32))
    for m in (L,) + _gla_levels(L):
        blocks.append(((j > t) & (j // m == t // m)).astype(np.float32))
    return np.concatenate(blocks, axis=0)


def _gla_kernel(q_ref, k_ref, v_ref, r_ref, lr_ref, wa_ref, ba_ref, ng_ref, lvl_ref, s0_ref,
                o_ref, s_ref, *, n_pad, L, phase):
    c = pl.program_id(1)
    GLA_LEVELS = _gla_levels(L)

    if phase == "init":
        @pl.when(c == 0)
        def _():
            s_ref[...] = s0_ref[...]
        return

    row = c * L + lax.broadcasted_iota(jnp.int32, (L, 1), 0)
    valid = row >= n_pad
    x = _dot(lr_ref[...].astype(BF16), wa_ref[...]) + ba_ref[...]
    la = jnp.where(valid, _log_sigmoid(x) * (1.0 / GLA_TAU), 0.0)
    la_hi, la_lo = _split2(la)
    lvl = lvl_ref[...]
    ex = jnp.exp(_dot(lvl, la_hi) + _dot(lvl, la_lo))
    nlev = len(GLA_LEVELS) + 1

    ti = lax.broadcasted_iota(jnp.int32, (L, L), 0)
    si = lax.broadcasted_iota(jnp.int32, (L, L), 1)
    level_masks = []
    for m in GLA_LEVELS:
        sh = m.bit_length() - 1
        tb = lax.shift_right_logical(ti, sh)
        sb = lax.shift_right_logical(si, sh)
        level_masks.append((lax.shift_right_logical(tb, 1) == lax.shift_right_logical(sb, 1))
                           & ((tb & 1) == 1) & ((sb & 1) == 0))
    eye = ti == si

    outs = []
    states = []
    for h in range(GLA_HEADS):
        dk = slice(h * GLA_DK, (h + 1) * GLA_DK)
        dv = slice(h * GLA_DV, (h + 1) * GLA_DV)
        q = q_ref[:, dk] * (GLA_DK ** -0.5)
        k = jnp.where(valid, k_ref[:, dk], 0.0)
        v = jnp.where(valid, v_ref[:, dv], 0.0).astype(BF16)
        att = jnp.where(eye, _dot_nt(q.astype(BF16), k.astype(BF16)), 0.0)
        for lev in range(len(GLA_LEVELS)):
            qe = ex[(1 + lev) * L:(2 + lev) * L, dk]
            ke = ex[(nlev + 1 + lev) * L:(nlev + 2 + lev) * L, dk]
            a = _dot_nt((q * qe).astype(BF16), (k * ke).astype(BF16))
            att = jnp.where(level_masks[lev], a, att)
        eb = ex[0:L, dk]
        st = s_ref[0, h]
        o = _dot(att.astype(BF16), v) + _dot_nt((q * eb).astype(BF16), st.astype(BF16))
        ke_end = ex[nlev * L:(nlev + 1) * L, dk]
        states.append(st * eb[L - 1:L, :] + _dot_tn(v, (k * ke_end).astype(BF16)))
        mu = jnp.mean(o, axis=-1, keepdims=True)
        oc = o - mu
        var = jnp.mean(oc * oc, axis=-1, keepdims=True)
        on = oc * lax.rsqrt(var + LN_EPS) * ng_ref[...]
        r = r_ref[:, dv]
        outs.append((on * (r * _sigmoid(r))).astype(BF16))
    o_ref[...] = jnp.concatenate(outs, axis=-1)
    s_ref[0] = jnp.stack(states, axis=0)


def _gla_call_parts(z_main, z_small, wa, ba, ng, s0t, *, row0, nb, nchunk, chunk):
    L = chunk
    assert row0 % L == 0
    lvl = jnp.asarray(_gla_level_matrix(L), BF16)
    rb0 = row0 // L
    hk = GLA_HEADS * GLA_DK
    hv = GLA_HEADS * GLA_DV

    def rows(colblk):
        return lambda b, c: (rb0 + b * nchunk + c, colblk)

    return dict(
        args=(z_main, z_main, z_main, z_main, z_small, wa, ba, ng, lvl, s0t),
        in_specs=[pl.BlockSpec((L, hk), rows(C_GQ // hk)),
                  pl.BlockSpec((L, hk), rows(C_GK // hk)),
                  pl.BlockSpec((L, hv), rows(C_GV // hv)),
                  pl.BlockSpec((L, hv), rows(C_GR // hv)),
                  pl.BlockSpec((L, 128), rows(0)),
                  pl.BlockSpec((128, hk), lambda b, c: (0, 0)),
                  pl.BlockSpec((1, hk), lambda b, c: (0, 0)),
                  pl.BlockSpec((1, GLA_DV), lambda b, c: (0, 0)),
                  pl.BlockSpec(lvl.shape, lambda b, c: (0, 0)),
                  pl.BlockSpec((1, GLA_HEADS, GLA_DV, GLA_DK), lambda b, c: (b, 0, 0, 0))],
        out_specs=[pl.BlockSpec((L, hv), lambda b, c: (b * nchunk + c, 0)),
                   pl.BlockSpec((1, GLA_HEADS, GLA_DV, GLA_DK), lambda b, c: (b, 0, 0, 0))],
        out_shape=[jax.ShapeDtypeStruct((nb * nchunk * L, hv), BF16),
                   jax.ShapeDtypeStruct((nb, GLA_HEADS, GLA_DV, GLA_DK), F32)],
        scratch=[])


def _mlstm_kernel(u_ref, mo_ref, g_ref, gb_ref, cw_ref, cb_ref, wqk_ref, wv_ref, ng_ref, tri_ref,
                  conv0_ref, c0_ref, n0_ref, m0_ref,
                  o_ref, c_ref, n_ref, m_ref, ext_ref, *, n_pad, L, phase):
    c = pl.program_id(1)
    W = ML_HEADS * ML_DH

    if phase == "init":
        @pl.when(c == 0)
        def _():
            ext_ref[0:8, :] = conv0_ref[0]
            c_ref[...] = c0_ref[0]
            n_ref[...] = n0_ref[...]
            m_ref[...] = m0_ref[...]
        return

    row = c * L + lax.broadcasted_iota(jnp.int32, (L, 1), 0)
    valid = row >= n_pad

    u = jnp.where(valid, u_ref[...], 0.0)
    ext_ref[8:8 + L, :] = u
    uc = cb_ref[...]
    for i in range(ML_CONV):
        uc = uc + ext_ref[8 - (ML_CONV - 1) + i:8 - (ML_CONV - 1) + i + L, :] * cw_ref[i:i + 1, :]
    ext_ref[0:8, :] = u[L - 8:L, :]
    ucs = (uc * _sigmoid(uc)).astype(BF16)
    ub = u_ref[...].astype(BF16)

    g = g_ref[...] + gb_ref[...]
    li = jnp.where(valid, g, NEG_BIG)
    lf = jnp.where(valid, _log_sigmoid(g), 0.0)
    tri = tri_ref[...]
    lf_hi, lf_lo = _split2(lf)
    b_col = _dot(tri, lf_hi) + _dot(tri, lf_lo)
    a_col = b_col + m_ref[0]
    li_row = li.T
    b_row = b_col.T

    ti = lax.broadcasted_iota(jnp.int32, (L, L), 0)
    si = lax.broadcasted_iota(jnp.int32, (L, L), 1)
    causal = si <= ti
    lane = lax.broadcasted_iota(jnp.int32, (1, 128), 1)
    m_vec = m_ref[0]
    m_out = m_vec
    outs, c_new, n_new = [], [], []

    for h in range(ML_HEADS):
        d = slice(h * ML_DH, (h + 1) * ML_DH)
        qk_proj = _dot(ucs[:, d], wqk_ref[h])
        q = qk_proj[:, 0:ML_DH].astype(BF16)
        k = qk_proj[:, ML_DH:2 * ML_DH] * (ML_DH ** -0.5)
        v = _dot(ub[:, d], wv_ref[h]).astype(BF16)
        lf_l = L_MF + h
        li_l = L_MI + h
        bc = b_col[:, lf_l:lf_l + 1]
        ac = a_col[:, lf_l:lf_l + 1]
        lic = li[:, li_l:li_l + 1]
        dmat = jnp.where(causal, bc - b_row[lf_l:lf_l + 1, :] + li_row[li_l:li_l + 1, :], NEG_BIG)
        mt = jnp.maximum(ac, jnp.max(dmat, axis=-1, keepdims=True))
        w_intra = jnp.exp(dmat - mt)
        w_inter = jnp.exp(ac - mt)
        qk = _dot_nt(q, k.astype(BF16)) * w_intra
        cmat = c_ref[0, h]
        nrow = n_ref[0, h:h + 1, :]
        num = _dot(qk.astype(BF16), v) + w_inter * _dot(q, cmat.astype(BF16))
        qn = jnp.sum(q.astype(F32) * nrow.astype(BF16).astype(F32), axis=-1, keepdims=True)
        den = jnp.sum(qk, axis=-1, keepdims=True) + w_inter * qn
        hh = num / jnp.maximum(jnp.abs(den), jnp.exp(-mt))
        m_new = mt[L - 1:L, :]
        b_end = bc[L - 1:L, :]
        wk = jnp.exp(b_end - bc + lic - m_new)
        ws = jnp.exp(b_end + m_vec[:, lf_l:lf_l + 1] - m_new)
        kw = k * wk
        c_new.append(ws * cmat + _dot_tn(kw.astype(BF16), v))
        n_new.append(ws * nrow + jnp.sum(kw, axis=0, keepdims=True))
        m_out = jnp.where(lane == lf_l, m_new, m_out)
        mu = jnp.mean(hh, axis=-1, keepdims=True)
        hc = hh - mu
        var = jnp.mean(hc * hc, axis=-1, keepdims=True)
        hn = hc * lax.rsqrt(var + LN_EPS) * ng_ref[...]
        outs.append((hn * _sigmoid(mo_ref[:, d])).astype(BF16))
    o_ref[...] = jnp.concatenate(outs, axis=-1)
    c_ref[0] = jnp.stack(c_new, axis=0)
    n_ref[0] = jnp.concatenate(n_new, axis=0)
    m_ref[0] = m_out


def _mlstm_call_parts(z_main, z_small, gbias, cw, cb, wqk, wv, ng, conv0, c0, n0, m0,
                      *, c0_layer, row0, nb, nchunk, chunk):
    L = chunk
    assert row0 % L == 0
    tri = jnp.asarray(_tri_lower(L), BF16)
    rb0 = row0 // L
    W = ML_HEADS * ML_DH

    def rows(colblk):
        return lambda b, c: (rb0 + b * nchunk + c, colblk)

    const2 = lambda b, c: (0, 0)
    const3 = lambda b, c: (0, 0, 0)
    return dict(
        args=(z_main, z_main, z_small, gbias, cw, cb, wqk, wv, ng, tri, conv0, c0, n0, m0),
        in_specs=[pl.BlockSpec((L, W), rows(C_MU // W)),
                  pl.BlockSpec((L, W), rows(C_MO // W)),
                  pl.BlockSpec((L, 128), rows(0)),
                  pl.BlockSpec((1, 128), const2),
                  pl.BlockSpec((ML_CONV, W), const2),
                  pl.BlockSpec((1, W), const2),
                  pl.BlockSpec((ML_HEADS, ML_DH, 2 * ML_DH), const3),
                  pl.BlockSpec((ML_HEADS, ML_DH, ML_DH), const3),
                  pl.BlockSpec((1, ML_DH), const2),
                  pl.BlockSpec((L, L), const2),
                  pl.BlockSpec((1, 8, W), lambda b, c: (b, 0, 0)),
                  pl.BlockSpec((1, 1, ML_HEADS, ML_DH, ML_DH), lambda b, c: (c0_layer, b, 0, 0, 0)),
                  pl.BlockSpec((1, ML_HEADS, ML_DH), lambda b, c: (b, 0, 0)),
                  pl.BlockSpec((1, 1, 128), lambda b, c: (b, 0, 0))],
        out_specs=[pl.BlockSpec((L, W), lambda b, c: (b * nchunk + c, 0)),
                   pl.BlockSpec((1, ML_HEADS, ML_DH, ML_DH), lambda b, c: (b, 0, 0, 0)),
                   pl.BlockSpec((1, ML_HEADS, ML_DH), lambda b, c: (b, 0, 0)),
                   pl.BlockSpec((1, 1, 128), lambda b, c: (b, 0, 0))],
        out_shape=[jax.ShapeDtypeStruct((nb * nchunk * L, W), BF16),
                   jax.ShapeDtypeStruct((nb, ML_HEADS, ML_DH, ML_DH), F32),
                   jax.ShapeDtypeStruct((nb, ML_HEADS, ML_DH), F32),
                   jax.ShapeDtypeStruct((nb, 1, 128), F32)],
        scratch=[pltpu.VMEM((8 + L, W), F32)])


def _recurrent_kernel(*refs, n_gla_in, n_ml_in, n_pad, L):
    g_in = refs[:n_gla_in]
    m_in = refs[n_gla_in:n_gla_in + n_ml_in]
    g_out = refs[n_gla_in + n_ml_in:n_gla_in + n_ml_in + 2]
    m_rest = refs[n_gla_in + n_ml_in + 2:]
    for phase in ("init", "body"):
        _gla_kernel(*g_in, *g_out, n_pad=n_pad, L=L, phase=phase)
        _mlstm_kernel(*m_in, *m_rest, n_pad=n_pad, L=L, phase=phase)


def recurrent_branches(gla_parts, ml_parts, *, nb, nchunk, chunk, n_pad):
    return pl.pallas_call(
        functools.partial(_recurrent_kernel, n_gla_in=len(gla_parts["args"]), n_ml_in=len(ml_parts["args"]),
                          n_pad=n_pad, L=chunk),
        grid=(nb, nchunk),
        in_specs=gla_parts["in_specs"] + ml_parts["in_specs"],
        out_specs=gla_parts["out_specs"] + ml_parts["out_specs"],
        out_shape=gla_parts["out_shape"] + ml_parts["out_shape"],
        scratch_shapes=gla_parts["scratch"] + ml_parts["scratch"],
        compiler_params=_cparams(("arbitrary", "arbitrary")),
        name="gla_mlstm",
    )(*gla_parts["args"], *ml_parts["args"])


def _sb_heads(qs, ks, vs, vis, cs, umat):
    nh = len(qs)
    bq = qs[0].shape[0]
    lss, l1s, parts = [], [], []
    for h in range(nh):
        z = _dot_nt(qs[h], ks[h])
        t = jnp.log(1.0 + jnp.exp(-jnp.abs(z)))
        lss.append(jnp.minimum(z, 0.0) - t)
        l1 = jnp.minimum(-z, 0.0) - t
        if vis is not None:
            l1 = jnp.where(vis, l1, 0.0)
        l1s.append(l1)
        parts.extend(_split2(l1))
    suf = _dot(jnp.concatenate(parts, axis=0), umat)
    pvs, new_cs = [], []
    for h in range(nh):
        suffix = suf[2 * h * bq:(2 * h + 1) * bq] + suf[(2 * h + 1) * bq:(2 * h + 2) * bq]
        att = jnp.exp(lss[h] + suffix + cs[h])
        if vis is not None:
            att = jnp.where(vis, att, 0.0)
        pvs.append(_dot(att.astype(BF16), vs[h]))
        new_cs.append(cs[h] + jnp.sum(l1s[h], axis=-1, keepdims=True))
    return jnp.concatenate(pvs, axis=-1), new_cs, jnp.max(functools.reduce(jnp.maximum, new_cs))


def _sb_prompt_kernel(q_ref, k_ref, v_ref, u_ref, o_ref, acc_ref, *, bq, n_pad):
    qi = pl.program_id(2)
    bk = SB_BK
    assert bq == bk and n_pad < bk
    qpos = lax.broadcasted_iota(jnp.int32, (bq, 1), 0)
    kio = lax.broadcasted_iota(jnp.int32, (1, bk), 1)
    umat = u_ref[...]
    heads = [slice(h * SB_DH, (h + 1) * SB_DH) for h in range(SB_HG)]
    qs = [(q_ref[:, hs] * (SB_DH ** -0.5)).astype(BF16) for hs in heads]

    def block(j, vis, cs):
        start = pl.multiple_of(j * bk, bk)
        return _sb_heads(qs, [k_ref[pl.ds(start, bk), hs] for hs in heads],
                         [v_ref[pl.ds(start, bk), hs] for hs in heads], vis, cs, umat)

    vis_diag = (kio < qpos) & (kio >= jnp.where(qi > 0, 0, n_pad))
    pv, cs, cmax = block(qi, vis_diag, [jnp.zeros((bq, 1), F32) for _ in heads])
    acc_ref[...] = pv

    def body(carry):
        j, cs, _ = carry
        pv, cs, cmax = block(j, None, list(cs))
        acc_ref[...] += pv
        return j - 1, tuple(cs), cmax

    def cond(carry):
        j, _, cmax = carry
        return (j >= 1) & (cmax > -SB_SKIP)

    j, cs, cmax = lax.while_loop(cond, body, (qi - 1, tuple(cs), cmax))

    @pl.when((j == 0) & (cmax > -SB_SKIP))
    def _():
        pv, _, _ = block(0, jnp.broadcast_to(kio >= n_pad, (bq, bk)), list(cs))
        acc_ref[...] += pv

    o_ref[...] = acc_ref[...].astype(BF16)


def sb_attention_prompt(z_main, kvb, umat, *, nb, tp, bq, n_pad):
    nq = tp // bq
    gw = SB_HG * SB_DH
    ng = SB_HEADS // SB_HG
    return pl.pallas_call(
        functools.partial(_sb_prompt_kernel, bq=bq, n_pad=n_pad),
        grid=(nb, ng, nq),
        in_specs=[pl.BlockSpec((bq, gw), lambda b, g, i: (b * nq + i, C_SQ // gw + g)),
                  pl.BlockSpec((tp, gw), lambda b, g, i: (b, C_SK // gw + g)),
                  pl.BlockSpec((tp, gw), lambda b, g, i: (b, C_SV // gw + g)),
                  pl.BlockSpec((SB_BK, SB_BK), lambda b, g, i: (0, 0))],
        out_specs=pl.BlockSpec((bq, gw), lambda b, g, i: (b * nq + i, g)),
        out_shape=jax.ShapeDtypeStruct((nb * tp, SB_HEADS * SB_DH), BF16),
        scratch_shapes=[pltpu.VMEM((bq, gw), F32)],
        compiler_params=_cparams(("arbitrary", "arbitrary", "arbitrary")),
        name="sb_prompt",
    )(z_main, kvb, kvb, umat)


def _sb_sample_kernel(q_ref, kn_ref, vn_ref, kc_ref, vc_ref, u_ref, o_ref, acc_ref, *, n_cache):
    T = CHUNK
    bk = SB_BK
    nfull = n_cache // bk
    rem = n_cache % bk
    umat = u_ref[...]
    heads = [slice(h * SB_DH, (h + 1) * SB_DH) for h in range(SB_HEADS)]
    qs = [(q_ref[:, hs] * (SB_DH ** -0.5)).astype(BF16) for hs in heads]
    vis_new = lax.broadcasted_iota(jnp.int32, (1, T), 1) < lax.broadcasted_iota(jnp.int32, (T, 1), 0)
    pv, cs, cmax = _sb_heads(qs, [kn_ref[:, hs].astype(BF16) for hs in heads],
                             [vn_ref[:, hs].astype(BF16) for hs in heads], vis_new,
                             [jnp.zeros((T, 1), F32) for _ in heads], umat[0:T, 0:T])
    acc_ref[...] = pv

    def cache_block(start, vis, cs):
        return _sb_heads(qs, [kc_ref[0, 0, pl.ds(start, bk), h, :].astype(BF16) for h in range(SB_HEADS)],
                         [vc_ref[0, 0, pl.ds(start, bk), h, :].astype(BF16) for h in range(SB_HEADS)],
                         vis, cs, umat)

    def body(carry):
        j, cs, _ = carry
        pv, cs, cmax = cache_block(pl.multiple_of(n_cache - bk * (j + 1), 8), None, list(cs))
        acc_ref[...] += pv
        return j + 1, tuple(cs), cmax

    def cond(carry):
        j, _, cmax = carry
        return (j < nfull) & (cmax > -SB_SKIP)

    _, cs, cmax = lax.while_loop(cond, body, (jnp.int32(0), tuple(cs), cmax))
    if rem:
        @pl.when(cmax > -SB_SKIP)
        def _():
            vis_rem = jnp.broadcast_to(lax.broadcasted_iota(jnp.int32, (1, bk), 1) < rem, (T, bk))
            pv, _, _ = cache_block(0, vis_rem, list(cs))
            acc_ref[...] += pv
    o_ref[...] = acc_ref[...].astype(BF16)


def sb_attention_sample(z_main, z_kv, cache_k, cache_v, umat, *, layer, row0, nb):
    T = CHUNK
    hd = SB_HEADS * SB_DH
    n_cache = cache_k.shape[2]
    assert n_cache % 8 == 0 and n_cache >= SB_BK
    rb0 = row0 // T
    cache_spec = pl.BlockSpec((1, 1, n_cache, SB_HEADS, SB_DH), lambda b: (layer, b, 0, 0, 0))
    return pl.pallas_call(
        functools.partial(_sb_sample_kernel, n_cache=n_cache),
        grid=(nb,),
        in_specs=[pl.BlockSpec((T, hd), lambda b: (rb0 + b, C_SQ // hd)),
                  pl.BlockSpec((T, hd), lambda b: (rb0 + b, C_SK // hd)),
                  pl.BlockSpec((T, hd), lambda b: (rb0 + b, C_SV // hd)),
                  cache_spec, cache_spec,
                  pl.BlockSpec((SB_BK, SB_BK), lambda b: (0, 0))],
        out_specs=pl.BlockSpec((T, hd), lambda b: (b, 0)),
        out_shape=jax.ShapeDtypeStruct((nb * T, hd), BF16),
        scratch_shapes=[pltpu.VMEM((T, hd), F32)],
        compiler_params=_cparams(("arbitrary",)),
        name="sb_sample",
    )(z_main, z_kv, z_kv, cache_k, cache_v, umat)


def _kv_out_kernel(*refs, depth, nb, nj, rt, n_front):
    z_refs = refs[:depth]
    k_hbm, v_hbm, kbuf, vbuf, sem = refs[depth:]
    l = pl.program_id(0)
    b = pl.program_id(1)
    j = pl.program_id(2)
    lin = (l * nb + b) * nj + j
    total = depth * nb * nj
    slot = lin % 2

    def copies(s, first, dst_lo):
        src_lo, n = (n_front, rt - n_front) if first else (0, rt)
        return [pltpu.make_async_copy(buf.at[s, pl.ds(src_lo, n)], out.at[l, b, pl.ds(dst_lo, n)], sem.at[s, c])
                for c, (buf, out) in enumerate(((kbuf, k_hbm), (vbuf, v_hbm)))]

    def wait_step(step, s):
        jj = step % nj

        @pl.when(jj == 0)
        def _():
            for cp in copies(s, True, 0):
                cp.wait()

        @pl.when(jj != 0)
        def _():
            for cp in copies(s, False, 0):
                cp.wait()

    @pl.when(lin >= 2)
    def _():
        wait_step(lin - 2, slot)

    x = z_refs[0][...]
    for i in range(1, depth):
        x = jnp.where(l == i, z_refs[i][...], x)
    hd = SB_HEADS * SB_DH
    kbuf[slot] = x[:, C_SK:C_SK + hd].reshape(rt, SB_HEADS, SB_DH)
    vbuf[slot] = x[:, C_SV:C_SV + hd].reshape(rt, SB_HEADS, SB_DH)

    @pl.when(j == 0)
    def _():
        for cp in copies(slot, True, 0):
            cp.start()

    @pl.when(j > 0)
    def _():
        for cp in copies(slot, False, j * rt - n_front):
            cp.start()

    @pl.when(lin == total - 1)
    def _():
        if total >= 2:
            wait_step(lin - 1, 1 - slot)
        wait_step(lin, slot)


def kv_state_outputs(z_kvs, *, nb, tp, n_front):
    depth = len(z_kvs)
    rt = _pick(tp, (768, 640, 512, 384, 256))
    assert rt > n_front
    nj = tp // rt
    w = z_kvs[0].shape[1]
    last = nb * nj - 1

    def zspec(i):
        return pl.BlockSpec((rt, w), lambda l, b, j: (jnp.where(l == i, b * nj + j, jnp.where(l < i, 0, last)), 0))

    shape = jax.ShapeDtypeStruct((depth, nb, tp - n_front, SB_HEADS, SB_DH), F32)
    return pl.pallas_call(
        functools.partial(_kv_out_kernel, depth=depth, nb=nb, nj=nj, rt=rt, n_front=n_front),
        grid=(depth, nb, nj),
        in_specs=[zspec(i) for i in range(depth)],
        out_specs=[pl.BlockSpec(memory_space=pl.ANY), pl.BlockSpec(memory_space=pl.ANY)],
        out_shape=[shape, shape],
        scratch_shapes=[pltpu.VMEM((2, rt, SB_HEADS, SB_DH), F32), pltpu.VMEM((2, rt, SB_HEADS, SB_DH), F32),
                        pltpu.SemaphoreType.DMA((2, 2))],
        compiler_params=_cparams(("arbitrary", "arbitrary", "arbitrary")),
        name="kv_state_out",
    )(*z_kvs)


def _merge_kernel(agp_ref, amp_ref, asp_ref, ags_ref, ams_ref, ass_ref, g0_ref, g1_ref, g2_ref, x_ref,
                  wg_ref, wm_ref, ws_ref, wo_ref, lg_ref, lb_ref, o_ref, or_ref, *, alpha, np_tiles):
    is_p = pl.program_id(0) < np_tiles
    ag = jnp.where(is_p, agp_ref[...], ags_ref[...])
    am = jnp.where(is_p, amp_ref[...], ams_ref[...])
    asb = jnp.where(is_p, asp_ref[...], ass_ref[...])
    merged = (_sigmoid(g0_ref[...]) * _dot(ag, wg_ref[...])
              + _sigmoid(g1_ref[...]) * _dot(am, wm_ref[...])
              + _sigmoid(g2_ref[...]) * _dot(asb, ws_ref[...]))
    out = _dot(merged.astype(BF16), wo_ref[...])
    y = _ln_math(alpha * x_ref[...] + out, lg_ref[...], lb_ref[...])
    o_ref[...] = y
    or_ref[...] = y.reshape(or_ref.shape)


def merge_ln1(acts_p, acts_s, z_main, x, wg, wm, ws, wo, lg, lb, alpha):
    n, d = x.shape
    n_p = acts_p[0].shape[0]
    n_s = acts_s[0].shape[0]
    tm = _pick(np.gcd(n_p, n_s), (256, 128, 64))
    assert n_p + n_s == n
    np_tiles = n_p // tm
    row = lambda i: (i, 0)
    row_p = lambda i: (jnp.minimum(i, np_tiles - 1), 0)
    row_s = lambda i: (jnp.maximum(i - np_tiles, 0), 0)
    const = lambda i: (0, 0)
    wspec = pl.BlockSpec((d, d), const)
    gcb = C_GATE // d
    return pl.pallas_call(
        functools.partial(_merge_kernel, alpha=alpha, np_tiles=np_tiles),
        grid=(n // tm,),
        in_specs=[pl.BlockSpec((tm, d), row_p), pl.BlockSpec((tm, d), row_p), pl.BlockSpec((tm, d), row_p),
                  pl.BlockSpec((tm, d), row_s), pl.BlockSpec((tm, d), row_s), pl.BlockSpec((tm, d), row_s),
                  pl.BlockSpec((tm, d), lambda i: (i, gcb)),
                  pl.BlockSpec((tm, d), lambda i: (i, gcb + 1)),
                  pl.BlockSpec((tm, d), lambda i: (i, gcb + 2)),
                  pl.BlockSpec((tm, d), row),
                  wspec, wspec, wspec, wspec,
                  pl.BlockSpec((1, d), const), pl.BlockSpec((1, d), const)],
        out_specs=[pl.BlockSpec((tm, d), row), pl.BlockSpec((tm, ROW_SUB, d // ROW_SUB), lambda i: (i, 0, 0))],
        out_shape=[jax.ShapeDtypeStruct((n, d), F32), jax.ShapeDtypeStruct((n, ROW_SUB, d // ROW_SUB), F32)],
        compiler_params=_cparams(("arbitrary",)),
        name="merge_ln1",
    )(*acts_p, *acts_s, z_main, z_main, z_main, x, wg, wm, ws, wo, lg.reshape(1, d), lb.reshape(1, d))


R_ROWS = 8 + N_EXPERTS


def _router_kernel(x_ref, w_ref, b_ref, u_ref, idx_ref, gate_ref, cnt_ref):
    i = pl.program_id(0)
    tm = x_ref.shape[0]

    @pl.when(i == 0)
    def _():
        cnt_ref[...] = jnp.zeros_like(cnt_ref)

    logits = _dot_nt(w_ref[...].astype(BF16), x_ref[...].astype(BF16)) + b_ref[...]

    g = [logits[j:j + 1, :] for j in range(N_GROUPS)]
    gmax = jnp.maximum(jnp.maximum(g[0], g[1]), jnp.maximum(g[2], g[3]))
    gsel = jnp.where(g[0] == gmax, 0, jnp.where(g[1] == gmax, 1, jnp.where(g[2] == gmax, 2, 3)))
    gden = (jnp.exp(g[0] - gmax) + jnp.exp(g[1] - gmax)) + (jnp.exp(g[2] - gmax) + jnp.exp(g[3] - gmax))
    gprob = 1.0 / gden
    e_in = jnp.where(gsel == 0, logits[8:16, :],
                     jnp.where(gsel == 1, logits[16:24, :],
                               jnp.where(gsel == 2, logits[24:32, :], logits[32:40, :])))
    ridx = lax.broadcasted_iota(jnp.int32, (EXPERTS_PER_GROUP, tm), 0)
    v1 = jnp.max(e_in, axis=0, keepdims=True)
    i1 = jnp.min(jnp.where(e_in == v1, ridx, EXPERTS_PER_GROUP), axis=0, keepdims=True)
    e2 = jnp.where(ridx == i1, -jnp.inf, e_in)
    v2 = jnp.max(e2, axis=0, keepdims=True)
    i2 = jnp.min(jnp.where(e2 == v2, ridx, EXPERTS_PER_GROUP), axis=0, keepdims=True)
    t = jnp.exp(v2 - v1)
    p1 = 1.0 / (1.0 + t)
    gate_ref[0:1, :] = p1 * gprob
    gate_ref[1:2, :] = (t * p1) * gprob
    gate_ref[2:8, :] = jnp.zeros((6, tm), F32)
    eid1 = gsel * EXPERTS_PER_GROUP + i1
    eid2 = gsel * EXPERTS_PER_GROUP + i2

    eidx = lax.broadcasted_iota(jnp.int32, (N_EXPERTS, tm), 0)
    oh1 = eidx == eid1
    oh2 = eidx == eid2
    oh1f = jnp.where(oh1, 1.0, 0.0)
    oh2f = jnp.where(oh2, 1.0, 0.0)
    umat = u_ref[...]
    pre1 = _dot(oh1f.astype(BF16), umat)
    pre2 = _dot(oh2f.astype(BF16), umat)
    cnt1 = jnp.sum(oh1f, axis=1, keepdims=True)
    cnt2 = jnp.sum(oh2f, axis=1, keepdims=True)
    base = cnt_ref[:, 0:1]
    rank1 = jnp.sum(jnp.where(oh1, base + pre1, 0.0), axis=0, keepdims=True)
    rank2 = jnp.sum(jnp.where(oh2, base + cnt1 + pre2, 0.0), axis=0, keepdims=True)
    idx_ref[0:1, :] = eid1
    idx_ref[1:2, :] = eid2
    idx_ref[2:3, :] = rank1.astype(jnp.int32)
    idx_ref[3:4, :] = rank2.astype(jnp.int32)
    idx_ref[4:8, :] = jnp.zeros((4, tm), jnp.int32)
    cnt_ref[...] = cnt_ref[...] + (cnt1 + cnt2)


def moe_router(x, w_r, b_r, umat):
    n, d = x.shape
    tm = umat.shape[0]
    return pl.pallas_call(
        _router_kernel,
        grid=(n // tm,),
        in_specs=[pl.BlockSpec((tm, d), lambda i: (i, 0)),
                  pl.BlockSpec((R_ROWS, d), lambda i: (0, 0)),
                  pl.BlockSpec((R_ROWS, 1), lambda i: (0, 0)),
                  pl.BlockSpec((tm, tm), lambda i: (0, 0))],
        out_specs=[pl.BlockSpec((8, tm), lambda i: (0, i)),
                   pl.BlockSpec((8, tm), lambda i: (0, i)),
                   pl.BlockSpec((N_EXPERTS, 128), lambda i: (0, 0))],
        out_shape=[jax.ShapeDtypeStruct((8, n), jnp.int32),
                   jax.ShapeDtypeStruct((8, n), F32),
                   jax.ShapeDtypeStruct((N_EXPERTS, 128), F32)],
        compiler_params=_cparams(("arbitrary",)),
        name="moe_router",
    )(x, w_r, b_r, umat)


def _row_copy(src_hbm, dst_vmem, src_row, dst_row, sem):
    return pltpu.make_async_copy(src_hbm.at[pl.ds(src_row, 1)], dst_vmem.at[pl.ds(dst_row, 1)], sem)


GATHER_UNROLL = 8


def _start_row_gather(idx_ref, base, src_hbm, dst_vmem, nrows, sem):
    def start(r, carry):
        _row_copy(src_hbm, dst_vmem, idx_ref[base + r], r, sem).start()
        return carry

    lax.fori_loop(0, nrows, start, 0, unroll=GATHER_UNROLL)


def _wait_row_gather(src_hbm, dst_vmem, nrows, sem):
    def wait(r, carry):
        _row_copy(src_hbm, dst_vmem, 0, r, sem).wait()
        return carry

    lax.fori_loop(0, nrows, wait, 0, unroll=GATHER_UNROLL)


def _moe_kernel(be_ref, tok_ref, nblk_ref, x_hbm, wg_ref, wu_ref, wd_ref, o_ref,
                xbuf, wgb, wub, wdb, sem):
    i = pl.program_id(0)
    nblk = nblk_ref[0]
    slot = i % 2

    @pl.when((i == 0) & (nblk > 0))
    def _():
        _start_row_gather(tok_ref, 0, x_hbm, xbuf.at[0], MOE_BLK, sem.at[0])

    @pl.when(i + 1 < nblk)
    def _():
        _start_row_gather(tok_ref, (i + 1) * MOE_BLK, x_hbm, xbuf.at[1 - slot], MOE_BLK, sem.at[1 - slot])

    prev_e = be_ref[jnp.maximum(i - 1, 0)]

    @pl.when((i == 0) | (be_ref[i] != prev_e))
    def _():
        wgb[...] = wg_ref[0, 0].astype(BF16)
        wub[...] = wu_ref[0, 0].astype(BF16)
        wdb[...] = wd_ref[0, 0].astype(BF16)

    @pl.when(i < nblk)
    def _():
        _wait_row_gather(x_hbm, xbuf.at[slot], MOE_BLK, sem.at[slot])
        xb = xbuf[slot].reshape(MOE_BLK, D_MODEL).astype(BF16)
        a = _dot(xb, wgb[...])
        hmid = (a * _sigmoid(a)) * _dot(xb, wub[...])
        o_ref[...] = _dot(hmid.astype(BF16), wdb[...]).reshape(o_ref.shape)

    @pl.when(i >= nblk)
    def _():
        o_ref[...] = jnp.zeros_like(o_ref)


def moe_experts(x, block_e, tok_pad, nblk, w_gate, w_up, w_down, *, layer):
    n = x.shape[0]
    d = D_MODEL
    rs = (ROW_SUB, d // ROW_SUB)
    nb = block_e.shape[0]
    wmap = lambda i, be, tok, nblk: (layer, be[i], 0, 0)
    grid_spec = pltpu.PrefetchScalarGridSpec(
        num_scalar_prefetch=3,
        grid=(nb,),
        in_specs=[pl.BlockSpec(memory_space=pl.ANY),
                  pl.BlockSpec((1, 1, d, D_EXPERT), wmap),
                  pl.BlockSpec((1, 1, d, D_EXPERT), wmap),
                  pl.BlockSpec((1, 1, D_EXPERT, d), wmap)],
        out_specs=pl.BlockSpec((MOE_BLK,) + rs, lambda i, be, tok, nblk: (i, 0, 0)),
        scratch_shapes=[pltpu.VMEM((2, MOE_BLK) + rs, F32),
                        pltpu.VMEM((d, D_EXPERT), BF16),
                        pltpu.VMEM((d, D_EXPERT), BF16),
                        pltpu.VMEM((D_EXPERT, d), BF16),
                        pltpu.SemaphoreType.DMA((2,))],
    )
    return pl.pallas_call(
        _moe_kernel,
        grid_spec=grid_spec,
        out_shape=jax.ShapeDtypeStruct((nb * MOE_BLK,) + rs, F32),
        compiler_params=_cparams(("arbitrary",)),
        name="moe_experts",
    )(block_e, tok_pad, nblk, x, w_gate, w_up, w_down)


def _combine_kernel(dest_ref, yb_hbm, x_ref, gate_ref, lg_ref, lb_ref, o_ref, ob_ref, ybuf, sem, *, alpha, n):
    i = pl.program_id(0)
    nt = pl.num_programs(0)
    tm = x_ref.shape[0]
    slot = i % 2

    def start(tile, s):
        for k in range(TOP_K):
            _start_row_gather(dest_ref, k * n + tile * tm, yb_hbm, ybuf.at[s, k], tm, sem.at[s, k])

    @pl.when(i == 0)
    def _():
        start(0, 0)

    @pl.when(i + 1 < nt)
    def _():
        start(i + 1, 1 - slot)

    for k in range(TOP_K):
        _wait_row_gather(yb_hbm, ybuf.at[slot, k], tm, sem.at[slot, k])
    gate = gate_ref[...]
    d = x_ref.shape[1]
    y = ybuf[slot, 0].reshape(tm, d) * gate[:, 0:1] + ybuf[slot, 1].reshape(tm, d) * gate[:, 1:2]
    out = _ln_math(alpha * x_ref[...] + y, lg_ref[...], lb_ref[...])
    o_ref[...] = out
    ob_ref[...] = out.astype(BF16)


def moe_combine_ln2(x, yb, dest, gate_cols, lg, lb, alpha):
    n, d = x.shape
    tm = _pick(n, (256, 128, 64))
    grid_spec = pltpu.PrefetchScalarGridSpec(
        num_scalar_prefetch=1,
        grid=(n // tm,),
        in_specs=[pl.BlockSpec(memory_space=pl.ANY),
                  pl.BlockSpec((tm, d), lambda i, dest: (i, 0)),
                  pl.BlockSpec((tm, TOP_K), lambda i, dest: (i, 0)),
                  pl.BlockSpec((1, d), lambda i, dest: (0, 0)),
                  pl.BlockSpec((1, d), lambda i, dest: (0, 0))],
        out_specs=[pl.BlockSpec((tm, d), lambda i, dest: (i, 0)), pl.BlockSpec((tm, d), lambda i, dest: (i, 0))],
        scratch_shapes=[pltpu.VMEM((2, TOP_K, tm, ROW_SUB, d // ROW_SUB), F32),
                        pltpu.SemaphoreType.DMA((2, TOP_K))],
    )
    return pl.pallas_call(
        functools.partial(_combine_kernel, alpha=alpha, n=n),
        grid_spec=grid_spec,
        out_shape=[jax.ShapeDtypeStruct((n, d), F32), jax.ShapeDtypeStruct((n, d), BF16)],
        compiler_params=_cparams(("arbitrary",)),
        name="moe_combine_ln2",
    )(dest, yb, x, gate_cols, lg.reshape(1, d), lb.reshape(1, d))


def _tri_lower(n):
    t = np.arange(n)
    return (t[None, :] <= t[:, None]).astype(np.float32)


def _strict_upper(n):
    t = np.arange(n)
    return (t[:, None] < t[None, :]).astype(np.float32)


def kernel(x_prompt, x_sample, state_gla, state_ml_C, state_ml_n, state_ml_m, state_ml_conv, cache_sb_k, cache_sb_v, meta_tokens, ln_in_g, ln_in_b, w_in, gla_w_a2, gla_b_a, gla_norm_g, ml_conv_w, ml_conv_b, ml_wq, ml_wk, ml_wv, ml_b_i, ml_b_f, ml_norm_g, w_br_gla, w_br_ml, w_br_sb, w_out, ln1_g, ln1_b, w_rg, b_rg, w_re, b_re, w_e_gate, w_e_up, w_e_down, ln2_g, ln2_b):
    D = D_MODEL
    B, SEQ, _ = x_prompt.shape
    DB, DS, _ = x_sample.shape
    depth = w_in.shape[0]
    n_cache = cache_sb_k.shape[2]
    assert DS == CHUNK and SEQ % 128 == 0 and x_prompt.shape[2] == D
    alpha = (2.0 * depth) ** 0.25
    TP = PAD_FRONT + N_META + SEQ
    NP = B * TP
    NS = DB * DS
    N = NP + NS
    assert TP % CHUNK_PROMPT == 0 and NP % CHUNK == 0
    nchunk_p = TP // CHUNK_PROMPT

    pieces = []
    for b in range(B):
        pieces += [jnp.zeros((PAD_FRONT, D), F32), meta_tokens.astype(F32), x_prompt[b].astype(F32)]
    x = jnp.concatenate(pieces + [x_sample.reshape(NS, D).astype(F32)], axis=0)
    x, xb = layer_norm_rows(x, ln_in_g, ln_in_b)

    u_sb = jnp.asarray(_strict_upper(SB_BK).T, BF16)
    tm_r = _pick(N, (256, 128, 64))
    u_r = jnp.asarray(_strict_upper(tm_r), BF16)

    A = N * TOP_K
    NB = -(-(A + N_EXPERTS * (MOE_BLK - 1)) // MOE_BLK)

    outs_p = [[] for _ in range(5)]
    outs_s = [[] for _ in range(7)]
    z_kvs = []
    for l in range(depth):
        wi = w_in[l]
        w_main = jnp.concatenate([wi[:, 0:3072], wi[:, 3088:4112], wi[:, 4120:5144], wi[:, 5144:6168],
                                  wi[:, 8216:]], axis=1).astype(BF16)
        w_kv = wi[:, 6168:8216].astype(BF16)
        w_small = jnp.concatenate([wi[:, 3072:3088], wi[:, 4112:4120],
                                   jnp.zeros((D, 128 - 24), F32)], axis=1).astype(BF16)
        z_main = in_projection(xb, w_main, 1024, "in_proj_main")
        z_kv, kvb = in_projection(xb, w_kv, 1024, "in_proj_kv", with_bf16=True)
        z_small = in_projection(xb, w_small, 128, "in_proj_small")
        z_kvs.append(z_kv)

        wa = jnp.concatenate([gla_w_a2[l], jnp.zeros((128 - GLA_RANK, GLA_HEADS * GLA_DK), F32)], axis=0).astype(BF16)
        ba = gla_b_a[l].reshape(1, -1)
        ng = gla_norm_g[l].reshape(1, -1)
        s0_p = jnp.zeros((B, GLA_HEADS, GLA_DV, GLA_DK), F32)
        s0_s = jnp.swapaxes(state_gla[l].astype(F32), -1, -2)
        geo_p = dict(row0=0, nb=B, nchunk=nchunk_p, chunk=CHUNK_PROMPT)
        geo_s = dict(row0=NP, nb=DB, nchunk=1, chunk=CHUNK)
        gla_p = _gla_call_parts(z_main, z_small, wa, ba, ng, s0_p, **geo_p)
        gla_s = _gla_call_parts(z_main, z_small, wa, ba, ng, s0_s, **geo_s)

        gbias = jnp.zeros((1, 128), F32).at[0, L_MI:L_MI + ML_HEADS].set(ml_b_i[l]).at[0, L_MF:L_MF + ML_HEADS].set(ml_b_f[l])
        cw = ml_conv_w[l]
        cb = ml_conv_b[l].reshape(1, -1)
        wqk = jnp.concatenate([ml_wq[l], ml_wk[l]], axis=-1).astype(BF16)
        wv = ml_wv[l].astype(BF16)
        mng = ml_norm_g[l].reshape(1, -1)
        W = ML_HEADS * ML_DH
        conv0_p = jnp.zeros((B, 8, W), F32)
        conv0_s = jnp.concatenate([jnp.zeros((DB, 8 - (ML_CONV - 1), W), F32), state_ml_conv[l].astype(F32)], axis=1)
        m0_s = jnp.zeros((DB, 1, 128), F32).at[:, 0, L_MF:L_MF + ML_HEADS].set(state_ml_m[l].astype(F32))
        ml_p = _mlstm_call_parts(
            z_main, z_small, gbias, cw, cb, wqk, wv, mng, conv0_p,
            jnp.zeros((1, B, ML_HEADS, ML_DH, ML_DH), F32), jnp.zeros((B, ML_HEADS, ML_DH), F32),
            jnp.zeros((B, 1, 128), F32), c0_layer=0, **geo_p)
        ml_s = _mlstm_call_parts(
            z_main, z_small, gbias, cw, cb, wqk, wv, mng, conv0_s,
            state_ml_C.astype(F32), state_ml_n[l].astype(F32), m0_s, c0_layer=l, **geo_s)
        a_gla_p, sT_p, a_ml_p, c_p, n_p, m_p = recurrent_branches(
            gla_p, ml_p, nb=B, nchunk=nchunk_p, chunk=CHUNK_PROMPT, n_pad=PAD_FRONT)
        a_gla_s, sT_s, a_ml_s, c_s, n_s, m_s = recurrent_branches(
            gla_s, ml_s, nb=DB, nchunk=1, chunk=CHUNK, n_pad=0)

        a_sb_p = sb_attention_prompt(z_main, kvb, u_sb, nb=B, tp=TP, bq=SB_BK, n_pad=PAD_FRONT)
        a_sb_s = sb_attention_sample(z_main, z_kv, cache_sb_k.astype(F32), cache_sb_v.astype(F32),
                                     u_sb, layer=l, row0=NP, nb=DB)

        x1, x1r = merge_ln1((a_gla_p, a_ml_p, a_sb_p), (a_gla_s, a_ml_s, a_sb_s), z_main, x,
                       w_br_gla[l].astype(BF16), w_br_ml[l].astype(BF16), w_br_sb[l].astype(BF16),
                       w_out[l].astype(BF16), ln1_g[l], ln1_b[l], alpha)

        w_r = jnp.concatenate([w_rg[l].T, jnp.zeros((8 - N_GROUPS, D), F32), w_re[l].T], axis=0)
        b_r = jnp.concatenate([b_rg[l], jnp.zeros((8 - N_GROUPS,), F32), b_re[l]]).reshape(R_ROWS, 1)
        idx, gates, cnt = moe_router(x1, w_r, b_r, u_r)
        counts = cnt[:, 0].astype(jnp.int32)
        padded = (counts + MOE_BLK - 1) // MOE_BLK * MOE_BLK
        ends_pad = jnp.cumsum(padded)
        start_pad = ends_pad - padded
        eids = jnp.arange(N_EXPERTS, dtype=jnp.int32)
        start_of = jnp.sum(jnp.where(idx[0:2, :, None] == eids, start_pad, 0), axis=-1)
        dest = (start_of + idx[2:4]).reshape(-1)
        tok = jnp.tile(jnp.arange(N, dtype=jnp.int32), TOP_K)
        tok_pad = jnp.zeros((NB * MOE_BLK,), jnp.int32).at[dest].set(
            tok, unique_indices=True, mode="promise_in_bounds")
        blk_lo = jnp.arange(NB, dtype=jnp.int32) * MOE_BLK
        block_e = jnp.minimum(jnp.sum((ends_pad[None, :] <= blk_lo[:, None]).astype(jnp.int32), axis=-1),
                              N_EXPERTS - 1)
        nblk = (ends_pad[-1:] // MOE_BLK).astype(jnp.int32)
        yb = moe_experts(x1r, block_e, tok_pad, nblk, w_e_gate, w_e_up, w_e_down, layer=l)
        x, xb = moe_combine_ln2(x1, yb, dest, gates[0:2].T, ln2_g[l], ln2_b[l], alpha)

        nct = ML_CONV - 1
        conv_p = jnp.stack([z_main[(b + 1) * TP - nct:(b + 1) * TP, C_MU:C_MU + W] for b in range(B)], axis=0)
        conv_s = z_main[NP:, C_MU:C_MU + W].reshape(DB, DS, W)[:, DS - nct:]
        zkv_s = z_kv[NP:].reshape(DB, DS, 2 * D)
        kv_shape = (SB_HEADS, SB_DH)
        outs_p[0].append(jnp.swapaxes(sT_p, -1, -2))
        outs_s[0].append(jnp.swapaxes(sT_s, -1, -2))
        outs_p[1].append(c_p); outs_s[1].append(c_s)
        outs_p[2].append(n_p); outs_s[2].append(n_s)
        outs_p[3].append(m_p[:, 0, L_MF:L_MF + ML_HEADS]); outs_s[3].append(m_s[:, 0, L_MF:L_MF + ML_HEADS])
        outs_p[4].append(conv_p); outs_s[4].append(conv_s)
        outs_s[5].append(zkv_s[:, :, C_SK:C_SK + D].reshape(DB, DS, *kv_shape))
        outs_s[6].append(zkv_s[:, :, C_SV:C_SV + D].reshape(DB, DS, *kv_shape))

    sbk_p, sbv_p = kv_state_outputs(z_kvs, nb=B, tp=TP, n_front=PAD_FRONT)
    sp = [jnp.stack(a, axis=0) for a in outs_p]
    ss = [jnp.stack(a, axis=0) for a in outs_s]
    y_prompt = jnp.stack([x[b * TP + PAD_FRONT + N_META:(b + 1) * TP] for b in range(B)], axis=0)
    y_sample = x[NP:].reshape(DB, DS, D)
    return (y_prompt, y_sample, sp[0], ss[0], sp[1], ss[1], sp[2], ss[2], sp[3], ss[3],
            sp[4], ss[4], sbk_p, ss[5], sbv_p, ss[6])
```

```python
import functools

import numpy as np
import jax
import jax.numpy as jnp
from jax import lax
from jax.experimental import pallas as pl
from jax.experimental.pallas import tpu as pltpu

F32 = jnp.float32
BF16 = jnp.bfloat16

D_MODEL = 1024
N_META = 16
CHUNK = 64
CHUNK_PROMPT = 256
PAD_FRONT = 240
GLA_HEADS, GLA_DK, GLA_DV, GLA_RANK, GLA_TAU = 4, 128, 256, 16, 16.0
ML_HEADS, ML_DH, ML_CONV = 4, 256, 4
SB_HEADS, SB_DH = 8, 128
N_GROUPS, EXPERTS_PER_GROUP, TOP_K = 4, 8, 2
N_EXPERTS = N_GROUPS * EXPERTS_PER_GROUP
D_EXPERT = D_MODEL // 2
MOE_BLK = 256
ROW_SUB = 8
LN_EPS = 1e-5
NEG_BIG = -1e30
SB_SKIP = 110.0
SB_BK = 256

C_GQ, C_GK, C_GV, C_GR = 0, 512, 1024, 2048
C_MU, C_MO = 3072, 4096
C_SQ = 5120
C_GATE = 6144
C_SK, C_SV, C_SMALL = 0, 1024, 2048
SB_HG = 4
L_MI, L_MF = 16, 20

VMEM_LIMIT = 48 * 1024 * 1024


def _cparams(sem):
    return pltpu.CompilerParams(dimension_semantics=sem, vmem_limit_bytes=VMEM_LIMIT)


def _pick(n, cands):
    for c in cands:
        if n % c == 0:
            return c
    raise ValueError(f"no tile in {cands} divides {n}")


def _log_sigmoid(x):
    return jnp.minimum(x, 0.0) - jnp.log1p(jnp.exp(-jnp.abs(x)))


def _sigmoid(x):
    return 1.0 / (1.0 + jnp.exp(-x))


def _split2(x):
    hi = x.astype(BF16)
    lo = (x - hi.astype(F32)).astype(BF16)
    return hi, lo


def _dot(a, b):
    return jnp.dot(a, b, preferred_element_type=F32)


def _dot_nt(a, b):
    return lax.dot_general(a, b, (((1,), (1,)), ((), ())), preferred_element_type=F32)


def _dot_tn(a, b):
    return lax.dot_general(a, b, (((0,), (0,)), ((), ())), preferred_element_type=F32)


def _ln_math(x, g, b):
    mu = jnp.mean(x, axis=-1, keepdims=True)
    xc = x - mu
    var = jnp.mean(xc * xc, axis=-1, keepdims=True)
    return xc * lax.rsqrt(var + LN_EPS) * g + b


def _ln_kernel(x_ref, g_ref, b_ref, o_ref, ob_ref):
    y = _ln_math(x_ref[...], g_ref[...], b_ref[...])
    o_ref[...] = y
    ob_ref[...] = y.astype(BF16)


def layer_norm_rows(x, g, b):
    n, d = x.shape
    tm = _pick(n, (512, 256, 128, 64))
    return pl.pallas_call(
        _ln_kernel,
        grid=(n // tm,),
        in_specs=[pl.BlockSpec((tm, d), lambda i: (i, 0)),
                  pl.BlockSpec((1, d), lambda i: (0, 0)),
                  pl.BlockSpec((1, d), lambda i: (0, 0))],
        out_specs=[pl.BlockSpec((tm, d), lambda i: (i, 0)), pl.BlockSpec((tm, d), lambda i: (i, 0))],
        out_shape=[jax.ShapeDtypeStruct((n, d), F32), jax.ShapeDtypeStruct((n, d), BF16)],
        compiler_params=_cparams(("arbitrary",)),
        name="ln_in",
    )(x, g.reshape(1, d), b.reshape(1, d))


def _mm_kernel(x_ref, w_ref, o_ref):
    o_ref[...] = _dot(x_ref[...], w_ref[...])


def _mm2_kernel(x_ref, w_ref, o_ref, ob_ref):
    z = _dot(x_ref[...], w_ref[...])
    o_ref[...] = z
    ob_ref[...] = z.astype(BF16)


def in_projection(x, w_bf16, tn, name, with_bf16=False):
    n, d = x.shape
    cols = w_bf16.shape[1]
    tm = _pick(n, (896, 768, 512, 384, 256, 128, 64))
    out_spec = pl.BlockSpec((tm, tn), lambda j, i: (i, j))
    return pl.pallas_call(
        _mm2_kernel if with_bf16 else _mm_kernel,
        grid=(cols // tn, n // tm),
        in_specs=[pl.BlockSpec((tm, d), lambda j, i: (i, 0)),
                  pl.BlockSpec((d, tn), lambda j, i: (0, j))],
        out_specs=[out_spec, out_spec] if with_bf16 else out_spec,
        out_shape=([jax.ShapeDtypeStruct((n, cols), F32), jax.ShapeDtypeStruct((n, cols), BF16)]
                   if with_bf16 else jax.ShapeDtypeStruct((n, cols), F32)),
        compiler_params=_cparams(("arbitrary", "arbitrary")),
        name=name,
    )(x, w_bf16)


def _gla_levels(L):
    return tuple(L >> i for i in range(1, L.bit_length()))


def _gla_level_matrix(L):
    t = np.arange(L)[:, None]
    j = np.arange(L)[None, :]
    blocks = []
    for m in (L,) + _gla_levels(L):
        blocks.append(((j <= t) & (j // m == t // m)).astype(np.float32))
    for m in (L,) + _gla_levels(L):
        blocks.append(((j > t) & (j // m == t // m)).astype(np.float32))
    return np.concatenate(blocks, axis=0)


def _gla_kernel(q_ref, k_ref, v_ref, r_ref, lr_ref, wa_ref, ba_ref, ng_ref, lvl_ref, s0_ref,
                o_ref, s_ref, *, n_pad, L, phase):
    c = pl.program_id(1)
    GLA_LEVELS = _gla_levels(L)

    if phase == "init":
        @pl.when(c == 0)
        def _():
            s_ref[...] = s0_ref[...]
        return

    row = c * L + lax.broadcasted_iota(jnp.int32, (L, 1), 0)
    valid = row >= n_pad
    x = _dot(lr_ref[...].astype(BF16), wa_ref[...]) + ba_ref[...]
    la = jnp.where(valid, _log_sigmoid(x) * (1.0 / GLA_TAU), 0.0)
    la_hi, la_lo = _split2(la)
    lvl = lvl_ref[...]
    ex = jnp.exp(_dot(lvl, la_hi) + _dot(lvl, la_lo))
    nlev = len(GLA_LEVELS) + 1

    ti = lax.broadcasted_iota(jnp.int32, (L, L), 0)
    si = lax.broadcasted_iota(jnp.int32, (L, L), 1)
    level_masks = []
    for m in GLA_LEVELS:
        sh = m.bit_length() - 1
        tb = lax.shift_right_logical(ti, sh)
        sb = lax.shift_right_logical(si, sh)
        level_masks.append((lax.shift_right_logical(tb, 1) == lax.shift_right_logical(sb, 1))
                           & ((tb & 1) == 1) & ((sb & 1) == 0))
    eye = ti == si

    outs = []
    states = []
    for h in range(GLA_HEADS):
        dk = slice(h * GLA_DK, (h + 1) * GLA_DK)
        dv = slice(h * GLA_DV, (h + 1) * GLA_DV)
        q = q_ref[:, dk] * (GLA_DK ** -0.5)
        k = jnp.where(valid, k_ref[:, dk], 0.0)
        v = jnp.where(valid, v_ref[:, dv], 0.0).astype(BF16)
        att = jnp.where(eye, _dot_nt(q.astype(BF16), k.astype(BF16)), 0.0)
        for lev in range(len(GLA_LEVELS)):
            qe = ex[(1 + lev) * L:(2 + lev) * L, dk]
            ke = ex[(nlev + 1 + lev) * L:(nlev + 2 + lev) * L, dk]
            a = _dot_nt((q * qe).astype(BF16), (k * ke).astype(BF16))
            att = jnp.where(level_masks[lev], a, att)
        eb = ex[0:L, dk]
        st = s_ref[0, h]
        o = _dot(att.astype(BF16), v) + _dot_nt((q * eb).astype(BF16), st.astype(BF16))
        ke_end = ex[nlev * L:(nlev + 1) * L, dk]
        states.append(st * eb[L - 1:L, :] + _dot_tn(v, (k * ke_end).astype(BF16)))
        mu = jnp.mean(o, axis=-1, keepdims=True)
        oc = o - mu
        var = jnp.mean(oc * oc, axis=-1, keepdims=True)
        on = oc * lax.rsqrt(var + LN_EPS) * ng_ref[...]
        r = r_ref[:, dv]
        outs.append((on * (r * _sigmoid(r))).astype(BF16))
    o_ref[...] = jnp.concatenate(outs, axis=-1)
    s_ref[0] = jnp.stack(states, axis=0)


def _gla_call_parts(z_main, z_small, wa, ba, ng, s0t, *, row0, nb, nchunk, chunk):
    L = chunk
    assert row0 % L == 0
    lvl = jnp.asarray(_gla_level_matrix(L), BF16)
    rb0 = row0 // L
    hk = GLA_HEADS * GLA_DK
    hv = GLA_HEADS * GLA_DV

    def rows(colblk):
        return lambda b, c: (rb0 + b * nchunk + c, colblk)

    return dict(
        args=(z_main, z_main, z_main, z_main, z_small, wa, ba, ng, lvl, s0t),
        in_specs=[pl.BlockSpec((L, hk), rows(C_GQ // hk)),
                  pl.BlockSpec((L, hk), rows(C_GK // hk)),
                  pl.BlockSpec((L, hv), rows(C_GV // hv)),
                  pl.BlockSpec((L, hv), rows(C_GR // hv)),
                  pl.BlockSpec((L, 128), rows(C_SMALL // 128)),
                  pl.BlockSpec((128, hk), lambda b, c: (0, 0)),
                  pl.BlockSpec((1, hk), lambda b, c: (0, 0)),
                  pl.BlockSpec((1, GLA_DV), lambda b, c: (0, 0)),
                  pl.BlockSpec(lvl.shape, lambda b, c: (0, 0)),
                  pl.BlockSpec((1, GLA_HEADS, GLA_DV, GLA_DK), lambda b, c: (b, 0, 0, 0))],
        out_specs=[pl.BlockSpec((L, hv), lambda b, c: (b * nchunk + c, 0)),
                   pl.BlockSpec((1, GLA_HEADS, GLA_DV, GLA_DK), lambda b, c: (b, 0, 0, 0))],
        out_shape=[jax.ShapeDtypeStruct((nb * nchunk * L, hv), BF16),
                   jax.ShapeDtypeStruct((nb, GLA_HEADS, GLA_DV, GLA_DK), F32)],
        scratch=[])


def _mlstm_kernel(u_ref, mo_ref, g_ref, gb_ref, cw_ref, cb_ref, wqk_ref, wv_ref, ng_ref, tri_ref,
                  conv0_ref, c0_ref, n0_ref, m0_ref,
                  o_ref, c_ref, n_ref, m_ref, ext_ref, *, n_pad, L, phase):
    c = pl.program_id(1)
    W = ML_HEADS * ML_DH

    if phase == "init":
        @pl.when(c == 0)
        def _():
            ext_ref[0:8, :] = conv0_ref[0]
            c_ref[...] = c0_ref[0]
            n_ref[...] = n0_ref[...]
            m_ref[...] = m0_ref[...]
        return

    row = c * L + lax.broadcasted_iota(jnp.int32, (L, 1), 0)
    valid = row >= n_pad

    u = jnp.where(valid, u_ref[...], 0.0)
    ext_ref[8:8 + L, :] = u
    uc = cb_ref[...]
    for i in range(ML_CONV):
        uc = uc + ext_ref[8 - (ML_CONV - 1) + i:8 - (ML_CONV - 1) + i + L, :] * cw_ref[i:i + 1, :]
    ext_ref[0:8, :] = u[L - 8:L, :]
    ucs = (uc * _sigmoid(uc)).astype(BF16)
    ub = u_ref[...].astype(BF16)

    g = g_ref[...] + gb_ref[...]
    li = jnp.where(valid, g, NEG_BIG)
    lf = jnp.where(valid, _log_sigmoid(g), 0.0)
    tri = tri_ref[...]
    lf_hi, lf_lo = _split2(lf)
    b_col = _dot(tri, lf_hi) + _dot(tri, lf_lo)
    a_col = b_col + m_ref[0]
    li_row = li.T
    b_row = b_col.T

    ti = lax.broadcasted_iota(jnp.int32, (L, L), 0)
    si = lax.broadcasted_iota(jnp.int32, (L, L), 1)
    causal = si <= ti
    lane = lax.broadcasted_iota(jnp.int32, (1, 128), 1)
    m_vec = m_ref[0]
    m_out = m_vec
    outs, c_new, n_new = [], [], []

    for h in range(ML_HEADS):
        d = slice(h * ML_DH, (h + 1) * ML_DH)
        qk_proj = _dot(ucs[:, d], wqk_ref[h])
        q = qk_proj[:, 0:ML_DH].astype(BF16)
        k = qk_proj[:, ML_DH:2 * ML_DH] * (ML_DH ** -0.5)
        v = _dot(ub[:, d], wv_ref[h]).astype(BF16)
        lf_l = L_MF + h
        li_l = L_MI + h
        bc = b_col[:, lf_l:lf_l + 1]
        ac = a_col[:, lf_l:lf_l + 1]
        lic = li[:, li_l:li_l + 1]
        dmat = jnp.where(causal, bc - b_row[lf_l:lf_l + 1, :] + li_row[li_l:li_l + 1, :], NEG_BIG)
        mt = jnp.maximum(ac, jnp.max(dmat, axis=-1, keepdims=True))
        w_intra = jnp.exp(dmat - mt)
        w_inter = jnp.exp(ac - mt)
        qk = _dot_nt(q, k.astype(BF16)) * w_intra
        cmat = c_ref[0, h]
        nrow = n_ref[0, h:h + 1, :]
        num = _dot(qk.astype(BF16), v) + w_inter * _dot(q, cmat.astype(BF16))
        qn = jnp.sum(q.astype(F32) * nrow.astype(BF16).astype(F32), axis=-1, keepdims=True)
        den = jnp.sum(qk, axis=-1, keepdims=True) + w_inter * qn
        hh = num / jnp.maximum(jnp.abs(den), jnp.exp(-mt))
        m_new = mt[L - 1:L, :]
        b_end = bc[L - 1:L, :]
        wk = jnp.exp(b_end - bc + lic - m_new)
        ws = jnp.exp(b_end + m_vec[:, lf_l:lf_l + 1] - m_new)
        kw = k * wk
        c_new.append(ws * cmat + _dot_tn(kw.astype(BF16), v))
        n_new.append(ws * nrow + jnp.sum(kw, axis=0, keepdims=True))
        m_out = jnp.where(lane == lf_l, m_new, m_out)
        mu = jnp.mean(hh, axis=-1, keepdims=True)
        hc = hh - mu
        var = jnp.mean(hc * hc, axis=-1, keepdims=True)
        hn = hc * lax.rsqrt(var + LN_EPS) * ng_ref[...]
        outs.append((hn * _sigmoid(mo_ref[:, d])).astype(BF16))
    o_ref[...] = jnp.concatenate(outs, axis=-1)
    c_ref[0] = jnp.stack(c_new, axis=0)
    n_ref[0] = jnp.concatenate(n_new, axis=0)
    m_ref[0] = m_out


def _mlstm_call_parts(z_main, z_small, gbias, cw, cb, wqk, wv, ng, conv0, c0, n0, m0,
                      *, c0_layer, row0, nb, nchunk, chunk):
    L = chunk
    assert row0 % L == 0
    tri = jnp.asarray(_tri_lower(L), BF16)
    rb0 = row0 // L
    W = ML_HEADS * ML_DH

    def rows(colblk):
        return lambda b, c: (rb0 + b * nchunk + c, colblk)

    const2 = lambda b, c: (0, 0)
    const3 = lambda b, c: (0, 0, 0)
    return dict(
        args=(z_main, z_main, z_small, gbias, cw, cb, wqk, wv, ng, tri, conv0, c0, n0, m0),
        in_specs=[pl.BlockSpec((L, W), rows(C_MU // W)),
                  pl.BlockSpec((L, W), rows(C_MO // W)),
                  pl.BlockSpec((L, 128), rows(C_SMALL // 128)),
                  pl.BlockSpec((1, 128), const2),
                  pl.BlockSpec((ML_CONV, W), const2),
                  pl.BlockSpec((1, W), const2),
                  pl.BlockSpec((ML_HEADS, ML_DH, 2 * ML_DH), const3),
                  pl.BlockSpec((ML_HEADS, ML_DH, ML_DH), const3),
                  pl.BlockSpec((1, ML_DH), const2),
                  pl.BlockSpec((L, L), const2),
                  pl.BlockSpec((1, 8, W), lambda b, c: (b, 0, 0)),
                  pl.BlockSpec((1, 1, ML_HEADS, ML_DH, ML_DH), lambda b, c: (c0_layer, b, 0, 0, 0)),
                  pl.BlockSpec((1, ML_HEADS, ML_DH), lambda b, c: (b, 0, 0)),
                  pl.BlockSpec((1, 1, 128), lambda b, c: (b, 0, 0))],
        out_specs=[pl.BlockSpec((L, W), lambda b, c: (b * nchunk + c, 0)),
                   pl.BlockSpec((1, ML_HEADS, ML_DH, ML_DH), lambda b, c: (b, 0, 0, 0)),
                   pl.BlockSpec((1, ML_HEADS, ML_DH), lambda b, c: (b, 0, 0)),
                   pl.BlockSpec((1, 1, 128), lambda b, c: (b, 0, 0))],
        out_shape=[jax.ShapeDtypeStruct((nb * nchunk * L, W), BF16),
                   jax.ShapeDtypeStruct((nb, ML_HEADS, ML_DH, ML_DH), F32),
                   jax.ShapeDtypeStruct((nb, ML_HEADS, ML_DH), F32),
                   jax.ShapeDtypeStruct((nb, 1, 128), F32)],
        scratch=[pltpu.VMEM((8 + L, W), F32)])


def _recurrent_kernel(*refs, n_gla_in, n_ml_in, n_pad, L):
    g_in = refs[:n_gla_in]
    m_in = refs[n_gla_in:n_gla_in + n_ml_in]
    g_out = refs[n_gla_in + n_ml_in:n_gla_in + n_ml_in + 2]
    m_rest = refs[n_gla_in + n_ml_in + 2:]
    for phase in ("init", "body"):
        _gla_kernel(*g_in, *g_out, n_pad=n_pad, L=L, phase=phase)
        _mlstm_kernel(*m_in, *m_rest, n_pad=n_pad, L=L, phase=phase)


def recurrent_branches(gla_parts, ml_parts, *, nb, nchunk, chunk, n_pad):
    return pl.pallas_call(
        functools.partial(_recurrent_kernel, n_gla_in=len(gla_parts["args"]), n_ml_in=len(ml_parts["args"]),
                          n_pad=n_pad, L=chunk),
        grid=(nb, nchunk),
        in_specs=gla_parts["in_specs"] + ml_parts["in_specs"],
        out_specs=gla_parts["out_specs"] + ml_parts["out_specs"],
        out_shape=gla_parts["out_shape"] + ml_parts["out_shape"],
        scratch_shapes=gla_parts["scratch"] + ml_parts["scratch"],
        compiler_params=_cparams(("arbitrary", "arbitrary")),
        name="gla_mlstm",
    )(*gla_parts["args"], *ml_parts["args"])


def _sb_heads(qs, ks, vs, vis, cs, umat):
    nh = len(qs)
    bq = qs[0].shape[0]
    lss, l1s, parts = [], [], []
    for h in range(nh):
        z = _dot_nt(qs[h], ks[h])
        t = jnp.log(1.0 + jnp.exp(-jnp.abs(z)))
        lss.append(jnp.minimum(z, 0.0) - t)
        l1 = jnp.minimum(-z, 0.0) - t
        if vis is not None:
            l1 = jnp.where(vis, l1, 0.0)
        l1s.append(l1)
        parts.extend(_split2(l1))
    suf = _dot(jnp.concatenate(parts, axis=0), umat)
    pvs, new_cs = [], []
    for h in range(nh):
        suffix = suf[2 * h * bq:(2 * h + 1) * bq] + suf[(2 * h + 1) * bq:(2 * h + 2) * bq]
        att = jnp.exp(lss[h] + suffix + cs[h])
        if vis is not None:
            att = jnp.where(vis, att, 0.0)
        pvs.append(_dot(att.astype(BF16), vs[h]))
        new_cs.append(cs[h] + jnp.sum(l1s[h], axis=-1, keepdims=True))
    return jnp.concatenate(pvs, axis=-1), new_cs, jnp.max(functools.reduce(jnp.maximum, new_cs))


def _sb_prompt_kernel(q_ref, k_ref, v_ref, u_ref, o_ref, acc_ref, *, bq, n_pad):
    qi = pl.program_id(2)
    bk = SB_BK
    assert bq == bk and n_pad < bk
    qpos = lax.broadcasted_iota(jnp.int32, (bq, 1), 0)
    kio = lax.broadcasted_iota(jnp.int32, (1, bk), 1)
    umat = u_ref[...]
    heads = [slice(h * SB_DH, (h + 1) * SB_DH) for h in range(SB_HG)]
    qs = [(q_ref[:, hs] * (SB_DH ** -0.5)).astype(BF16) for hs in heads]

    def block(j, vis, cs):
        start = pl.multiple_of(j * bk, bk)
        return _sb_heads(qs, [k_ref[pl.ds(start, bk), hs] for hs in heads],
                         [v_ref[pl.ds(start, bk), hs] for hs in heads], vis, cs, umat)

    vis_diag = (kio < qpos) & (kio >= jnp.where(qi > 0, 0, n_pad))
    pv, cs, cmax = block(qi, vis_diag, [jnp.zeros((bq, 1), F32) for _ in heads])
    acc_ref[...] = pv

    def body(carry):
        j, cs, _ = carry
        pv, cs, cmax = block(j, None, list(cs))
        acc_ref[...] += pv
        return j - 1, tuple(cs), cmax

    def cond(carry):
        j, _, cmax = carry
        return (j >= 1) & (cmax > -SB_SKIP)

    j, cs, cmax = lax.while_loop(cond, body, (qi - 1, tuple(cs), cmax))

    @pl.when((j == 0) & (cmax > -SB_SKIP))
    def _():
        pv, _, _ = block(0, jnp.broadcast_to(kio >= n_pad, (bq, bk)), list(cs))
        acc_ref[...] += pv

    o_ref[...] = acc_ref[...].astype(BF16)


def sb_attention_prompt(z_main, kvb, umat, *, nb, tp, bq, n_pad):
    nq = tp // bq
    gw = SB_HG * SB_DH
    ng = SB_HEADS // SB_HG
    return pl.pallas_call(
        functools.partial(_sb_prompt_kernel, bq=bq, n_pad=n_pad),
        grid=(nb, ng, nq),
        in_specs=[pl.BlockSpec((bq, gw), lambda b, g, i: (b * nq + i, C_SQ // gw + g)),
                  pl.BlockSpec((tp, gw), lambda b, g, i: (b, C_SK // gw + g)),
                  pl.BlockSpec((tp, gw), lambda b, g, i: (b, C_SV // gw + g)),
                  pl.BlockSpec((SB_BK, SB_BK), lambda b, g, i: (0, 0))],
        out_specs=pl.BlockSpec((bq, gw), lambda b, g, i: (b * nq + i, g)),
        out_shape=jax.ShapeDtypeStruct((nb * tp, SB_HEADS * SB_DH), BF16),
        scratch_shapes=[pltpu.VMEM((bq, gw), F32)],
        compiler_params=_cparams(("arbitrary", "arbitrary", "arbitrary")),
        name="sb_prompt",
    )(z_main, kvb, kvb, umat)


def _sb_sample_kernel(q_ref, kn_ref, vn_ref, kc_ref, vc_ref, u_ref, o_ref, acc_ref, *, n_cache):
    T = CHUNK
    bk = SB_BK
    nfull = n_cache // bk
    rem = n_cache % bk
    umat = u_ref[...]
    heads = [slice(h * SB_DH, (h + 1) * SB_DH) for h in range(SB_HEADS)]
    qs = [(q_ref[:, hs] * (SB_DH ** -0.5)).astype(BF16) for hs in heads]
    vis_new = lax.broadcasted_iota(jnp.int32, (1, T), 1) < lax.broadcasted_iota(jnp.int32, (T, 1), 0)
    pv, cs, cmax = _sb_heads(qs, [kn_ref[:, hs].astype(BF16) for hs in heads],
                             [vn_ref[:, hs].astype(BF16) for hs in heads], vis_new,
                             [jnp.zeros((T, 1), F32) for _ in heads], umat[0:T, 0:T])
    acc_ref[...] = pv

    def cache_block(start, vis, cs):
        return _sb_heads(qs, [kc_ref[0, 0, pl.ds(start, bk), h, :].astype(BF16) for h in range(SB_HEADS)],
                         [vc_ref[0, 0, pl.ds(start, bk), h, :].astype(BF16) for h in range(SB_HEADS)],
                         vis, cs, umat)

    def body(carry):
        j, cs, _ = carry
        pv, cs, cmax = cache_block(pl.multiple_of(n_cache - bk * (j + 1), 8), None, list(cs))
        acc_ref[...] += pv
        return j + 1, tuple(cs), cmax

    def cond(carry):
        j, _, cmax = carry
        return (j < nfull) & (cmax > -SB_SKIP)

    _, cs, cmax = lax.while_loop(cond, body, (jnp.int32(0), tuple(cs), cmax))
    if rem:
        @pl.when(cmax > -SB_SKIP)
        def _():
            vis_rem = jnp.broadcast_to(lax.broadcasted_iota(jnp.int32, (1, bk), 1) < rem, (T, bk))
            pv, _, _ = cache_block(0, vis_rem, list(cs))
            acc_ref[...] += pv
    o_ref[...] = acc_ref[...].astype(BF16)


def sb_attention_sample(z_main, z_kv, cache_k, cache_v, umat, *, layer, row0, nb):
    T = CHUNK
    hd = SB_HEADS * SB_DH
    n_cache = cache_k.shape[2]
    assert n_cache % 8 == 0 and n_cache >= SB_BK
    rb0 = row0 // T
    cache_spec = pl.BlockSpec((1, 1, n_cache, SB_HEADS, SB_DH), lambda b: (layer, b, 0, 0, 0))
    return pl.pallas_call(
        functools.partial(_sb_sample_kernel, n_cache=n_cache),
        grid=(nb,),
        in_specs=[pl.BlockSpec((T, hd), lambda b: (rb0 + b, C_SQ // hd)),
                  pl.BlockSpec((T, hd), lambda b: (rb0 + b, C_SK // hd)),
                  pl.BlockSpec((T, hd), lambda b: (rb0 + b, C_SV // hd)),
                  cache_spec, cache_spec,
                  pl.BlockSpec((SB_BK, SB_BK), lambda b: (0, 0))],
        out_specs=pl.BlockSpec((T, hd), lambda b: (b, 0)),
        out_shape=jax.ShapeDtypeStruct((nb * T, hd), BF16),
        scratch_shapes=[pltpu.VMEM((T, hd), F32)],
        compiler_params=_cparams(("arbitrary",)),
        name="sb_sample",
    )(z_main, z_kv, z_kv, cache_k, cache_v, umat)


def _kv_out_kernel(*refs, depth, nb, nj, rt, n_front):
    z_refs = refs[:depth]
    k_hbm, v_hbm, kbuf, vbuf, sem = refs[depth:]
    l = pl.program_id(0)
    b = pl.program_id(1)
    j = pl.program_id(2)
    lin = (l * nb + b) * nj + j
    total = depth * nb * nj
    slot = lin % 2

    def copies(s, first, dst_lo):
        src_lo, n = (n_front, rt - n_front) if first else (0, rt)
        return [pltpu.make_async_copy(buf.at[s, pl.ds(src_lo, n)], out.at[l, b, pl.ds(dst_lo, n)], sem.at[s, c])
                for c, (buf, out) in enumerate(((kbuf, k_hbm), (vbuf, v_hbm)))]

    def wait_step(step, s):
        jj = step % nj

        @pl.when(jj == 0)
        def _():
            for cp in copies(s, True, 0):
                cp.wait()

        @pl.when(jj != 0)
        def _():
            for cp in copies(s, False, 0):
                cp.wait()

    @pl.when(lin >= 2)
    def _():
        wait_step(lin - 2, slot)

    x = z_refs[0][...]
    for i in range(1, depth):
        x = jnp.where(l == i, z_refs[i][...], x)
    hd = SB_HEADS * SB_DH
    kbuf[slot] = x[:, C_SK:C_SK + hd].reshape(rt, SB_HEADS, SB_DH)
    vbuf[slot] = x[:, C_SV:C_SV + hd].reshape(rt, SB_HEADS, SB_DH)

    @pl.when(j == 0)
    def _():
        for cp in copies(slot, True, 0):
            cp.start()

    @pl.when(j > 0)
    def _():
        for cp in copies(slot, False, j * rt - n_front):
            cp.start()

    @pl.when(lin == total - 1)
    def _():
        if total >= 2:
            wait_step(lin - 1, 1 - slot)
        wait_step(lin, slot)


def kv_state_outputs(z_kvs, *, nb, tp, n_front):
    depth = len(z_kvs)
    rt = _pick(tp, (768, 640, 512, 384, 256))
    assert rt > n_front
    nj = tp // rt
    w = z_kvs[0].shape[1]
    last = nb * nj - 1

    def zspec(i):
        return pl.BlockSpec((rt, w), lambda l, b, j: (jnp.where(l == i, b * nj + j, jnp.where(l < i, 0, last)), 0))

    shape = jax.ShapeDtypeStruct((depth, nb, tp - n_front, SB_HEADS, SB_DH), F32)
    return pl.pallas_call(
        functools.partial(_kv_out_kernel, depth=depth, nb=nb, nj=nj, rt=rt, n_front=n_front),
        grid=(depth, nb, nj),
        in_specs=[zspec(i) for i in range(depth)],
        out_specs=[pl.BlockSpec(memory_space=pl.ANY), pl.BlockSpec(memory_space=pl.ANY)],
        out_shape=[shape, shape],
        scratch_shapes=[pltpu.VMEM((2, rt, SB_HEADS, SB_DH), F32), pltpu.VMEM((2, rt, SB_HEADS, SB_DH), F32),
                        pltpu.SemaphoreType.DMA((2, 2))],
        compiler_params=_cparams(("arbitrary", "arbitrary", "arbitrary")),
        name="kv_state_out",
    )(*z_kvs)


def _merge_kernel(agp_ref, amp_ref, asp_ref, ags_ref, ams_ref, ass_ref, g0_ref, g1_ref, g2_ref, x_ref,
                  wg_ref, wm_ref, ws_ref, wo_ref, lg_ref, lb_ref, wr_ref, br_ref, ur_ref,
                  o_ref, or_ref, idx_ref, gate_ref, cnt_ref, *, alpha, np_tiles):
    is_p = pl.program_id(0) < np_tiles
    ag = jnp.where(is_p, agp_ref[...], ags_ref[...])
    am = jnp.where(is_p, amp_ref[...], ams_ref[...])
    asb = jnp.where(is_p, asp_ref[...], ass_ref[...])
    merged = (_sigmoid(g0_ref[...]) * _dot(ag, wg_ref[...])
              + _sigmoid(g1_ref[...]) * _dot(am, wm_ref[...])
              + _sigmoid(g2_ref[...]) * _dot(asb, ws_ref[...]))
    out = _dot(merged.astype(BF16), wo_ref[...])
    y = _ln_math(alpha * x_ref[...] + out, lg_ref[...], lb_ref[...])
    o_ref[...] = y
    or_ref[...] = y.reshape(or_ref.shape)
    _router_kernel(o_ref, wr_ref, br_ref, ur_ref, idx_ref, gate_ref, cnt_ref)


def merge_ln1(acts_p, acts_s, z_main, x, wg, wm, ws, wo, lg, lb, w_r, b_r, umat, alpha):
    n, d = x.shape
    n_p = acts_p[0].shape[0]
    n_s = acts_s[0].shape[0]
    tm = umat.shape[0]
    assert n_p + n_s == n and n_p % tm == 0 and n_s % tm == 0
    np_tiles = n_p // tm
    row = lambda i: (i, 0)
    row_p = lambda i: (jnp.minimum(i, np_tiles - 1), 0)
    row_s = lambda i: (jnp.maximum(i - np_tiles, 0), 0)
    const = lambda i: (0, 0)
    wspec = pl.BlockSpec((d, d), const)
    gcb = C_GATE // d
    return pl.pallas_call(
        functools.partial(_merge_kernel, alpha=alpha, np_tiles=np_tiles),
        grid=(n // tm,),
        in_specs=[pl.BlockSpec((tm, d), row_p), pl.BlockSpec((tm, d), row_p), pl.BlockSpec((tm, d), row_p),
                  pl.BlockSpec((tm, d), row_s), pl.BlockSpec((tm, d), row_s), pl.BlockSpec((tm, d), row_s),
                  pl.BlockSpec((tm, d), lambda i: (i, gcb)),
                  pl.BlockSpec((tm, d), lambda i: (i, gcb + 1)),
                  pl.BlockSpec((tm, d), lambda i: (i, gcb + 2)),
                  pl.BlockSpec((tm, d), row),
                  wspec, wspec, wspec, wspec,
                  pl.BlockSpec((1, d), const), pl.BlockSpec((1, d), const),
                  pl.BlockSpec((R_ROWS, d), const), pl.BlockSpec((R_ROWS, 1), const),
                  pl.BlockSpec((tm, tm), const)],
        out_specs=[pl.BlockSpec((tm, d), row), pl.BlockSpec((tm, ROW_SUB, d // ROW_SUB), lambda i: (i, 0, 0)),
                   pl.BlockSpec((8, tm), lambda i: (0, i)), pl.BlockSpec((8, tm), lambda i: (0, i)),
                   pl.BlockSpec((N_EXPERTS, 128), const)],
        out_shape=[jax.ShapeDtypeStruct((n, d), F32), jax.ShapeDtypeStruct((n, ROW_SUB, d // ROW_SUB), F32),
                   jax.ShapeDtypeStruct((8, n), jnp.int32), jax.ShapeDtypeStruct((8, n), F32),
                   jax.ShapeDtypeStruct((N_EXPERTS, 128), F32)],
        compiler_params=_cparams(("arbitrary",)),
        name="merge_ln1",
    )(*acts_p, *acts_s, z_main, z_main, z_main, x, wg, wm, ws, wo, lg.reshape(1, d), lb.reshape(1, d),
      w_r, b_r, umat)


R_ROWS = 8 + N_EXPERTS


def _router_kernel(x_ref, w_ref, b_ref, u_ref, idx_ref, gate_ref, cnt_ref):
    i = pl.program_id(0)
    tm = x_ref.shape[0]

    @pl.when(i == 0)
    def _():
        cnt_ref[...] = jnp.zeros_like(cnt_ref)

    logits = _dot_nt(w_ref[...].astype(BF16), x_ref[...].astype(BF16)) + b_ref[...]

    g = [logits[j:j + 1, :] for j in range(N_GROUPS)]
    gmax = jnp.maximum(jnp.maximum(g[0], g[1]), jnp.maximum(g[2], g[3]))
    gsel = jnp.where(g[0] == gmax, 0, jnp.where(g[1] == gmax, 1, jnp.where(g[2] == gmax, 2, 3)))
    gden = (jnp.exp(g[0] - gmax) + jnp.exp(g[1] - gmax)) + (jnp.exp(g[2] - gmax) + jnp.exp(g[3] - gmax))
    gprob = 1.0 / gden
    e_in = jnp.where(gsel == 0, logits[8:16, :],
                     jnp.where(gsel == 1, logits[16:24, :],
                               jnp.where(gsel == 2, logits[24:32, :], logits[32:40, :])))
    ridx = lax.broadcasted_iota(jnp.int32, (EXPERTS_PER_GROUP, tm), 0)
    v1 = jnp.max(e_in, axis=0, keepdims=True)
    i1 = jnp.min(jnp.where(e_in == v1, ridx, EXPERTS_PER_GROUP), axis=0, keepdims=True)
    e2 = jnp.where(ridx == i1, -jnp.inf, e_in)
    v2 = jnp.max(e2, axis=0, keepdims=True)
    i2 = jnp.min(jnp.where(e2 == v2, ridx, EXPERTS_PER_GROUP), axis=0, keepdims=True)
    t = jnp.exp(v2 - v1)
    p1 = 1.0 / (1.0 + t)
    gate_ref[0:1, :] = p1 * gprob
    gate_ref[1:2, :] = (t * p1) * gprob
    gate_ref[2:8, :] = jnp.zeros((6, tm), F32)
    eid1 = gsel * EXPERTS_PER_GROUP + i1
    eid2 = gsel * EXPERTS_PER_GROUP + i2

    eidx = lax.broadcasted_iota(jnp.int32, (N_EXPERTS, tm), 0)
    oh1 = eidx == eid1
    oh2 = eidx == eid2
    oh1f = jnp.where(oh1, 1.0, 0.0)
    oh2f = jnp.where(oh2, 1.0, 0.0)
    umat = u_ref[...]
    pre1 = _dot(oh1f.astype(BF16), umat)
    pre2 = _dot(oh2f.astype(BF16), umat)
    cnt1 = jnp.sum(oh1f, axis=1, keepdims=True)
    cnt2 = jnp.sum(oh2f, axis=1, keepdims=True)
    base = cnt_ref[:, 0:1]
    rank1 = jnp.sum(jnp.where(oh1, base + pre1, 0.0), axis=0, keepdims=True)
    rank2 = jnp.sum(jnp.where(oh2, base + cnt1 + pre2, 0.0), axis=0, keepdims=True)
    idx_ref[0:1, :] = eid1
    idx_ref[1:2, :] = eid2
    idx_ref[2:3, :] = rank1.astype(jnp.int32)
    idx_ref[3:4, :] = rank2.astype(jnp.int32)
    idx_ref[4:8, :] = jnp.zeros((4, tm), jnp.int32)
    cnt_ref[...] = cnt_ref[...] + (cnt1 + cnt2)


def _row_copy(src_hbm, dst_vmem, src_row, dst_row, sem):
    return pltpu.make_async_copy(src_hbm.at[pl.ds(src_row, 1)], dst_vmem.at[pl.ds(dst_row, 1)], sem)


GATHER_UNROLL = 8


def _start_row_gather(idx_ref, base, src_hbm, dst_vmem, nrows, sem):
    def start(r, carry):
        _row_copy(src_hbm, dst_vmem, idx_ref[base + r], r, sem).start()
        return carry

    lax.fori_loop(0, nrows, start, 0, unroll=GATHER_UNROLL)


def _wait_row_gather(src_hbm, dst_vmem, nrows, sem):
    def wait(r, carry):
        _row_copy(src_hbm, dst_vmem, 0, r, sem).wait()
        return carry

    lax.fori_loop(0, nrows, wait, 0, unroll=GATHER_UNROLL)


def _moe_kernel(be_ref, tok_ref, nblk_ref, x_hbm, wg_ref, wu_ref, wd_ref, o_ref,
                xbuf, wgb, wub, wdb, sem):
    i = pl.program_id(0)
    nblk = nblk_ref[0]
    slot = i % 2

    @pl.when((i == 0) & (nblk > 0))
    def _():
        _start_row_gather(tok_ref, 0, x_hbm, xbuf.at[0], MOE_BLK, sem.at[0])

    @pl.when(i + 1 < nblk)
    def _():
        _start_row_gather(tok_ref, (i + 1) * MOE_BLK, x_hbm, xbuf.at[1 - slot], MOE_BLK, sem.at[1 - slot])

    prev_e = be_ref[jnp.maximum(i - 1, 0)]

    @pl.when((i == 0) | (be_ref[i] != prev_e))
    def _():
        wgb[...] = wg_ref[0, 0].astype(BF16)
        wub[...] = wu_ref[0, 0].astype(BF16)
        wdb[...] = wd_ref[0, 0].astype(BF16)

    @pl.when(i < nblk)
    def _():
        _wait_row_gather(x_hbm, xbuf.at[slot], MOE_BLK, sem.at[slot])
        xb = xbuf[slot].reshape(MOE_BLK, D_MODEL).astype(BF16)
        a = _dot(xb, wgb[...])
        hmid = (a * _sigmoid(a)) * _dot(xb, wub[...])
        o_ref[...] = _dot(hmid.astype(BF16), wdb[...]).reshape(o_ref.shape)

    @pl.when(i >= nblk)
    def _():
        o_ref[...] = jnp.zeros_like(o_ref)


def moe_experts(x, block_e, tok_pad, nblk, w_gate, w_up, w_down, *, layer):
    n = x.shape[0]
    d = D_MODEL
    rs = (ROW_SUB, d // ROW_SUB)
    nb = block_e.shape[0]
    wmap = lambda i, be, tok, nblk: (layer, be[i], 0, 0)
    grid_spec = pltpu.PrefetchScalarGridSpec(
        num_scalar_prefetch=3,
        grid=(nb,),
        in_specs=[pl.BlockSpec(memory_space=pl.ANY),
                  pl.BlockSpec((1, 1, d, D_EXPERT), wmap),
                  pl.BlockSpec((1, 1, d, D_EXPERT), wmap),
                  pl.BlockSpec((1, 1, D_EXPERT, d), wmap)],
        out_specs=pl.BlockSpec((MOE_BLK,) + rs, lambda i, be, tok, nblk: (i, 0, 0)),
        scratch_shapes=[pltpu.VMEM((2, MOE_BLK) + rs, F32),
                        pltpu.VMEM((d, D_EXPERT), BF16),
                        pltpu.VMEM((d, D_EXPERT), BF16),
                        pltpu.VMEM((D_EXPERT, d), BF16),
                        pltpu.SemaphoreType.DMA((2,))],
    )
    return pl.pallas_call(
        _moe_kernel,
        grid_spec=grid_spec,
        out_shape=jax.ShapeDtypeStruct((nb * MOE_BLK,) + rs, F32),
        compiler_params=_cparams(("arbitrary",)),
        name="moe_experts",
    )(block_e, tok_pad, nblk, x, w_gate, w_up, w_down)


def _combine_kernel(dest_ref, yb_hbm, x_ref, gate_ref, lg_ref, lb_ref, o_ref, ob_ref, ybuf, sem, *, alpha, n):
    i = pl.program_id(0)
    nt = pl.num_programs(0)
    tm = x_ref.shape[0]
    slot = i % 2

    def start(tile, s):
        for k in range(TOP_K):
            _start_row_gather(dest_ref, k * n + tile * tm, yb_hbm, ybuf.at[s, k], tm, sem.at[s, k])

    @pl.when(i == 0)
    def _():
        start(0, 0)

    @pl.when(i + 1 < nt)
    def _():
        start(i + 1, 1 - slot)

    for k in range(TOP_K):
        _wait_row_gather(yb_hbm, ybuf.at[slot, k], tm, sem.at[slot, k])
    gate = gate_ref[...]
    d = x_ref.shape[1]
    y = ybuf[slot, 0].reshape(tm, d) * gate[:, 0:1] + ybuf[slot, 1].reshape(tm, d) * gate[:, 1:2]
    out = _ln_math(alpha * x_ref[...] + y, lg_ref[...], lb_ref[...])
    o_ref[...] = out
    ob_ref[...] = out.astype(BF16)


def moe_combine_ln2(x, yb, dest, gate_cols, lg, lb, alpha):
    n, d = x.shape
    tm = _pick(n, (256, 128, 64))
    grid_spec = pltpu.PrefetchScalarGridSpec(
        num_scalar_prefetch=1,
        grid=(n // tm,),
        in_specs=[pl.BlockSpec(memory_space=pl.ANY),
                  pl.BlockSpec((tm, d), lambda i, dest: (i, 0)),
                  pl.BlockSpec((tm, TOP_K), lambda i, dest: (i, 0)),
                  pl.BlockSpec((1, d), lambda i, dest: (0, 0)),
                  pl.BlockSpec((1, d), lambda i, dest: (0, 0))],
        out_specs=[pl.BlockSpec((tm, d), lambda i, dest: (i, 0)), pl.BlockSpec((tm, d), lambda i, dest: (i, 0))],
        scratch_shapes=[pltpu.VMEM((2, TOP_K, tm, ROW_SUB, d // ROW_SUB), F32),
                        pltpu.SemaphoreType.DMA((2, TOP_K))],
    )
    return pl.pallas_call(
        functools.partial(_combine_kernel, alpha=alpha, n=n),
        grid_spec=grid_spec,
        out_shape=[jax.ShapeDtypeStruct((n, d), F32), jax.ShapeDtypeStruct((n, d), BF16)],
        compiler_params=_cparams(("arbitrary",)),
        name="moe_combine_ln2",
    )(dest, yb, x, gate_cols, lg.reshape(1, d), lb.reshape(1, d))


def _tri_lower(n):
    t = np.arange(n)
    return (t[None, :] <= t[:, None]).astype(np.float32)


def _strict_upper(n):
    t = np.arange(n)
    return (t[:, None] < t[None, :]).astype(np.float32)


def kernel(x_prompt, x_sample, state_gla, state_ml_C, state_ml_n, state_ml_m, state_ml_conv, cache_sb_k, cache_sb_v, meta_tokens, ln_in_g, ln_in_b, w_in, gla_w_a2, gla_b_a, gla_norm_g, ml_conv_w, ml_conv_b, ml_wq, ml_wk, ml_wv, ml_b_i, ml_b_f, ml_norm_g, w_br_gla, w_br_ml, w_br_sb, w_out, ln1_g, ln1_b, w_rg, b_rg, w_re, b_re, w_e_gate, w_e_up, w_e_down, ln2_g, ln2_b):
    D = D_MODEL
    B, SEQ, _ = x_prompt.shape
    DB, DS, _ = x_sample.shape
    depth = w_in.shape[0]
    n_cache = cache_sb_k.shape[2]
    assert DS == CHUNK and SEQ % 128 == 0 and x_prompt.shape[2] == D
    alpha = (2.0 * depth) ** 0.25
    TP = PAD_FRONT + N_META + SEQ
    NP = B * TP
    NS = DB * DS
    N = NP + NS
    assert TP % CHUNK_PROMPT == 0 and NP % CHUNK == 0
    nchunk_p = TP // CHUNK_PROMPT

    pieces = []
    for b in range(B):
        pieces += [jnp.zeros((PAD_FRONT, D), F32), meta_tokens.astype(F32), x_prompt[b].astype(F32)]
    x = jnp.concatenate(pieces + [x_sample.reshape(NS, D).astype(F32)], axis=0)
    x, xb = layer_norm_rows(x, ln_in_g, ln_in_b)

    u_sb = jnp.asarray(_strict_upper(SB_BK).T, BF16)
    tm_r = _pick(N, (256, 128, 64))
    u_r = jnp.asarray(_strict_upper(tm_r), BF16)

    A = N * TOP_K
    NB = -(-(A + N_EXPERTS * (MOE_BLK - 1)) // MOE_BLK)

    outs_p = [[] for _ in range(5)]
    outs_s = [[] for _ in range(7)]
    z_kvs = []
    for l in range(depth):
        wi = w_in[l]
        w_main = jnp.concatenate([wi[:, 0:3072], wi[:, 3088:4112], wi[:, 4120:5144], wi[:, 5144:6168],
                                  wi[:, 8216:]], axis=1).astype(BF16)
        w_kv = jnp.concatenate([wi[:, 6168:8216], wi[:, 3072:3088], wi[:, 4112:4120],
                                jnp.zeros((D, 128 - 24), F32)], axis=1).astype(BF16)
        z_main = in_projection(xb, w_main, 1024, "in_proj_main")
        z_kv, kvb = in_projection(xb, w_kv, w_kv.shape[1], "in_proj_kv", with_bf16=True)
        z_small = z_kv
        z_kvs.append(z_kv)

        wa = jnp.concatenate([gla_w_a2[l], jnp.zeros((128 - GLA_RANK, GLA_HEADS * GLA_DK), F32)], axis=0).astype(BF16)
        ba = gla_b_a[l].reshape(1, -1)
        ng = gla_norm_g[l].reshape(1, -1)
        s0_p = jnp.zeros((B, GLA_HEADS, GLA_DV, GLA_DK), F32)
        s0_s = jnp.swapaxes(state_gla[l].astype(F32), -1, -2)
        geo_p = dict(row0=0, nb=B, nchunk=nchunk_p, chunk=CHUNK_PROMPT)
        geo_s = dict(row0=NP, nb=DB, nchunk=1, chunk=CHUNK)
        gla_p = _gla_call_parts(z_main, z_small, wa, ba, ng, s0_p, **geo_p)
        gla_s = _gla_call_parts(z_main, z_small, wa, ba, ng, s0_s, **geo_s)

        gbias = jnp.zeros((1, 128), F32).at[0, L_MI:L_MI + ML_HEADS].set(ml_b_i[l]).at[0, L_MF:L_MF + ML_HEADS].set(ml_b_f[l])
        cw = ml_conv_w[l]
        cb = ml_conv_b[l].reshape(1, -1)
        wqk = jnp.concatenate([ml_wq[l], ml_wk[l]], axis=-1).astype(BF16)
        wv = ml_wv[l].astype(BF16)
        mng = ml_norm_g[l].reshape(1, -1)
        W = ML_HEADS * ML_DH
        conv0_p = jnp.zeros((B, 8, W), F32)
        conv0_s = jnp.concatenate([jnp.zeros((DB, 8 - (ML_CONV - 1), W), F32), state_ml_conv[l].astype(F32)], axis=1)
        m0_s = jnp.zeros((DB, 1, 128), F32).at[:, 0, L_MF:L_MF + ML_HEADS].set(state_ml_m[l].astype(F32))
        ml_p = _mlstm_call_parts(
            z_main, z_small, gbias, cw, cb, wqk, wv, mng, conv0_p,
            jnp.zeros((1, B, ML_HEADS, ML_DH, ML_DH), F32), jnp.zeros((B, ML_HEADS, ML_DH), F32),
            jnp.zeros((B, 1, 128), F32), c0_layer=0, **geo_p)
        ml_s = _mlstm_call_parts(
            z_main, z_small, gbias, cw, cb, wqk, wv, mng, conv0_s,
            state_ml_C.astype(F32), state_ml_n[l].astype(F32), m0_s, c0_layer=l, **geo_s)
        a_gla_p, sT_p, a_ml_p, c_p, n_p, m_p = recurrent_branches(
            gla_p, ml_p, nb=B, nchunk=nchunk_p, chunk=CHUNK_PROMPT, n_pad=PAD_FRONT)
        a_gla_s, sT_s, a_ml_s, c_s, n_s, m_s = recurrent_branches(
            gla_s, ml_s, nb=DB, nchunk=1, chunk=CHUNK, n_pad=0)

        a_sb_p = sb_attention_prompt(z_main, kvb, u_sb, nb=B, tp=TP, bq=SB_BK, n_pad=PAD_FRONT)
        a_sb_s = sb_attention_sample(z_main, z_kv, cache_sb_k.astype(F32), cache_sb_v.astype(F32),
                                     u_sb, layer=l, row0=NP, nb=DB)

        w_r = jnp.concatenate([w_rg[l].T, jnp.zeros((8 - N_GROUPS, D), F32), w_re[l].T], axis=0)
        b_r = jnp.concatenate([b_rg[l], jnp.zeros((8 - N_GROUPS,), F32), b_re[l]]).reshape(R_ROWS, 1)
        x1, x1r, idx, gates, cnt = merge_ln1(
            (a_gla_p, a_ml_p, a_sb_p), (a_gla_s, a_ml_s, a_sb_s), z_main, x,
            w_br_gla[l].astype(BF16), w_br_ml[l].astype(BF16), w_br_sb[l].astype(BF16),
            w_out[l].astype(BF16), ln1_g[l], ln1_b[l], w_r, b_r, u_r, alpha)
        counts = cnt[:, 0].astype(jnp.int32)
        padded = (counts + MOE_BLK - 1) // MOE_BLK * MOE_BLK
        ends_pad = jnp.cumsum(padded)
        start_pad = ends_pad - padded
        eids = jnp.arange(N_EXPERTS, dtype=jnp.int32)
        start_of = jnp.sum(jnp.where(idx[0:2, :, None] == eids, start_pad, 0), axis=-1)
        dest = (start_of + idx[2:4]).reshape(-1)
        tok = jnp.tile(jnp.arange(N, dtype=jnp.int32), TOP_K)
        tok_pad = jnp.zeros((NB * MOE_BLK,), jnp.int32).at[dest].set(tok)
        blk_lo = jnp.arange(NB, dtype=jnp.int32) * MOE_BLK
        block_e = jnp.minimum(jnp.sum((ends_pad[None, :] <= blk_lo[:, None]).astype(jnp.int32), axis=-1),
                              N_EXPERTS - 1)
        nblk = (ends_pad[-1:] // MOE_BLK).astype(jnp.int32)
        yb = moe_experts(x1r, block_e, tok_pad, nblk, w_e_gate, w_e_up, w_e_down, layer=l)
        x, xb = moe_combine_ln2(x1, yb, dest, gates[0:2].T, ln2_g[l], ln2_b[l], alpha)

        nct = ML_CONV - 1
        conv_p = jnp.stack([z_main[(b + 1) * TP - nct:(b + 1) * TP, C_MU:C_MU + W] for b in range(B)], axis=0)
        conv_s = z_main[NP:, C_MU:C_MU + W].reshape(DB, DS, W)[:, DS - nct:]
        zkv_s = z_kv[NP:, :2 * D].reshape(DB, DS, 2 * D)
        kv_shape = (SB_HEADS, SB_DH)
        outs_p[0].append(jnp.swapaxes(sT_p, -1, -2))
        outs_s[0].append(jnp.swapaxes(sT_s, -1, -2))
        outs_p[1].append(c_p); outs_s[1].append(c_s)
        outs_p[2].append(n_p); outs_s[2].append(n_s)
        outs_p[3].append(m_p[:, 0, L_MF:L_MF + ML_HEADS]); outs_s[3].append(m_s[:, 0, L_MF:L_MF + ML_HEADS])
        outs_p[4].append(conv_p); outs_s[4].append(conv_s)
        outs_s[5].append(zkv_s[:, :, C_SK:C_SK + D].reshape(DB, DS, *kv_shape))
        outs_s[6].append(zkv_s[:, :, C_SV:C_SV + D].reshape(DB, DS, *kv_shape))

    sbk_p, sbv_p = kv_state_outputs(z_kvs, nb=B, tp=TP, n_front=PAD_FRONT)
    sp = [jnp.stack(a, axis=0) for a in outs_p]
    ss = [jnp.stack(a, axis=0) for a in outs_s]
    y_prompt = jnp.stack([x[b * TP + PAD_FRONT + N_META:(b + 1) * TP] for b in range(B)], axis=0)
    y_sample = x[NP:].reshape(DB, DS, D)
    return (y_prompt, y_sample, sp[0], ss[0], sp[1], ss[1], sp[2], ss[2], sp[3], ss[3],
            sp[4], ss[4], sbk_p, ss[5], sbv_p, ss[6])
```

```python
import functools

import numpy as np
import jax
import jax.numpy as jnp
from jax import lax
from jax.experimental import pallas as pl
from jax.experimental.pallas import tpu as pltpu

F32 = jnp.float32
BF16 = jnp.bfloat16

D_MODEL = 1024
N_META = 16
CHUNK = 64
CHUNK_PROMPT = 256
PAD_FRONT = 240
GLA_HEADS, GLA_DK, GLA_DV, GLA_RANK, GLA_TAU = 4, 128, 256, 16, 16.0
ML_HEADS, ML_DH, ML_CONV = 4, 256, 4
SB_HEADS, SB_DH = 8, 128
N_GROUPS, EXPERTS_PER_GROUP, TOP_K = 4, 8, 2
N_EXPERTS = N_GROUPS * EXPERTS_PER_GROUP
D_EXPERT = D_MODEL // 2
MOE_BLK = 256
ROW_SUB = 8
LN_EPS = 1e-5
NEG_BIG = -1e30
SB_SKIP = 110.0
SB_BK = 256

C_GQ, C_GK, C_GV, C_GR = 0, 512, 1024, 2048
C_MU, C_MO = 3072, 4096
C_SQ = 5120
C_GATE = 6144
C_SK, C_SV, C_SMALL = 0, 1024, 2048
SB_HG = 4
L_MI, L_MF = 16, 20

VMEM_LIMIT = 48 * 1024 * 1024


def _cparams(sem):
    return pltpu.CompilerParams(dimension_semantics=sem, vmem_limit_bytes=VMEM_LIMIT)


def _pick(n, cands):
    for c in cands:
        if n % c == 0:
            return c
    raise ValueError(f"no tile in {cands} divides {n}")


def _log_sigmoid(x):
    return jnp.minimum(x, 0.0) - jnp.log1p(jnp.exp(-jnp.abs(x)))


def _sigmoid(x):
    return 1.0 / (1.0 + jnp.exp(-x))


def _split2(x):
    hi = x.astype(BF16)
    lo = (x - hi.astype(F32)).astype(BF16)
    return hi, lo


def _dot(a, b):
    return jnp.dot(a, b, preferred_element_type=F32)


def _dot_nt(a, b):
    return lax.dot_general(a, b, (((1,), (1,)), ((), ())), preferred_element_type=F32)


def _dot_tn(a, b):
    return lax.dot_general(a, b, (((0,), (0,)), ((), ())), preferred_element_type=F32)


def _ln_math(x, g, b):
    mu = jnp.mean(x, axis=-1, keepdims=True)
    xc = x - mu
    var = jnp.mean(xc * xc, axis=-1, keepdims=True)
    return xc * lax.rsqrt(var + LN_EPS) * g + b


def _ln_kernel(x_ref, g_ref, b_ref, o_ref, ob_ref):
    y = _ln_math(x_ref[...], g_ref[...], b_ref[...])
    o_ref[...] = y
    ob_ref[...] = y.astype(BF16)


def layer_norm_rows(x, g, b):
    n, d = x.shape
    tm = _pick(n, (512, 256, 128, 64))
    return pl.pallas_call(
        _ln_kernel,
        grid=(n // tm,),
        in_specs=[pl.BlockSpec((tm, d), lambda i: (i, 0)),
                  pl.BlockSpec((1, d), lambda i: (0, 0)),
                  pl.BlockSpec((1, d), lambda i: (0, 0))],
        out_specs=[pl.BlockSpec((tm, d), lambda i: (i, 0)), pl.BlockSpec((tm, d), lambda i: (i, 0))],
        out_shape=[jax.ShapeDtypeStruct((n, d), F32), jax.ShapeDtypeStruct((n, d), BF16)],
        compiler_params=_cparams(("arbitrary",)),
        name="ln_in",
    )(x, g.reshape(1, d), b.reshape(1, d))


def _mm_kernel(x_ref, w_ref, o_ref):
    o_ref[...] = _dot(x_ref[...], w_ref[...])


def _mm2_kernel(x_ref, w_ref, o_ref, ob_ref):
    z = _dot(x_ref[...], w_ref[...])
    o_ref[...] = z
    ob_ref[...] = z.astype(BF16)


def in_projection(x, w_bf16, tn, name, with_bf16=False):
    n, d = x.shape
    cols = w_bf16.shape[1]
    tm = _pick(n, (896, 768, 512, 384, 256, 128, 64))
    out_spec = pl.BlockSpec((tm, tn), lambda j, i: (i, j))
    return pl.pallas_call(
        _mm2_kernel if with_bf16 else _mm_kernel,
        grid=(cols // tn, n // tm),
        in_specs=[pl.BlockSpec((tm, d), lambda j, i: (i, 0)),
                  pl.BlockSpec((d, tn), lambda j, i: (0, j))],
        out_specs=[out_spec, out_spec] if with_bf16 else out_spec,
        out_shape=([jax.ShapeDtypeStruct((n, cols), F32), jax.ShapeDtypeStruct((n, cols), BF16)]
                   if with_bf16 else jax.ShapeDtypeStruct((n, cols), F32)),
        compiler_params=_cparams(("arbitrary", "arbitrary")),
        name=name,
    )(x, w_bf16)


def _gla_levels(L):
    return tuple(L >> i for i in range(1, L.bit_length()))


def _gla_level_matrix(L):
    t = np.arange(L)[:, None]
    j = np.arange(L)[None, :]
    blocks = []
    for m in (L,) + _gla_levels(L):
        blocks.append(((j <= t) & (j // m == t // m)).astype(np.float32))
    for m in (L,) + _gla_levels(L):
        blocks.append(((j > t) & (j // m == t // m)).astype(np.float32))
    return np.concatenate(blocks, axis=0)


def _gla_kernel(q_ref, k_ref, v_ref, r_ref, lr_ref, wa_ref, ba_ref, ng_ref, lvl_ref, s0_ref,
                o_ref, s_ref, *, n_pad, L, phase):
    c = pl.program_id(1)
    GLA_LEVELS = _gla_levels(L)

    if phase == "init":
        @pl.when(c == 0)
        def _():
            s_ref[...] = s0_ref[...]
        return

    row = c * L + lax.broadcasted_iota(jnp.int32, (L, 1), 0)
    valid = row >= n_pad
    x = _dot(lr_ref[...].astype(BF16), wa_ref[...]) + ba_ref[...]
    la = jnp.where(valid, _log_sigmoid(x) * (1.0 / GLA_TAU), 0.0)
    la_hi, la_lo = _split2(la)
    lvl = lvl_ref[...]
    ex = jnp.exp(_dot(lvl, la_hi) + _dot(lvl, la_lo))
    nlev = len(GLA_LEVELS) + 1

    ti = lax.broadcasted_iota(jnp.int32, (L, L), 0)
    si = lax.broadcasted_iota(jnp.int32, (L, L), 1)
    level_masks = []
    for m in GLA_LEVELS:
        sh = m.bit_length() - 1
        tb = lax.shift_right_logical(ti, sh)
        sb = lax.shift_right_logical(si, sh)
        level_masks.append((lax.shift_right_logical(tb, 1) == lax.shift_right_logical(sb, 1))
                           & ((tb & 1) == 1) & ((sb & 1) == 0))
    eye = ti == si

    outs = []
    states = []
    for h in range(GLA_HEADS):
        dk = slice(h * GLA_DK, (h + 1) * GLA_DK)
        dv = slice(h * GLA_DV, (h + 1) * GLA_DV)
        q = q_ref[:, dk] * (GLA_DK ** -0.5)
        k = jnp.where(valid, k_ref[:, dk], 0.0)
        v = jnp.where(valid, v_ref[:, dv], 0.0).astype(BF16)
        att = jnp.where(eye, _dot_nt(q.astype(BF16), k.astype(BF16)), 0.0)
        for lev in range(len(GLA_LEVELS)):
            qe = ex[(1 + lev) * L:(2 + lev) * L, dk]
            ke = ex[(nlev + 1 + lev) * L:(nlev + 2 + lev) * L, dk]
            a = _dot_nt((q * qe).astype(BF16), (k * ke).astype(BF16))
            att = jnp.where(level_masks[lev], a, att)
        eb = ex[0:L, dk]
        st = s_ref[0, h]
        o = _dot(att.astype(BF16), v) + _dot_nt((q * eb).astype(BF16), st.astype(BF16))
        ke_end = ex[nlev * L:(nlev + 1) * L, dk]
        states.append(st * eb[L - 1:L, :] + _dot_tn(v, (k * ke_end).astype(BF16)))
        mu = jnp.mean(o, axis=-1, keepdims=True)
        oc = o - mu
        var = jnp.mean(oc * oc, axis=-1, keepdims=True)
        on = oc * lax.rsqrt(var + LN_EPS) * ng_ref[...]
        r = r_ref[:, dv]
        outs.append((on * (r * _sigmoid(r))).astype(BF16))
    o_ref[...] = jnp.concatenate(outs, axis=-1)
    s_ref[0] = jnp.stack(states, axis=0)


def _gla_call_parts(z_main, z_small, wa, ba, ng, s0t, *, row0, nb, nchunk, chunk):
    L = chunk
    assert row0 % L == 0
    lvl = jnp.asarray(_gla_level_matrix(L), BF16)
    rb0 = row0 // L
    hk = GLA_HEADS * GLA_DK
    hv = GLA_HEADS * GLA_DV

    def rows(colblk):
        return lambda b, c: (rb0 + b * nchunk + c, colblk)

    return dict(
        args=(z_main, z_main, z_main, z_main, z_small, wa, ba, ng, lvl, s0t),
        in_specs=[pl.BlockSpec((L, hk), rows(C_GQ // hk)),
                  pl.BlockSpec((L, hk), rows(C_GK // hk)),
                  pl.BlockSpec((L, hv), rows(C_GV // hv)),
                  pl.BlockSpec((L, hv), rows(C_GR // hv)),
                  pl.BlockSpec((L, 128), rows(C_SMALL // 128)),
                  pl.BlockSpec((128, hk), lambda b, c: (0, 0)),
                  pl.BlockSpec((1, hk), lambda b, c: (0, 0)),
                  pl.BlockSpec((1, GLA_DV), lambda b, c: (0, 0)),
                  pl.BlockSpec(lvl.shape, lambda b, c: (0, 0)),
                  pl.BlockSpec((1, GLA_HEADS, GLA_DV, GLA_DK), lambda b, c: (b, 0, 0, 0))],
        out_specs=[pl.BlockSpec((L, hv), lambda b, c: (b * nchunk + c, 0)),
                   pl.BlockSpec((1, GLA_HEADS, GLA_DV, GLA_DK), lambda b, c: (b, 0, 0, 0))],
        out_shape=[jax.ShapeDtypeStruct((nb * nchunk * L, hv), BF16),
                   jax.ShapeDtypeStruct((nb, GLA_HEADS, GLA_DV, GLA_DK), F32)],
        scratch=[])


def _mlstm_kernel(u_ref, mo_ref, g_ref, gb_ref, cw_ref, cb_ref, wqk_ref, wv_ref, ng_ref, tri_ref,
                  conv0_ref, c0_ref, n0_ref, m0_ref,
                  o_ref, c_ref, n_ref, m_ref, ext_ref, *, n_pad, L, phase):
    c = pl.program_id(1)
    W = ML_HEADS * ML_DH

    if phase == "init":
        @pl.when(c == 0)
        def _():
            ext_ref[0:8, :] = conv0_ref[0]
            c_ref[...] = c0_ref[0]
            n_ref[...] = n0_ref[...]
            m_ref[...] = m0_ref[...]
        return

    row = c * L + lax.broadcasted_iota(jnp.int32, (L, 1), 0)
    valid = row >= n_pad

    u = jnp.where(valid, u_ref[...], 0.0)
    ext_ref[8:8 + L, :] = u
    uc = cb_ref[...]
    for i in range(ML_CONV):
        uc = uc + ext_ref[8 - (ML_CONV - 1) + i:8 - (ML_CONV - 1) + i + L, :] * cw_ref[i:i + 1, :]
    ext_ref[0:8, :] = u[L - 8:L, :]
    ucs = (uc * _sigmoid(uc)).astype(BF16)
    ub = u_ref[...].astype(BF16)

    g = g_ref[...] + gb_ref[...]
    li = jnp.where(valid, g, NEG_BIG)
    lf = jnp.where(valid, _log_sigmoid(g), 0.0)
    tri = tri_ref[...]
    lf_hi, lf_lo = _split2(lf)
    b_col = _dot(tri, lf_hi) + _dot(tri, lf_lo)
    a_col = b_col + m_ref[0]
    li_row = li.T
    b_row = b_col.T

    ti = lax.broadcasted_iota(jnp.int32, (L, L), 0)
    si = lax.broadcasted_iota(jnp.int32, (L, L), 1)
    causal = si <= ti
    lane = lax.broadcasted_iota(jnp.int32, (1, 128), 1)
    m_vec = m_ref[0]
    m_out = m_vec
    outs, c_new, n_new = [], [], []

    for h in range(ML_HEADS):
        d = slice(h * ML_DH, (h + 1) * ML_DH)
        qk_proj = _dot(ucs[:, d], wqk_ref[h])
        q = qk_proj[:, 0:ML_DH].astype(BF16)
        k = qk_proj[:, ML_DH:2 * ML_DH] * (ML_DH ** -0.5)
        v = _dot(ub[:, d], wv_ref[h]).astype(BF16)
        lf_l = L_MF + h
        li_l = L_MI + h
        bc = b_col[:, lf_l:lf_l + 1]
        ac = a_col[:, lf_l:lf_l + 1]
        lic = li[:, li_l:li_l + 1]
        dmat = jnp.where(causal, bc - b_row[lf_l:lf_l + 1, :] + li_row[li_l:li_l + 1, :], NEG_BIG)
        mt = jnp.maximum(ac, jnp.max(dmat, axis=-1, keepdims=True))
        w_intra = jnp.exp(dmat - mt)
        w_inter = jnp.exp(ac - mt)
        qk = _dot_nt(q, k.astype(BF16)) * w_intra
        cmat = c_ref[0, h]
        nrow = n_ref[0, h:h + 1, :]
        num = _dot(qk.astype(BF16), v) + w_inter * _dot(q, cmat.astype(BF16))
        qn = jnp.sum(q.astype(F32) * nrow.astype(BF16).astype(F32), axis=-1, keepdims=True)
        den = jnp.sum(qk, axis=-1, keepdims=True) + w_inter * qn
        hh = num / jnp.maximum(jnp.abs(den), jnp.exp(-mt))
        m_new = mt[L - 1:L, :]
        b_end = bc[L - 1:L, :]
        wk = jnp.exp(b_end - bc + lic - m_new)
        ws = jnp.exp(b_end + m_vec[:, lf_l:lf_l + 1] - m_new)
        kw = k * wk
        c_new.append(ws * cmat + _dot_tn(kw.astype(BF16), v))
        n_new.append(ws * nrow + jnp.sum(kw, axis=0, keepdims=True))
        m_out = jnp.where(lane == lf_l, m_new, m_out)
        mu = jnp.mean(hh, axis=-1, keepdims=True)
        hc = hh - mu
        var = jnp.mean(hc * hc, axis=-1, keepdims=True)
        hn = hc * lax.rsqrt(var + LN_EPS) * ng_ref[...]
        outs.append((hn * _sigmoid(mo_ref[:, d])).astype(BF16))
    o_ref[...] = jnp.concatenate(outs, axis=-1)
    c_ref[0] = jnp.stack(c_new, axis=0)
    n_ref[0] = jnp.concatenate(n_new, axis=0)
    m_ref[0] = m_out


def _mlstm_call_parts(z_main, z_small, gbias, cw, cb, wqk, wv, ng, conv0, c0, n0, m0,
                      *, c0_layer, row0, nb, nchunk, chunk):
    L = chunk
    assert row0 % L == 0
    tri = jnp.asarray(_tri_lower(L), BF16)
    rb0 = row0 // L
    W = ML_HEADS * ML_DH

    def rows(colblk):
        return lambda b, c: (rb0 + b * nchunk + c, colblk)

    const2 = lambda b, c: (0, 0)
    const3 = lambda b, c: (0, 0, 0)
    return dict(
        args=(z_main, z_main, z_small, gbias, cw, cb, wqk, wv, ng, tri, conv0, c0, n0, m0),
        in_specs=[pl.BlockSpec((L, W), rows(C_MU // W)),
                  pl.BlockSpec((L, W), rows(C_MO // W)),
                  pl.BlockSpec((L, 128), rows(C_SMALL // 128)),
                  pl.BlockSpec((1, 128), const2),
                  pl.BlockSpec((ML_CONV, W), const2),
                  pl.BlockSpec((1, W), const2),
                  pl.BlockSpec((ML_HEADS, ML_DH, 2 * ML_DH), const3),
                  pl.BlockSpec((ML_HEADS, ML_DH, ML_DH), const3),
                  pl.BlockSpec((1, ML_DH), const2),
                  pl.BlockSpec((L, L), const2),
                  pl.BlockSpec((1, 8, W), lambda b, c: (b, 0, 0)),
                  pl.BlockSpec((1, 1, ML_HEADS, ML_DH, ML_DH), lambda b, c: (c0_layer, b, 0, 0, 0)),
                  pl.BlockSpec((1, ML_HEADS, ML_DH), lambda b, c: (b, 0, 0)),
                  pl.BlockSpec((1, 1, 128), lambda b, c: (b, 0, 0))],
        out_specs=[pl.BlockSpec((L, W), lambda b, c: (b * nchunk + c, 0)),
                   pl.BlockSpec((1, ML_HEADS, ML_DH, ML_DH), lambda b, c: (b, 0, 0, 0)),
                   pl.BlockSpec((1, ML_HEADS, ML_DH), lambda b, c: (b, 0, 0)),
                   pl.BlockSpec((1, 1, 128), lambda b, c: (b, 0, 0))],
        out_shape=[jax.ShapeDtypeStruct((nb * nchunk * L, W), BF16),
                   jax.ShapeDtypeStruct((nb, ML_HEADS, ML_DH, ML_DH), F32),
                   jax.ShapeDtypeStruct((nb, ML_HEADS, ML_DH), F32),
                   jax.ShapeDtypeStruct((nb, 1, 128), F32)],
        scratch=[pltpu.VMEM((8 + L, W), F32)])


def _recurrent_kernel(*refs, n_gla_in, n_ml_in, n_pad, L):
    g_in = refs[:n_gla_in]
    m_in = refs[n_gla_in:n_gla_in + n_ml_in]
    g_out = refs[n_gla_in + n_ml_in:n_gla_in + n_ml_in + 2]
    m_rest = refs[n_gla_in + n_ml_in + 2:]
    for phase in ("init", "body"):
        _gla_kernel(*g_in, *g_out, n_pad=n_pad, L=L, phase=phase)
        _mlstm_kernel(*m_in, *m_rest, n_pad=n_pad, L=L, phase=phase)


def recurrent_branches(gla_parts, ml_parts, *, nb, nchunk, chunk, n_pad):
    return pl.pallas_call(
        functools.partial(_recurrent_kernel, n_gla_in=len(gla_parts["args"]), n_ml_in=len(ml_parts["args"]),
                          n_pad=n_pad, L=chunk),
        grid=(nb, nchunk),
        in_specs=gla_parts["in_specs"] + ml_parts["in_specs"],
        out_specs=gla_parts["out_specs"] + ml_parts["out_specs"],
        out_shape=gla_parts["out_shape"] + ml_parts["out_shape"],
        scratch_shapes=gla_parts["scratch"] + ml_parts["scratch"],
        compiler_params=_cparams(("arbitrary", "arbitrary")),
        name="gla_mlstm",
    )(*gla_parts["args"], *ml_parts["args"])


def _sb_heads(qs, ks, vs, vis, cs, umat):
    nh = len(qs)
    bq = qs[0].shape[0]
    lss, l1s, parts = [], [], []
    for h in range(nh):
        z = _dot_nt(qs[h], ks[h])
        t = jnp.log(1.0 + jnp.exp(-jnp.abs(z)))
        lss.append(jnp.minimum(z, 0.0) - t)
        l1 = jnp.minimum(-z, 0.0) - t
        if vis is not None:
            l1 = jnp.where(vis, l1, 0.0)
        l1s.append(l1)
        parts.extend(_split2(l1))
    suf = _dot(jnp.concatenate(parts, axis=0), umat)
    pvs, new_cs = [], []
    for h in range(nh):
        suffix = suf[2 * h * bq:(2 * h + 1) * bq] + suf[(2 * h + 1) * bq:(2 * h + 2) * bq]
        att = jnp.exp(lss[h] + suffix + cs[h])
        if vis is not None:
            att = jnp.where(vis, att, 0.0)
        pvs.append(_dot(att.astype(BF16), vs[h]))
        new_cs.append(cs[h] + jnp.sum(l1s[h], axis=-1, keepdims=True))
    return jnp.concatenate(pvs, axis=-1), new_cs, jnp.max(functools.reduce(jnp.maximum, new_cs))


def _sb_prompt_kernel(q_ref, k_ref, v_ref, u_ref, o_ref, acc_ref, *, bq, n_pad):
    qi = pl.program_id(2)
    bk = SB_BK
    assert bq == bk and n_pad < bk
    qpos = lax.broadcasted_iota(jnp.int32, (bq, 1), 0)
    kio = lax.broadcasted_iota(jnp.int32, (1, bk), 1)
    umat = u_ref[...]
    heads = [slice(h * SB_DH, (h + 1) * SB_DH) for h in range(SB_HG)]
    qs = [(q_ref[:, hs] * (SB_DH ** -0.5)).astype(BF16) for hs in heads]

    def block(j, vis, cs):
        start = pl.multiple_of(j * bk, bk)
        return _sb_heads(qs, [k_ref[pl.ds(start, bk), hs] for hs in heads],
                         [v_ref[pl.ds(start, bk), hs] for hs in heads], vis, cs, umat)

    vis_diag = (kio < qpos) & (kio >= jnp.where(qi > 0, 0, n_pad))
    pv, cs, cmax = block(qi, vis_diag, [jnp.zeros((bq, 1), F32) for _ in heads])
    acc_ref[...] = pv

    def body(carry):
        j, cs, _ = carry
        pv, cs, cmax = block(j, None, list(cs))
        acc_ref[...] += pv
        return j - 1, tuple(cs), cmax

    def cond(carry):
        j, _, cmax = carry
        return (j >= 1) & (cmax > -SB_SKIP)

    j, cs, cmax = lax.while_loop(cond, body, (qi - 1, tuple(cs), cmax))

    @pl.when((j == 0) & (cmax > -SB_SKIP))
    def _():
        pv, _, _ = block(0, jnp.broadcast_to(kio >= n_pad, (bq, bk)), list(cs))
        acc_ref[...] += pv

    o_ref[...] = acc_ref[...].astype(BF16)


def sb_attention_prompt(z_main, kvb, umat, *, nb, tp, bq, n_pad):
    nq = tp // bq
    gw = SB_HG * SB_DH
    ng = SB_HEADS // SB_HG
    return pl.pallas_call(
        functools.partial(_sb_prompt_kernel, bq=bq, n_pad=n_pad),
        grid=(nb, ng, nq),
        in_specs=[pl.BlockSpec((bq, gw), lambda b, g, i: (b * nq + i, C_SQ // gw + g)),
                  pl.BlockSpec((tp, gw), lambda b, g, i: (b, C_SK // gw + g)),
                  pl.BlockSpec((tp, gw), lambda b, g, i: (b, C_SV // gw + g)),
                  pl.BlockSpec((SB_BK, SB_BK), lambda b, g, i: (0, 0))],
        out_specs=pl.BlockSpec((bq, gw), lambda b, g, i: (b * nq + i, g)),
        out_shape=jax.ShapeDtypeStruct((nb * tp, SB_HEADS * SB_DH), BF16),
        scratch_shapes=[pltpu.VMEM((bq, gw), F32)],
        compiler_params=_cparams(("arbitrary", "arbitrary", "arbitrary")),
        name="sb_prompt",
    )(z_main, kvb, kvb, umat)


def _sb_sample_kernel(q_ref, kn_ref, vn_ref, kc_ref, vc_ref, u_ref, o_ref, acc_ref, *, n_cache):
    T = CHUNK
    bk = SB_BK
    nfull = n_cache // bk
    rem = n_cache % bk
    umat = u_ref[...]
    heads = [slice(h * SB_DH, (h + 1) * SB_DH) for h in range(SB_HEADS)]
    qs = [(q_ref[:, hs] * (SB_DH ** -0.5)).astype(BF16) for hs in heads]
    vis_new = lax.broadcasted_iota(jnp.int32, (1, T), 1) < lax.broadcasted_iota(jnp.int32, (T, 1), 0)
    pv, cs, cmax = _sb_heads(qs, [kn_ref[:, hs].astype(BF16) for hs in heads],
                             [vn_ref[:, hs].astype(BF16) for hs in heads], vis_new,
                             [jnp.zeros((T, 1), F32) for _ in heads], umat[0:T, 0:T])
    acc_ref[...] = pv

    def cache_block(start, vis, cs):
        return _sb_heads(qs, [kc_ref[0, 0, pl.ds(start, bk), h, :].astype(BF16) for h in range(SB_HEADS)],
                         [vc_ref[0, 0, pl.ds(start, bk), h, :].astype(BF16) for h in range(SB_HEADS)],
                         vis, cs, umat)

    def body(carry):
        j, cs, _ = carry
        pv, cs, cmax = cache_block(pl.multiple_of(n_cache - bk * (j + 1), 8), None, list(cs))
        acc_ref[...] += pv
        return j + 1, tuple(cs), cmax

    def cond(carry):
        j, _, cmax = carry
        return (j < nfull) & (cmax > -SB_SKIP)

    _, cs, cmax = lax.while_loop(cond, body, (jnp.int32(0), tuple(cs), cmax))
    if rem:
        @pl.when(cmax > -SB_SKIP)
        def _():
            vis_rem = jnp.broadcast_to(lax.broadcasted_iota(jnp.int32, (1, bk), 1) < rem, (T, bk))
            pv, _, _ = cache_block(0, vis_rem, list(cs))
            acc_ref[...] += pv
    o_ref[...] = acc_ref[...].astype(BF16)


def sb_attention_sample(z_main, z_kv, cache_k, cache_v, umat, *, layer, row0, nb):
    T = CHUNK
    hd = SB_HEADS * SB_DH
    n_cache = cache_k.shape[2]
    assert n_cache % 8 == 0 and n_cache >= SB_BK
    rb0 = row0 // T
    cache_spec = pl.BlockSpec((1, 1, n_cache, SB_HEADS, SB_DH), lambda b: (layer, b, 0, 0, 0))
    return pl.pallas_call(
        functools.partial(_sb_sample_kernel, n_cache=n_cache),
        grid=(nb,),
        in_specs=[pl.BlockSpec((T, hd), lambda b: (rb0 + b, C_SQ // hd)),
                  pl.BlockSpec((T, hd), lambda b: (rb0 + b, C_SK // hd)),
                  pl.BlockSpec((T, hd), lambda b: (rb0 + b, C_SV // hd)),
                  cache_spec, cache_spec,
                  pl.BlockSpec((SB_BK, SB_BK), lambda b: (0, 0))],
        out_specs=pl.BlockSpec((T, hd), lambda b: (b, 0)),
        out_shape=jax.ShapeDtypeStruct((nb * T, hd), BF16),
        scratch_shapes=[pltpu.VMEM((T, hd), F32)],
        compiler_params=_cparams(("arbitrary",)),
        name="sb_sample",
    )(z_main, z_kv, z_kv, cache_k, cache_v, umat)


def _kv_out_kernel(*refs, depth, nb, nj, rt, n_front):
    z_refs = refs[:depth]
    k_hbm, v_hbm, kbuf, vbuf, sem = refs[depth:]
    l = pl.program_id(0)
    b = pl.program_id(1)
    j = pl.program_id(2)
    lin = (l * nb + b) * nj + j
    total = depth * nb * nj
    slot = lin % 2

    def copies(s, first, dst_lo):
        src_lo, n = (n_front, rt - n_front) if first else (0, rt)
        return [pltpu.make_async_copy(buf.at[s, pl.ds(src_lo, n)], out.at[l, b, pl.ds(dst_lo, n)], sem.at[s, c])
                for c, (buf, out) in enumerate(((kbuf, k_hbm), (vbuf, v_hbm)))]

    def wait_step(step, s):
        jj = step % nj

        @pl.when(jj == 0)
        def _():
            for cp in copies(s, True, 0):
                cp.wait()

        @pl.when(jj != 0)
        def _():
            for cp in copies(s, False, 0):
                cp.wait()

    @pl.when(lin >= 2)
    def _():
        wait_step(lin - 2, slot)

    x = z_refs[0][...]
    for i in range(1, depth):
        x = jnp.where(l == i, z_refs[i][...], x)
    hd = SB_HEADS * SB_DH
    kbuf[slot] = x[:, C_SK:C_SK + hd].reshape(rt, SB_HEADS, SB_DH)
    vbuf[slot] = x[:, C_SV:C_SV + hd].reshape(rt, SB_HEADS, SB_DH)

    @pl.when(j == 0)
    def _():
        for cp in copies(slot, True, 0):
            cp.start()

    @pl.when(j > 0)
    def _():
        for cp in copies(slot, False, j * rt - n_front):
            cp.start()

    @pl.when(lin == total - 1)
    def _():
        if total >= 2:
            wait_step(lin - 1, 1 - slot)
        wait_step(lin, slot)


def kv_state_outputs(z_kvs, *, nb, tp, n_front):
    depth = len(z_kvs)
    rt = _pick(tp, (768, 640, 512, 384, 256))
    assert rt > n_front
    nj = tp // rt
    w = z_kvs[0].shape[1]
    last = nb * nj - 1

    def zspec(i):
        return pl.BlockSpec((rt, w), lambda l, b, j: (jnp.where(l == i, b * nj + j, jnp.where(l < i, 0, last)), 0))

    shape = jax.ShapeDtypeStruct((depth, nb, tp - n_front, SB_HEADS, SB_DH), F32)
    return pl.pallas_call(
        functools.partial(_kv_out_kernel, depth=depth, nb=nb, nj=nj, rt=rt, n_front=n_front),
        grid=(depth, nb, nj),
        in_specs=[zspec(i) for i in range(depth)],
        out_specs=[pl.BlockSpec(memory_space=pl.ANY), pl.BlockSpec(memory_space=pl.ANY)],
        out_shape=[shape, shape],
        scratch_shapes=[pltpu.VMEM((2, rt, SB_HEADS, SB_DH), F32), pltpu.VMEM((2, rt, SB_HEADS, SB_DH), F32),
                        pltpu.SemaphoreType.DMA((2, 2))],
        compiler_params=_cparams(("arbitrary", "arbitrary", "arbitrary")),
        name="kv_state_out",
    )(*z_kvs)


def _merge_kernel(agp_ref, amp_ref, asp_ref, ags_ref, ams_ref, ass_ref, g0_ref, g1_ref, g2_ref, x_ref,
                  wg_ref, wm_ref, ws_ref, wo_ref, lg_ref, lb_ref, wr_ref, br_ref, ur_ref,
                  o_ref, or_ref, idx_ref, gate_ref, cnt_ref, *, alpha, np_tiles):
    is_p = pl.program_id(0) < np_tiles
    ag = jnp.where(is_p, agp_ref[...], ags_ref[...])
    am = jnp.where(is_p, amp_ref[...], ams_ref[...])
    asb = jnp.where(is_p, asp_ref[...], ass_ref[...])
    merged = (_sigmoid(g0_ref[...]) * _dot(ag, wg_ref[...])
              + _sigmoid(g1_ref[...]) * _dot(am, wm_ref[...])
              + _sigmoid(g2_ref[...]) * _dot(asb, ws_ref[...]))
    out = _dot(merged.astype(BF16), wo_ref[...])
    y = _ln_math(alpha * x_ref[...] + out, lg_ref[...], lb_ref[...])
    o_ref[...] = y
    or_ref[...] = y.reshape(or_ref.shape)
    _router_kernel(o_ref, wr_ref, br_ref, ur_ref, idx_ref, gate_ref, cnt_ref)


def merge_ln1(acts_p, acts_s, z_main, x, wg, wm, ws, wo, lg, lb, w_r, b_r, umat, alpha):
    n, d = x.shape
    n_p = acts_p[0].shape[0]
    n_s = acts_s[0].shape[0]
    tm = umat.shape[0]
    assert n_p + n_s == n and n_p % tm == 0 and n_s % tm == 0
    np_tiles = n_p // tm
    row = lambda i: (i, 0)
    row_p = lambda i: (jnp.minimum(i, np_tiles - 1), 0)
    row_s = lambda i: (jnp.maximum(i - np_tiles, 0), 0)
    const = lambda i: (0, 0)
    wspec = pl.BlockSpec((d, d), const)
    gcb = C_GATE // d
    return pl.pallas_call(
        functools.partial(_merge_kernel, alpha=alpha, np_tiles=np_tiles),
        grid=(n // tm,),
        in_specs=[pl.BlockSpec((tm, d), row_p), pl.BlockSpec((tm, d), row_p), pl.BlockSpec((tm, d), row_p),
                  pl.BlockSpec((tm, d), row_s), pl.BlockSpec((tm, d), row_s), pl.BlockSpec((tm, d), row_s),
                  pl.BlockSpec((tm, d), lambda i: (i, gcb)),
                  pl.BlockSpec((tm, d), lambda i: (i, gcb + 1)),
                  pl.BlockSpec((tm, d), lambda i: (i, gcb + 2)),
                  pl.BlockSpec((tm, d), row),
                  wspec, wspec, wspec, wspec,
                  pl.BlockSpec((1, d), const), pl.BlockSpec((1, d), const),
                  pl.BlockSpec((R_ROWS, d), const), pl.BlockSpec((R_ROWS, 1), const),
                  pl.BlockSpec((tm, tm), const)],
        out_specs=[pl.BlockSpec((tm, d), row), pl.BlockSpec((tm, ROW_SUB, d // ROW_SUB), lambda i: (i, 0, 0)),
                   pl.BlockSpec((8, tm), lambda i: (0, i)), pl.BlockSpec((8, tm), lambda i: (0, i)),
                   pl.BlockSpec((N_EXPERTS, 128), const)],
        out_shape=[jax.ShapeDtypeStruct((n, d), F32), jax.ShapeDtypeStruct((n, ROW_SUB, d // ROW_SUB), F32),
                   jax.ShapeDtypeStruct((8, n), jnp.int32), jax.ShapeDtypeStruct((8, n), F32),
                   jax.ShapeDtypeStruct((N_EXPERTS, 128), F32)],
        compiler_params=_cparams(("arbitrary",)),
        name="merge_ln1",
    )(*acts_p, *acts_s, z_main, z_main, z_main, x, wg, wm, ws, wo, lg.reshape(1, d), lb.reshape(1, d),
      w_r, b_r, umat)


R_ROWS = 8 + N_EXPERTS


def _router_kernel(x_ref, w_ref, b_ref, u_ref, idx_ref, gate_ref, cnt_ref):
    i = pl.program_id(0)
    tm = x_ref.shape[0]

    @pl.when(i == 0)
    def _():
        cnt_ref[...] = jnp.zeros_like(cnt_ref)

    logits = _dot_nt(w_ref[...].astype(BF16), x_ref[...].astype(BF16)) + b_ref[...]

    g = [logits[j:j + 1, :] for j in range(N_GROUPS)]
    gmax = jnp.maximum(jnp.maximum(g[0], g[1]), jnp.maximum(g[2], g[3]))
    gsel = jnp.where(g[0] == gmax, 0, jnp.where(g[1] == gmax, 1, jnp.where(g[2] == gmax, 2, 3)))
    gden = (jnp.exp(g[0] - gmax) + jnp.exp(g[1] - gmax)) + (jnp.exp(g[2] - gmax) + jnp.exp(g[3] - gmax))
    gprob = 1.0 / gden
    e_in = jnp.where(gsel == 0, logits[8:16, :],
                     jnp.where(gsel == 1, logits[16:24, :],
                               jnp.where(gsel == 2, logits[24:32, :], logits[32:40, :])))
    ridx = lax.broadcasted_iota(jnp.int32, (EXPERTS_PER_GROUP, tm), 0)
    v1 = jnp.max(e_in, axis=0, keepdims=True)
    i1 = jnp.min(jnp.where(e_in == v1, ridx, EXPERTS_PER_GROUP), axis=0, keepdims=True)
    e2 = jnp.where(ridx == i1, -jnp.inf, e_in)
    v2 = jnp.max(e2, axis=0, keepdims=True)
    i2 = jnp.min(jnp.where(e2 == v2, ridx, EXPERTS_PER_GROUP), axis=0, keepdims=True)
    t = jnp.exp(v2 - v1)
    p1 = 1.0 / (1.0 + t)
    gate_ref[0:1, :] = p1 * gprob
    gate_ref[1:2, :] = (t * p1) * gprob
    gate_ref[2:8, :] = jnp.zeros((6, tm), F32)
    eid1 = gsel * EXPERTS_PER_GROUP + i1
    eid2 = gsel * EXPERTS_PER_GROUP + i2

    eidx = lax.broadcasted_iota(jnp.int32, (N_EXPERTS, tm), 0)
    oh1 = eidx == eid1
    oh2 = eidx == eid2
    oh1f = jnp.where(oh1, 1.0, 0.0)
    oh2f = jnp.where(oh2, 1.0, 0.0)
    umat = u_ref[...]
    pre1 = _dot(oh1f.astype(BF16), umat)
    pre2 = _dot(oh2f.astype(BF16), umat)
    cnt1 = jnp.sum(oh1f, axis=1, keepdims=True)
    cnt2 = jnp.sum(oh2f, axis=1, keepdims=True)
    base = cnt_ref[:, 0:1]
    rank1 = jnp.sum(jnp.where(oh1, base + pre1, 0.0), axis=0, keepdims=True)
    rank2 = jnp.sum(jnp.where(oh2, base + cnt1 + pre2, 0.0), axis=0, keepdims=True)
    idx_ref[0:1, :] = eid1
    idx_ref[1:2, :] = eid2
    idx_ref[2:3, :] = rank1.astype(jnp.int32)
    idx_ref[3:4, :] = rank2.astype(jnp.int32)
    idx_ref[4:8, :] = jnp.zeros((4, tm), jnp.int32)
    cnt_ref[...] = cnt_ref[...] + (cnt1 + cnt2)


def _row_copy(src_hbm, dst_vmem, src_row, dst_row, sem):
    return pltpu.make_async_copy(src_hbm.at[pl.ds(src_row, 1)], dst_vmem.at[pl.ds(dst_row, 1)], sem)


GATHER_UNROLL = 8


def _start_row_gather(idx_ref, base, src_hbm, dst_vmem, nrows, sem):
    def start(r, carry):
        _row_copy(src_hbm, dst_vmem, idx_ref[base + r], r, sem).start()
        return carry

    lax.fori_loop(0, nrows, start, 0, unroll=GATHER_UNROLL)


def _wait_row_gather(src_hbm, dst_vmem, nrows, sem):
    def wait(r, carry):
        _row_copy(src_hbm, dst_vmem, 0, r, sem).wait()
        return carry

    lax.fori_loop(0, nrows, wait, 0, unroll=GATHER_UNROLL)


def _moe_kernel(be_ref, tok_ref, nblk_ref, x_hbm, wg_ref, wu_ref, wd_ref, o_ref,
                xbuf, wgb, wub, wdb, sem):
    i = pl.program_id(0)
    nblk = nblk_ref[0]
    slot = i % 2

    @pl.when((i == 0) & (nblk > 0))
    def _():
        _start_row_gather(tok_ref, 0, x_hbm, xbuf.at[0], MOE_BLK, sem.at[0])

    @pl.when(i + 1 < nblk)
    def _():
        _start_row_gather(tok_ref, (i + 1) * MOE_BLK, x_hbm, xbuf.at[1 - slot], MOE_BLK, sem.at[1 - slot])

    prev_e = be_ref[jnp.maximum(i - 1, 0)]

    @pl.when((i == 0) | (be_ref[i] != prev_e))
    def _():
        wgb[...] = wg_ref[0, 0].astype(BF16)
        wub[...] = wu_ref[0, 0].astype(BF16)
        wdb[...] = wd_ref[0, 0].astype(BF16)

    @pl.when(i < nblk)
    def _():
        _wait_row_gather(x_hbm, xbuf.at[slot], MOE_BLK, sem.at[slot])
        xb = xbuf[slot].reshape(MOE_BLK, D_MODEL).astype(BF16)
        a = _dot(xb, wgb[...])
        hmid = (a * _sigmoid(a)) * _dot(xb, wub[...])
        o_ref[...] = _dot(hmid.astype(BF16), wdb[...]).reshape(o_ref.shape)

    @pl.when(i >= nblk)
    def _():
        o_ref[...] = jnp.zeros_like(o_ref)


def moe_experts(x, block_e, tok_pad, nblk, w_gate, w_up, w_down, *, layer):
    n = x.shape[0]
    d = D_MODEL
    rs = (ROW_SUB, d // ROW_SUB)
    nb = block_e.shape[0]
    wmap = lambda i, be, tok, nblk: (layer, be[i], 0, 0)
    grid_spec = pltpu.PrefetchScalarGridSpec(
        num_scalar_prefetch=3,
        grid=(nb,),
        in_specs=[pl.BlockSpec(memory_space=pl.ANY),
                  pl.BlockSpec((1, 1, d, D_EXPERT), wmap),
                  pl.BlockSpec((1, 1, d, D_EXPERT), wmap),
                  pl.BlockSpec((1, 1, D_EXPERT, d), wmap)],
        out_specs=pl.BlockSpec((MOE_BLK,) + rs, lambda i, be, tok, nblk: (i, 0, 0)),
        scratch_shapes=[pltpu.VMEM((2, MOE_BLK) + rs, F32),
                        pltpu.VMEM((d, D_EXPERT), BF16),
                        pltpu.VMEM((d, D_EXPERT), BF16),
                        pltpu.VMEM((D_EXPERT, d), BF16),
                        pltpu.SemaphoreType.DMA((2,))],
    )
    return pl.pallas_call(
        _moe_kernel,
        grid_spec=grid_spec,
        out_shape=jax.ShapeDtypeStruct((nb * MOE_BLK,) + rs, F32),
        compiler_params=_cparams(("arbitrary",)),
        name="moe_experts",
    )(block_e, tok_pad, nblk, x, w_gate, w_up, w_down)


def _combine_kernel(dest_ref, yb_hbm, x_ref, gate_ref, lg_ref, lb_ref, o_ref, ob_ref, ybuf, sem, *, alpha, n):
    i = pl.program_id(0)
    nt = pl.num_programs(0)
    tm = x_ref.shape[0]
    slot = i % 2

    def start(tile, s):
        for k in range(TOP_K):
            _start_row_gather(dest_ref, k * n + tile * tm, yb_hbm, ybuf.at[s, k], tm, sem.at[s, k])

    @pl.when(i == 0)
    def _():
        start(0, 0)

    @pl.when(i + 1 < nt)
    def _():
        start(i + 1, 1 - slot)

    for k in range(TOP_K):
        _wait_row_gather(yb_hbm, ybuf.at[slot, k], tm, sem.at[slot, k])
    gate = gate_ref[...]
    d = x_ref.shape[1]
    y = ybuf[slot, 0].reshape(tm, d) * gate[:, 0:1] + ybuf[slot, 1].reshape(tm, d) * gate[:, 1:2]
    out = _ln_math(alpha * x_ref[...] + y, lg_ref[...], lb_ref[...])
    o_ref[...] = out
    ob_ref[...] = out.astype(BF16)


def moe_combine_ln2(x, yb, dest, gate_cols, lg, lb, alpha):
    n, d = x.shape
    tm = _pick(n, (256, 128, 64))
    grid_spec = pltpu.PrefetchScalarGridSpec(
        num_scalar_prefetch=1,
        grid=(n // tm,),
        in_specs=[pl.BlockSpec(memory_space=pl.ANY),
                  pl.BlockSpec((tm, d), lambda i, dest: (i, 0)),
                  pl.BlockSpec((tm, TOP_K), lambda i, dest: (i, 0)),
                  pl.BlockSpec((1, d), lambda i, dest: (0, 0)),
                  pl.BlockSpec((1, d), lambda i, dest: (0, 0))],
        out_specs=[pl.BlockSpec((tm, d), lambda i, dest: (i, 0)), pl.BlockSpec((tm, d), lambda i, dest: (i, 0))],
        scratch_shapes=[pltpu.VMEM((2, TOP_K, tm, ROW_SUB, d // ROW_SUB), F32),
                        pltpu.SemaphoreType.DMA((2, TOP_K))],
    )
    return pl.pallas_call(
        functools.partial(_combine_kernel, alpha=alpha, n=n),
        grid_spec=grid_spec,
        out_shape=[jax.ShapeDtypeStruct((n, d), F32), jax.ShapeDtypeStruct((n, d), BF16)],
        compiler_params=_cparams(("arbitrary",)),
        name="moe_combine_ln2",
    )(dest, yb, x, gate_cols, lg.reshape(1, d), lb.reshape(1, d))


def _tri_lower(n):
    t = np.arange(n)
    return (t[None, :] <= t[:, None]).astype(np.float32)


def _strict_upper(n):
    t = np.arange(n)
    return (t[:, None] < t[None, :]).astype(np.float32)


def kernel(x_prompt, x_sample, state_gla, state_ml_C, state_ml_n, state_ml_m, state_ml_conv, cache_sb_k, cache_sb_v, meta_tokens, ln_in_g, ln_in_b, w_in, gla_w_a2, gla_b_a, gla_norm_g, ml_conv_w, ml_conv_b, ml_wq, ml_wk, ml_wv, ml_b_i, ml_b_f, ml_norm_g, w_br_gla, w_br_ml, w_br_sb, w_out, ln1_g, ln1_b, w_rg, b_rg, w_re, b_re, w_e_gate, w_e_up, w_e_down, ln2_g, ln2_b):
    D = D_MODEL
    B, SEQ, _ = x_prompt.shape
    DB, DS, _ = x_sample.shape
    depth = w_in.shape[0]
    n_cache = cache_sb_k.shape[2]
    assert DS == CHUNK and SEQ % 128 == 0 and x_prompt.shape[2] == D
    alpha = (2.0 * depth) ** 0.25
    TP = PAD_FRONT + N_META + SEQ
    NP = B * TP
    NS = DB * DS
    N = NP + NS
    assert TP % CHUNK_PROMPT == 0 and NP % CHUNK == 0
    nchunk_p = TP // CHUNK_PROMPT

    pieces = []
    for b in range(B):
        pieces += [jnp.zeros((PAD_FRONT, D), F32), meta_tokens.astype(F32), x_prompt[b].astype(F32)]
    x = jnp.concatenate(pieces + [x_sample.reshape(NS, D).astype(F32)], axis=0)
    x, xb = layer_norm_rows(x, ln_in_g, ln_in_b)

    u_sb = jnp.asarray(_strict_upper(SB_BK).T, BF16)
    tm_r = _pick(N, (256, 128, 64))
    u_r = jnp.asarray(_strict_upper(tm_r), BF16)

    A = N * TOP_K
    NB = -(-(A + N_EXPERTS * (MOE_BLK - 1)) // MOE_BLK)

    outs_p = [[] for _ in range(5)]
    outs_s = [[] for _ in range(7)]
    z_kvs = []
    for l in range(depth):
        wi = w_in[l]
        w_main = jnp.concatenate([wi[:, 0:3072], wi[:, 3088:4112], wi[:, 4120:5144], wi[:, 5144:6168],
                                  wi[:, 8216:]], axis=1).astype(BF16)
        w_kv = jnp.concatenate([wi[:, 6168:8216], wi[:, 3072:3088], wi[:, 4112:4120],
                                jnp.zeros((D, 128 - 24), F32)], axis=1).astype(BF16)
        z_main = in_projection(xb, w_main, 2304, "in_proj_main")
        z_kv, kvb = in_projection(xb, w_kv, w_kv.shape[1], "in_proj_kv", with_bf16=True)
        z_small = z_kv
        z_kvs.append(z_kv)

        wa = jnp.concatenate([gla_w_a2[l], jnp.zeros((128 - GLA_RANK, GLA_HEADS * GLA_DK), F32)], axis=0).astype(BF16)
        ba = gla_b_a[l].reshape(1, -1)
        ng = gla_norm_g[l].reshape(1, -1)
        s0_p = jnp.zeros((B, GLA_HEADS, GLA_DV, GLA_DK), F32)
        s0_s = jnp.swapaxes(state_gla[l].astype(F32), -1, -2)
        geo_p = dict(row0=0, nb=B, nchunk=nchunk_p, chunk=CHUNK_PROMPT)
        geo_s = dict(row0=NP, nb=DB, nchunk=1, chunk=CHUNK)
        gla_p = _gla_call_parts(z_main, z_small, wa, ba, ng, s0_p, **geo_p)
        gla_s = _gla_call_parts(z_main, z_small, wa, ba, ng, s0_s, **geo_s)

        gbias = jnp.zeros((1, 128), F32).at[0, L_MI:L_MI + ML_HEADS].set(ml_b_i[l]).at[0, L_MF:L_MF + ML_HEADS].set(ml_b_f[l])
        cw = ml_conv_w[l]
        cb = ml_conv_b[l].reshape(1, -1)
        wqk = jnp.concatenate([ml_wq[l], ml_wk[l]], axis=-1).astype(BF16)
        wv = ml_wv[l].astype(BF16)
        mng = ml_norm_g[l].reshape(1, -1)
        W = ML_HEADS * ML_DH
        conv0_p = jnp.zeros((B, 8, W), F32)
        conv0_s = jnp.concatenate([jnp.zeros((DB, 8 - (ML_CONV - 1), W), F32), state_ml_conv[l].astype(F32)], axis=1)
        m0_s = jnp.zeros((DB, 1, 128), F32).at[:, 0, L_MF:L_MF + ML_HEADS].set(state_ml_m[l].astype(F32))
        ml_p = _mlstm_call_parts(
            z_main, z_small, gbias, cw, cb, wqk, wv, mng, conv0_p,
            jnp.zeros((1, B, ML_HEADS, ML_DH, ML_DH), F32), jnp.zeros((B, ML_HEADS, ML_DH), F32),
            jnp.zeros((B, 1, 128), F32), c0_layer=0, **geo_p)
        ml_s = _mlstm_call_parts(
            z_main, z_small, gbias, cw, cb, wqk, wv, mng, conv0_s,
            state_ml_C.astype(F32), state_ml_n[l].astype(F32), m0_s, c0_layer=l, **geo_s)
        a_gla_p, sT_p, a_ml_p, c_p, n_p, m_p = recurrent_branches(
            gla_p, ml_p, nb=B, nchunk=nchunk_p, chunk=CHUNK_PROMPT, n_pad=PAD_FRONT)
        a_gla_s, sT_s, a_ml_s, c_s, n_s, m_s = recurrent_branches(
            gla_s, ml_s, nb=DB, nchunk=1, chunk=CHUNK, n_pad=0)

        a_sb_p = sb_attention_prompt(z_main, kvb, u_sb, nb=B, tp=TP, bq=SB_BK, n_pad=PAD_FRONT)
        a_sb_s = sb_attention_sample(z_main, z_kv, cache_sb_k.astype(F32), cache_sb_v.astype(F32),
                                     u_sb, layer=l, row0=NP, nb=DB)

        w_r = jnp.concatenate([w_rg[l].T, jnp.zeros((8 - N_GROUPS, D), F32), w_re[l].T], axis=0)
        b_r = jnp.concatenate([b_rg[l], jnp.zeros((8 - N_GROUPS,), F32), b_re[l]]).reshape(R_ROWS, 1)
        x1, x1r, idx, gates, cnt = merge_ln1(
            (a_gla_p, a_ml_p, a_sb_p), (a_gla_s, a_ml_s, a_sb_s), z_main, x,
            w_br_gla[l].astype(BF16), w_br_ml[l].astype(BF16), w_br_sb[l].astype(BF16),
            w_out[l].astype(BF16), ln1_g[l], ln1_b[l], w_r, b_r, u_r, alpha)
        counts = cnt[:, 0].astype(jnp.int32)
        padded = (counts + MOE_BLK - 1) // MOE_BLK * MOE_BLK
        ends_pad = jnp.cumsum(padded)
        start_pad = ends_pad - padded
        eids = jnp.arange(N_EXPERTS, dtype=jnp.int32)
        start_of = jnp.sum(jnp.where(idx[0:2, :, None] == eids, start_pad, 0), axis=-1)
        dest = (start_of + idx[2:4]).reshape(-1)
        tok = jnp.tile(jnp.arange(N, dtype=jnp.int32), TOP_K)
        tok_pad = jnp.zeros((NB * MOE_BLK,), jnp.int32).at[dest].set(tok)
        blk_lo = jnp.arange(NB, dtype=jnp.int32) * MOE_BLK
        block_e = jnp.minimum(jnp.sum((ends_pad[None, :] <= blk_lo[:, None]).astype(jnp.int32), axis=-1),
                              N_EXPERTS - 1)
        nblk = (ends_pad[-1:] // MOE_BLK).astype(jnp.int32)
        yb = moe_experts(x1r, block_e, tok_pad, nblk, w_e_gate, w_e_up, w_e_down, layer=l)
        x, xb = moe_combine_ln2(x1, yb, dest, gates[0:2].T, ln2_g[l], ln2_b[l], alpha)

        nct = ML_CONV - 1
        conv_p = jnp.stack([z_main[(b + 1) * TP - nct:(b + 1) * TP, C_MU:C_MU + W] for b in range(B)], axis=0)
        conv_s = z_main[NP:, C_MU:C_MU + W].reshape(DB, DS, W)[:, DS - nct:]
        zkv_s = z_kv[NP:, :2 * D].reshape(DB, DS, 2 * D)
        kv_shape = (SB_HEADS, SB_DH)
        outs_p[0].append(jnp.swapaxes(sT_p, -1, -2))
        outs_s[0].append(jnp.swapaxes(sT_s, -1, -2))
        outs_p[1].append(c_p); outs_s[1].append(c_s)
        outs_p[2].append(n_p); outs_s[2].append(n_s)
        outs_p[3].append(m_p[:, 0, L_MF:L_MF + ML_HEADS]); outs_s[3].append(m_s[:, 0, L_MF:L_MF + ML_HEADS])
        outs_p[4].append(conv_p); outs_s[4].append(conv_s)
        outs_s[5].append(zkv_s[:, :, C_SK:C_SK + D].reshape(DB, DS, *kv_shape))
        outs_s[6].append(zkv_s[:, :, C_SV:C_SV + D].reshape(DB, DS, *kv_shape))

    sbk_p, sbv_p = kv_state_outputs(z_kvs, nb=B, tp=TP, n_front=PAD_FRONT)
    sp = [jnp.stack(a, axis=0) for a in outs_p]
    ss = [jnp.stack(a, axis=0) for a in outs_s]
    y_prompt = jnp.stack([x[b * TP + PAD_FRONT + N_META:(b + 1) * TP] for b in range(B)], axis=0)
    y_sample = x[NP:].reshape(DB, DS, D)
    return (y_prompt, y_sample, sp[0], ss[0], sp[1], ss[1], sp[2], ss[2], sp[3], ss[3],
            sp[4], ss[4], sbk_p, ss[5], sbv_p, ss[6])
```
